```python
import jax
import jax.numpy as jnp
from jax import lax
import numpy as np

D_MODEL = 1024
BATCH = 32
SEQ = 256
DEPTH = 4
DEC_BATCH = 2
DEC_SEQ = 2048
PAST_LEN = 256

GRID_W = 64
N_MIXERS = 3
N_CONV_LAYERS = (DEPTH + 2) // 3
N_NA_LAYERS = (DEPTH + 1) // 3
N_HG_LAYERS = DEPTH // 3
CONV_WIDTH = 3
NA_HEADS = 16
NA_HEAD_DIM = D_MODEL // NA_HEADS
NA_ROWS_MAX = 8
NA_COLS = 16
NA_SCALE = NA_HEAD_DIM ** -0.5
Q_BLOCK = 128
HG_HEADS = 8
HG_KEY_DIM = 128
HG_VAL_DIM = D_MODEL // HG_HEADS
HG_FDIM = HG_HEADS * HG_KEY_DIM
HG_CHUNK = 32
N_EXPERTS = 32
TOP_K = 4
D_FF = D_MODEL
SWIGLU_LIMIT = 7.0
SWIGLU_ALPHA = 1.702
MOE_BLOCK = 128
EPS = 1e-6
F32 = jnp.float32

kernel_name = 'hybrid_diffusion_conv_na_hgrn2_moe_step'


def rmsnorm(x, g):
    xf = x.astype(F32)
    y = xf * lax.rsqrt(jnp.mean(xf * xf, axis=-1, keepdims=True) + EPS)
    return (y * g.astype(F32)).astype(x.dtype)


def adaln_params(cvec, w_mod, b_mod):
    m = jax.nn.silu(cvec) @ w_mod + b_mod
    return jnp.split(m[:, None, :], 6, axis=-1)


def modulate(x, g, shift, scale):
    return rmsnorm(x, g) * (1 + scale) + shift


def short_conv_mixer(h, w_in, conv_k, w_out):
    L = h.shape[1]
    b_gate, c_gate, u = jnp.split(h @ w_in, 3, axis=-1)
    pad = CONV_WIDTH // 2
    v = jnp.pad(c_gate * u, ((0, 0), (pad, pad), (0, 0)))
    conv = sum(v[:, j:j + L] * conv_k[j] for j in range(CONV_WIDTH))
    return (b_gate * conv) @ w_out


def na_context_mixer(h, w_qkv, w_o):
    B, L, _ = h.shape
    qkv = (h @ w_qkv).reshape(B, L, 3, NA_HEADS, NA_HEAD_DIM)
    q, k, v = qkv[:, :, 0], qkv[:, :, 1], qkv[:, :, 2]
    qb = q.reshape(B, L // Q_BLOCK, Q_BLOCK, NA_HEADS, NA_HEAD_DIM).swapaxes(0, 1)

    def block(qi):
        s = jnp.einsum('bqhd,bkhd->bhqk', qi, k).astype(F32) * NA_SCALE
        p = jax.nn.softmax(s, axis=-1).astype(v.dtype)
        return jnp.einsum('bhqk,bkhd->bqhd', p, v)

    o = lax.map(block, qb).swapaxes(0, 1).reshape(B, L, D_MODEL)
    return o @ w_o, k, v


def na_latent_mixer(h, ctx_k, ctx_v, w_qkv, rpb, w_o):
    B, L, _ = h.shape
    rows = L // GRID_W
    kr = min(NA_ROWS_MAX, rows)
    n_nb = kr * NA_COLS
    qkv = (h @ w_qkv).reshape(B, rows, GRID_W, 3, NA_HEADS, NA_HEAD_DIM)
    q, k, v = qkv[:, :, :, 0], qkv[:, :, :, 1], qkv[:, :, :, 2]
    col = np.arange(GRID_W)
    col_start = np.clip(col - NA_COLS // 2, 0, GRID_W - NA_COLS)
    col_idx = col_start[:, None] + np.arange(NA_COLS)[None, :]
    dc = col_idx - col[:, None] + (NA_COLS - 1)
    row_start = jnp.asarray(np.clip(np.arange(rows) - kr // 2, 0, rows - kr), jnp.int32)

    def one_row(args):
        q_r, r, r0 = args
        k_rows = lax.dynamic_slice_in_dim(k, r0, kr, axis=1)
        v_rows = lax.dynamic_slice_in_dim(v, r0, kr, axis=1)
        k_nb = k_rows[:, :, col_idx]
        v_nb = v_rows[:, :, col_idx]
        dr = r0 + jnp.arange(kr) - r + (NA_ROWS_MAX - 1)
        bias = rpb[:, dr][:, :, dc].transpose(0, 2, 1, 3)
        s_nb = jnp.einsum('bwhd,brwkhd->bhwrk', q_r, k_nb).astype(F32) * NA_SCALE + bias[None].astype(F32)
        s_ctx = jnp.einsum('bwhd,bnhd->bhwn', q_r, ctx_k).astype(F32) * NA_SCALE
        s = jnp.concatenate([s_nb.reshape(B, NA_HEADS, GRID_W, n_nb), s_ctx], axis=-1)
        p = jax.nn.softmax(s, axis=-1).astype(v.dtype)
        p_nb = p[..., :n_nb].reshape(B, NA_HEADS, GRID_W, kr, NA_COLS)
        return (jnp.einsum('bhwrk,brwkhd->bwhd', p_nb, v_nb)
                + jnp.einsum('bhwn,bnhd->bwhd', p[..., n_nb:], ctx_v))

    o = lax.map(one_row, (q.swapaxes(0, 1), jnp.arange(rows, dtype=jnp.int32), row_start))
    return o.swapaxes(0, 1).reshape(B, L, D_MODEL) @ w_o


def hgrn_chunk_scan(q, k, v, log_f, s0):
    B, L = q.shape[:2]
    n = L // HG_CHUNK

    def chunks(a):
        return a.astype(F32).reshape(B, n, HG_CHUNK, *a.shape[2:]).swapaxes(0, 1)

    lower = jnp.tril(jnp.ones((HG_CHUNK, HG_CHUNK), bool))[None, :, :, None, None]

    def step(S, inp):
        qc, kc, vc, gc = inp
        b = jnp.cumsum(gc, axis=1)
        decay = jnp.exp(jnp.where(lower, b[:, :, None] - b[:, None, :], -jnp.inf))
        att = jnp.einsum('bthd,btshd,bshd->bhts', qc, decay, kc)
        o = (jnp.einsum('bhts,bshv->bthv', att, vc)
             + jnp.einsum('bthd,bhdv->bthv', qc * jnp.exp(b), S))
        b_last = b[:, -1]
        S = (jnp.exp(b_last)[..., None] * S
             + jnp.einsum('bshd,bshv->bhdv', kc * jnp.exp(b_last[:, None] - b), vc))
        return S, o

    S, o = lax.scan(step, s0.astype(F32), (chunks(q), chunks(k), chunks(v), chunks(log_f)))
    return o.swapaxes(0, 1).reshape(B, L, HG_HEADS, HG_VAL_DIM), S


def hgrn_mixer(h, s0, w_qig, w_f, lb, norm_g, w_o):
    B, L, _ = h.shape
    qig = h @ w_qig
    q = qig[..., :HG_FDIM].reshape(B, L, HG_HEADS, HG_KEY_DIM)
    v = qig[..., HG_FDIM:HG_FDIM + D_MODEL].reshape(B, L, HG_HEADS, HG_VAL_DIM)
    g_out = qig[..., HG_FDIM + D_MODEL:]
    o_sum = 0.0
    finals = []
    for d in range(2):
        flip = (lambda a: a) if d == 0 else (lambda a: a[:, ::-1])
        z = (h @ w_f[d]).astype(F32).reshape(B, L, HG_HEADS, HG_KEY_DIM)
        lbd = lb[d].reshape(HG_HEADS, HG_KEY_DIM)
        log_f = jnp.logaddexp(jnp.log(lbd), jnp.log1p(-lbd) + jax.nn.log_sigmoid(z))
        k = (1.0 - lbd) * jax.nn.sigmoid(-z)
        o, S = hgrn_chunk_scan(flip(q), flip(k), flip(v), flip(log_f), s0[:, d])
        o_sum = o_sum + flip(o)
        finals.append(S)
    o = o_sum * lax.rsqrt(jnp.mean(o_sum * o_sum, axis=-1, keepdims=True) + EPS)
    o = (o * norm_g.astype(F32).reshape(HG_HEADS, HG_VAL_DIM)).reshape(B, L, D_MODEL).astype(h.dtype)
    return (o * jax.nn.silu(g_out)) @ w_o, jnp.stack(finals, axis=1)


def moe_ffn(h, w_router, b_router, w1, b1, w2, b2):
    shape = h.shape
    x = h.reshape(-1, D_MODEL)
    T = x.shape[0]
    n_assign = T * TOP_K
    n_blocks = -(-n_assign // MOE_BLOCK) + N_EXPERTS
    n_rows = n_blocks * MOE_BLOCK
    logits = (x @ w_router).astype(F32) + b_router.astype(F32)
    top_val, top_idx = lax.top_k(logits, TOP_K)
    gate = jax.nn.softmax(top_val, axis=-1).reshape(-1)
    flat_e = top_idx.reshape(-1)
    flat_tok = jnp.arange(n_assign, dtype=jnp.int32) // TOP_K
    order = jnp.argsort(flat_e)
    sorted_e = flat_e[order]
    counts = jnp.bincount(flat_e, length=N_EXPERTS)
    padded = (counts + MOE_BLOCK - 1) // MOE_BLOCK * MOE_BLOCK
    padded_end = jnp.cumsum(padded)
    group_start = jnp.cumsum(counts) - counts
    dest = (padded_end - padded)[sorted_e] + jnp.arange(n_assign, dtype=jnp.int32) - group_start[sorted_e]
    row_tok = jnp.full((n_rows,), T, jnp.int32).at[dest].set(flat_tok[order])
    row_gate = jnp.zeros((n_rows,), F32).at[dest].set(gate[order])
    block_expert = jnp.minimum(
        jnp.searchsorted(padded_end, jnp.arange(n_blocks, dtype=jnp.int32) * MOE_BLOCK, side='right'),
        N_EXPERTS - 1)
    x_rows = jnp.concatenate([x, jnp.zeros((1, D_MODEL), x.dtype)], axis=0)[row_tok]

    def expert_block(args):
        xe, e = args
        gu = xe @ w1[e] + b1[e]
        g, u = gu[:, :D_FF], gu[:, D_FF:]
        g = jnp.minimum(g, SWIGLU_LIMIT)
        u = jnp.clip(u, -SWIGLU_LIMIT, SWIGLU_LIMIT)
        return ((u + 1) * (g * jax.nn.sigmoid(SWIGLU_ALPHA * g))) @ w2[e] + b2[e]

    y_rows = lax.map(expert_block, (x_rows.reshape(n_blocks, MOE_BLOCK, D_MODEL), block_expert))
    y = jnp.zeros((T + 1, D_MODEL), F32).at[row_tok].add(
        y_rows.reshape(n_rows, D_MODEL).astype(F32) * row_gate[:, None])
    return y[:T].astype(h.dtype).reshape(shape)


def setup_inputs(seed: int = 0) -> dict:
    key = jax.random.key(seed)
    ks = iter(jax.random.split(key, 40))
    D = D_MODEL

    def nrm(shape, s):
        return jax.random.normal(next(ks), shape, F32) * s

    return {
        'x_prompt': nrm((BATCH, SEQ, D), 1.0),
        'x_sample': nrm((DEC_BATCH, DEC_SEQ, D), 1.0),
        'c': nrm((DEC_BATCH, D), 1.0),
        'c_ctx': nrm((D,), 1.0),
        'cache_na_k': nrm((DEC_BATCH, N_NA_LAYERS, PAST_LEN, NA_HEADS, NA_HEAD_DIM), 1.0),
        'cache_na_v': nrm((DEC_BATCH, N_NA_LAYERS, PAST_LEN, NA_HEADS, NA_HEAD_DIM), 1.0),
        'state_hgrn': nrm((DEC_BATCH, N_HG_LAYERS, 2, HG_HEADS, HG_KEY_DIM, HG_VAL_DIM), 0.5),
        'norm_mix': 1.0 + nrm((DEPTH, D), 0.05),
        'norm_ffn': 1.0 + nrm((DEPTH, D), 0.05),
        'w_mod': nrm((DEPTH, D, 6 * D), 0.5 * D ** -0.5),
        'b_mod': nrm((DEPTH, 6 * D), 0.02),
        'conv_w_in': nrm((N_CONV_LAYERS, D, 3 * D), D ** -0.5),
        'conv_k': nrm((N_CONV_LAYERS, CONV_WIDTH, D), CONV_WIDTH ** -0.5),
        'conv_w_out': nrm((N_CONV_LAYERS, D, D), D ** -0.5),
        'na_w_qkv': nrm((N_NA_LAYERS, D, 3 * D), D ** -0.5),
        'na_rpb': nrm((N_NA_LAYERS, NA_HEADS, 2 * NA_ROWS_MAX - 1, 2 * NA_COLS - 1), 0.5),
        'na_w_o': nrm((N_NA_LAYERS, D, D), D ** -0.5),
        'hg_w_qig': nrm((N_HG_LAYERS, D, HG_FDIM + 2 * D), D ** -0.5),
        'hg_w_f': nrm((N_HG_LAYERS, 2, D, HG_FDIM), D ** -0.5),
        'hg_lb': nrm((DEPTH, 2, HG_FDIM), 1.0),
        'hg_norm': 1.0 + nrm((N_HG_LAYERS, D), 0.05),
        'hg_w_o': nrm((N_HG_LAYERS, D, D), D ** -0.5),
        'moe_w_router': nrm((DEPTH, D, N_EXPERTS), D ** -0.5),
        'moe_b_router': nrm((DEPTH, N_EXPERTS), 0.01),
        'moe_w1': nrm((DEPTH, N_EXPERTS, D, 2 * D_FF), D ** -0.5),
        'moe_b1': nrm((DEPTH, N_EXPERTS, 2 * D_FF), 0.02),
        'moe_w2': nrm((DEPTH, N_EXPERTS, D_FF, D), D_FF ** -0.5),
        'moe_b2': nrm((DEPTH, N_EXPERTS, D), 0.02),
        'final_norm': 1.0 + nrm((D,), 0.05),
    }


def reference(x_prompt, x_sample, c, c_ctx, cache_na_k, cache_na_v, state_hgrn,
              norm_mix, norm_ffn, w_mod, b_mod, conv_w_in, conv_k, conv_w_out,
              na_w_qkv, na_rpb, na_w_o, hg_w_qig, hg_w_f, hg_lb, hg_norm, hg_w_o,
              moe_w_router, moe_b_router, moe_w1, moe_b1, moe_w2, moe_b2, final_norm):
    xp, xs = x_prompt, x_sample
    lb_soft = jax.nn.softmax(hg_lb.astype(F32), axis=0)
    lower_bounds = jnp.cumsum(lb_soft, axis=0) - lb_soft[0]
    new_k, new_v, new_s = [], [], []
    for l in range(DEPTH):
        kind, j = l % N_MIXERS, l // N_MIXERS
        shp1, scp1, gp1, shp2, scp2, gp2 = adaln_params(c_ctx[None], w_mod[l], b_mod[l])
        shs1, scs1, gs1, shs2, scs2, gs2 = adaln_params(c, w_mod[l], b_mod[l])
        hp = modulate(xp, norm_mix[l], shp1, scp1)
        hs = modulate(xs, norm_mix[l], shs1, scs1)
        if kind == 0:
            mp = short_conv_mixer(hp, conv_w_in[j], conv_k[j], conv_w_out[j])
            ms = short_conv_mixer(hs, conv_w_in[j], conv_k[j], conv_w_out[j])
        elif kind == 1:
            mp, kp, vp = na_context_mixer(hp, na_w_qkv[j], na_w_o[j])
            ms = na_latent_mixer(hs, cache_na_k[:, j], cache_na_v[:, j], na_w_qkv[j], na_rpb[j], na_w_o[j])
            new_k.append(kp)
            new_v.append(vp)
        else:
            zero_state = jnp.zeros((xp.shape[0], 2, HG_HEADS, HG_KEY_DIM, HG_VAL_DIM), F32)
            mp, sp = hgrn_mixer(hp, zero_state, hg_w_qig[j], hg_w_f[j], lower_bounds[l], hg_norm[j], hg_w_o[j])
            ms, _ = hgrn_mixer(hs, state_hgrn[:, j], hg_w_qig[j], hg_w_f[j], lower_bounds[l], hg_norm[j], hg_w_o[j])
            new_s.append(sp)
        xp = xp + gp1 * mp
        xs = xs + gs1 * ms
        xp = xp + gp2 * moe_ffn(modulate(xp, norm_ffn[l], shp2, scp2), moe_w_router[l], moe_b_router[l],
                                moe_w1[l], moe_b1[l], moe_w2[l], moe_b2[l])
        xs = xs + gs2 * moe_ffn(modulate(xs, norm_ffn[l], shs2, scs2), moe_w_router[l], moe_b_router[l],
                                moe_w1[l], moe_b1[l], moe_w2[l], moe_b2[l])
    y_prompt = rmsnorm(xp, final_norm)
    y_sample = rmsnorm(xs, final_norm)
    new_na_k = jnp.stack(new_k, axis=1).astype(x_prompt.dtype)
    new_na_v = jnp.stack(new_v, axis=1).astype(x_prompt.dtype)
    new_hgrn_state = jnp.stack(new_s, axis=1).astype(x_prompt.dtype)
    return (y_prompt, y_sample, new_na_k, new_na_v, new_hgrn_state)
```

```python
import functools

import numpy as np
import jax
import jax.numpy as jnp
from jax import lax
from jax.experimental import pallas as pl
from jax.experimental.pallas import tpu as pltpu

F32 = jnp.float32
BF16 = jnp.bfloat16
HIGHEST = lax.Precision.HIGHEST

N_MIXERS = 3
GRID_W = 64
CONV_WIDTH = 3
NA_HEADS = 16
NA_ROWS_MAX = 8
NA_COLS = 16
HG_HEADS = 8
HG_KEY_DIM = 128
HG_CHUNK = 32
N_EXPERTS = 32
TOP_K = 4
SWIGLU_LIMIT = 7.0
SWIGLU_ALPHA = 1.702
EPS = 1e-6

LANES = 128
SUBLANES = 8
VMEM_LIMIT_CAP = 60 * 1024 * 1024

NEG_BIG = -1e30
TOKEN_TILE = 256
MOE_ROWS = 256
HG_TILE = 128


def _params(sem, vmem_mb):
    return pltpu.CompilerParams(dimension_semantics=sem,
                                vmem_limit_bytes=min(vmem_mb * 1024 * 1024, VMEM_LIMIT_CAP))


def _rms(x):
    return x * lax.rsqrt(jnp.mean(x * x, axis=-1, keepdims=True) + EPS)


def _modulate(x, g, shift, scale):
    return (_rms(x) * g) * (1.0 + scale) + shift


def _group_of_tile(i, tile, n_prompt, dec_seq):
    start = i * tile
    return jnp.where(start < n_prompt, 0, 1 + (start - n_prompt) // dec_seq)


def _adaln_kernel(c_ref, w_ref, b_ref, o_ref):
    cv = c_ref[...]
    s = cv * jax.nn.sigmoid(cv)
    o_ref[...] = jnp.dot(s, w_ref[...], precision=HIGHEST, preferred_element_type=F32) + b_ref[...]


def _adaln_all(cvecs, w_mod, b_mod):
    depth, d, n = w_mod.shape
    tn = 2048
    return pl.pallas_call(
        _adaln_kernel,
        grid=(depth, n // tn),
        in_specs=[pl.BlockSpec((SUBLANES, d), lambda l, j: (0, 0)),
                  pl.BlockSpec((None, d, tn), lambda l, j: (l, 0, j)),
                  pl.BlockSpec((None, 1, tn), lambda l, j: (l, 0, j))],
        out_specs=pl.BlockSpec((None, SUBLANES, tn), lambda l, j: (l, 0, j)),
        out_shape=jax.ShapeDtypeStruct((depth, SUBLANES, n), F32),
        compiler_params=_params(("arbitrary", "arbitrary"), 40),
        name="adaln",
    )(cvecs, w_mod, b_mod.reshape(depth, 1, n))


def _in_kernel(x_ref, mod_ref, g_ref, w_ref, *out_refs, d, conv):
    mod = mod_ref[...]
    h = _modulate(x_ref[...], g_ref[...], mod[:, 0:d], mod[:, d:2 * d]).astype(BF16)
    if conv:
        b_ref, cu_ref = out_refs
        b_ref[...] = jnp.dot(h, w_ref[:, 0:d], preferred_element_type=F32)
        c = jnp.dot(h, w_ref[:, d:2 * d], preferred_element_type=F32)
        u = jnp.dot(h, w_ref[:, 2 * d:3 * d], preferred_element_type=F32)
        cu_ref[...] = c * u
    else:
        for j, o_ref in enumerate(out_refs):
            o_ref[...] = jnp.dot(h, w_ref[:, j * d:(j + 1) * d], preferred_element_type=F32)


def _mixer_in(x, mods_l, g, w, geom, conv=False):
    t, d = x.shape
    n = w.shape[1]
    n_out = 2 if conv else n // d
    tm = TOKEN_TILE
    grp = functools.partial(_group_of_tile, tile=tm, n_prompt=geom[0], dec_seq=geom[1])
    return pl.pallas_call(
        functools.partial(_in_kernel, d=d, conv=conv),
        grid=(t // tm,),
        in_specs=[pl.BlockSpec((tm, d), lambda i: (i, 0)),
                  pl.BlockSpec((None, 1, 6 * d), lambda i: (grp(i), 0, 0)),
                  pl.BlockSpec((1, d), lambda i: (0, 0)),
                  pl.BlockSpec((d, n), lambda i: (0, 0))],
        out_specs=[pl.BlockSpec((tm, d), lambda i: (i, 0))] * n_out,
        out_shape=[jax.ShapeDtypeStruct((t, d), F32)] * n_out,
        compiler_params=_params(("arbitrary",), 48),
        name="mixer_in",
    )(x, mods_l, g, w)


def _tail(m, x_ref, mod, g2_ref, wr_ref, br_ref, x1_ref, h2_ref, idx_ref, gw_ref, d):
    x1 = x_ref[...] + mod[:, 2 * d:3 * d] * m
    x1_ref[...] = x1
    h2 = _modulate(x1, g2_ref[...], mod[:, 3 * d:4 * d], mod[:, 4 * d:5 * d])
    h2_ref[...] = h2
    logits = jnp.dot(h2, wr_ref[...], precision=HIGHEST, preferred_element_type=F32) + br_ref[...]
    lane = lax.broadcasted_iota(jnp.int32, logits.shape, 1)
    vals, idxs = [], []
    for _ in range(TOP_K):
        mx = jnp.max(logits, axis=-1, keepdims=True)
        ix = jnp.min(jnp.where(logits == mx, lane, LANES), axis=-1, keepdims=True)
        vals.append(mx)
        idxs.append(ix)
        logits = jnp.where(lane == ix, NEG_BIG, logits)
    es = [jnp.exp(v - vals[0]) for v in vals]
    den = es[0] + es[1] + es[2] + es[3]
    idx_out = jnp.zeros(lane.shape, jnp.int32)
    gw_out = jnp.zeros(lane.shape, F32)
    for k in range(TOP_K):
        idx_out = jnp.where(lane == k, idxs[k], idx_out)
        gw_out = jnp.where(lane == k, es[k] / den, gw_out)
    idx_ref[...] = idx_out
    gw_ref[...] = gw_out


def _plain_out_kernel(a_ref, w_ref, x_ref, mod_ref, g2_ref, wr_ref, br_ref,
                      x1_ref, h2_ref, idx_ref, gw_ref, *, d):
    m = jnp.dot(a_ref[...].astype(BF16), w_ref[...], preferred_element_type=F32)
    _tail(m, x_ref, mod_ref[...], g2_ref, wr_ref, br_ref, x1_ref, h2_ref, idx_ref, gw_ref, d)


def _conv_out_kernel(b_ref, cu_ref, prev_ref, next_ref, ck_ref, w_ref, x_ref, mod_ref, g2_ref, wr_ref,
                     br_ref, x1_ref, h2_ref, idx_ref, gw_ref, *, d, tm, n_prompt, seq, dec_seq):
    i = pl.program_id(0)
    cu = cu_ref[...]
    row = lax.broadcasted_iota(jnp.int32, (tm, 1), 0)
    start = i * tm
    in_prompt = start < n_prompt
    seq_len = jnp.where(in_prompt, seq, dec_seq)
    pos = lax.rem(jnp.where(in_prompt, start, start - n_prompt) + row, seq_len)
    prev = jnp.where(row == 0, prev_ref[SUBLANES - 1:SUBLANES, :], pltpu.roll(cu, 1, 0))
    prev = jnp.where(pos == 0, 0.0, prev)
    nxt = jnp.where(row == tm - 1, next_ref[0:1, :], pltpu.roll(cu, tm - 1, 0))
    nxt = jnp.where(pos == seq_len - 1, 0.0, nxt)
    conv = prev * ck_ref[0:1, :] + cu * ck_ref[1:2, :] + nxt * ck_ref[2:3, :]
    m = jnp.dot((b_ref[...] * conv).astype(BF16), w_ref[...], preferred_element_type=F32)
    _tail(m, x_ref, mod_ref[...], g2_ref, wr_ref, br_ref, x1_ref, h2_ref, idx_ref, gw_ref, d)


def _hgrn_out_kernel(o_ref, gout_ref, ng_ref, w_ref, x_ref, mod_ref, g2_ref, wr_ref, br_ref,
                     x1_ref, h2_ref, idx_ref, gw_ref, *, d):
    parts = []
    for h in range(HG_HEADS):
        oh = o_ref[:, h * LANES:(h + 1) * LANES]
        parts.append(oh * lax.rsqrt(jnp.mean(oh * oh, axis=-1, keepdims=True) + EPS))
    gout = gout_ref[...]
    y = (jnp.concatenate(parts, axis=1) * ng_ref[...]) * (gout * jax.nn.sigmoid(gout))
    m = jnp.dot(y.astype(BF16), w_ref[...], preferred_element_type=F32)
    _tail(m, x_ref, mod_ref[...], g2_ref, wr_ref, br_ref, x1_ref, h2_ref, idx_ref, gw_ref, d)


def _mixer_out(kind, acts, w_out, x, mods_l, g2, w_router, b_router, geom, extra=()):
    t, d = x.shape
    tm = TOKEN_TILE
    n_prompt, dec_seq, seq = geom
    grp = functools.partial(_group_of_tile, tile=tm, n_prompt=n_prompt, dec_seq=dec_seq)
    row_spec = pl.BlockSpec((tm, d), lambda i: (i, 0))
    wr = jnp.zeros((d, LANES), F32).at[:, :N_EXPERTS].set(w_router)
    br = jnp.full((1, LANES), NEG_BIG, F32).at[0, :N_EXPERTS].set(b_router)
    common_specs = [pl.BlockSpec((d, d), lambda i: (0, 0)),
                    row_spec,
                    pl.BlockSpec((None, 1, 6 * d), lambda i: (grp(i), 0, 0)),
                    pl.BlockSpec((1, d), lambda i: (0, 0)),
                    pl.BlockSpec((d, LANES), lambda i: (0, 0)),
                    pl.BlockSpec((1, LANES), lambda i: (0, 0))]
    common_args = [w_out, x, mods_l, g2, wr, br]
    if kind == "plain":
        body = functools.partial(_plain_out_kernel, d=d)
        specs = [row_spec] + common_specs
        args = list(acts) + common_args
    elif kind == "conv":
        body = functools.partial(_conv_out_kernel, d=d, tm=tm, n_prompt=n_prompt, seq=seq, dec_seq=dec_seq)
        per = tm // SUBLANES
        last = t // SUBLANES - 1
        specs = [row_spec, row_spec,
                 pl.BlockSpec((SUBLANES, d), lambda i: (jnp.maximum(i * per - 1, 0), 0)),
                 pl.BlockSpec((SUBLANES, d), lambda i: (jnp.minimum((i + 1) * per, last), 0)),
                 pl.BlockSpec((CONV_WIDTH, d), lambda i: (0, 0))] + common_specs
        b, cu = acts
        args = [b, cu, cu, cu, extra[0]] + common_args
    else:
        body = functools.partial(_hgrn_out_kernel, d=d)
        specs = [row_spec, row_spec, pl.BlockSpec((1, d), lambda i: (0, 0))] + common_specs
        args = list(acts) + [extra[0]] + common_args
    lane_spec = pl.BlockSpec((tm, LANES), lambda i: (i, 0))
    return pl.pallas_call(
        body,
        grid=(t // tm,),
        in_specs=specs,
        out_specs=[row_spec, row_spec, lane_spec, lane_spec],
        out_shape=[jax.ShapeDtypeStruct((t, d), F32), jax.ShapeDtypeStruct((t, d), F32),
                   jax.ShapeDtypeStruct((t, LANES), jnp.int32), jax.ShapeDtypeStruct((t, LANES), F32)],
        compiler_params=_params(("arbitrary",), 40),
        name="mixer_out_" + kind,
    )(*args)


def _softmax_rows(parts):
    mx = functools.reduce(jnp.maximum, [jnp.max(s, axis=-1, keepdims=True) for s in parts])
    es = [jnp.exp(s - mx) for s in parts]
    den = functools.reduce(lambda a, b: a + b, [jnp.sum(e, axis=-1, keepdims=True) for e in es])
    return [e / den for e in es]


def _na_ctx_kernel(q_ref, k_ref, v_ref, o_ref, *, dh, scale):
    nt = (((1,), (1,)), ((), ()))
    for h in range(NA_HEADS):
        sl = slice(h * dh, (h + 1) * dh)
        s = lax.dot_general(q_ref[:, sl].astype(BF16), k_ref[:, sl].astype(BF16), nt,
                            preferred_element_type=F32) * scale
        (p,) = _softmax_rows([s])
        o_ref[:, sl] = jnp.dot(p.astype(BF16), v_ref[:, sl].astype(BF16), preferred_element_type=F32)


def _na_context(q, k, v, batch, seq):
    d = q.shape[1]
    dh = d // NA_HEADS
    spec = pl.BlockSpec((seq, d), lambda b: (b, 0))
    return pl.pallas_call(
        functools.partial(_na_ctx_kernel, dh=dh, scale=dh ** -0.5),
        grid=(batch,),
        in_specs=[spec, spec, spec],
        out_specs=spec,
        out_shape=jax.ShapeDtypeStruct((batch * seq, d), F32),
        compiler_params=_params(("arbitrary",), 32),
        name="na_context",
    )(q, k, v)


def _na_lat_kernel(*refs, dh, scale, kr):
    q_ref = refs[0]
    k_refs = refs[1:1 + kr]
    v_refs = refs[1 + kr:1 + 2 * kr]
    ck_ref, cv_ref, bias_ref, o_ref = refs[1 + 2 * kr:]
    nt = (((1,), (1,)), ((), ()))
    for h in range(NA_HEADS):
        sl = slice(h * dh, (h + 1) * dh)
        qh = q_ref[:, sl].astype(BF16)
        kw = jnp.concatenate([r[:, sl] for r in k_refs], axis=0).astype(BF16)
        vw = jnp.concatenate([r[:, sl] for r in v_refs], axis=0).astype(BF16)
        s_nb = lax.dot_general(qh, kw, nt, preferred_element_type=F32) * scale + bias_ref[h]
        s_cx = lax.dot_general(qh, ck_ref[:, sl].astype(BF16), nt, preferred_element_type=F32) * scale
        p_nb, p_cx = _softmax_rows([s_nb, s_cx])
        o_ref[:, sl] = (jnp.dot(p_nb.astype(BF16), vw, preferred_element_type=F32)
                        + jnp.dot(p_cx.astype(BF16), cv_ref[:, sl].astype(BF16), preferred_element_type=F32))


def _na_bias_table(rpb, kr):
    col = np.arange(GRID_W)
    col_start = np.clip(col - NA_COLS // 2, 0, GRID_W - NA_COLS)
    delta = col[None, :] - col[:, None] + (NA_COLS - 1)
    inside = (col[None, :] >= col_start[:, None]) & (col[None, :] < col_start[:, None] + NA_COLS)
    tab = rpb[:, :, np.clip(delta, 0, 2 * NA_COLS - 2)]
    tab = jnp.where(jnp.asarray(inside)[None, None], tab, NEG_BIG)
    out = []
    for d0 in range(NA_ROWS_MAX):
        rows = tab[:, d0:d0 + kr]
        out.append(rows.transpose(0, 2, 1, 3).reshape(NA_HEADS, GRID_W, kr * GRID_W))
    return jnp.stack(out, axis=0)


def _na_latent(q, k, v, ctx_k, ctx_v, rpb, n_prompt, dec_batch, dec_seq):
    d = q.shape[1]
    dh = d // NA_HEADS
    rows = dec_seq // GRID_W
    kr = min(NA_ROWS_MAX, rows)
    base = n_prompt // GRID_W
    past = ctx_k.shape[1]
    bias = _na_bias_table(rpb, kr)

    def row_start(r):
        return jnp.clip(r - kr // 2, 0, rows - kr)

    def win_spec(j):
        return pl.BlockSpec((GRID_W, d), lambda b, r: (base + b * rows + row_start(r) + j, 0))

    in_specs = ([pl.BlockSpec((GRID_W, d), lambda b, r: (base + b * rows + r, 0))]
                + [win_spec(j) for j in range(kr)] * 2
                + [pl.BlockSpec((None, past, d), lambda b, r: (b, 0, 0))] * 2
                + [pl.BlockSpec((None, NA_HEADS, GRID_W, kr * GRID_W),
                                lambda b, r: (row_start(r) - r + NA_ROWS_MAX - 1, 0, 0, 0))])
    return pl.pallas_call(
        functools.partial(_na_lat_kernel, dh=dh, scale=dh ** -0.5, kr=kr),
        grid=(dec_batch, rows),
        in_specs=in_specs,
        out_specs=pl.BlockSpec((GRID_W, d), lambda b, r: (b * rows + r, 0)),
        out_shape=jax.ShapeDtypeStruct((dec_batch * dec_seq, d), F32),
        compiler_params=_params(("arbitrary", "arbitrary"), 40),
        name="na_latent",
    )(q, *([k] * kr), *([v] * kr), ctx_k.reshape(dec_batch, past, d), ctx_v.reshape(dec_batch, past, d), bias)


def _hgrn_scan_kernel(*refs, seq_len, has_s0, want_state):
    q_ref, v_ref, z0_ref, z1_ref, lb_ref, tri_ref = refs[:6]
    rest = list(refs[6:])
    s0_ref = rest.pop(0) if has_s0 else None
    o_ref = rest.pop(0)
    sf_ref = rest.pop(0) if want_state else None
    c, r = HG_CHUNK, HG_TILE
    nc, nt = r // c, seq_len // r
    dk = HG_KEY_DIM
    tpos = lax.broadcasted_iota(jnp.int32, (1, c, 1), 1)
    nt_dims = (((1,), (1,)), ((), ()))
    tn_dims = (((0,), (0,)), ((), ()))

    for dr in range(2):
        z_ref = (z0_ref, z1_ref)[dr]
        lb = lb_ref[dr:dr + 1, :]
        tri = tri_ref[dr]

        def tile(i, st, dr=dr, z_ref=z_ref, lb=lb, tri=tri):
            ti = i if dr == 0 else nt - 1 - i
            r0 = pl.multiple_of(ti * r, r)
            q = q_ref[pl.ds(r0, r), :]
            v = v_ref[pl.ds(r0, r), :]
            z = z_ref[pl.ds(r0, r), :]
            g = jnp.log(lb + (1.0 - lb) * jax.nn.sigmoid(z))
            k = (1.0 - lb) * jax.nn.sigmoid(-z)
            b = jnp.dot(tri, g, precision=HIGHEST, preferred_element_type=F32)
            b3, q3, k3, v3 = (a.reshape(nc, c, dk) for a in (b, q, k, v))
            oi = jnp.zeros((nc, c, dk), F32)
            for s in range(c):
                keep = (tpos >= s) if dr == 0 else (tpos <= s)
                e = jnp.exp(jnp.where(keep, b3 - b3[:, s:s + 1, :], NEG_BIG))
                a = jnp.sum(q3 * e * k3[:, s:s + 1, :], axis=-1, keepdims=True)
                oi = oi + a * v3[:, s:s + 1, :]
            bl = b3[:, c - 1:c, :] if dr == 0 else b3[:, 0:1, :]
            qt = (q3 * jnp.exp(b3)).astype(BF16)
            kt = (k3 * jnp.exp(bl - b3)).astype(BF16)
            vb = v3.astype(BF16)
            outs = [None] * nc
            for ci in (range(nc) if dr == 0 else range(nc - 1, -1, -1)):
                oc = lax.dot_general(qt[ci], st.astype(BF16), nt_dims, preferred_element_type=F32)
                outs[ci] = oi[ci] + oc
                kv = lax.dot_general(vb[ci], kt[ci], tn_dims, preferred_element_type=F32)
                st = st * jnp.exp(bl[ci]) + kv
            o = jnp.concatenate(outs, axis=0)
            if dr == 0:
                o_ref[pl.ds(r0, r), :] = o
            else:
                o_ref[pl.ds(r0, r), :] += o
            return st

        st0 = s0_ref[dr] if has_s0 else jnp.zeros((dk, dk), F32)
        st = lax.fori_loop(0, nt, tile, st0)
        if want_state:
            sf_ref[dr] = st.T


def _hgrn_tri():
    t = np.arange(HG_TILE)
    same = (t[:, None] // HG_CHUNK) == (t[None, :] // HG_CHUNK)
    fwd = same & (t[None, :] <= t[:, None])
    bwd = same & (t[None, :] >= t[:, None])
    return jnp.asarray(np.stack([fwd, bwd]).astype(np.float32))


def _hgrn_scan(q, v, z0, z1, lb, row0, n_seq, seq_len, s0_t=None, want_state=False):
    d = q.shape[1]
    dk = HG_KEY_DIM
    blk0 = row0 // seq_len
    seq_spec = pl.BlockSpec((seq_len, dk), lambda s, h: (blk0 + s, h))
    st_spec = pl.BlockSpec((None, 2, None, dk, dk), lambda s, h: (s, 0, h, 0, 0))
    in_specs = [seq_spec] * 4 + [pl.BlockSpec((2, dk), lambda s, h: (0, h)),
                                 pl.BlockSpec((2, HG_TILE, HG_TILE), lambda s, h: (0, 0, 0))]
    args = [q, v, z0, z1, lb, _hgrn_tri()]
    if s0_t is not None:
        in_specs.append(st_spec)
        args.append(s0_t)
    out_specs = [pl.BlockSpec((seq_len, dk), lambda s, h: (s, h))]
    out_shape = [jax.ShapeDtypeStruct((n_seq * seq_len, d), F32)]
    if want_state:
        out_specs.append(st_spec)
        out_shape.append(jax.ShapeDtypeStruct((n_seq, 2, HG_HEADS, dk, dk), F32))
    res = pl.pallas_call(
        functools.partial(_hgrn_scan_kernel, seq_len=seq_len, has_s0=s0_t is not None, want_state=want_state),
        grid=(n_seq, HG_HEADS),
        in_specs=in_specs,
        out_specs=out_specs,
        out_shape=out_shape,
        compiler_params=_params(("arbitrary", "arbitrary"), 32),
        name="hgrn_scan",
    )(*args)
    return res if want_state else (res[0], None)


def _route(idx, gw, t):
    n = t * TOP_K
    rb = MOE_ROWS
    n_blocks = -(-n // rb) + N_EXPERTS
    n_rows = n_blocks * rb
    flat_e = idx[:, :TOP_K].reshape(n)
    gate = gw[:, :TOP_K].reshape(n)
    onehot = (flat_e[:, None] == jnp.arange(N_EXPERTS, dtype=jnp.int32)[None, :]).astype(jnp.int32)
    csum = jnp.cumsum(onehot, axis=0)
    rank = jnp.sum(csum * onehot, axis=1) - 1
    counts = csum[-1]
    padded = (counts + rb - 1) // rb * rb
    padded_end = jnp.cumsum(padded)
    dest = ((padded_end - padded)[flat_e] + rank).astype(jnp.int32)
    flat_tok = jnp.arange(n, dtype=jnp.int32) // TOP_K
    row_tok = jnp.zeros((n_rows,), jnp.int32).at[dest].set(flat_tok)
    row_gate = jnp.zeros((n_rows,), F32).at[dest].set(gate)
    block_expert = jnp.minimum(
        jnp.searchsorted(padded_end, jnp.arange(n_blocks, dtype=jnp.int32) * rb, side="right"),
        N_EXPERTS - 1).astype(jnp.int32)
    n_used = (padded_end[-1] // rb).astype(jnp.int32).reshape(1)
    return row_tok, row_gate, block_expert, n_used, dest


def _row_copy(src_hbm, dst_vmem, src_row, dst_row, sem):
    return pltpu.make_async_copy(src_hbm.at[pl.ds(src_row, 1), :], dst_vmem.at[pl.ds(dst_row, 1), :], sem)


def _gather_kernel(tok_ref, h_hbm, o_ref, sem, *, rb):
    def issue(r, carry):
        _row_copy(h_hbm, o_ref, tok_ref[0, 0, r], r, sem).start()
        return carry

    lax.fori_loop(0, rb, issue, 0)

    def drain(r, carry):
        _row_copy(h_hbm, o_ref, 0, r, sem).wait()
        return carry

    lax.fori_loop(0, rb, drain, 0)


def _gather_rows(h2, row_tok):
    t, d = h2.shape
    rb = MOE_ROWS
    n_blocks = row_tok.shape[0] // rb
    return pl.pallas_call(
        functools.partial(_gather_kernel, rb=rb),
        grid=(n_blocks,),
        in_specs=[pl.BlockSpec((1, 1, rb), lambda i: (i, 0, 0), memory_space=pltpu.SMEM),
                  pl.BlockSpec(memory_space=pl.ANY)],
        out_specs=pl.BlockSpec((rb, d), lambda i: (i, 0)),
        out_shape=jax.ShapeDtypeStruct((n_blocks * rb, d), F32),
        scratch_shapes=[pltpu.SemaphoreType.DMA],
        compiler_params=_params(("arbitrary",), 16),
        name="moe_gather",
    )(row_tok.reshape(n_blocks, 1, rb), h2)


def _moe_kernel(be_ref, nu_ref, x_ref, w1_ref, b1_ref, w2_ref, b2_ref, rg_ref, o_ref, w1b, w2b, *, dff):
    i = pl.program_id(0)
    e = be_ref[i]
    prev = be_ref[jnp.maximum(i - 1, 0)]

    @pl.when(jnp.logical_or(i == 0, e != prev))
    def _():
        w1b[...] = w1_ref[...].astype(BF16)
        w2b[...] = w2_ref[...].astype(BF16)

    @pl.when(i < nu_ref[0])
    def _():
        gu = jnp.dot(x_ref[...].astype(BF16), w1b[...], preferred_element_type=F32) + b1_ref[...]
        g = jnp.minimum(gu[:, :dff], SWIGLU_LIMIT)
        u = jnp.clip(gu[:, dff:], -SWIGLU_LIMIT, SWIGLU_LIMIT)
        a = (u + 1.0) * (g * jax.nn.sigmoid(SWIGLU_ALPHA * g))
        y = jnp.dot(a.astype(BF16), w2b[...], preferred_element_type=F32) + b2_ref[...]
        o_ref[...] = y * rg_ref[...]

    @pl.when(i >= nu_ref[0])
    def _():
        o_ref[...] = jnp.zeros(o_ref.shape, F32)


def _moe_experts(x_rows, row_gate, block_expert, n_used, w1, b1, w2, b2):
    n_rows, d = x_rows.shape
    rb = MOE_ROWS
    n_blocks = n_rows // rb
    n_e, _, dff2 = w1.shape
    dff = dff2 // 2
    grid_spec = pltpu.PrefetchScalarGridSpec(
        num_scalar_prefetch=2,
        grid=(n_blocks,),
        in_specs=[pl.BlockSpec((rb, d), lambda i, be, nu: (i, 0)),
                  pl.BlockSpec((None, d, dff2), lambda i, be, nu: (be[i], 0, 0)),
                  pl.BlockSpec((None, 1, dff2), lambda i, be, nu: (be[i], 0, 0)),
                  pl.BlockSpec((None, dff, d), lambda i, be, nu: (be[i], 0, 0)),
                  pl.BlockSpec((None, 1, d), lambda i, be, nu: (be[i], 0, 0)),
                  pl.BlockSpec((rb, 1), lambda i, be, nu: (i, 0))],
        out_specs=pl.BlockSpec((rb, d), lambda i, be, nu: (i, 0)),
        scratch_shapes=[pltpu.VMEM((d, dff2), BF16), pltpu.VMEM((dff, d), BF16)],
    )
    return pl.pallas_call(
        functools.partial(_moe_kernel, dff=dff),
        grid_spec=grid_spec,
        out_shape=jax.ShapeDtypeStruct((n_rows, d), F32),
        compiler_params=_params(("arbitrary",), 56),
        name="moe_experts",
    )(block_expert, n_used, x_rows, w1, b1.reshape(n_e, 1, dff2), w2, b2.reshape(n_e, 1, d),
      row_gate.reshape(n_rows, 1))


def _combine_kernel(dest_ref, y_hbm, x_ref, mod_ref, o_ref, buf, sem, *, tm, d):
    def issue(tk, carry):
        _row_copy(y_hbm, buf, dest_ref[0, 0, tk], tk, sem).start()
        return carry

    lax.fori_loop(0, tm * TOP_K, issue, 0)

    def drain(tk, carry):
        _row_copy(y_hbm, buf, 0, tk, sem).wait()
        return carry

    lax.fori_loop(0, tm * TOP_K, drain, 0)
    y = buf[0:tm, :] + buf[tm:2 * tm, :] + buf[2 * tm:3 * tm, :] + buf[3 * tm:4 * tm, :]
    o_ref[...] = x_ref[...] + mod_ref[:, 5 * d:6 * d] * y


def _moe_combine(y_rows, dest, x1, mods_l, geom):
    t, d = x1.shape
    tm = TOKEN_TILE
    grp = functools.partial(_group_of_tile, tile=tm, n_prompt=geom[0], dec_seq=geom[1])
    dest_km = dest.reshape(t // tm, tm, TOP_K).transpose(0, 2, 1).reshape(t // tm, 1, tm * TOP_K)
    return pl.pallas_call(
        functools.partial(_combine_kernel, tm=tm, d=d),
        grid=(t // tm,),
        in_specs=[pl.BlockSpec((1, 1, tm * TOP_K), lambda i: (i, 0, 0), memory_space=pltpu.SMEM),
                  pl.BlockSpec(memory_space=pl.ANY),
                  pl.BlockSpec((tm, d), lambda i: (i, 0)),
                  pl.BlockSpec((None, 1, 6 * d), lambda i: (grp(i), 0, 0))],
        out_specs=pl.BlockSpec((tm, d), lambda i: (i, 0)),
        out_shape=jax.ShapeDtypeStruct((t, d), F32),
        scratch_shapes=[pltpu.VMEM((tm * TOP_K, d), F32), pltpu.SemaphoreType.DMA],
        compiler_params=_params(("arbitrary",), 24),
        name="moe_combine",
    )(dest_km, y_rows, x1, mods_l)


def _final_norm_kernel(x_ref, g_ref, o_ref):
    o_ref[...] = _rms(x_ref[...]) * g_ref[...]


def _final_norm(x, g):
    t, d = x.shape
    tm = TOKEN_TILE
    return pl.pallas_call(
        _final_norm_kernel,
        grid=(t // tm,),
        in_specs=[pl.BlockSpec((tm, d), lambda i: (i, 0)), pl.BlockSpec((1, d), lambda i: (0, 0))],
        out_specs=pl.BlockSpec((tm, d), lambda i: (i, 0)),
        out_shape=jax.ShapeDtypeStruct((t, d), F32),
        compiler_params=_params(("arbitrary",), 16),
        name="final_norm",
    )(x, g)


def kernel(x_prompt, x_sample, c, c_ctx, cache_na_k, cache_na_v, state_hgrn, norm_mix, norm_ffn, w_mod, b_mod,
           conv_w_in, conv_k, conv_w_out, na_w_qkv, na_rpb, na_w_o, hg_w_qig, hg_w_f, hg_lb, hg_norm, hg_w_o,
           moe_w_router, moe_b_router, moe_w1, moe_b1, moe_w2, moe_b2, final_norm):
    batch, seq, d = x_prompt.shape
    dec_batch, dec_seq, _ = x_sample.shape
    depth = w_mod.shape[0]
    n_prompt = batch * seq
    t = n_prompt + dec_batch * dec_seq
    geom = (n_prompt, dec_seq, seq)
    assert n_prompt % TOKEN_TILE == 0 and dec_seq % TOKEN_TILE == 0 and seq % HG_TILE == 0
    assert n_prompt % dec_seq == 0 and dec_seq % GRID_W == 0 and 1 + dec_batch <= SUBLANES

    x = jnp.concatenate([x_prompt.reshape(n_prompt, d), x_sample.reshape(dec_batch * dec_seq, d)], axis=0)
    cvecs = jnp.zeros((SUBLANES, d), F32).at[0].set(c_ctx).at[1:1 + dec_batch].set(c)
    mods = _adaln_all(cvecs, w_mod, b_mod)[:, :1 + dec_batch].reshape(depth, 1 + dec_batch, 1, 6 * d)

    lb_soft = jax.nn.softmax(hg_lb.astype(F32), axis=0)
    lower_bounds = jnp.cumsum(lb_soft, axis=0) - lb_soft[0]

    new_k, new_v, new_s = [], [], []
    for l in range(depth):
        kind, j = l % N_MIXERS, l // N_MIXERS
        g1 = norm_mix[l].reshape(1, d)
        g2 = norm_ffn[l].reshape(1, d)
        tail_args = (x, mods[l], g2, moe_w_router[l], moe_b_router[l], geom)
        if kind == 0:
            b, cu = _mixer_in(x, mods[l], g1, conv_w_in[j].astype(BF16), geom, conv=True)
            x1, h2, idx, gw = _mixer_out("conv", (b, cu), conv_w_out[j].astype(BF16), *tail_args,
                                         extra=(conv_k[j],))
        elif kind == 1:
            q, k, v = _mixer_in(x, mods[l], g1, na_w_qkv[j].astype(BF16), geom)
            o_p = _na_context(q, k, v, batch, seq)
            o_s = _na_latent(q, k, v, cache_na_k[:, j], cache_na_v[:, j], na_rpb[j], n_prompt, dec_batch, dec_seq)
            new_k.append(k[:n_prompt].reshape(batch, seq, NA_HEADS, d // NA_HEADS))
            new_v.append(v[:n_prompt].reshape(batch, seq, NA_HEADS, d // NA_HEADS))
            x1, h2, idx, gw = _mixer_out("plain", (jnp.concatenate([o_p, o_s], axis=0),),
                                         na_w_o[j].astype(BF16), *tail_args)
        else:
            w_in = jnp.concatenate([hg_w_qig[j], hg_w_f[j, 0], hg_w_f[j, 1]], axis=1).astype(BF16)
            q, v, gout, z0, z1 = _mixer_in(x, mods[l], g1, w_in, geom)
            lb = lower_bounds[l]
            o_p, s_p = _hgrn_scan(q, v, z0, z1, lb, 0, batch, seq, want_state=True)
            s0_t = jnp.swapaxes(state_hgrn[:, j].astype(F32), -1, -2)
            o_s, _ = _hgrn_scan(q, v, z0, z1, lb, n_prompt, dec_batch, dec_seq, s0_t=s0_t)
            new_s.append(s_p)
            x1, h2, idx, gw = _mixer_out("hgrn", (jnp.concatenate([o_p, o_s], axis=0), gout),
                                         hg_w_o[j].astype(BF16), *tail_args, extra=(hg_norm[j].reshape(1, d),))
        row_tok, row_gate, block_expert, n_used, dest = _route(idx, gw, t)
        x_rows = _gather_rows(h2, row_tok)
        y_rows = _moe_experts(x_rows, row_gate, block_expert, n_used, moe_w1[l], moe_b1[l], moe_w2[l], moe_b2[l])
        x = _moe_combine(y_rows, dest, x1, mods[l], geom)

    y = _final_norm(x, final_norm.reshape(1, d))
    y_prompt = y[:n_prompt].reshape(batch, seq, d)
    y_sample = y[n_prompt:].reshape(dec_batch, dec_seq, d)
    new_na_k = jnp.stack(new_k, axis=1).astype(x_prompt.dtype)
    new_na_v = jnp.stack(new_v, axis=1).astype(x_prompt.dtype)
    new_hgrn_state = jnp.stack(new_s, axis=1).astype(x_prompt.dtype)
    return (y_prompt, y_sample, new_na_k, new_na_v, new_hgrn_state)
```

```python
import functools

import numpy as np
import jax
import jax.numpy as jnp
from jax import lax
from jax.experimental import pallas as pl
from jax.experimental.pallas import tpu as pltpu
from jax.experimental.pallas import tpu_sc as plsc

F32 = jnp.float32
BF16 = jnp.bfloat16
HIGHEST = lax.Precision.HIGHEST

N_MIXERS = 3
GRID_W = 64
CONV_WIDTH = 3
NA_HEADS = 16
NA_ROWS_MAX = 8
NA_COLS = 16
HG_HEADS = 8
HG_KEY_DIM = 128
HG_CHUNK = 32
N_EXPERTS = 32
TOP_K = 4
SWIGLU_LIMIT = 7.0
SWIGLU_ALPHA = 1.702
EPS = 1e-6

LANES = 128
SUBLANES = 8
VMEM_LIMIT_CAP = 60 * 1024 * 1024

NEG_BIG = -1e30
TOKEN_TILE = 256
MOE_ROWS = 256
HG_TILE = 128
ROW_CHUNKS = 4
CHUNK_W = 256
SC_WINDOW = 128


def _params(sem, vmem_mb):
    return pltpu.CompilerParams(dimension_semantics=sem,
                                vmem_limit_bytes=min(vmem_mb * 1024 * 1024, VMEM_LIMIT_CAP))


def _rms(x):
    return x * lax.rsqrt(jnp.mean(x * x, axis=-1, keepdims=True) + EPS)


def _modulate(x, g, shift, scale):
    return (_rms(x) * g) * (1.0 + scale) + shift


def _group_of_tile(i, tile, n_prompt, dec_seq):
    start = i * tile
    return jnp.where(start < n_prompt, 0, 1 + (start - n_prompt) // dec_seq)


def _adaln_kernel(c_ref, w_ref, b_ref, o_ref):
    cv = c_ref[...]
    s = cv * jax.nn.sigmoid(cv)
    o_ref[...] = jnp.dot(s, w_ref[...], precision=HIGHEST, preferred_element_type=F32) + b_ref[...]


def _adaln_all(cvecs, w_mod, b_mod):
    depth, d, n = w_mod.shape
    tn = 2048
    return pl.pallas_call(
        _adaln_kernel,
        grid=(depth, n // tn),
        in_specs=[pl.BlockSpec((SUBLANES, d), lambda l, j: (0, 0)),
                  pl.BlockSpec((None, d, tn), lambda l, j: (l, 0, j)),
                  pl.BlockSpec((None, 1, tn), lambda l, j: (l, 0, j))],
        out_specs=pl.BlockSpec((None, SUBLANES, tn), lambda l, j: (l, 0, j)),
        out_shape=jax.ShapeDtypeStruct((depth, SUBLANES, n), F32),
        compiler_params=_params(("arbitrary", "arbitrary"), 40),
        name="adaln",
    )(cvecs, w_mod, b_mod.reshape(depth, 1, n))


def _in_kernel(x_ref, mod_ref, g_ref, w_ref, *out_refs, d, conv):
    mod = mod_ref[...]
    h = _modulate(x_ref[...], g_ref[...], mod[:, 0:d], mod[:, d:2 * d]).astype(BF16)
    if conv:
        b_ref, cu_ref = out_refs
        b_ref[...] = jnp.dot(h, w_ref[:, 0:d], preferred_element_type=F32)
        c = jnp.dot(h, w_ref[:, d:2 * d], preferred_element_type=F32)
        u = jnp.dot(h, w_ref[:, 2 * d:3 * d], preferred_element_type=F32)
        cu_ref[...] = c * u
    else:
        for j, o_ref in enumerate(out_refs):
            o_ref[...] = jnp.dot(h, w_ref[:, j * d:(j + 1) * d], preferred_element_type=F32)


def _mixer_in(x, mods_l, g, w, geom, conv=False):
    t, d = x.shape
    n = w.shape[1]
    n_out = 2 if conv else n // d
    tm = TOKEN_TILE
    grp = functools.partial(_group_of_tile, tile=tm, n_prompt=geom[0], dec_seq=geom[1])
    return pl.pallas_call(
        functools.partial(_in_kernel, d=d, conv=conv),
        grid=(t // tm,),
        in_specs=[pl.BlockSpec((tm, d), lambda i: (i, 0)),
                  pl.BlockSpec((None, 1, 6 * d), lambda i: (grp(i), 0, 0)),
                  pl.BlockSpec((1, d), lambda i: (0, 0)),
                  pl.BlockSpec((d, n), lambda i: (0, 0))],
        out_specs=[pl.BlockSpec((tm, d), lambda i: (i, 0))] * n_out,
        out_shape=[jax.ShapeDtypeStruct((t, d), F32)] * n_out,
        compiler_params=_params(("arbitrary",), 48),
        name="mixer_in",
    )(x, mods_l, g, w)


def _tail(m, x_ref, mod, g2_ref, wr_ref, br_ref, outs, run_ref, d):
    x1_ref, h2_ref, idx_ref, gw_ref, rank_ref, cnt_ref = outs
    x1 = x_ref[...] + mod[:, 2 * d:3 * d] * m
    x1_ref[...] = x1
    h2 = _modulate(x1, g2_ref[...], mod[:, 3 * d:4 * d], mod[:, 4 * d:5 * d])
    for j in range(ROW_CHUNKS):
        h2_ref[j] = h2[:, j * CHUNK_W:(j + 1) * CHUNK_W]
    logits = jnp.dot(h2, wr_ref[...], precision=HIGHEST, preferred_element_type=F32) + br_ref[...]
    lane = lax.broadcasted_iota(jnp.int32, logits.shape, 1)
    vals, idxs = [], []
    for _ in range(TOP_K):
        mx = jnp.max(logits, axis=-1, keepdims=True)
        ix = jnp.min(jnp.where(logits == mx, lane, LANES), axis=-1, keepdims=True)
        vals.append(mx)
        idxs.append(ix)
        logits = jnp.where(lane == ix, NEG_BIG, logits)
    es = [jnp.exp(v - vals[0]) for v in vals]
    den = es[0] + es[1] + es[2] + es[3]

    @pl.when(pl.program_id(0) == 0)
    def _():
        run_ref[...] = jnp.zeros(run_ref.shape, F32)

    tm = lane.shape[0]
    earlier = (lax.broadcasted_iota(jnp.int32, (tm, tm), 0)
               > lax.broadcasted_iota(jnp.int32, (tm, tm), 1)).astype(BF16)
    seen = run_ref[...]
    idx_out = jnp.zeros(lane.shape, jnp.int32)
    gw_out = jnp.zeros(lane.shape, F32)
    rank_out = jnp.zeros(lane.shape, jnp.int32)
    for k in range(TOP_K):
        hit = lane == idxs[k]
        before = jnp.dot(earlier, hit.astype(BF16), preferred_element_type=F32) + seen
        rank_k = jnp.sum(jnp.where(hit, before, 0.0), axis=-1, keepdims=True)
        seen = seen + jnp.sum(hit.astype(F32), axis=0, keepdims=True)
        idx_out = jnp.where(lane == k, idxs[k], idx_out)
        gw_out = jnp.where(lane == k, es[k] / den, gw_out)
        rank_out = jnp.where(lane == k, rank_k.astype(jnp.int32), rank_out)
    run_ref[...] = seen
    cnt_ref[...] = seen
    idx_ref[...] = idx_out
    gw_ref[...] = gw_out
    rank_ref[...] = rank_out


def _plain_out_kernel(a_ref, w_ref, x_ref, mod_ref, g2_ref, wr_ref, br_ref, *rest, d):
    m = jnp.dot(a_ref[...].astype(BF16), w_ref[...], preferred_element_type=F32)
    _tail(m, x_ref, mod_ref[...], g2_ref, wr_ref, br_ref, rest[:-1], rest[-1], d)


def _conv_out_kernel(b_ref, cu_ref, prev_ref, next_ref, ck_ref, w_ref, x_ref, mod_ref, g2_ref, wr_ref,
                     br_ref, *rest, d, tm, n_prompt, seq, dec_seq):
    i = pl.program_id(0)
    cu = cu_ref[...]
    row = lax.broadcasted_iota(jnp.int32, (tm, 1), 0)
    start = i * tm
    in_prompt = start < n_prompt
    seq_len = jnp.where(in_prompt, seq, dec_seq)
    pos = lax.rem(jnp.where(in_prompt, start, start - n_prompt) + row, seq_len)
    prev = jnp.where(row == 0, prev_ref[SUBLANES - 1:SUBLANES, :], pltpu.roll(cu, 1, 0))
    prev = jnp.where(pos == 0, 0.0, prev)
    nxt = jnp.where(row == tm - 1, next_ref[0:1, :], pltpu.roll(cu, tm - 1, 0))
    nxt = jnp.where(pos == seq_len - 1, 0.0, nxt)
    conv = prev * ck_ref[0:1, :] + cu * ck_ref[1:2, :] + nxt * ck_ref[2:3, :]
    m = jnp.dot((b_ref[...] * conv).astype(BF16), w_ref[...], preferred_element_type=F32)
    _tail(m, x_ref, mod_ref[...], g2_ref, wr_ref, br_ref, rest[:-1], rest[-1], d)


def _hgrn_out_kernel(o_ref, gout_ref, ng_ref, w_ref, x_ref, mod_ref, g2_ref, wr_ref, br_ref, *rest, d):
    parts = []
    for h in range(HG_HEADS):
        oh = o_ref[:, h * LANES:(h + 1) * LANES]
        parts.append(oh * lax.rsqrt(jnp.mean(oh * oh, axis=-1, keepdims=True) + EPS))
    gout = gout_ref[...]
    y = (jnp.concatenate(parts, axis=1) * ng_ref[...]) * (gout * jax.nn.sigmoid(gout))
    m = jnp.dot(y.astype(BF16), w_ref[...], preferred_element_type=F32)
    _tail(m, x_ref, mod_ref[...], g2_ref, wr_ref, br_ref, rest[:-1], rest[-1], d)


def _mixer_out(kind, acts, w_out, x, mods_l, g2, w_router, b_router, geom, extra=()):
    t, d = x.shape
    tm = TOKEN_TILE
    n_prompt, dec_seq, seq = geom
    grp = functools.partial(_group_of_tile, tile=tm, n_prompt=n_prompt, dec_seq=dec_seq)
    row_spec = pl.BlockSpec((tm, d), lambda i: (i, 0))
    wr = jnp.zeros((d, LANES), F32).at[:, :N_EXPERTS].set(w_router)
    br = jnp.full((1, LANES), NEG_BIG, F32).at[0, :N_EXPERTS].set(b_router)
    common_specs = [pl.BlockSpec((d, d), lambda i: (0, 0)),
                    row_spec,
                    pl.BlockSpec((None, 1, 6 * d), lambda i: (grp(i), 0, 0)),
                    pl.BlockSpec((1, d), lambda i: (0, 0)),
                    pl.BlockSpec((d, LANES), lambda i: (0, 0)),
                    pl.BlockSpec((1, LANES), lambda i: (0, 0))]
    common_args = [w_out, x, mods_l, g2, wr, br]
    if kind == "plain":
        body = functools.partial(_plain_out_kernel, d=d)
        specs = [row_spec] + common_specs
        args = list(acts) + common_args
    elif kind == "conv":
        body = functools.partial(_conv_out_kernel, d=d, tm=tm, n_prompt=n_prompt, seq=seq, dec_seq=dec_seq)
        per = tm // SUBLANES
        last = t // SUBLANES - 1
        specs = [row_spec, row_spec,
                 pl.BlockSpec((SUBLANES, d), lambda i: (jnp.maximum(i * per - 1, 0), 0)),
                 pl.BlockSpec((SUBLANES, d), lambda i: (jnp.minimum((i + 1) * per, last), 0)),
                 pl.BlockSpec((CONV_WIDTH, d), lambda i: (0, 0))] + common_specs
        b, cu = acts
        args = [b, cu, cu, cu, extra[0]] + common_args
    else:
        body = functools.partial(_hgrn_out_kernel, d=d)
        specs = [row_spec, row_spec, pl.BlockSpec((1, d), lambda i: (0, 0))] + common_specs
        args = list(acts) + [extra[0]] + common_args
    lane_spec = pl.BlockSpec((tm, LANES), lambda i: (i, 0))
    return pl.pallas_call(
        body,
        grid=(t // tm,),
        in_specs=specs,
        out_specs=[row_spec, pl.BlockSpec((ROW_CHUNKS, tm, CHUNK_W), lambda i: (0, i, 0)),
                   lane_spec, lane_spec, lane_spec, pl.BlockSpec((1, LANES), lambda i: (0, 0))],
        out_shape=[jax.ShapeDtypeStruct((t, d), F32), jax.ShapeDtypeStruct((ROW_CHUNKS, t, CHUNK_W), F32),
                   jax.ShapeDtypeStruct((t, LANES), jnp.int32), jax.ShapeDtypeStruct((t, LANES), F32),
                   jax.ShapeDtypeStruct((t, LANES), jnp.int32), jax.ShapeDtypeStruct((1, LANES), F32)],
        scratch_shapes=[pltpu.VMEM((1, LANES), F32)],
        compiler_params=_params(("arbitrary",), 40),
        name="mixer_out_" + kind,
    )(*args)


def _softmax_rows(parts):
    mx = functools.reduce(jnp.maximum, [jnp.max(s, axis=-1, keepdims=True) for s in parts])
    es = [jnp.exp(s - mx) for s in parts]
    den = functools.reduce(lambda a, b: a + b, [jnp.sum(e, axis=-1, keepdims=True) for e in es])
    return [e / den for e in es]


def _na_ctx_kernel(q_ref, k_ref, v_ref, o_ref, *, dh, scale):
    nt = (((1,), (1,)), ((), ()))
    for h in range(NA_HEADS):
        sl = slice(h * dh, (h + 1) * dh)
        s = lax.dot_general(q_ref[:, sl].astype(BF16), k_ref[:, sl].astype(BF16), nt,
                            preferred_element_type=F32) * scale
        (p,) = _softmax_rows([s])
        o_ref[:, sl] = jnp.dot(p.astype(BF16), v_ref[:, sl].astype(BF16), preferred_element_type=F32)


def _na_context(q, k, v, batch, seq):
    d = q.shape[1]
    dh = d // NA_HEADS
    spec = pl.BlockSpec((seq, d), lambda b: (b, 0))
    return pl.pallas_call(
        functools.partial(_na_ctx_kernel, dh=dh, scale=dh ** -0.5),
        grid=(batch,),
        in_specs=[spec, spec, spec],
        out_specs=spec,
        out_shape=jax.ShapeDtypeStruct((batch * seq, d), F32),
        compiler_params=_params(("arbitrary",), 32),
        name="na_context",
    )(q, k, v)


def _na_lat_kernel(*refs, dh, scale, kr):
    q_ref = refs[0]
    k_refs = refs[1:1 + kr]
    v_refs = refs[1 + kr:1 + 2 * kr]
    ck_ref, cv_ref, bias_ref, o_ref = refs[1 + 2 * kr:]
    nt = (((1,), (1,)), ((), ()))
    for h in range(NA_HEADS):
        sl = slice(h * dh, (h + 1) * dh)
        qh = q_ref[:, sl].astype(BF16)
        kw = jnp.concatenate([r[:, sl] for r in k_refs], axis=0).astype(BF16)
        vw = jnp.concatenate([r[:, sl] for r in v_refs], axis=0).astype(BF16)
        s_nb = lax.dot_general(qh, kw, nt, preferred_element_type=F32) * scale + bias_ref[h]
        s_cx = lax.dot_general(qh, ck_ref[:, sl].astype(BF16), nt, preferred_element_type=F32) * scale
        p_nb, p_cx = _softmax_rows([s_nb, s_cx])
        o_ref[:, sl] = (jnp.dot(p_nb.astype(BF16), vw, preferred_element_type=F32)
                        + jnp.dot(p_cx.astype(BF16), cv_ref[:, sl].astype(BF16), preferred_element_type=F32))


def _na_bias_table(rpb, kr):
    col = np.arange(GRID_W)
    col_start = np.clip(col - NA_COLS // 2, 0, GRID_W - NA_COLS)
    delta = col[None, :] - col[:, None] + (NA_COLS - 1)
    inside = (col[None, :] >= col_start[:, None]) & (col[None, :] < col_start[:, None] + NA_COLS)
    tab = rpb[:, :, np.clip(delta, 0, 2 * NA_COLS - 2)]
    tab = jnp.where(jnp.asarray(inside)[None, None], tab, NEG_BIG)
    out = []
    for d0 in range(NA_ROWS_MAX):
        rows = tab[:, d0:d0 + kr]
        out.append(rows.transpose(0, 2, 1, 3).reshape(NA_HEADS, GRID_W, kr * GRID_W))
    return jnp.stack(out, axis=0)


def _na_latent(q, k, v, ctx_k, ctx_v, rpb, n_prompt, dec_batch, dec_seq):
    d = q.shape[1]
    dh = d // NA_HEADS
    rows = dec_seq // GRID_W
    kr = min(NA_ROWS_MAX, rows)
    base = n_prompt // GRID_W
    past = ctx_k.shape[1]
    bias = _na_bias_table(rpb, kr)

    def row_start(r):
        return jnp.clip(r - kr // 2, 0, rows - kr)

    def win_spec(j):
        return pl.BlockSpec((GRID_W, d), lambda b, r: (base + b * rows + row_start(r) + j, 0))

    in_specs = ([pl.BlockSpec((GRID_W, d), lambda b, r: (base + b * rows + r, 0))]
                + [win_spec(j) for j in range(kr)] * 2
                + [pl.BlockSpec((None, past, d), lambda b, r: (b, 0, 0))] * 2
                + [pl.BlockSpec((None, NA_HEADS, GRID_W, kr * GRID_W),
                                lambda b, r: (row_start(r) - r + NA_ROWS_MAX - 1, 0, 0, 0))])
    return pl.pallas_call(
        functools.partial(_na_lat_kernel, dh=dh, scale=dh ** -0.5, kr=kr),
        grid=(dec_batch, rows),
        in_specs=in_specs,
        out_specs=pl.BlockSpec((GRID_W, d), lambda b, r: (b * rows + r, 0)),
        out_shape=jax.ShapeDtypeStruct((dec_batch * dec_seq, d), F32),
        compiler_params=_params(("arbitrary", "arbitrary"), 40),
        name="na_latent",
    )(q, *([k] * kr), *([v] * kr), ctx_k.reshape(dec_batch, past, d), ctx_v.reshape(dec_batch, past, d), bias)


def _hgrn_scan_kernel(*refs, seq_len, has_s0, want_state):
    q_ref, v_ref, z0_ref, z1_ref, lb_ref, tri_ref = refs[:6]
    rest = list(refs[6:])
    s0_ref = rest.pop(0) if has_s0 else None
    o_ref = rest.pop(0)
    sf_ref = rest.pop(0) if want_state else None
    c, r = HG_CHUNK, HG_TILE
    nc, nt = r // c, seq_len // r
    dk = HG_KEY_DIM
    tpos = lax.broadcasted_iota(jnp.int32, (1, c, 1), 1)
    nt_dims = (((1,), (1,)), ((), ()))
    tn_dims = (((0,), (0,)), ((), ()))

    for dr in range(2):
        z_ref = (z0_ref, z1_ref)[dr]
        lb = lb_ref[dr:dr + 1, :]
        tri = tri_ref[dr]

        def tile(i, st, dr=dr, z_ref=z_ref, lb=lb, tri=tri):
            ti = i if dr == 0 else nt - 1 - i
            r0 = pl.multiple_of(ti * r, r)
            q = q_ref[pl.ds(r0, r), :]
            v = v_ref[pl.ds(r0, r), :]
            z = z_ref[pl.ds(r0, r), :]
            g = jnp.log(lb + (1.0 - lb) * jax.nn.sigmoid(z))
            k = (1.0 - lb) * jax.nn.sigmoid(-z)
            b = jnp.dot(tri, g, precision=HIGHEST, preferred_element_type=F32)
            b3, q3, k3, v3 = (a.reshape(nc, c, dk) for a in (b, q, k, v))
            oi = jnp.zeros((nc, c, dk), F32)
            for s in range(c):
                keep = (tpos >= s) if dr == 0 else (tpos <= s)
                e = jnp.exp(jnp.where(keep, b3 - b3[:, s:s + 1, :], NEG_BIG))
                a = jnp.sum(q3 * e * k3[:, s:s + 1, :], axis=-1, keepdims=True)
                oi = oi + a * v3[:, s:s + 1, :]
            bl = b3[:, c - 1:c, :] if dr == 0 else b3[:, 0:1, :]
            qt = (q3 * jnp.exp(b3)).astype(BF16)
            kt = (k3 * jnp.exp(bl - b3)).astype(BF16)
            vb = v3.astype(BF16)
            outs = [None] * nc
            for ci in (range(nc) if dr == 0 else range(nc - 1, -1, -1)):
                oc = lax.dot_general(qt[ci], st.astype(BF16), nt_dims, preferred_element_type=F32)
                outs[ci] = oi[ci] + oc
                kv = lax.dot_general(vb[ci], kt[ci], tn_dims, preferred_element_type=F32)
                st = st * jnp.exp(bl[ci]) + kv
            o = jnp.concatenate(outs, axis=0)
            if dr == 0:
                o_ref[pl.ds(r0, r), :] = o
            else:
                o_ref[pl.ds(r0, r), :] += o
            return st

        st0 = s0_ref[dr] if has_s0 else jnp.zeros((dk, dk), F32)
        st = lax.fori_loop(0, nt, tile, st0)
        if want_state:
            sf_ref[dr] = st.T


def _hgrn_tri():
    t = np.arange(HG_TILE)
    same = (t[:, None] // HG_CHUNK) == (t[None, :] // HG_CHUNK)
    fwd = same & (t[None, :] <= t[:, None])
    bwd = same & (t[None, :] >= t[:, None])
    return jnp.asarray(np.stack([fwd, bwd]).astype(np.float32))


def _hgrn_scan(q, v, z0, z1, lb, row0, n_seq, seq_len, s0_t=None, want_state=False):
    d = q.shape[1]
    dk = HG_KEY_DIM
    blk0 = row0 // seq_len
    seq_spec = pl.BlockSpec((seq_len, dk), lambda s, h: (blk0 + s, h))
    st_spec = pl.BlockSpec((None, 2, None, dk, dk), lambda s, h: (s, 0, h, 0, 0))
    in_specs = [seq_spec] * 4 + [pl.BlockSpec((2, dk), lambda s, h: (0, h)),
                                 pl.BlockSpec((2, HG_TILE, HG_TILE), lambda s, h: (0, 0, 0))]
    args = [q, v, z0, z1, lb, _hgrn_tri()]
    if s0_t is not None:
        in_specs.append(st_spec)
        args.append(s0_t)
    out_specs = [pl.BlockSpec((seq_len, dk), lambda s, h: (s, h))]
    out_shape = [jax.ShapeDtypeStruct((n_seq * seq_len, d), F32)]
    if want_state:
        out_specs.append(st_spec)
        out_shape.append(jax.ShapeDtypeStruct((n_seq, 2, HG_HEADS, dk, dk), F32))
    res = pl.pallas_call(
        functools.partial(_hgrn_scan_kernel, seq_len=seq_len, has_s0=s0_t is not None, want_state=want_state),
        grid=(n_seq, HG_HEADS),
        in_specs=in_specs,
        out_specs=out_specs,
        out_shape=out_shape,
        compiler_params=_params(("arbitrary", "arbitrary"), 32),
        name="hgrn_scan",
    )(*args)
    return res if want_state else (res[0], None)


def _route(idx, rank, counts, t):
    n = t * TOP_K
    rb = MOE_ROWS
    n_blocks = -(-n // rb) + N_EXPERTS
    n_rows = n_blocks * rb
    experts = jnp.arange(N_EXPERTS, dtype=jnp.int32)
    counts = counts[0, :N_EXPERTS].astype(jnp.int32)
    padded = (counts + rb - 1) // rb * rb
    padded_end = jnp.cumsum(padded)
    start = padded_end - padded
    e = idx[:, :TOP_K]
    dest = jnp.sum(jnp.where(e[:, :, None] == experts, start, 0), axis=-1) + rank[:, :TOP_K]
    dest_km = dest.T.astype(jnp.int32)
    q = jnp.arange(rb, dtype=jnp.int32)
    pad_rows = jnp.where(q[None, :] < (padded - counts)[:, None], (start + counts)[:, None] + q[None, :], n_rows - 1)
    per_chunk = jnp.concatenate([dest_km.reshape(-1), pad_rows.reshape(-1)])
    sc_index = (jnp.arange(ROW_CHUNKS, dtype=jnp.int32)[:, None] * n_rows + per_chunk[None, :]).reshape(1, -1)
    block_start = jnp.arange(n_blocks, dtype=jnp.int32) * rb
    block_expert = jnp.minimum(jnp.sum((padded_end[None, :] <= block_start[:, None]).astype(jnp.int32), axis=1),
                               N_EXPERTS - 1)
    n_used = (padded_end[-1] // rb).reshape(1)
    return sc_index, block_expert, n_used, n_rows


def _sc_mesh():
    return plsc.VectorSubcoreMesh(core_axis_name="c", subcore_axis_name="s")


def _sc_dispatch(h2c, sc_index, t, n_rows):
    win = SC_WINDOW
    tw = t // win
    nw = (t * TOP_K + N_EXPERTS * MOE_ROWS) // win

    def scatter(x_hbm, i_hbm, o_hbm):
        def body(x_vmem, i_vmem):
            pltpu.sync_copy(x_vmem, o_hbm.at[i_vmem.at[0]])

        pltpu.emit_pipeline(
            body,
            grid=(ROW_CHUNKS * nw,),
            in_specs=[pl.BlockSpec((win, CHUNK_W), index_map=lambda i: ((i // nw) * tw + (i % nw) % tw, 0)),
                      pl.BlockSpec((1, win), index_map=lambda i: (0, i))],
            out_specs=[],
            core_axis_name=("c", "s"),
            dimension_semantics=(pltpu.PARALLEL,),
        )(x_hbm, i_hbm)

    out = pl.kernel(scatter, out_type=jax.ShapeDtypeStruct((ROW_CHUNKS * n_rows, CHUNK_W), F32),
                    mesh=_sc_mesh(), name="moe_dispatch")(h2c.reshape(ROW_CHUNKS * t, CHUNK_W), sc_index)
    return out.reshape(ROW_CHUNKS, n_rows, CHUNK_W)


def _sc_collect(y_rows, sc_index, t, n_rows):
    win = SC_WINDOW
    nw = (t * TOP_K + N_EXPERTS * MOE_ROWS) // win
    aw = t * TOP_K // win

    def gather(y_hbm, i_hbm, o_hbm):
        def body(i_vmem, o_vmem):
            pltpu.sync_copy(y_hbm.at[i_vmem.at[0]], o_vmem)

        pltpu.emit_pipeline(
            body,
            grid=(ROW_CHUNKS * aw,),
            in_specs=[pl.BlockSpec((1, win), index_map=lambda i: (0, (i // aw) * nw + i % aw))],
            out_specs=[pl.BlockSpec((win, CHUNK_W), index_map=lambda i: (i, 0))],
            core_axis_name=("c", "s"),
            dimension_semantics=(pltpu.PARALLEL,),
        )(i_hbm, o_hbm)

    out = pl.kernel(gather, out_type=jax.ShapeDtypeStruct((ROW_CHUNKS * TOP_K * t, CHUNK_W), F32),
                    mesh=_sc_mesh(), name="moe_collect")(y_rows.reshape(ROW_CHUNKS * n_rows, CHUNK_W), sc_index)
    return out.reshape(ROW_CHUNKS, TOP_K, t, CHUNK_W)


def _moe_kernel(be_ref, nu_ref, x_ref, w1_ref, b1_ref, w2_ref, b2_ref, o_ref, w1b, w2b, *, dff):
    i = pl.program_id(0)
    e = be_ref[i]
    prev = be_ref[jnp.maximum(i - 1, 0)]

    @pl.when(jnp.logical_or(i == 0, e != prev))
    def _():
        w1b[...] = w1_ref[...].astype(BF16)
        w2b[...] = w2_ref[...].astype(BF16)

    @pl.when(i < nu_ref[0])
    def _():
        gu = b1_ref[...]
        for j in range(ROW_CHUNKS):
            gu = gu + jnp.dot(x_ref[j].astype(BF16), w1b[j * CHUNK_W:(j + 1) * CHUNK_W, :],
                              preferred_element_type=F32)
        g = jnp.minimum(gu[:, :dff], SWIGLU_LIMIT)
        u = jnp.clip(gu[:, dff:], -SWIGLU_LIMIT, SWIGLU_LIMIT)
        a = (u + 1.0) * (g * jax.nn.sigmoid(SWIGLU_ALPHA * g))
        y = jnp.dot(a.astype(BF16), w2b[...], preferred_element_type=F32) + b2_ref[...]
        for j in range(ROW_CHUNKS):
            o_ref[j] = y[:, j * CHUNK_W:(j + 1) * CHUNK_W]

    @pl.when(i >= nu_ref[0])
    def _():
        o_ref[...] = jnp.zeros(o_ref.shape, F32)


def _moe_experts(x_rows, block_expert, n_used, layer, w1, b1, w2, b2):
    _, n_rows, _ = x_rows.shape
    rb = MOE_ROWS
    n_blocks = n_rows // rb
    depth, n_e, d, dff2 = w1.shape
    dff = dff2 // 2
    row_spec = pl.BlockSpec((ROW_CHUNKS, rb, CHUNK_W), lambda i, be, nu: (0, jnp.minimum(i, nu[0] - 1), 0))
    grid_spec = pltpu.PrefetchScalarGridSpec(
        num_scalar_prefetch=2,
        grid=(n_blocks,),
        in_specs=[row_spec,
                  pl.BlockSpec((None, None, d, dff2), lambda i, be, nu: (layer, be[i], 0, 0)),
                  pl.BlockSpec((None, None, 1, dff2), lambda i, be, nu: (layer, be[i], 0, 0)),
                  pl.BlockSpec((None, None, dff, d), lambda i, be, nu: (layer, be[i], 0, 0)),
                  pl.BlockSpec((None, None, 1, d), lambda i, be, nu: (layer, be[i], 0, 0))],
        out_specs=pl.BlockSpec((ROW_CHUNKS, rb, CHUNK_W), lambda i, be, nu: (0, i, 0)),
        scratch_shapes=[pltpu.VMEM((d, dff2), BF16), pltpu.VMEM((dff, d), BF16)],
    )
    return pl.pallas_call(
        functools.partial(_moe_kernel, dff=dff),
        grid_spec=grid_spec,
        out_shape=jax.ShapeDtypeStruct((ROW_CHUNKS, n_rows, CHUNK_W), F32),
        compiler_params=_params(("arbitrary",), 56),
        name="moe_experts",
    )(block_expert, n_used, x_rows, w1, b1.reshape(depth, n_e, 1, dff2), w2, b2.reshape(depth, n_e, 1, d))


def _combine_kernel(y_ref, gw_ref, x_ref, mod_ref, o_ref, *, d):
    gw = gw_ref[...]
    for j in range(ROW_CHUNKS):
        cols = slice(j * CHUNK_W, (j + 1) * CHUNK_W)
        acc = gw[:, 0:1] * y_ref[j, 0]
        for k in range(1, TOP_K):
            acc = acc + gw[:, k:k + 1] * y_ref[j, k]
        o_ref[:, cols] = x_ref[:, cols] + mod_ref[:, 5 * d + j * CHUNK_W:5 * d + (j + 1) * CHUNK_W] * acc


def _moe_combine(y_slots, gw, x1, mods_l, geom):
    t, d = x1.shape
    tm = TOKEN_TILE
    grp = functools.partial(_group_of_tile, tile=tm, n_prompt=geom[0], dec_seq=geom[1])
    return pl.pallas_call(
        functools.partial(_combine_kernel, d=d),
        grid=(t // tm,),
        in_specs=[pl.BlockSpec((ROW_CHUNKS, TOP_K, tm, CHUNK_W), lambda i: (0, 0, i, 0)),
                  pl.BlockSpec((tm, LANES), lambda i: (i, 0)),
                  pl.BlockSpec((tm, d), lambda i: (i, 0)),
                  pl.BlockSpec((None, 1, 6 * d), lambda i: (grp(i), 0, 0))],
        out_specs=pl.BlockSpec((tm, d), lambda i: (i, 0)),
        out_shape=jax.ShapeDtypeStruct((t, d), F32),
        compiler_params=_params(("arbitrary",), 24),
        name="moe_combine",
    )(y_slots, gw, x1, mods_l)


def _final_norm_kernel(x_ref, g_ref, o_ref):
    o_ref[...] = _rms(x_ref[...]) * g_ref[...]


def _final_norm(x, g):
    t, d = x.shape
    tm = TOKEN_TILE
    return pl.pallas_call(
        _final_norm_kernel,
        grid=(t // tm,),
        in_specs=[pl.BlockSpec((tm, d), lambda i: (i, 0)), pl.BlockSpec((1, d), lambda i: (0, 0))],
        out_specs=pl.BlockSpec((tm, d), lambda i: (i, 0)),
        out_shape=jax.ShapeDtypeStruct((t, d), F32),
        compiler_params=_params(("arbitrary",), 16),
        name="final_norm",
    )(x, g)


def kernel(x_prompt, x_sample, c, c_ctx, cache_na_k, cache_na_v, state_hgrn, norm_mix, norm_ffn, w_mod, b_mod,
           conv_w_in, conv_k, conv_w_out, na_w_qkv, na_rpb, na_w_o, hg_w_qig, hg_w_f, hg_lb, hg_norm, hg_w_o,
           moe_w_router, moe_b_router, moe_w1, moe_b1, moe_w2, moe_b2, final_norm):
    batch, seq, d = x_prompt.shape
    dec_batch, dec_seq, _ = x_sample.shape
    depth = w_mod.shape[0]
    n_prompt = batch * seq
    t = n_prompt + dec_batch * dec_seq
    geom = (n_prompt, dec_seq, seq)
    assert n_prompt % TOKEN_TILE == 0 and dec_seq % TOKEN_TILE == 0 and seq % HG_TILE == 0
    assert n_prompt % dec_seq == 0 and dec_seq % GRID_W == 0 and 1 + dec_batch <= SUBLANES
    assert d == ROW_CHUNKS * CHUNK_W and t % SC_WINDOW == 0 and MOE_ROWS % SC_WINDOW == 0

    x = jnp.concatenate([x_prompt.reshape(n_prompt, d), x_sample.reshape(dec_batch * dec_seq, d)], axis=0)
    cvecs = jnp.zeros((SUBLANES, d), F32).at[0].set(c_ctx).at[1:1 + dec_batch].set(c)
    mods = _adaln_all(cvecs, w_mod, b_mod)[:, :1 + dec_batch].reshape(depth, 1 + dec_batch, 1, 6 * d)

    lb_soft = jax.nn.softmax(hg_lb.astype(F32), axis=0)
    lower_bounds = jnp.cumsum(lb_soft, axis=0) - lb_soft[0]

    new_k, new_v, new_s = [], [], []
    for l in range(depth):
        kind, j = l % N_MIXERS, l // N_MIXERS
        g1 = norm_mix[l].reshape(1, d)
        g2 = norm_ffn[l].reshape(1, d)
        tail_args = (x, mods[l], g2, moe_w_router[l], moe_b_router[l], geom)
        if kind == 0:
            b, cu = _mixer_in(x, mods[l], g1, conv_w_in[j].astype(BF16), geom, conv=True)
            x1, h2c, idx, gw, rank, counts = _mixer_out("conv", (b, cu), conv_w_out[j].astype(BF16), *tail_args,
                                         extra=(conv_k[j],))
        elif kind == 1:
            q, k, v = _mixer_in(x, mods[l], g1, na_w_qkv[j].astype(BF16), geom)
            o_p = _na_context(q, k, v, batch, seq)
            o_s = _na_latent(q, k, v, cache_na_k[:, j], cache_na_v[:, j], na_rpb[j], n_prompt, dec_batch, dec_seq)
            new_k.append(k[:n_prompt].reshape(batch, seq, NA_HEADS, d // NA_HEADS))
            new_v.append(v[:n_prompt].reshape(batch, seq, NA_HEADS, d // NA_HEADS))
            x1, h2c, idx, gw, rank, counts = _mixer_out("plain", (jnp.concatenate([o_p, o_s], axis=0),),
                                         na_w_o[j].astype(BF16), *tail_args)
        else:
            w_in = jnp.concatenate([hg_w_qig[j], hg_w_f[j, 0], hg_w_f[j, 1]], axis=1).astype(BF16)
            q, v, gout, z0, z1 = _mixer_in(x, mods[l], g1, w_in, geom)
            lb = lower_bounds[l]
            o_p, s_p = _hgrn_scan(q, v, z0, z1, lb, 0, batch, seq, want_state=True)
            s0_t = jnp.swapaxes(state_hgrn[:, j].astype(F32), -1, -2)
            o_s, _ = _hgrn_scan(q, v, z0, z1, lb, n_prompt, dec_batch, dec_seq, s0_t=s0_t)
            new_s.append(s_p)
            x1, h2c, idx, gw, rank, counts = _mixer_out("hgrn", (jnp.concatenate([o_p, o_s], axis=0), gout),
                                         hg_w_o[j].astype(BF16), *tail_args, extra=(hg_norm[j].reshape(1, d),))
        sc_index, block_expert, n_used, n_rows = _route(idx, rank, counts, t)
        x_rows = _sc_dispatch(h2c, sc_index, t, n_rows)
        y_rows = _moe_experts(x_rows, block_expert, n_used, l, moe_w1, moe_b1, moe_w2, moe_b2)
        y_slots = _sc_collect(y_rows, sc_index, t, n_rows)
        x = _moe_combine(y_slots, gw, x1, mods[l], geom)

    y = _final_norm(x, final_norm.reshape(1, d))
    y_prompt = y[:n_prompt].reshape(batch, seq, d)
    y_sample = y[n_prompt:].reshape(dec_batch, dec_seq, d)
    new_na_k = jnp.stack(new_k, axis=1).astype(x_prompt.dtype)
    new_na_v = jnp.stack(new_v, axis=1).astype(x_prompt.dtype)
    new_hgrn_state = jnp.stack(new_s, axis=1).astype(x_prompt.dtype)
    return (y_prompt, y_sample, new_na_k, new_na_v, new_hgrn_state)
```

```python
import functools

import numpy as np
import jax
import jax.numpy as jnp
from jax import lax
from jax.experimental import pallas as pl
from jax.experimental.pallas import tpu as pltpu
from jax.experimental.pallas import tpu_sc as plsc

F32 = jnp.float32
BF16 = jnp.bfloat16
HIGHEST = lax.Precision.HIGHEST

N_MIXERS = 3
GRID_W = 64
CONV_WIDTH = 3
NA_HEADS = 16
NA_ROWS_MAX = 8
NA_COLS = 16
HG_HEADS = 8
HG_KEY_DIM = 128
HG_CHUNK = 32
N_EXPERTS = 32
TOP_K = 4
SWIGLU_LIMIT = 7.0
SWIGLU_ALPHA = 1.702
EPS = 1e-6

LANES = 128
SUBLANES = 8
VMEM_LIMIT_CAP = 60 * 1024 * 1024

NEG_BIG = -1e30
TOKEN_TILE = 256
MOE_ROWS = 256
HG_TILE = 128
ROW_CHUNKS = 4
PACKED_CHUNKS = 2
CHUNK_W = 256
SC_WINDOW = 128


def _params(sem, vmem_mb, flags=None):
    return pltpu.CompilerParams(dimension_semantics=sem, flags=flags,
                                vmem_limit_bytes=min(vmem_mb * 1024 * 1024, VMEM_LIMIT_CAP))


def _rms(x):
    return x * lax.rsqrt(jnp.mean(x * x, axis=-1, keepdims=True) + EPS)


def _modulate(x, g, shift, scale):
    return (_rms(x) * g) * (1.0 + scale) + shift


def _bf16_part(x):
    bits = lax.bitcast_convert_type(x, jnp.uint32) & jnp.uint32(0xFFFF0000)
    return lax.bitcast_convert_type(bits, F32)


def _group_of_tile(i, tile, n_prompt, dec_seq):
    start = i * tile
    return jnp.where(start < n_prompt, 0, 1 + (start - n_prompt) // dec_seq)


def _adaln_kernel(c_ref, w_ref, b_ref, o_ref):
    cv = c_ref[...]
    s = cv * jax.nn.sigmoid(cv)
    o_ref[...] = jnp.dot(s, w_ref[...], precision=HIGHEST, preferred_element_type=F32) + b_ref[...]


def _adaln_all(cvecs, w_mod, b_mod):
    depth, d, n = w_mod.shape
    tn = 2048
    return pl.pallas_call(
        _adaln_kernel,
        grid=(depth, n // tn),
        in_specs=[pl.BlockSpec((SUBLANES, d), lambda l, j: (0, 0)),
                  pl.BlockSpec((None, d, tn), lambda l, j: (l, 0, j)),
                  pl.BlockSpec((None, 1, tn), lambda l, j: (l, 0, j))],
        out_specs=pl.BlockSpec((None, SUBLANES, tn), lambda l, j: (l, 0, j)),
        out_shape=jax.ShapeDtypeStruct((depth, SUBLANES, n), F32),
        compiler_params=_params(("arbitrary", "arbitrary"), 40),
        name="adaln",
    )(cvecs, w_mod, b_mod.reshape(depth, 1, n))


def _in_kernel(x_ref, mod_ref, g_ref, w_ref, *out_refs, d, conv):
    mod = mod_ref[...]
    h = _modulate(x_ref[...], g_ref[...], mod[:, 0:d], mod[:, d:2 * d]).astype(BF16)
    if conv:
        b_ref, cu_ref = out_refs
        b_ref[...] = jnp.dot(h, w_ref[:, 0:d], preferred_element_type=F32)
        c = jnp.dot(h, w_ref[:, d:2 * d], preferred_element_type=F32)
        u = jnp.dot(h, w_ref[:, 2 * d:3 * d], preferred_element_type=F32)
        cu_ref[...] = c * u
    else:
        for j, o_ref in enumerate(out_refs):
            o_ref[...] = jnp.dot(h, w_ref[:, j * d:(j + 1) * d], preferred_element_type=F32)


def _mixer_in(x, mods_l, g, w, geom, conv=False):
    t, d = x.shape
    n = w.shape[1]
    n_out = 2 if conv else n // d
    tm = TOKEN_TILE
    grp = functools.partial(_group_of_tile, tile=tm, n_prompt=geom[0], dec_seq=geom[1])
    return pl.pallas_call(
        functools.partial(_in_kernel, d=d, conv=conv),
        grid=(t // tm,),
        in_specs=[pl.BlockSpec((tm, d), lambda i: (i, 0)),
                  pl.BlockSpec((None, 1, 6 * d), lambda i: (grp(i), 0, 0)),
                  pl.BlockSpec((1, d), lambda i: (0, 0)),
                  pl.BlockSpec((d, n), lambda i: (0, 0))],
        out_specs=[pl.BlockSpec((tm, d), lambda i: (i, 0))] * n_out,
        out_shape=[jax.ShapeDtypeStruct((t, d), F32)] * n_out,
        compiler_params=_params(("arbitrary",), 48),
        name="mixer_in",
    )(x, mods_l, g, w)


def _tail(m, x_ref, mod, g2_ref, wr_ref, br_ref, outs, run_ref, d):
    x1_ref, h2_ref, idx_ref, gw_ref, rank_ref, cnt_ref = outs
    x1 = x_ref[...] + mod[:, 2 * d:3 * d] * m
    x1_ref[...] = x1
    h2 = _modulate(x1, g2_ref[...], mod[:, 3 * d:4 * d], mod[:, 4 * d:5 * d])
    bits = lax.bitcast_convert_type(h2, jnp.uint32)
    top = (bits + (jnp.uint32(0x7FFF) + ((bits >> 16) & jnp.uint32(1)))) & jnp.uint32(0xFFFF0000)
    words = lax.bitcast_convert_type(top[:, :d // 2] | (top[:, d // 2:] >> 16), jnp.int32)
    for j in range(PACKED_CHUNKS):
        h2_ref[j] = words[:, j * CHUNK_W:(j + 1) * CHUNK_W]
    h2_top = _bf16_part(h2)
    h2_hi = h2_top.astype(BF16)
    h2_lo = (h2 - h2_top).astype(BF16)
    logits = (jnp.dot(h2_hi, wr_ref[0], preferred_element_type=F32)
              + jnp.dot(h2_lo, wr_ref[0], preferred_element_type=F32)
              + jnp.dot(h2_hi, wr_ref[1], preferred_element_type=F32)) + br_ref[...]
    lane = lax.broadcasted_iota(jnp.int32, logits.shape, 1)
    vals, idxs = [], []
    for _ in range(TOP_K):
        mx = jnp.max(logits, axis=-1, keepdims=True)
        ix = jnp.argmax(logits, axis=-1, keepdims=True).astype(jnp.int32)
        vals.append(mx)
        idxs.append(ix)
        logits = jnp.where(lane == ix, NEG_BIG, logits)
    es = [jnp.exp(v - vals[0]) for v in vals]
    den = es[0] + es[1] + es[2] + es[3]

    @pl.when(pl.program_id(0) == 0)
    def _():
        run_ref[...] = jnp.zeros(run_ref.shape, F32)

    tm = lane.shape[0]
    earlier = (lax.broadcasted_iota(jnp.int32, (tm, tm), 0)
               > lax.broadcasted_iota(jnp.int32, (tm, tm), 1)).astype(BF16)
    seen = run_ref[...]
    idx_out = jnp.zeros(lane.shape, jnp.int32)
    gw_out = jnp.zeros(lane.shape, F32)
    rank_out = jnp.zeros(lane.shape, jnp.int32)
    for k in range(TOP_K):
        hit = lane == idxs[k]
        before = jnp.dot(earlier, hit.astype(BF16), preferred_element_type=F32) + seen
        rank_k = jnp.sum(jnp.where(hit, before, 0.0), axis=-1, keepdims=True)
        seen = seen + jnp.sum(hit.astype(F32), axis=0, keepdims=True)
        idx_out = jnp.where(lane == k, idxs[k], idx_out)
        gw_out = jnp.where(lane == k, es[k] / den, gw_out)
        rank_out = jnp.where(lane == k, rank_k.astype(jnp.int32), rank_out)
    run_ref[...] = seen
    cnt_ref[...] = seen
    idx_ref[...] = idx_out
    gw_ref[...] = gw_out
    rank_ref[...] = rank_out


def _plain_out_kernel(a_ref, w_ref, x_ref, mod_ref, g2_ref, wr_ref, br_ref, *rest, d):
    m = jnp.dot(a_ref[...].astype(BF16), w_ref[...], preferred_element_type=F32)
    _tail(m, x_ref, mod_ref[...], g2_ref, wr_ref, br_ref, rest[:-1], rest[-1], d)


def _conv_out_kernel(b_ref, cu_ref, prev_ref, next_ref, ck_ref, w_ref, x_ref, mod_ref, g2_ref, wr_ref,
                     br_ref, *rest, d, tm, n_prompt, seq, dec_seq):
    i = pl.program_id(0)
    cu = cu_ref[...]
    row = lax.broadcasted_iota(jnp.int32, (tm, 1), 0)
    start = i * tm
    in_prompt = start < n_prompt
    seq_len = jnp.where(in_prompt, seq, dec_seq)
    pos = lax.rem(jnp.where(in_prompt, start, start - n_prompt) + row, seq_len)
    prev = jnp.where(row == 0, prev_ref[SUBLANES - 1:SUBLANES, :], pltpu.roll(cu, 1, 0))
    prev = jnp.where(pos == 0, 0.0, prev)
    nxt = jnp.where(row == tm - 1, next_ref[0:1, :], pltpu.roll(cu, tm - 1, 0))
    nxt = jnp.where(pos == seq_len - 1, 0.0, nxt)
    conv = prev * ck_ref[0:1, :] + cu * ck_ref[1:2, :] + nxt * ck_ref[2:3, :]
    m = jnp.dot((b_ref[...] * conv).astype(BF16), w_ref[...], preferred_element_type=F32)
    _tail(m, x_ref, mod_ref[...], g2_ref, wr_ref, br_ref, rest[:-1], rest[-1], d)


def _hgrn_out_kernel(o_ref, gout_ref, ng_ref, w_ref, x_ref, mod_ref, g2_ref, wr_ref, br_ref, *rest, d):
    parts = []
    for h in range(HG_HEADS):
        oh = o_ref[:, h * LANES:(h + 1) * LANES]
        parts.append(oh * lax.rsqrt(jnp.mean(oh * oh, axis=-1, keepdims=True) + EPS))
    gout = gout_ref[...]
    y = (jnp.concatenate(parts, axis=1) * ng_ref[...]) * (gout * jax.nn.sigmoid(gout))
    m = jnp.dot(y.astype(BF16), w_ref[...], preferred_element_type=F32)
    _tail(m, x_ref, mod_ref[...], g2_ref, wr_ref, br_ref, rest[:-1], rest[-1], d)


def _mixer_out(kind, acts, w_out, x, mods_l, g2, w_router, b_router, geom, extra=()):
    t, d = x.shape
    tm = TOKEN_TILE
    n_prompt, dec_seq, seq = geom
    grp = functools.partial(_group_of_tile, tile=tm, n_prompt=n_prompt, dec_seq=dec_seq)
    row_spec = pl.BlockSpec((tm, d), lambda i: (i, 0))
    wr = jnp.zeros((d, LANES), F32).at[:, :N_EXPERTS].set(w_router)
    wr_top = _bf16_part(wr)
    wr = jnp.stack([wr_top.astype(BF16), (wr - wr_top).astype(BF16)])
    br = jnp.full((1, LANES), NEG_BIG, F32).at[0, :N_EXPERTS].set(b_router)
    common_specs = [pl.BlockSpec((d, d), lambda i: (0, 0)),
                    row_spec,
                    pl.BlockSpec((None, 1, 6 * d), lambda i: (grp(i), 0, 0)),
                    pl.BlockSpec((1, d), lambda i: (0, 0)),
                    pl.BlockSpec((2, d, LANES), lambda i: (0, 0, 0)),
                    pl.BlockSpec((1, LANES), lambda i: (0, 0))]
    common_args = [w_out, x, mods_l, g2, wr, br]
    if kind == "plain":
        body = functools.partial(_plain_out_kernel, d=d)
        specs = [row_spec] + common_specs
        args = list(acts) + common_args
    elif kind == "conv":
        body = functools.partial(_conv_out_kernel, d=d, tm=tm, n_prompt=n_prompt, seq=seq, dec_seq=dec_seq)
        per = tm // SUBLANES
        last = t // SUBLANES - 1
        specs = [row_spec, row_spec,
                 pl.BlockSpec((SUBLANES, d), lambda i: (jnp.maximum(i * per - 1, 0), 0)),
                 pl.BlockSpec((SUBLANES, d), lambda i: (jnp.minimum((i + 1) * per, last), 0)),
                 pl.BlockSpec((CONV_WIDTH, d), lambda i: (0, 0))] + common_specs
        b, cu = acts
        args = [b, cu, cu, cu, extra[0]] + common_args
    else:
        body = functools.partial(_hgrn_out_kernel, d=d)
        specs = [row_spec, row_spec, pl.BlockSpec((1, d), lambda i: (0, 0))] + common_specs
        args = list(acts) + [extra[0]] + common_args
    lane_spec = pl.BlockSpec((tm, LANES), lambda i: (i, 0))
    return pl.pallas_call(
        body,
        grid=(t // tm,),
        in_specs=specs,
        out_specs=[row_spec, pl.BlockSpec((PACKED_CHUNKS, tm, CHUNK_W), lambda i: (0, i, 0)),
                   lane_spec, lane_spec, lane_spec, pl.BlockSpec((1, LANES), lambda i: (0, 0))],
        out_shape=[jax.ShapeDtypeStruct((t, d), F32), jax.ShapeDtypeStruct((PACKED_CHUNKS, t, CHUNK_W), jnp.int32),
                   jax.ShapeDtypeStruct((t, LANES), jnp.int32), jax.ShapeDtypeStruct((t, LANES), F32),
                   jax.ShapeDtypeStruct((t, LANES), jnp.int32), jax.ShapeDtypeStruct((1, LANES), F32)],
        scratch_shapes=[pltpu.VMEM((1, LANES), F32)],
        compiler_params=_params(("arbitrary",), 40),
        name="mixer_out_" + kind,
    )(*args)


def _softmax_rows(parts):
    mx = functools.reduce(jnp.maximum, [jnp.max(s, axis=-1, keepdims=True) for s in parts])
    es = [jnp.exp(s - mx) for s in parts]
    den = functools.reduce(lambda a, b: a + b, [jnp.sum(e, axis=-1, keepdims=True) for e in es])
    return [e / den for e in es]


def _na_ctx_kernel(q_ref, k_ref, v_ref, o_ref, *, dh, scale):
    nt = (((1,), (1,)), ((), ()))
    for h in range(NA_HEADS):
        sl = slice(h * dh, (h + 1) * dh)
        s = lax.dot_general(q_ref[:, sl].astype(BF16), k_ref[:, sl].astype(BF16), nt,
                            preferred_element_type=F32) * scale
        (p,) = _softmax_rows([s])
        o_ref[:, sl] = jnp.dot(p.astype(BF16), v_ref[:, sl].astype(BF16), preferred_element_type=F32)


def _na_context(q, k, v, batch, seq):
    d = q.shape[1]
    dh = d // NA_HEADS
    spec = pl.BlockSpec((seq, d), lambda b: (b, 0))
    return pl.pallas_call(
        functools.partial(_na_ctx_kernel, dh=dh, scale=dh ** -0.5),
        grid=(batch,),
        in_specs=[spec, spec, spec],
        out_specs=spec,
        out_shape=jax.ShapeDtypeStruct((batch * seq, d), F32),
        compiler_params=_params(("arbitrary",), 32),
        name="na_context",
    )(q, k, v)


def _na_lat_kernel(*refs, dh, scale, kr):
    q_ref = refs[0]
    k_refs = refs[1:1 + kr]
    v_refs = refs[1 + kr:1 + 2 * kr]
    ck_ref, cv_ref, bias_ref, o_ref = refs[1 + 2 * kr:]
    nt = (((1,), (1,)), ((), ()))
    for h in range(NA_HEADS):
        sl = slice(h * dh, (h + 1) * dh)
        qh = q_ref[:, sl].astype(BF16)
        kw = jnp.concatenate([r[:, sl] for r in k_refs], axis=0).astype(BF16)
        vw = jnp.concatenate([r[:, sl] for r in v_refs], axis=0).astype(BF16)
        s_nb = lax.dot_general(qh, kw, nt, preferred_element_type=F32) * scale + bias_ref[h]
        s_cx = lax.dot_general(qh, ck_ref[:, sl].astype(BF16), nt, preferred_element_type=F32) * scale
        p_nb, p_cx = _softmax_rows([s_nb, s_cx])
        o_ref[:, sl] = (jnp.dot(p_nb.astype(BF16), vw, preferred_element_type=F32)
                        + jnp.dot(p_cx.astype(BF16), cv_ref[:, sl].astype(BF16), preferred_element_type=F32))


def _na_bias_table(rpb, kr):
    col = np.arange(GRID_W)
    col_start = np.clip(col - NA_COLS // 2, 0, GRID_W - NA_COLS)
    delta = col[None, :] - col[:, None] + (NA_COLS - 1)
    inside = (col[None, :] >= col_start[:, None]) & (col[None, :] < col_start[:, None] + NA_COLS)
    tab = rpb[:, :, np.clip(delta, 0, 2 * NA_COLS - 2)]
    tab = jnp.where(jnp.asarray(inside)[None, None], tab, NEG_BIG)
    out = []
    for d0 in range(NA_ROWS_MAX):
        rows = tab[:, d0:d0 + kr]
        out.append(rows.transpose(0, 2, 1, 3).reshape(NA_HEADS, GRID_W, kr * GRID_W))
    return jnp.stack(out, axis=0)


def _na_latent(q, k, v, ctx_k, ctx_v, rpb, n_prompt, dec_batch, dec_seq):
    d = q.shape[1]
    dh = d // NA_HEADS
    rows = dec_seq // GRID_W
    kr = min(NA_ROWS_MAX, rows)
    base = n_prompt // GRID_W
    past = ctx_k.shape[1]
    bias = _na_bias_table(rpb, kr)

    def row_start(r):
        return jnp.clip(r - kr // 2, 0, rows - kr)

    def win_spec(j):
        return pl.BlockSpec((GRID_W, d), lambda b, r: (base + b * rows + row_start(r) + j, 0))

    in_specs = ([pl.BlockSpec((GRID_W, d), lambda b, r: (base + b * rows + r, 0))]
                + [win_spec(j) for j in range(kr)] * 2
                + [pl.BlockSpec((None, past, d), lambda b, r: (b, 0, 0))] * 2
                + [pl.BlockSpec((None, NA_HEADS, GRID_W, kr * GRID_W),
                                lambda b, r: (row_start(r) - r + NA_ROWS_MAX - 1, 0, 0, 0))])
    return pl.pallas_call(
        functools.partial(_na_lat_kernel, dh=dh, scale=dh ** -0.5, kr=kr),
        grid=(dec_batch, rows),
        in_specs=in_specs,
        out_specs=pl.BlockSpec((GRID_W, d), lambda b, r: (b * rows + r, 0)),
        out_shape=jax.ShapeDtypeStruct((dec_batch * dec_seq, d), F32),
        compiler_params=_params(("arbitrary", "arbitrary"), 40),
        name="na_latent",
    )(q, *([k] * kr), *([v] * kr), ctx_k.reshape(dec_batch, past, d), ctx_v.reshape(dec_batch, past, d), bias)


def _hgrn_scan_kernel(*refs, seq_len, has_s0, want_state):
    q_ref, v_ref, z0_ref, z1_ref, lb_ref, tri_ref = refs[:6]
    rest = list(refs[6:])
    s0_ref = rest.pop(0) if has_s0 else None
    o_ref = rest.pop(0)
    sf_ref = rest.pop(0) if want_state else None
    c, r = HG_CHUNK, HG_TILE
    nc, nt = r // c, seq_len // r
    dk = HG_KEY_DIM
    tpos = lax.broadcasted_iota(jnp.int32, (1, c, 1), 1)
    nt_dims = (((1,), (1,)), ((), ()))
    tn_dims = (((0,), (0,)), ((), ()))

    for dr in range(2):
        z_ref = (z0_ref, z1_ref)[dr]
        lb = lb_ref[dr:dr + 1, :]
        tri = tri_ref[dr]

        def tile(i, st, dr=dr, z_ref=z_ref, lb=lb, tri=tri):
            ti = i if dr == 0 else nt - 1 - i
            r0 = pl.multiple_of(ti * r, r)
            q = q_ref[pl.ds(r0, r), :]
            v = v_ref[pl.ds(r0, r), :]
            z = z_ref[pl.ds(r0, r), :]
            g = jnp.log(lb + (1.0 - lb) * jax.nn.sigmoid(z))
            k = (1.0 - lb) * jax.nn.sigmoid(-z)
            b = jnp.dot(tri, g, precision=HIGHEST, preferred_element_type=F32)
            b3, q3, k3, v3 = (a.reshape(nc, c, dk) for a in (b, q, k, v))
            oi = jnp.zeros((nc, c, dk), F32)
            for s in range(c):
                keep = (tpos >= s) if dr == 0 else (tpos <= s)
                e = jnp.exp(jnp.where(keep, b3 - b3[:, s:s + 1, :], NEG_BIG))
                a = jnp.sum(q3 * e * k3[:, s:s + 1, :], axis=-1, keepdims=True)
                oi = oi + a * v3[:, s:s + 1, :]
            bl = b3[:, c - 1:c, :] if dr == 0 else b3[:, 0:1, :]
            qt = (q3 * jnp.exp(b3)).astype(BF16)
            kt = (k3 * jnp.exp(bl - b3)).astype(BF16)
            vb = v3.astype(BF16)
            outs = [None] * nc
            for ci in (range(nc) if dr == 0 else range(nc - 1, -1, -1)):
                oc = lax.dot_general(qt[ci], st.astype(BF16), nt_dims, preferred_element_type=F32)
                outs[ci] = oi[ci] + oc
                kv = lax.dot_general(vb[ci], kt[ci], tn_dims, preferred_element_type=F32)
                st = st * jnp.exp(bl[ci]) + kv
            o = jnp.concatenate(outs, axis=0)
            if dr == 0:
                o_ref[pl.ds(r0, r), :] = o
            else:
                o_ref[pl.ds(r0, r), :] += o
            return st

        st0 = s0_ref[dr] if has_s0 else jnp.zeros((dk, dk), F32)
        st = lax.fori_loop(0, nt, tile, st0)
        if want_state:
            sf_ref[dr] = st.T


def _hgrn_tri():
    t = np.arange(HG_TILE)
    same = (t[:, None] // HG_CHUNK) == (t[None, :] // HG_CHUNK)
    fwd = same & (t[None, :] <= t[:, None])
    bwd = same & (t[None, :] >= t[:, None])
    return jnp.asarray(np.stack([fwd, bwd]).astype(np.float32))


def _hgrn_scan(q, v, z0, z1, lb, row0, n_seq, seq_len, s0_t=None, want_state=False):
    d = q.shape[1]
    dk = HG_KEY_DIM
    blk0 = row0 // seq_len
    seq_spec = pl.BlockSpec((seq_len, dk), lambda s, h: (blk0 + s, h))
    st_spec = pl.BlockSpec((None, 2, None, dk, dk), lambda s, h: (s, 0, h, 0, 0))
    in_specs = [seq_spec] * 4 + [pl.BlockSpec((2, dk), lambda s, h: (0, h)),
                                 pl.BlockSpec((2, HG_TILE, HG_TILE), lambda s, h: (0, 0, 0))]
    args = [q, v, z0, z1, lb, _hgrn_tri()]
    if s0_t is not None:
        in_specs.append(st_spec)
        args.append(s0_t)
    out_specs = [pl.BlockSpec((seq_len, dk), lambda s, h: (s, h))]
    out_shape = [jax.ShapeDtypeStruct((n_seq * seq_len, d), F32)]
    if want_state:
        out_specs.append(st_spec)
        out_shape.append(jax.ShapeDtypeStruct((n_seq, 2, HG_HEADS, dk, dk), F32))
    res = pl.pallas_call(
        functools.partial(_hgrn_scan_kernel, seq_len=seq_len, has_s0=s0_t is not None, want_state=want_state),
        grid=(n_seq, HG_HEADS),
        in_specs=in_specs,
        out_specs=out_specs,
        out_shape=out_shape,
        compiler_params=_params(("arbitrary", "arbitrary"), 32),
        name="hgrn_scan",
    )(*args)
    return res if want_state else (res[0], None)


def _route(idx, rank, counts, t):
    n = t * TOP_K
    rb = MOE_ROWS
    n_blocks = -(-n // rb) + N_EXPERTS
    n_rows = (n_blocks + N_EXPERTS) * rb
    experts = jnp.arange(N_EXPERTS, dtype=jnp.int32)
    counts = counts[0, :N_EXPERTS].astype(jnp.int32)
    padded = (counts + rb - 1) // rb * rb
    padded_end = jnp.cumsum(padded)
    start = padded_end - padded
    e = idx[:, :TOP_K]
    dest = jnp.sum(jnp.where(e[:, :, None] == experts, start, 0), axis=-1) + rank[:, :TOP_K]
    dest_km = dest.T.astype(jnp.int32)
    q = jnp.arange(rb, dtype=jnp.int32)
    spare = n_blocks * rb + experts[:, None] * rb + q[None, :]
    pad_rows = jnp.where(q[None, :] < (padded - counts)[:, None], (start + counts)[:, None] + q[None, :], spare)
    per_chunk = jnp.concatenate([dest_km.reshape(-1), pad_rows.reshape(-1)])
    sc_index = (jnp.arange(ROW_CHUNKS, dtype=jnp.int32)[:, None] * n_rows + per_chunk[None, :]).reshape(1, -1)
    block_start = jnp.arange(n_blocks, dtype=jnp.int32) * rb
    block_expert = jnp.minimum(jnp.sum((padded_end[None, :] <= block_start[:, None]).astype(jnp.int32), axis=1),
                               N_EXPERTS - 1)
    n_used = (padded_end[-1] // rb).reshape(1)
    return sc_index, block_expert, n_used, n_rows


def _sc_mesh():
    return plsc.VectorSubcoreMesh(core_axis_name="c", subcore_axis_name="s")


def _sc_dispatch(h2c, sc_index, t, n_rows):
    win = SC_WINDOW
    tw = t // win
    nw = (t * TOP_K + N_EXPERTS * MOE_ROWS) // win
    n_chunks = h2c.shape[0]

    def scatter(x_hbm, i_hbm, o_hbm):
        def body(x_vmem, i_vmem):
            pltpu.sync_copy(x_vmem, o_hbm.at[i_vmem.at[0]])

        pltpu.emit_pipeline(
            body,
            grid=(n_chunks * nw,),
            in_specs=[pl.BlockSpec((win, CHUNK_W), index_map=lambda i: ((i // nw) * tw + (i % nw) % tw, 0)),
                      pl.BlockSpec((1, win), index_map=lambda i: (0, i))],
            out_specs=[],
            core_axis_name=("c", "s"),
            dimension_semantics=(pltpu.PARALLEL,),
        )(x_hbm, i_hbm)

    out = pl.kernel(scatter, out_type=jax.ShapeDtypeStruct((n_chunks * n_rows, CHUNK_W), h2c.dtype),
                    mesh=_sc_mesh(), name="moe_dispatch")(h2c.reshape(n_chunks * t, CHUNK_W), sc_index)
    return out.reshape(n_chunks, n_rows, CHUNK_W)


def _sc_collect(y_rows, sc_index, t, n_rows):
    win = SC_WINDOW
    nw = (t * TOP_K + N_EXPERTS * MOE_ROWS) // win
    aw = t * TOP_K // win

    def gather(y_hbm, i_hbm, o_hbm):
        def body(i_vmem, o_vmem):
            pltpu.sync_copy(y_hbm.at[i_vmem.at[0]], o_vmem)

        pltpu.emit_pipeline(
            body,
            grid=(ROW_CHUNKS * aw,),
            in_specs=[pl.BlockSpec((1, win), index_map=lambda i: (0, (i // aw) * nw + i % aw))],
            out_specs=[pl.BlockSpec((win, CHUNK_W), index_map=lambda i: (i, 0))],
            core_axis_name=("c", "s"),
            dimension_semantics=(pltpu.PARALLEL,),
        )(i_hbm, o_hbm)

    out = pl.kernel(gather, out_type=jax.ShapeDtypeStruct((ROW_CHUNKS * TOP_K * t, CHUNK_W), F32),
                    mesh=_sc_mesh(), name="moe_collect")(y_rows.reshape(ROW_CHUNKS * n_rows, CHUNK_W), sc_index)
    return out.reshape(ROW_CHUNKS, TOP_K, t, CHUNK_W)


def _moe_kernel(be_ref, nu_ref, x_ref, w1_ref, b1_ref, w2_ref, b2_ref, o_ref, w1b, w2b, *, dff):
    i = pl.program_id(0)
    d = w1_ref.shape[0]
    e = be_ref[i]
    prev = be_ref[jnp.maximum(i - 1, 0)]

    @pl.when(jnp.logical_or(i == 0, e != prev))
    def _():
        w1b[...] = w1_ref[...].astype(BF16)
        w2b[...] = w2_ref[...].astype(BF16)

    @pl.when(i < nu_ref[0])
    def _():
        gu = b1_ref[...]
        for j in range(PACKED_CHUNKS):
            words = lax.bitcast_convert_type(x_ref[j], jnp.uint32)
            halves = (words & jnp.uint32(0xFFFF0000), words << 16)
            for part, col0 in zip(halves, (j * CHUNK_W, d // 2 + j * CHUNK_W)):
                xb = lax.bitcast_convert_type(part, F32).astype(BF16)
                gu = gu + jnp.dot(xb, w1b[col0:col0 + CHUNK_W, :], preferred_element_type=F32)
        g = jnp.minimum(gu[:, :dff], SWIGLU_LIMIT)
        u = jnp.clip(gu[:, dff:], -SWIGLU_LIMIT, SWIGLU_LIMIT)
        a = (u + 1.0) * (g * jax.nn.sigmoid(SWIGLU_ALPHA * g))
        y = jnp.dot(a.astype(BF16), w2b[...], preferred_element_type=F32) + b2_ref[...]
        for j in range(ROW_CHUNKS):
            o_ref[j] = y[:, j * CHUNK_W:(j + 1) * CHUNK_W]

    @pl.when(i >= nu_ref[0])
    def _():
        o_ref[...] = jnp.zeros(o_ref.shape, F32)


def _moe_experts(x_rows, block_expert, n_used, layer, w1, b1, w2, b2):
    _, n_rows, _ = x_rows.shape
    rb = MOE_ROWS
    n_blocks = n_rows // rb - N_EXPERTS
    depth, n_e, d, dff2 = w1.shape
    dff = dff2 // 2
    row_spec = pl.BlockSpec((PACKED_CHUNKS, rb, CHUNK_W), lambda i, be, nu: (0, jnp.minimum(i, nu[0] - 1), 0))
    grid_spec = pltpu.PrefetchScalarGridSpec(
        num_scalar_prefetch=2,
        grid=(n_blocks,),
        in_specs=[row_spec,
                  pl.BlockSpec((None, None, d, dff2), lambda i, be, nu: (layer, be[i], 0, 0)),
                  pl.BlockSpec((None, None, 1, dff2), lambda i, be, nu: (layer, be[i], 0, 0)),
                  pl.BlockSpec((None, None, dff, d), lambda i, be, nu: (layer, be[i], 0, 0)),
                  pl.BlockSpec((None, None, 1, d), lambda i, be, nu: (layer, be[i], 0, 0))],
        out_specs=pl.BlockSpec((ROW_CHUNKS, rb, CHUNK_W), lambda i, be, nu: (0, i, 0)),
        scratch_shapes=[pltpu.VMEM((d, dff2), BF16), pltpu.VMEM((dff, d), BF16)],
    )
    return pl.pallas_call(
        functools.partial(_moe_kernel, dff=dff),
        grid_spec=grid_spec,
        out_shape=jax.ShapeDtypeStruct((ROW_CHUNKS, n_rows, CHUNK_W), F32),
        compiler_params=_params(("arbitrary",), 56),
        name="moe_experts",
    )(block_expert, n_used, x_rows, w1, b1.reshape(depth, n_e, 1, dff2), w2, b2.reshape(depth, n_e, 1, d))


def _combine_kernel(y_ref, gw_ref, x_ref, mod_ref, o_ref, *, d):
    gw = gw_ref[...]
    for j in range(ROW_CHUNKS):
        cols = slice(j * CHUNK_W, (j + 1) * CHUNK_W)
        acc = gw[:, 0:1] * y_ref[j, 0]
        for k in range(1, TOP_K):
            acc = acc + gw[:, k:k + 1] * y_ref[j, k]
        o_ref[:, cols] = x_ref[:, cols] + mod_ref[:, 5 * d + j * CHUNK_W:5 * d + (j + 1) * CHUNK_W] * acc


def _moe_combine(y_slots, gw, x1, mods_l, geom):
    t, d = x1.shape
    tm = TOKEN_TILE
    grp = functools.partial(_group_of_tile, tile=tm, n_prompt=geom[0], dec_seq=geom[1])
    return pl.pallas_call(
        functools.partial(_combine_kernel, d=d),
        grid=(t // tm,),
        in_specs=[pl.BlockSpec((ROW_CHUNKS, TOP_K, tm, CHUNK_W), lambda i: (0, 0, i, 0)),
                  pl.BlockSpec((tm, LANES), lambda i: (i, 0)),
                  pl.BlockSpec((tm, d), lambda i: (i, 0)),
                  pl.BlockSpec((None, 1, 6 * d), lambda i: (grp(i), 0, 0))],
        out_specs=pl.BlockSpec((tm, d), lambda i: (i, 0)),
        out_shape=jax.ShapeDtypeStruct((t, d), F32),
        compiler_params=_params(("arbitrary",), 24),
        name="moe_combine",
    )(y_slots, gw, x1, mods_l)


def _final_norm_kernel(x_ref, g_ref, o_ref):
    o_ref[...] = _rms(x_ref[...]) * g_ref[...]


def _final_norm(x, g):
    t, d = x.shape
    tm = TOKEN_TILE
    return pl.pallas_call(
        _final_norm_kernel,
        grid=(t // tm,),
        in_specs=[pl.BlockSpec((tm, d), lambda i: (i, 0)), pl.BlockSpec((1, d), lambda i: (0, 0))],
        out_specs=pl.BlockSpec((tm, d), lambda i: (i, 0)),
        out_shape=jax.ShapeDtypeStruct((t, d), F32),
        compiler_params=_params(("arbitrary",), 16),
        name="final_norm",
    )(x, g)


def kernel(x_prompt, x_sample, c, c_ctx, cache_na_k, cache_na_v, state_hgrn, norm_mix, norm_ffn, w_mod, b_mod,
           conv_w_in, conv_k, conv_w_out, na_w_qkv, na_rpb, na_w_o, hg_w_qig, hg_w_f, hg_lb, hg_norm, hg_w_o,
           moe_w_router, moe_b_router, moe_w1, moe_b1, moe_w2, moe_b2, final_norm):
    batch, seq, d = x_prompt.shape
    dec_batch, dec_seq, _ = x_sample.shape
    depth = w_mod.shape[0]
    n_prompt = batch * seq
    t = n_prompt + dec_batch * dec_seq
    geom = (n_prompt, dec_seq, seq)
    assert n_prompt % TOKEN_TILE == 0 and dec_seq % TOKEN_TILE == 0 and seq % HG_TILE == 0
    assert n_prompt % dec_seq == 0 and dec_seq % GRID_W == 0 and 1 + dec_batch <= SUBLANES
    assert d == ROW_CHUNKS * CHUNK_W and t % SC_WINDOW == 0 and MOE_ROWS % SC_WINDOW == 0

    x = jnp.concatenate([x_prompt.reshape(n_prompt, d), x_sample.reshape(dec_batch * dec_seq, d)], axis=0)
    cvecs = jnp.zeros((SUBLANES, d), F32).at[0].set(c_ctx).at[1:1 + dec_batch].set(c)
    mods = _adaln_all(cvecs, w_mod, b_mod)[:, :1 + dec_batch].reshape(depth, 1 + dec_batch, 1, 6 * d)

    lb_soft = jax.nn.softmax(hg_lb.astype(F32), axis=0)
    lower_bounds = jnp.cumsum(lb_soft, axis=0) - lb_soft[0]

    new_k, new_v, new_s = [], [], []
    for l in range(depth):
        kind, j = l % N_MIXERS, l // N_MIXERS
        g1 = norm_mix[l].reshape(1, d)
        g2 = norm_ffn[l].reshape(1, d)
        tail_args = (x, mods[l], g2, moe_w_router[l], moe_b_router[l], geom)
        if kind == 0:
            b, cu = _mixer_in(x, mods[l], g1, conv_w_in[j].astype(BF16), geom, conv=True)
            x1, h2c, idx, gw, rank, counts = _mixer_out("conv", (b, cu), conv_w_out[j].astype(BF16), *tail_args,
                                         extra=(conv_k[j],))
        elif kind == 1:
            q, k, v = _mixer_in(x, mods[l], g1, na_w_qkv[j].astype(BF16), geom)
            o_p = _na_context(q, k, v, batch, seq)
            o_s = _na_latent(q, k, v, cache_na_k[:, j], cache_na_v[:, j], na_rpb[j], n_prompt, dec_batch, dec_seq)
            new_k.append(k[:n_prompt].reshape(batch, seq, NA_HEADS, d // NA_HEADS))
            new_v.append(v[:n_prompt].reshape(batch, seq, NA_HEADS, d // NA_HEADS))
            x1, h2c, idx, gw, rank, counts = _mixer_out("plain", (jnp.concatenate([o_p, o_s], axis=0),),
                                         na_w_o[j].astype(BF16), *tail_args)
        else:
            w_in = jnp.concatenate([hg_w_qig[j], hg_w_f[j, 0], hg_w_f[j, 1]], axis=1).astype(BF16)
            q, v, gout, z0, z1 = _mixer_in(x, mods[l], g1, w_in, geom)
            lb = lower_bounds[l]
            o_p, s_p = _hgrn_scan(q, v, z0, z1, lb, 0, batch, seq, want_state=True)
            s0_t = jnp.swapaxes(state_hgrn[:, j].astype(F32), -1, -2)
            o_s, _ = _hgrn_scan(q, v, z0, z1, lb, n_prompt, dec_batch, dec_seq, s0_t=s0_t)
            new_s.append(s_p)
            x1, h2c, idx, gw, rank, counts = _mixer_out("hgrn", (jnp.concatenate([o_p, o_s], axis=0), gout),
                                         hg_w_o[j].astype(BF16), *tail_args, extra=(hg_norm[j].reshape(1, d),))
        sc_index, block_expert, n_used, n_rows = _route(idx, rank, counts, t)
        x_rows = _sc_dispatch(h2c, sc_index, t, n_rows)
        y_rows = _moe_experts(x_rows, block_expert, n_used, l, moe_w1, moe_b1, moe_w2, moe_b2)
        y_slots = _sc_collect(y_rows, sc_index, t, n_rows)
        x = _moe_combine(y_slots, gw, x1, mods[l], geom)

    y = _final_norm(x, final_norm.reshape(1, d))
    y_prompt = y[:n_prompt].reshape(batch, seq, d)
    y_sample = y[n_prompt:].reshape(dec_batch, dec_seq, d)
    new_na_k = jnp.stack(new_k, axis=1).astype(x_prompt.dtype)
    new_na_v = jnp.stack(new_v, axis=1).astype(x_prompt.dtype)
    new_hgrn_state = jnp.stack(new_s, axis=1).astype(x_prompt.dtype)
    return (y_prompt, y_sample, new_na_k, new_na_v, new_hgrn_state)
```

```python
import functools

import numpy as np
import jax
import jax.numpy as jnp
from jax import lax
from jax.experimental import pallas as pl
from jax.experimental.pallas import tpu as pltpu
from jax.experimental.pallas import tpu_sc as plsc

F32 = jnp.float32
BF16 = jnp.bfloat16
HIGHEST = lax.Precision.HIGHEST

N_MIXERS = 3
GRID_W = 64
CONV_WIDTH = 3
NA_HEADS = 16
NA_ROWS_MAX = 8
NA_COLS = 16
HG_HEADS = 8
HG_KEY_DIM = 128
HG_CHUNK = 32
N_EXPERTS = 32
TOP_K = 4
SWIGLU_LIMIT = 7.0
SWIGLU_ALPHA = 1.702
EPS = 1e-6

LANES = 128
SUBLANES = 8
VMEM_LIMIT_CAP = 60 * 1024 * 1024

NEG_BIG = -1e30
TOKEN_TILE = 256
MOE_ROWS = 256
HG_TILE = 128
ROW_CHUNKS = 4
PACKED_CHUNKS = 2
CHUNK_W = 256
SC_WINDOW = 128


def _params(sem, vmem_mb, flags=None):
    return pltpu.CompilerParams(dimension_semantics=sem, flags=flags,
                                vmem_limit_bytes=min(vmem_mb * 1024 * 1024, VMEM_LIMIT_CAP))


def _rms(x):
    return x * lax.rsqrt(jnp.mean(x * x, axis=-1, keepdims=True) + EPS)


def _modulate(x, g, shift, scale):
    return (_rms(x) * g) * (1.0 + scale) + shift


def _bf16_part(x):
    bits = lax.bitcast_convert_type(x, jnp.uint32) & jnp.uint32(0xFFFF0000)
    return lax.bitcast_convert_type(bits, F32)


def _group_of_tile(i, tile, n_prompt, dec_seq):
    start = i * tile
    return jnp.where(start < n_prompt, 0, 1 + (start - n_prompt) // dec_seq)


def _adaln_kernel(c_ref, w_ref, b_ref, o_ref):
    cv = c_ref[...]
    s = cv * jax.nn.sigmoid(cv)
    o_ref[...] = jnp.dot(s, w_ref[...], precision=HIGHEST, preferred_element_type=F32) + b_ref[...]


def _adaln_all(cvecs, w_mod, b_mod):
    depth, d, n = w_mod.shape
    tn = 2048
    return pl.pallas_call(
        _adaln_kernel,
        grid=(depth, n // tn),
        in_specs=[pl.BlockSpec((SUBLANES, d), lambda l, j: (0, 0)),
                  pl.BlockSpec((None, d, tn), lambda l, j: (l, 0, j)),
                  pl.BlockSpec((None, 1, tn), lambda l, j: (l, 0, j))],
        out_specs=pl.BlockSpec((None, SUBLANES, tn), lambda l, j: (l, 0, j)),
        out_shape=jax.ShapeDtypeStruct((depth, SUBLANES, n), F32),
        compiler_params=_params(("arbitrary", "arbitrary"), 40),
        name="adaln",
    )(cvecs, w_mod, b_mod.reshape(depth, 1, n))


def _in_kernel(x_ref, mod_ref, g_ref, w_ref, *out_refs, d, conv):
    mod = mod_ref[...]
    h = _modulate(x_ref[...], g_ref[...], mod[:, 0:d], mod[:, d:2 * d]).astype(BF16)
    if conv:
        b_ref, cu_ref = out_refs
        b_ref[...] = jnp.dot(h, w_ref[:, 0:d], preferred_element_type=F32)
        c = jnp.dot(h, w_ref[:, d:2 * d], preferred_element_type=F32)
        u = jnp.dot(h, w_ref[:, 2 * d:3 * d], preferred_element_type=F32)
        cu_ref[...] = c * u
    else:
        for j, o_ref in enumerate(out_refs):
            o_ref[...] = jnp.dot(h, w_ref[:, j * d:(j + 1) * d], preferred_element_type=F32)


def _mixer_in(x, mods_l, g, w, geom, conv=False):
    t, d = x.shape
    n = w.shape[1]
    n_out = 2 if conv else n // d
    tm = TOKEN_TILE
    grp = functools.partial(_group_of_tile, tile=tm, n_prompt=geom[0], dec_seq=geom[1])
    return pl.pallas_call(
        functools.partial(_in_kernel, d=d, conv=conv),
        grid=(t // tm,),
        in_specs=[pl.BlockSpec((tm, d), lambda i: (i, 0)),
                  pl.BlockSpec((None, 1, 6 * d), lambda i: (grp(i), 0, 0)),
                  pl.BlockSpec((1, d), lambda i: (0, 0)),
                  pl.BlockSpec((d, n), lambda i: (0, 0))],
        out_specs=[pl.BlockSpec((tm, d), lambda i: (i, 0))] * n_out,
        out_shape=[jax.ShapeDtypeStruct((t, d), F32)] * n_out,
        compiler_params=_params(("arbitrary",), 48),
        name="mixer_in",
    )(x, mods_l, g, w)


def _tail(m, x_ref, mod, g2_ref, wr_ref, br_ref, outs, run_ref, d):
    x1_ref, h2_ref, idx_ref, gw_ref, rank_ref, cnt_ref = outs
    x1 = x_ref[...] + mod[:, 2 * d:3 * d] * m
    x1_ref[...] = x1
    h2 = _modulate(x1, g2_ref[...], mod[:, 3 * d:4 * d], mod[:, 4 * d:5 * d])
    bits = lax.bitcast_convert_type(h2, jnp.uint32)
    top = (bits + (jnp.uint32(0x7FFF) + ((bits >> 16) & jnp.uint32(1)))) & jnp.uint32(0xFFFF0000)
    words = lax.bitcast_convert_type(top[:, :d // 2] | (top[:, d // 2:] >> 16), jnp.int32)
    for j in range(PACKED_CHUNKS):
        h2_ref[j] = words[:, j * CHUNK_W:(j + 1) * CHUNK_W]
    h2_top = _bf16_part(h2)
    h2_hi = h2_top.astype(BF16)
    h2_lo = (h2 - h2_top).astype(BF16)
    logits = (jnp.dot(h2_hi, wr_ref[0], preferred_element_type=F32)
              + jnp.dot(h2_lo, wr_ref[0], preferred_element_type=F32)
              + jnp.dot(h2_hi, wr_ref[1], preferred_element_type=F32)) + br_ref[...]
    lane = lax.broadcasted_iota(jnp.int32, logits.shape, 1)
    vals, idxs = [], []
    for _ in range(TOP_K):
        mx = jnp.max(logits, axis=-1, keepdims=True)
        ix = jnp.argmax(logits, axis=-1, keepdims=True).astype(jnp.int32)
        vals.append(mx)
        idxs.append(ix)
        logits = jnp.where(lane == ix, NEG_BIG, logits)
    es = [jnp.exp(v - vals[0]) for v in vals]
    den = es[0] + es[1] + es[2] + es[3]

    @pl.when(pl.program_id(0) == 0)
    def _():
        run_ref[...] = jnp.zeros(run_ref.shape, F32)

    tm = lane.shape[0]
    earlier = (lax.broadcasted_iota(jnp.int32, (tm, tm), 0)
               > lax.broadcasted_iota(jnp.int32, (tm, tm), 1)).astype(BF16)
    seen = run_ref[...]
    idx_out = jnp.zeros(lane.shape, jnp.int32)
    gw_out = jnp.zeros(lane.shape, F32)
    rank_out = jnp.zeros(lane.shape, jnp.int32)
    for k in range(TOP_K):
        hit = lane == idxs[k]
        before = jnp.dot(earlier, hit.astype(BF16), preferred_element_type=F32) + seen
        rank_k = jnp.sum(jnp.where(hit, before, 0.0), axis=-1, keepdims=True)
        seen = seen + jnp.sum(hit.astype(F32), axis=0, keepdims=True)
        idx_out = jnp.where(lane == k, idxs[k], idx_out)
        gw_out = jnp.where(lane == k, es[k] / den, gw_out)
        rank_out = jnp.where(lane == k, rank_k.astype(jnp.int32), rank_out)
    run_ref[...] = seen
    cnt_ref[...] = seen
    idx_ref[...] = idx_out
    gw_ref[...] = gw_out
    rank_ref[...] = rank_out


def _plain_out_kernel(a_ref, w_ref, x_ref, mod_ref, g2_ref, wr_ref, br_ref, *rest, d):
    m = jnp.dot(a_ref[...].astype(BF16), w_ref[...], preferred_element_type=F32)
    _tail(m, x_ref, mod_ref[...], g2_ref, wr_ref, br_ref, rest[:-1], rest[-1], d)


def _conv_out_kernel(b_ref, cu_ref, prev_ref, next_ref, ck_ref, w_ref, x_ref, mod_ref, g2_ref, wr_ref,
                     br_ref, *rest, d, tm, n_prompt, seq, dec_seq):
    i = pl.program_id(0)
    cu = cu_ref[...]
    row = lax.broadcasted_iota(jnp.int32, (tm, 1), 0)
    start = i * tm
    in_prompt = start < n_prompt
    seq_len = jnp.where(in_prompt, seq, dec_seq)
    pos = lax.rem(jnp.where(in_prompt, start, start - n_prompt), seq_len) + row
    prev = jnp.where(row == 0, prev_ref[SUBLANES - 1:SUBLANES, :], pltpu.roll(cu, 1, 0))
    prev = jnp.where(pos == 0, 0.0, prev)
    nxt = jnp.where(row == tm - 1, next_ref[0:1, :], pltpu.roll(cu, tm - 1, 0))
    nxt = jnp.where(pos == seq_len - 1, 0.0, nxt)
    conv = prev * ck_ref[0:1, :] + cu * ck_ref[1:2, :] + nxt * ck_ref[2:3, :]
    m = jnp.dot((b_ref[...] * conv).astype(BF16), w_ref[...], preferred_element_type=F32)
    _tail(m, x_ref, mod_ref[...], g2_ref, wr_ref, br_ref, rest[:-1], rest[-1], d)


def _hgrn_out_kernel(o_ref, gout_ref, ng_ref, w_ref, x_ref, mod_ref, g2_ref, wr_ref, br_ref, *rest, d):
    parts = []
    for h in range(HG_HEADS):
        oh = o_ref[:, h * LANES:(h + 1) * LANES]
        parts.append(oh * lax.rsqrt(jnp.mean(oh * oh, axis=-1, keepdims=True) + EPS))
    gout = gout_ref[...]
    y = (jnp.concatenate(parts, axis=1) * ng_ref[...]) * (gout * jax.nn.sigmoid(gout))
    m = jnp.dot(y.astype(BF16), w_ref[...], preferred_element_type=F32)
    _tail(m, x_ref, mod_ref[...], g2_ref, wr_ref, br_ref, rest[:-1], rest[-1], d)


def _mixer_out(kind, acts, w_out, x, mods_l, g2, w_router, b_router, geom, extra=()):
    t, d = x.shape
    tm = TOKEN_TILE
    n_prompt, dec_seq, seq = geom
    grp = functools.partial(_group_of_tile, tile=tm, n_prompt=n_prompt, dec_seq=dec_seq)
    row_spec = pl.BlockSpec((tm, d), lambda i: (i, 0))
    wr = jnp.zeros((d, LANES), F32).at[:, :N_EXPERTS].set(w_router)
    wr_top = _bf16_part(wr)
    wr = jnp.stack([wr_top.astype(BF16), (wr - wr_top).astype(BF16)])
    br = jnp.full((1, LANES), NEG_BIG, F32).at[0, :N_EXPERTS].set(b_router)
    common_specs = [pl.BlockSpec((d, d), lambda i: (0, 0)),
                    row_spec,
                    pl.BlockSpec((None, 1, 6 * d), lambda i: (grp(i), 0, 0)),
                    pl.BlockSpec((1, d), lambda i: (0, 0)),
                    pl.BlockSpec((2, d, LANES), lambda i: (0, 0, 0)),
                    pl.BlockSpec((1, LANES), lambda i: (0, 0))]
    common_args = [w_out, x, mods_l, g2, wr, br]
    if kind == "plain":
        body = functools.partial(_plain_out_kernel, d=d)
        specs = [row_spec] + common_specs
        args = list(acts) + common_args
    elif kind == "conv":
        body = functools.partial(_conv_out_kernel, d=d, tm=tm, n_prompt=n_prompt, seq=seq, dec_seq=dec_seq)
        per = tm // SUBLANES
        last = t // SUBLANES - 1
        specs = [row_spec, row_spec,
                 pl.BlockSpec((SUBLANES, d), lambda i: (jnp.maximum(i * per - 1, 0), 0)),
                 pl.BlockSpec((SUBLANES, d), lambda i: (jnp.minimum((i + 1) * per, last), 0)),
                 pl.BlockSpec((CONV_WIDTH, d), lambda i: (0, 0))] + common_specs
        b, cu = acts
        args = [b, cu, cu, cu, extra[0]] + common_args
    else:
        body = functools.partial(_hgrn_out_kernel, d=d)
        specs = [row_spec, row_spec, pl.BlockSpec((1, d), lambda i: (0, 0))] + common_specs
        args = list(acts) + [extra[0]] + common_args
    lane_spec = pl.BlockSpec((tm, LANES), lambda i: (i, 0))
    return pl.pallas_call(
        body,
        grid=(t // tm,),
        in_specs=specs,
        out_specs=[row_spec, pl.BlockSpec((PACKED_CHUNKS, tm, CHUNK_W), lambda i: (0, i, 0)),
                   lane_spec, lane_spec, lane_spec, pl.BlockSpec((1, LANES), lambda i: (0, 0))],
        out_shape=[jax.ShapeDtypeStruct((t, d), F32), jax.ShapeDtypeStruct((PACKED_CHUNKS, t, CHUNK_W), jnp.int32),
                   jax.ShapeDtypeStruct((t, LANES), jnp.int32), jax.ShapeDtypeStruct((t, LANES), F32),
                   jax.ShapeDtypeStruct((t, LANES), jnp.int32), jax.ShapeDtypeStruct((1, LANES), F32)],
        scratch_shapes=[pltpu.VMEM((1, LANES), F32)],
        compiler_params=_params(("arbitrary",), 40),
        name="mixer_out_" + kind,
    )(*args)


def _softmax_rows(parts):
    mx = functools.reduce(jnp.maximum, [jnp.max(s, axis=-1, keepdims=True) for s in parts])
    es = [jnp.exp(s - mx) for s in parts]
    den = functools.reduce(lambda a, b: a + b, [jnp.sum(e, axis=-1, keepdims=True) for e in es])
    return [e / den for e in es]


def _head_pair_queries(q, dh):
    first = lax.broadcasted_iota(jnp.int32, (1, q.shape[1]), 1) < dh
    return first, (jnp.where(first, q, 0.0).astype(BF16), jnp.where(first, 0.0, q).astype(BF16))


def _na_ctx_kernel(q_ref, k_ref, v_ref, o_ref, *, dh, scale):
    nt = (((1,), (1,)), ((), ()))
    for pair in range(NA_HEADS * dh // LANES):
        sl = slice(pair * LANES, (pair + 1) * LANES)
        first, queries = _head_pair_queries(q_ref[:, sl], dh)
        kb = k_ref[:, sl].astype(BF16)
        vb = v_ref[:, sl].astype(BF16)
        outs = []
        for qh in queries:
            (p,) = _softmax_rows([lax.dot_general(qh, kb, nt, preferred_element_type=F32) * scale])
            outs.append(jnp.dot(p.astype(BF16), vb, preferred_element_type=F32))
        o_ref[:, sl] = jnp.where(first, outs[0], outs[1])


def _na_context(q, k, v, batch, seq):
    d = q.shape[1]
    dh = d // NA_HEADS
    spec = pl.BlockSpec((seq, d), lambda b: (b, 0))
    return pl.pallas_call(
        functools.partial(_na_ctx_kernel, dh=dh, scale=dh ** -0.5),
        grid=(batch,),
        in_specs=[spec, spec, spec],
        out_specs=spec,
        out_shape=jax.ShapeDtypeStruct((batch * seq, d), F32),
        compiler_params=_params(("arbitrary",), 32),
        name="na_context",
    )(q, k, v)


def _na_lat_kernel(*refs, dh, scale, kr):
    q_ref = refs[0]
    k_refs = refs[1:1 + kr]
    v_refs = refs[1 + kr:1 + 2 * kr]
    ck_ref, cv_ref, bias_ref, o_ref = refs[1 + 2 * kr:]
    nt = (((1,), (1,)), ((), ()))
    heads_per_group = LANES // dh
    for pair in range(NA_HEADS // heads_per_group):
        sl = slice(pair * LANES, (pair + 1) * LANES)
        first, queries = _head_pair_queries(q_ref[:, sl], dh)
        kw = jnp.concatenate([r[:, sl] for r in k_refs], axis=0).astype(BF16)
        vw = jnp.concatenate([r[:, sl] for r in v_refs], axis=0).astype(BF16)
        ckb = ck_ref[:, sl].astype(BF16)
        cvb = cv_ref[:, sl].astype(BF16)
        outs = []
        for j, qh in enumerate(queries):
            s_nb = (lax.dot_general(qh, kw, nt, preferred_element_type=F32) * scale
                    + bias_ref[pair * heads_per_group + j])
            s_cx = lax.dot_general(qh, ckb, nt, preferred_element_type=F32) * scale
            p_nb, p_cx = _softmax_rows([s_nb, s_cx])
            outs.append(jnp.dot(p_nb.astype(BF16), vw, preferred_element_type=F32)
                        + jnp.dot(p_cx.astype(BF16), cvb, preferred_element_type=F32))
        o_ref[:, sl] = jnp.where(first, outs[0], outs[1])


def _na_bias_table(rpb, kr):
    col = np.arange(GRID_W)
    col_start = np.clip(col - NA_COLS // 2, 0, GRID_W - NA_COLS)
    delta = col[None, :] - col[:, None] + (NA_COLS - 1)
    inside = (col[None, :] >= col_start[:, None]) & (col[None, :] < col_start[:, None] + NA_COLS)
    pick = (np.arange(2 * NA_COLS - 1)[:, None, None] == delta[None]).astype(np.float32)
    tab = jnp.einsum("hrk,kwc->hrwc", rpb, jnp.asarray(pick), precision=HIGHEST)
    tab = jnp.where(jnp.asarray(inside)[None, None], tab, NEG_BIG)
    out = []
    for d0 in range(NA_ROWS_MAX):
        rows = tab[:, d0:d0 + kr]
        out.append(rows.transpose(0, 2, 1, 3).reshape(NA_HEADS, GRID_W, kr * GRID_W))
    return jnp.stack(out, axis=0)


def _na_latent(q, k, v, ctx_k, ctx_v, rpb, n_prompt, dec_batch, dec_seq):
    d = q.shape[1]
    dh = d // NA_HEADS
    rows = dec_seq // GRID_W
    kr = min(NA_ROWS_MAX, rows)
    base = n_prompt // GRID_W
    past = ctx_k.shape[1]
    bias = _na_bias_table(rpb, kr)

    def row_start(r):
        return jnp.clip(r - kr // 2, 0, rows - kr)

    def win_spec(j):
        return pl.BlockSpec((GRID_W, d), lambda b, r: (base + b * rows + row_start(r) + j, 0))

    in_specs = ([pl.BlockSpec((GRID_W, d), lambda b, r: (base + b * rows + r, 0))]
                + [win_spec(j) for j in range(kr)] * 2
                + [pl.BlockSpec((None, past, d), lambda b, r: (b, 0, 0))] * 2
                + [pl.BlockSpec((None, NA_HEADS, GRID_W, kr * GRID_W),
                                lambda b, r: (row_start(r) - r + NA_ROWS_MAX - 1, 0, 0, 0))])
    return pl.pallas_call(
        functools.partial(_na_lat_kernel, dh=dh, scale=dh ** -0.5, kr=kr),
        grid=(dec_batch, rows),
        in_specs=in_specs,
        out_specs=pl.BlockSpec((GRID_W, d), lambda b, r: (b * rows + r, 0)),
        out_shape=jax.ShapeDtypeStruct((dec_batch * dec_seq, d), F32),
        compiler_params=_params(("arbitrary", "arbitrary"), 40),
        name="na_latent",
    )(q, *([k] * kr), *([v] * kr), ctx_k.reshape(dec_batch, past, d), ctx_v.reshape(dec_batch, past, d), bias)


def _hgrn_scan_kernel(*refs, seq_len, has_s0, want_state):
    q_ref, v_ref, z0_ref, z1_ref, lb_ref, tri_ref = refs[:6]
    rest = list(refs[6:])
    s0_ref = rest.pop(0) if has_s0 else None
    o_ref = rest.pop(0)
    sf_ref = rest.pop(0) if want_state else None
    ob_ref = rest.pop(0)
    c, r = HG_CHUNK, HG_TILE
    nc, nt = r // c, seq_len // r
    dk = HG_KEY_DIM
    tpos8 = lax.broadcasted_iota(jnp.int32, (1, SUBLANES, 1), 1)
    row = lax.broadcasted_iota(jnp.int32, (r, r), 0)
    col = lax.broadcasted_iota(jnp.int32, (r, r), 1)
    same16 = (row // (2 * SUBLANES)) == (col // (2 * SUBLANES))
    same32 = (row // c) == (col // c)
    nt_dims = (((1,), (1,)), ((), ()))
    tn_dims = (((0,), (0,)), ((), ()))

    def tile(dr, i, st):
        z_ref = (z0_ref, z1_ref)[dr]
        lb = lb_ref[dr:dr + 1, :]
        tri = tri_ref[dr]
        ti = i if dr == 0 else nt - 1 - i
        r0 = pl.multiple_of(ti * r, r)
        q = q_ref[pl.ds(r0, r), :]
        v = v_ref[pl.ds(r0, r), :]
        z = z_ref[pl.ds(r0, r), :]
        g = jnp.log(lb + (1.0 - lb) * jax.nn.sigmoid(z))
        k = (1.0 - lb) * jax.nn.sigmoid(-z)
        b = jnp.dot(tri, g, precision=HIGHEST, preferred_element_type=F32)
        b3, q3, k3, v3 = (a.reshape(nc, c, dk) for a in (b, q, k, v))
        nb = r // SUBLANES
        b8, q8, k8, v8 = (a.reshape(nb, SUBLANES, dk) for a in (b, q, k, v))
        oi = jnp.zeros((nb, SUBLANES, dk), F32)
        for s in range(SUBLANES):
            keep = (tpos8 >= s) if dr == 0 else (tpos8 <= s)
            e = jnp.exp(jnp.where(keep, b8 - b8[:, s:s + 1, :], NEG_BIG))
            a = jnp.sum(q8 * e * k8[:, s:s + 1, :], axis=-1, keepdims=True)
            oi = oi + a * v8[:, s:s + 1, :]
        att = None
        for span, same_group in ((2 * SUBLANES, same16), (c, same32)):
            half = span // 2
            bg = b.reshape(r // span, span, dk)
            pos = lax.broadcasted_iota(jnp.int32, (1, span, 1), 1)
            edge = bg[:, half - 1:half, :] if dr == 0 else bg[:, half:half + 1, :]
            is_query = (pos >= half) if dr == 0 else (pos < half)
            qf = q.reshape(bg.shape) * jnp.exp(jnp.where(is_query, bg - edge, NEG_BIG))
            kf = k.reshape(bg.shape) * jnp.exp(jnp.where(is_query, NEG_BIG, edge - bg))
            part = lax.dot_general(qf.reshape(r, dk).astype(BF16), kf.reshape(r, dk).astype(BF16), nt_dims,
                                   preferred_element_type=F32)
            part = jnp.where(same_group, part, 0.0)
            att = part if att is None else att + part
        oi = oi.reshape(r, dk) + jnp.dot(att.astype(BF16), v.astype(BF16), preferred_element_type=F32)
        oi = oi.reshape(nc, c, dk)
        bl = b3[:, c - 1:c, :] if dr == 0 else b3[:, 0:1, :]
        qt = (q3 * jnp.exp(b3)).astype(BF16)
        kt = (k3 * jnp.exp(bl - b3)).astype(BF16)
        vb = v3.astype(BF16)
        outs = [None] * nc
        for ci in (range(nc) if dr == 0 else range(nc - 1, -1, -1)):
            oc = lax.dot_general(qt[ci], st.astype(BF16), nt_dims, preferred_element_type=F32)
            outs[ci] = oi[ci] + oc
            kv = lax.dot_general(vb[ci], kt[ci], tn_dims, preferred_element_type=F32)
            st = st * jnp.exp(bl[ci]) + kv
        return r0, jnp.concatenate(outs, axis=0), st

    def step(i, carry):
        st_f, st_b = carry
        r_f, o_f, st_f = tile(0, i, st_f)
        r_b, o_b, st_b = tile(1, i, st_b)
        o_ref[pl.ds(r_f, r), :] = o_f
        ob_ref[pl.ds(r_b, r), :] = o_b
        return st_f, st_b

    zero = jnp.zeros((dk, dk), F32)
    st_f, st_b = lax.fori_loop(0, nt, step, (s0_ref[0], s0_ref[1]) if has_s0 else (zero, zero))
    o_ref[...] += ob_ref[...]
    if want_state:
        sf_ref[0] = st_f.T
        sf_ref[1] = st_b.T


def _hgrn_tri():
    t = np.arange(HG_TILE)
    same = (t[:, None] // HG_CHUNK) == (t[None, :] // HG_CHUNK)
    fwd = same & (t[None, :] <= t[:, None])
    bwd = same & (t[None, :] >= t[:, None])
    return jnp.asarray(np.stack([fwd, bwd]).astype(np.float32))


def _hgrn_scan(q, v, z0, z1, lb, row0, n_seq, seq_len, s0_t=None, want_state=False):
    d = q.shape[1]
    dk = HG_KEY_DIM
    blk0 = row0 // seq_len
    seq_spec = pl.BlockSpec((seq_len, dk), lambda s, h: (blk0 + s, h))
    st_spec = pl.BlockSpec((None, 2, None, dk, dk), lambda s, h: (s, 0, h, 0, 0))
    in_specs = [seq_spec] * 4 + [pl.BlockSpec((2, dk), lambda s, h: (0, h)),
                                 pl.BlockSpec((2, HG_TILE, HG_TILE), lambda s, h: (0, 0, 0))]
    args = [q, v, z0, z1, lb, _hgrn_tri()]
    if s0_t is not None:
        in_specs.append(st_spec)
        args.append(s0_t)
    out_specs = [pl.BlockSpec((seq_len, dk), lambda s, h: (s, h))]
    out_shape = [jax.ShapeDtypeStruct((n_seq * seq_len, d), F32)]
    if want_state:
        out_specs.append(st_spec)
        out_shape.append(jax.ShapeDtypeStruct((n_seq, 2, HG_HEADS, dk, dk), F32))
    res = pl.pallas_call(
        functools.partial(_hgrn_scan_kernel, seq_len=seq_len, has_s0=s0_t is not None, want_state=want_state),
        grid=(n_seq, HG_HEADS),
        in_specs=in_specs,
        out_specs=out_specs,
        out_shape=out_shape,
        scratch_shapes=[pltpu.VMEM((seq_len, dk), F32)],
        compiler_params=_params(("arbitrary", "arbitrary"), 32),
        name="hgrn_scan",
    )(*args)
    return res if want_state else (res[0], None)


def _route(idx, rank, counts, t):
    n = t * TOP_K
    rb = MOE_ROWS
    n_blocks = -(-n // rb) + N_EXPERTS
    n_rows = (n_blocks + N_EXPERTS) * rb
    experts = jnp.arange(N_EXPERTS, dtype=jnp.int32)
    counts = counts[0, :N_EXPERTS].astype(jnp.int32)
    padded = (counts + rb - 1) // rb * rb
    padded_end = jnp.cumsum(padded)
    start = padded_end - padded
    e = idx[:, :TOP_K]
    dest = jnp.sum(jnp.where(e[:, :, None] == experts, start, 0), axis=-1) + rank[:, :TOP_K]
    dest_km = dest.T.astype(jnp.int32)
    q = jnp.arange(rb, dtype=jnp.int32)
    spare = n_blocks * rb + experts[:, None] * rb + q[None, :]
    pad_rows = jnp.where(q[None, :] < (padded - counts)[:, None], (start + counts)[:, None] + q[None, :], spare)
    per_chunk = jnp.concatenate([dest_km.reshape(-1), pad_rows.reshape(-1)])
    sc_index = (jnp.arange(ROW_CHUNKS, dtype=jnp.int32)[:, None] * n_rows + per_chunk[None, :]).reshape(1, -1)
    block_start = jnp.arange(n_blocks, dtype=jnp.int32) * rb
    block_expert = jnp.minimum(jnp.sum((padded_end[None, :] <= block_start[:, None]).astype(jnp.int32), axis=1),
                               N_EXPERTS - 1)
    n_used = (padded_end[-1] // rb).reshape(1)
    return sc_index, block_expert, n_used, n_rows


def _sc_mesh():
    return plsc.VectorSubcoreMesh(core_axis_name="c", subcore_axis_name="s")


def _sc_dispatch(h2c, sc_index, t, n_rows):
    win = SC_WINDOW
    tw = t // win
    nw = (t * TOP_K + N_EXPERTS * MOE_ROWS) // win
    n_chunks = h2c.shape[0]

    def scatter(x_hbm, i_hbm, o_hbm):
        def body(x_vmem, i_vmem):
            pltpu.sync_copy(x_vmem, o_hbm.at[i_vmem.at[0]])

        pltpu.emit_pipeline(
            body,
            grid=(n_chunks * nw,),
            in_specs=[pl.BlockSpec((win, CHUNK_W), index_map=lambda i: ((i // nw) * tw + (i % nw) % tw, 0)),
                      pl.BlockSpec((1, win), index_map=lambda i: (0, i))],
            out_specs=[],
            core_axis_name=("c", "s"),
            dimension_semantics=(pltpu.PARALLEL,),
        )(x_hbm, i_hbm)

    out = pl.kernel(scatter, out_type=jax.ShapeDtypeStruct((n_chunks * n_rows, CHUNK_W), h2c.dtype),
                    mesh=_sc_mesh(), name="moe_dispatch")(h2c.reshape(n_chunks * t, CHUNK_W), sc_index)
    return out.reshape(n_chunks, n_rows, CHUNK_W)


def _sc_collect(y_rows, sc_index, t, n_rows):
    win = SC_WINDOW
    nw = (t * TOP_K + N_EXPERTS * MOE_ROWS) // win
    aw = t * TOP_K // win

    def gather(y_hbm, i_hbm, o_hbm):
        def body(i_vmem, o_vmem):
            pltpu.sync_copy(y_hbm.at[i_vmem.at[0]], o_vmem)

        pltpu.emit_pipeline(
            body,
            grid=(ROW_CHUNKS * aw,),
            in_specs=[pl.BlockSpec((1, win), index_map=lambda i: (0, (i // aw) * nw + i % aw))],
            out_specs=[pl.BlockSpec((win, CHUNK_W), index_map=lambda i: (i, 0))],
            core_axis_name=("c", "s"),
            dimension_semantics=(pltpu.PARALLEL,),
        )(i_hbm, o_hbm)

    out = pl.kernel(gather, out_type=jax.ShapeDtypeStruct((ROW_CHUNKS * TOP_K * t, CHUNK_W), F32),
                    mesh=_sc_mesh(), name="moe_collect")(y_rows.reshape(ROW_CHUNKS * n_rows, CHUNK_W), sc_index)
    return out.reshape(ROW_CHUNKS, TOP_K, t, CHUNK_W)


def _moe_kernel(be_ref, nu_ref, x_ref, w1_ref, b1_ref, w2_ref, b2_ref, o_ref, w1b, w2b, *, dff):
    i = pl.program_id(0)
    d = w1_ref.shape[0]
    e = be_ref[i]
    prev = be_ref[jnp.maximum(i - 1, 0)]

    @pl.when(jnp.logical_or(i == 0, e != prev))
    def _():
        w1b[...] = w1_ref[...].astype(BF16)
        w2b[...] = w2_ref[...].astype(BF16)

    @pl.when(i < nu_ref[0])
    def _():
        words = [lax.bitcast_convert_type(x_ref[j], jnp.uint32) for j in range(PACKED_CHUNKS)]
        parts = [w & jnp.uint32(0xFFFF0000) for w in words] + [w << 16 for w in words]
        xb = jnp.concatenate([lax.bitcast_convert_type(p, F32).astype(BF16) for p in parts], axis=1)
        gu = jnp.dot(xb, w1b[...], preferred_element_type=F32) + b1_ref[...]
        g = jnp.minimum(gu[:, :dff], SWIGLU_LIMIT)
        u = jnp.clip(gu[:, dff:], -SWIGLU_LIMIT, SWIGLU_LIMIT)
        a = (u + 1.0) * (g * jax.nn.sigmoid(SWIGLU_ALPHA * g))
        y = jnp.dot(a.astype(BF16), w2b[...], preferred_element_type=F32) + b2_ref[...]
        for j in range(ROW_CHUNKS):
            o_ref[j] = y[:, j * CHUNK_W:(j + 1) * CHUNK_W]

    @pl.when(i >= nu_ref[0])
    def _():
        o_ref[...] = jnp.zeros(o_ref.shape, F32)


def _moe_experts(x_rows, block_expert, n_used, layer, w1, b1, w2, b2):
    _, n_rows, _ = x_rows.shape
    rb = MOE_ROWS
    n_blocks = n_rows // rb - N_EXPERTS
    depth, n_e, d, dff2 = w1.shape
    dff = dff2 // 2
    row_spec = pl.BlockSpec((PACKED_CHUNKS, rb, CHUNK_W), lambda i, be, nu: (0, jnp.minimum(i, nu[0] - 1), 0))
    grid_spec = pltpu.PrefetchScalarGridSpec(
        num_scalar_prefetch=2,
        grid=(n_blocks,),
        in_specs=[row_spec,
                  pl.BlockSpec((None, None, d, dff2), lambda i, be, nu: (layer, be[i], 0, 0)),
                  pl.BlockSpec((None, None, 1, dff2), lambda i, be, nu: (layer, be[i], 0, 0)),
                  pl.BlockSpec((None, None, dff, d), lambda i, be, nu: (layer, be[i], 0, 0)),
                  pl.BlockSpec((None, None, 1, d), lambda i, be, nu: (layer, be[i], 0, 0))],
        out_specs=pl.BlockSpec((ROW_CHUNKS, rb, CHUNK_W), lambda i, be, nu: (0, i, 0)),
        scratch_shapes=[pltpu.VMEM((d, dff2), BF16), pltpu.VMEM((dff, d), BF16)],
    )
    return pl.pallas_call(
        functools.partial(_moe_kernel, dff=dff),
        grid_spec=grid_spec,
        out_shape=jax.ShapeDtypeStruct((ROW_CHUNKS, n_rows, CHUNK_W), F32),
        compiler_params=_params(("arbitrary",), 56),
        name="moe_experts",
    )(block_expert, n_used, x_rows, w1, b1.reshape(depth, n_e, 1, dff2), w2, b2.reshape(depth, n_e, 1, d))


def _combine_kernel(y_ref, gw_ref, x_ref, mod_ref, o_ref, *, d):
    gw = gw_ref[...]
    for j in range(ROW_CHUNKS):
        cols = slice(j * CHUNK_W, (j + 1) * CHUNK_W)
        acc = gw[:, 0:1] * y_ref[j, 0]
        for k in range(1, TOP_K):
            acc = acc + gw[:, k:k + 1] * y_ref[j, k]
        o_ref[:, cols] = x_ref[:, cols] + mod_ref[:, 5 * d + j * CHUNK_W:5 * d + (j + 1) * CHUNK_W] * acc


def _moe_combine(y_slots, gw, x1, mods_l, geom):
    t, d = x1.shape
    tm = TOKEN_TILE
    grp = functools.partial(_group_of_tile, tile=tm, n_prompt=geom[0], dec_seq=geom[1])
    return pl.pallas_call(
        functools.partial(_combine_kernel, d=d),
        grid=(t // tm,),
        in_specs=[pl.BlockSpec((ROW_CHUNKS, TOP_K, tm, CHUNK_W), lambda i: (0, 0, i, 0)),
                  pl.BlockSpec((tm, LANES), lambda i: (i, 0)),
                  pl.BlockSpec((tm, d), lambda i: (i, 0)),
                  pl.BlockSpec((None, 1, 6 * d), lambda i: (grp(i), 0, 0))],
        out_specs=pl.BlockSpec((tm, d), lambda i: (i, 0)),
        out_shape=jax.ShapeDtypeStruct((t, d), F32),
        compiler_params=_params(("arbitrary",), 24),
        name="moe_combine",
    )(y_slots, gw, x1, mods_l)


def _final_norm_kernel(x_ref, g_ref, o_ref):
    o_ref[...] = _rms(x_ref[...]) * g_ref[...]


def _final_norm(x, g):
    t, d = x.shape
    tm = TOKEN_TILE
    return pl.pallas_call(
        _final_norm_kernel,
        grid=(t // tm,),
        in_specs=[pl.BlockSpec((tm, d), lambda i: (i, 0)), pl.BlockSpec((1, d), lambda i: (0, 0))],
        out_specs=pl.BlockSpec((tm, d), lambda i: (i, 0)),
        out_shape=jax.ShapeDtypeStruct((t, d), F32),
        compiler_params=_params(("arbitrary",), 16),
        name="final_norm",
    )(x, g)


def kernel(x_prompt, x_sample, c, c_ctx, cache_na_k, cache_na_v, state_hgrn, norm_mix, norm_ffn, w_mod, b_mod,
           conv_w_in, conv_k, conv_w_out, na_w_qkv, na_rpb, na_w_o, hg_w_qig, hg_w_f, hg_lb, hg_norm, hg_w_o,
           moe_w_router, moe_b_router, moe_w1, moe_b1, moe_w2, moe_b2, final_norm):
    batch, seq, d = x_prompt.shape
    dec_batch, dec_seq, _ = x_sample.shape
    depth = w_mod.shape[0]
    n_prompt = batch * seq
    t = n_prompt + dec_batch * dec_seq
    geom = (n_prompt, dec_seq, seq)
    assert seq % TOKEN_TILE == 0 and dec_seq % TOKEN_TILE == 0 and seq % HG_TILE == 0
    assert n_prompt % dec_seq == 0 and dec_seq % GRID_W == 0 and 1 + dec_batch <= SUBLANES
    assert d == ROW_CHUNKS * CHUNK_W and t % SC_WINDOW == 0 and MOE_ROWS % SC_WINDOW == 0

    x = jnp.concatenate([x_prompt.reshape(n_prompt, d), x_sample.reshape(dec_batch * dec_seq, d)], axis=0)
    cvecs = jnp.zeros((SUBLANES, d), F32).at[0].set(c_ctx).at[1:1 + dec_batch].set(c)
    mods = _adaln_all(cvecs, w_mod, b_mod)[:, :1 + dec_batch].reshape(depth, 1 + dec_batch, 1, 6 * d)

    lb_soft = jax.nn.softmax(hg_lb.astype(F32), axis=0)
    lower_bounds = jnp.cumsum(lb_soft, axis=0) - lb_soft[0]

    new_k, new_v, new_s = [], [], []
    for l in range(depth):
        kind, j = l % N_MIXERS, l // N_MIXERS
        g1 = norm_mix[l].reshape(1, d)
        g2 = norm_ffn[l].reshape(1, d)
        tail_args = (x, mods[l], g2, moe_w_router[l], moe_b_router[l], geom)
        if kind == 0:
            b, cu = _mixer_in(x, mods[l], g1, conv_w_in[j].astype(BF16), geom, conv=True)
            x1, h2c, idx, gw, rank, counts = _mixer_out("conv", (b, cu), conv_w_out[j].astype(BF16), *tail_args,
                                         extra=(conv_k[j],))
        elif kind == 1:
            q, k, v = _mixer_in(x, mods[l], g1, na_w_qkv[j].astype(BF16), geom)
            o_p = _na_context(q, k, v, batch, seq)
            o_s = _na_latent(q, k, v, cache_na_k[:, j], cache_na_v[:, j], na_rpb[j], n_prompt, dec_batch, dec_seq)
            new_k.append(k[:n_prompt].reshape(batch, seq, NA_HEADS, d // NA_HEADS))
            new_v.append(v[:n_prompt].reshape(batch, seq, NA_HEADS, d // NA_HEADS))
            x1, h2c, idx, gw, rank, counts = _mixer_out("plain", (jnp.concatenate([o_p, o_s], axis=0),),
                                         na_w_o[j].astype(BF16), *tail_args)
        else:
            w_in = jnp.concatenate([hg_w_qig[j], hg_w_f[j, 0], hg_w_f[j, 1]], axis=1).astype(BF16)
            q, v, gout, z0, z1 = _mixer_in(x, mods[l], g1, w_in, geom)
            lb = lower_bounds[l]
            o_p, s_p = _hgrn_scan(q, v, z0, z1, lb, 0, batch, seq, want_state=True)
            s0_t = jnp.swapaxes(state_hgrn[:, j].astype(F32), -1, -2)
            o_s, _ = _hgrn_scan(q, v, z0, z1, lb, n_prompt, dec_batch, dec_seq, s0_t=s0_t)
            new_s.append(s_p)
            x1, h2c, idx, gw, rank, counts = _mixer_out("hgrn", (jnp.concatenate([o_p, o_s], axis=0), gout),
                                         hg_w_o[j].astype(BF16), *tail_args, extra=(hg_norm[j].reshape(1, d),))
        sc_index, block_expert, n_used, n_rows = _route(idx, rank, counts, t)
        x_rows = _sc_dispatch(h2c, sc_index, t, n_rows)
        y_rows = _moe_experts(x_rows, block_expert, n_used, l, moe_w1, moe_b1, moe_w2, moe_b2)
        y_slots = _sc_collect(y_rows, sc_index, t, n_rows)
        x = _moe_combine(y_slots, gw, x1, mods[l], geom)

    y = _final_norm(x, final_norm.reshape(1, d))
    y_prompt = y[:n_prompt].reshape(batch, seq, d)
    y_sample = y[n_prompt:].reshape(dec_batch, dec_seq, d)
    new_na_k = jnp.stack(new_k, axis=1).astype(x_prompt.dtype)
    new_na_v = jnp.stack(new_v, axis=1).astype(x_prompt.dtype)
    new_hgrn_state = jnp.stack(new_s, axis=1).astype(x_prompt.dtype)
    return (y_prompt, y_sample, new_na_k, new_na_v, new_hgrn_state)
```

```python
import functools

import numpy as np
import jax
import jax.numpy as jnp
from jax import lax
from jax.experimental import pallas as pl
from jax.experimental.pallas import tpu as pltpu
from jax.experimental.pallas import tpu_sc as plsc

F32 = jnp.float32
BF16 = jnp.bfloat16
HIGHEST = lax.Precision.HIGHEST

N_MIXERS = 3
GRID_W = 64
CONV_WIDTH = 3
NA_HEADS = 16
NA_ROWS_MAX = 8
NA_COLS = 16
HG_HEADS = 8
HG_KEY_DIM = 128
HG_CHUNK = 32
N_EXPERTS = 32
TOP_K = 4
SWIGLU_LIMIT = 7.0
SWIGLU_ALPHA = 1.702
EPS = 1e-6

LANES = 128
SUBLANES = 8
VMEM_LIMIT_CAP = 60 * 1024 * 1024

NEG_BIG = -1e30
TOKEN_TILE = 256
MOE_ROWS = 512
HG_TILE = 128
PACKED_CHUNKS = 2
CHUNK_W = 256
SC_WINDOW = 128


def _params(sem, vmem_mb, flags=None):
    return pltpu.CompilerParams(dimension_semantics=sem, flags=flags,
                                vmem_limit_bytes=min(vmem_mb * 1024 * 1024, VMEM_LIMIT_CAP))


def _rms(x):
    return x * lax.rsqrt(jnp.mean(x * x, axis=-1, keepdims=True) + EPS)


def _modulate(x, g, shift, scale):
    return (_rms(x) * g) * (1.0 + scale) + shift


def _bf16_part(x):
    bits = lax.bitcast_convert_type(x, jnp.uint32) & jnp.uint32(0xFFFF0000)
    return lax.bitcast_convert_type(bits, F32)


def _pack_bf16_pairs(x):
    n = x.shape[1]
    bits = lax.bitcast_convert_type(x, jnp.uint32)
    top = (bits + (jnp.uint32(0x7FFF) + ((bits >> 16) & jnp.uint32(1)))) & jnp.uint32(0xFFFF0000)
    return lax.bitcast_convert_type(top[:, :n // 2] | (top[:, n // 2:] >> 16), jnp.int32)


def _unpack_bf16_pairs(words):
    bits = lax.bitcast_convert_type(words, jnp.uint32)
    return (lax.bitcast_convert_type(bits & jnp.uint32(0xFFFF0000), F32),
            lax.bitcast_convert_type(bits << 16, F32))


def _group_of_tile(i, tile, n_prompt, dec_seq):
    start = i * tile
    return jnp.where(start < n_prompt, 0, 1 + (start - n_prompt) // dec_seq)


def _adaln_kernel(c_ref, w_ref, b_ref, o_ref):
    cv = c_ref[...]
    s = cv * jax.nn.sigmoid(cv)
    o_ref[...] = jnp.dot(s, w_ref[...], precision=HIGHEST, preferred_element_type=F32) + b_ref[...]


def _adaln_all(cvecs, w_mod, b_mod):
    depth, d, n = w_mod.shape
    tn = 2048
    return pl.pallas_call(
        _adaln_kernel,
        grid=(depth, n // tn),
        in_specs=[pl.BlockSpec((SUBLANES, d), lambda l, j: (0, 0)),
                  pl.BlockSpec((None, d, tn), lambda l, j: (l, 0, j)),
                  pl.BlockSpec((None, 1, tn), lambda l, j: (l, 0, j))],
        out_specs=pl.BlockSpec((None, SUBLANES, tn), lambda l, j: (l, 0, j)),
        out_shape=jax.ShapeDtypeStruct((depth, SUBLANES, n), F32),
        compiler_params=_params(("arbitrary", "arbitrary"), 40),
        name="adaln",
    )(cvecs, w_mod, b_mod.reshape(depth, 1, n))


def _in_kernel(x_ref, mod_ref, g_ref, w_ref, *out_refs, d, conv):
    mod = mod_ref[...]
    h = _modulate(x_ref[...], g_ref[...], mod[:, 0:d], mod[:, d:2 * d]).astype(BF16)
    if conv:
        b_ref, cu_ref = out_refs
        b_ref[...] = jnp.dot(h, w_ref[:, 0:d], preferred_element_type=F32)
        c = jnp.dot(h, w_ref[:, d:2 * d], preferred_element_type=F32)
        u = jnp.dot(h, w_ref[:, 2 * d:3 * d], preferred_element_type=F32)
        cu_ref[...] = c * u
    else:
        for j, o_ref in enumerate(out_refs):
            o_ref[...] = jnp.dot(h, w_ref[:, j * d:(j + 1) * d], preferred_element_type=F32)


def _mixer_in(x, mods_l, g, w, geom, conv=False):
    t, d = x.shape
    n = w.shape[1]
    n_out = 2 if conv else n // d
    tm = TOKEN_TILE
    grp = functools.partial(_group_of_tile, tile=tm, n_prompt=geom[0], dec_seq=geom[1])
    return pl.pallas_call(
        functools.partial(_in_kernel, d=d, conv=conv),
        grid=(t // tm,),
        in_specs=[pl.BlockSpec((tm, d), lambda i: (i, 0)),
                  pl.BlockSpec((None, 1, 6 * d), lambda i: (grp(i), 0, 0)),
                  pl.BlockSpec((1, d), lambda i: (0, 0)),
                  pl.BlockSpec((d, n), lambda i: (0, 0))],
        out_specs=[pl.BlockSpec((tm, d), lambda i: (i, 0))] * n_out,
        out_shape=[jax.ShapeDtypeStruct((t, d), F32)] * n_out,
        compiler_params=_params(("arbitrary",), 48),
        name="mixer_in",
    )(x, mods_l, g, w)


def _tail(m, x_ref, mod, g2_ref, wr_ref, br_ref, outs, run_ref, d):
    x1_ref, h2_ref, idx_ref, gw_ref, rank_ref, cnt_ref = outs
    x1 = x_ref[...] + mod[:, 2 * d:3 * d] * m
    x1_ref[...] = x1
    h2 = _modulate(x1, g2_ref[...], mod[:, 3 * d:4 * d], mod[:, 4 * d:5 * d])
    words = _pack_bf16_pairs(h2)
    for j in range(PACKED_CHUNKS):
        h2_ref[j] = words[:, j * CHUNK_W:(j + 1) * CHUNK_W]
    h2_top = _bf16_part(h2)
    h2_hi = h2_top.astype(BF16)
    h2_lo = (h2 - h2_top).astype(BF16)
    logits = (jnp.dot(h2_hi, wr_ref[0], preferred_element_type=F32)
              + jnp.dot(h2_lo, wr_ref[0], preferred_element_type=F32)
              + jnp.dot(h2_hi, wr_ref[1], preferred_element_type=F32)) + br_ref[...]
    lane = lax.broadcasted_iota(jnp.int32, logits.shape, 1)
    vals, idxs = [], []
    for _ in range(TOP_K):
        mx = jnp.max(logits, axis=-1, keepdims=True)
        ix = jnp.argmax(logits, axis=-1, keepdims=True).astype(jnp.int32)
        vals.append(mx)
        idxs.append(ix)
        logits = jnp.where(lane == ix, NEG_BIG, logits)
    es = [jnp.exp(v - vals[0]) for v in vals]
    den = es[0] + es[1] + es[2] + es[3]

    @pl.when(pl.program_id(0) == 0)
    def _():
        run_ref[...] = jnp.zeros(run_ref.shape, F32)

    tm = lane.shape[0]
    earlier = (lax.broadcasted_iota(jnp.int32, (tm, tm), 0)
               > lax.broadcasted_iota(jnp.int32, (tm, tm), 1)).astype(BF16)
    seen = run_ref[...]
    idx_out = jnp.zeros(lane.shape, jnp.int32)
    gw_out = jnp.zeros(lane.shape, F32)
    rank_out = jnp.zeros(lane.shape, jnp.int32)
    for k in range(TOP_K):
        hit = lane == idxs[k]
        before = jnp.dot(earlier, hit.astype(BF16), preferred_element_type=F32) + seen
        rank_k = jnp.sum(jnp.where(hit, before, 0.0), axis=-1, keepdims=True)
        seen = seen + jnp.sum(hit.astype(F32), axis=0, keepdims=True)
        idx_out = jnp.where(lane == k, idxs[k], idx_out)
        gw_out = jnp.where(lane == k, es[k] / den, gw_out)
        rank_out = jnp.where(lane == k, rank_k.astype(jnp.int32), rank_out)
    run_ref[...] = seen
    cnt_ref[...] = seen
    idx_ref[...] = idx_out
    gw_ref[...] = gw_out
    rank_ref[...] = rank_out


def _plain_out_kernel(a_ref, w_ref, x_ref, mod_ref, g2_ref, wr_ref, br_ref, *rest, d):
    m = jnp.dot(a_ref[...].astype(BF16), w_ref[...], preferred_element_type=F32)
    _tail(m, x_ref, mod_ref[...], g2_ref, wr_ref, br_ref, rest[:-1], rest[-1], d)


def _conv_out_kernel(b_ref, cu_ref, prev_ref, next_ref, ck_ref, w_ref, x_ref, mod_ref, g2_ref, wr_ref,
                     br_ref, *rest, d, tm, n_prompt, seq, dec_seq):
    i = pl.program_id(0)
    cu = cu_ref[...]
    row = lax.broadcasted_iota(jnp.int32, (tm, 1), 0)
    start = i * tm
    in_prompt = start < n_prompt
    seq_len = jnp.where(in_prompt, seq, dec_seq)
    pos = lax.rem(jnp.where(in_prompt, start, start - n_prompt), seq_len) + row
    prev = jnp.where(row == 0, prev_ref[SUBLANES - 1:SUBLANES, :], pltpu.roll(cu, 1, 0))
    prev = jnp.where(pos == 0, 0.0, prev)
    nxt = jnp.where(row == tm - 1, next_ref[0:1, :], pltpu.roll(cu, tm - 1, 0))
    nxt = jnp.where(pos == seq_len - 1, 0.0, nxt)
    conv = prev * ck_ref[0:1, :] + cu * ck_ref[1:2, :] + nxt * ck_ref[2:3, :]
    m = jnp.dot((b_ref[...] * conv).astype(BF16), w_ref[...], preferred_element_type=F32)
    _tail(m, x_ref, mod_ref[...], g2_ref, wr_ref, br_ref, rest[:-1], rest[-1], d)


def _hgrn_out_kernel(o_ref, gout_ref, ng_ref, w_ref, x_ref, mod_ref, g2_ref, wr_ref, br_ref, *rest, d):
    parts = []
    for h in range(HG_HEADS):
        oh = o_ref[:, h * LANES:(h + 1) * LANES]
        parts.append(oh * lax.rsqrt(jnp.mean(oh * oh, axis=-1, keepdims=True) + EPS))
    gout = gout_ref[...]
    y = (jnp.concatenate(parts, axis=1) * ng_ref[...]) * (gout * jax.nn.sigmoid(gout))
    m = jnp.dot(y.astype(BF16), w_ref[...], preferred_element_type=F32)
    _tail(m, x_ref, mod_ref[...], g2_ref, wr_ref, br_ref, rest[:-1], rest[-1], d)


def _mixer_out(kind, acts, w_out, x, mods_l, g2, w_router, b_router, geom, extra=()):
    t, d = x.shape
    tm = TOKEN_TILE
    n_prompt, dec_seq, seq = geom
    grp = functools.partial(_group_of_tile, tile=tm, n_prompt=n_prompt, dec_seq=dec_seq)
    row_spec = pl.BlockSpec((tm, d), lambda i: (i, 0))
    wr = jnp.zeros((d, LANES), F32).at[:, :N_EXPERTS].set(w_router)
    wr_top = _bf16_part(wr)
    wr = jnp.stack([wr_top.astype(BF16), (wr - wr_top).astype(BF16)])
    br = jnp.full((1, LANES), NEG_BIG, F32).at[0, :N_EXPERTS].set(b_router)
    common_specs = [pl.BlockSpec((d, d), lambda i: (0, 0)),
                    row_spec,
                    pl.BlockSpec((None, 1, 6 * d), lambda i: (grp(i), 0, 0)),
                    pl.BlockSpec((1, d), lambda i: (0, 0)),
                    pl.BlockSpec((2, d, LANES), lambda i: (0, 0, 0)),
                    pl.BlockSpec((1, LANES), lambda i: (0, 0))]
    common_args = [w_out, x, mods_l, g2, wr, br]
    if kind == "plain":
        body = functools.partial(_plain_out_kernel, d=d)
        specs = [row_spec] + common_specs
        args = list(acts) + common_args
    elif kind == "conv":
        body = functools.partial(_conv_out_kernel, d=d, tm=tm, n_prompt=n_prompt, seq=seq, dec_seq=dec_seq)
        per = tm // SUBLANES
        last = t // SUBLANES - 1
        specs = [row_spec, row_spec,
                 pl.BlockSpec((SUBLANES, d), lambda i: (jnp.maximum(i * per - 1, 0), 0)),
                 pl.BlockSpec((SUBLANES, d), lambda i: (jnp.minimum((i + 1) * per, last), 0)),
                 pl.BlockSpec((CONV_WIDTH, d), lambda i: (0, 0))] + common_specs
        b, cu = acts
        args = [b, cu, cu, cu, extra[0]] + common_args
    else:
        body = functools.partial(_hgrn_out_kernel, d=d)
        specs = [row_spec, row_spec, pl.BlockSpec((1, d), lambda i: (0, 0))] + common_specs
        args = list(acts) + [extra[0]] + common_args
    lane_spec = pl.BlockSpec((tm, LANES), lambda i: (i, 0))
    return pl.pallas_call(
        body,
        grid=(t // tm,),
        in_specs=specs,
        out_specs=[row_spec, pl.BlockSpec((PACKED_CHUNKS, tm, CHUNK_W), lambda i: (0, i, 0)),
                   lane_spec, lane_spec, lane_spec, pl.BlockSpec((1, LANES), lambda i: (0, 0))],
        out_shape=[jax.ShapeDtypeStruct((t, d), F32), jax.ShapeDtypeStruct((PACKED_CHUNKS, t, CHUNK_W), jnp.int32),
                   jax.ShapeDtypeStruct((t, LANES), jnp.int32), jax.ShapeDtypeStruct((t, LANES), F32),
                   jax.ShapeDtypeStruct((t, LANES), jnp.int32), jax.ShapeDtypeStruct((1, LANES), F32)],
        scratch_shapes=[pltpu.VMEM((1, LANES), F32)],
        compiler_params=_params(("arbitrary",), 40),
        name="mixer_out_" + kind,
    )(*args)


def _softmax_rows(parts):
    mx = functools.reduce(jnp.maximum, [jnp.max(s, axis=-1, keepdims=True) for s in parts])
    es = [jnp.exp(s - mx) for s in parts]
    den = functools.reduce(lambda a, b: a + b, [jnp.sum(e, axis=-1, keepdims=True) for e in es])
    return [e / den for e in es]


def _head_pair_queries(q, dh):
    first = lax.broadcasted_iota(jnp.int32, (1, q.shape[1]), 1) < dh
    return first, (jnp.where(first, q, 0.0).astype(BF16), jnp.where(first, 0.0, q).astype(BF16))


def _na_ctx_kernel(q_ref, k_ref, v_ref, o_ref, *, dh, scale):
    nt = (((1,), (1,)), ((), ()))
    for pair in range(NA_HEADS * dh // LANES):
        sl = slice(pair * LANES, (pair + 1) * LANES)
        first, queries = _head_pair_queries(q_ref[:, sl], dh)
        kb = k_ref[:, sl].astype(BF16)
        vb = v_ref[:, sl].astype(BF16)
        outs = []
        for qh in queries:
            (p,) = _softmax_rows([lax.dot_general(qh, kb, nt, preferred_element_type=F32) * scale])
            outs.append(jnp.dot(p.astype(BF16), vb, preferred_element_type=F32))
        o_ref[:, sl] = jnp.where(first, outs[0], outs[1])


def _na_context(q, k, v, batch, seq):
    d = q.shape[1]
    dh = d // NA_HEADS
    spec = pl.BlockSpec((seq, d), lambda b: (b, 0))
    return pl.pallas_call(
        functools.partial(_na_ctx_kernel, dh=dh, scale=dh ** -0.5),
        grid=(batch,),
        in_specs=[spec, spec, spec],
        out_specs=spec,
        out_shape=jax.ShapeDtypeStruct((batch * seq, d), F32),
        compiler_params=_params(("arbitrary",), 32),
        name="na_context",
    )(q, k, v)


def _na_lat_kernel(*refs, dh, scale, kr):
    q_ref = refs[0]
    k_refs = refs[1:1 + kr]
    v_refs = refs[1 + kr:1 + 2 * kr]
    ck_ref, cv_ref, bias_ref, o_ref = refs[1 + 2 * kr:]
    nt = (((1,), (1,)), ((), ()))
    heads_per_group = LANES // dh
    for pair in range(NA_HEADS // heads_per_group):
        sl = slice(pair * LANES, (pair + 1) * LANES)
        first, queries = _head_pair_queries(q_ref[:, sl], dh)
        kw = jnp.concatenate([r[:, sl] for r in k_refs], axis=0).astype(BF16)
        vw = jnp.concatenate([r[:, sl] for r in v_refs], axis=0).astype(BF16)
        ckb = ck_ref[:, sl].astype(BF16)
        cvb = cv_ref[:, sl].astype(BF16)
        outs = []
        for j, qh in enumerate(queries):
            s_nb = (lax.dot_general(qh, kw, nt, preferred_element_type=F32) * scale
                    + bias_ref[pair * heads_per_group + j])
            s_cx = lax.dot_general(qh, ckb, nt, preferred_element_type=F32) * scale
            p_nb, p_cx = _softmax_rows([s_nb, s_cx])
            outs.append(jnp.dot(p_nb.astype(BF16), vw, preferred_element_type=F32)
                        + jnp.dot(p_cx.astype(BF16), cvb, preferred_element_type=F32))
        o_ref[:, sl] = jnp.where(first, outs[0], outs[1])


def _na_bias_table(rpb, kr):
    col = np.arange(GRID_W)
    col_start = np.clip(col - NA_COLS // 2, 0, GRID_W - NA_COLS)
    delta = col[None, :] - col[:, None] + (NA_COLS - 1)
    inside = (col[None, :] >= col_start[:, None]) & (col[None, :] < col_start[:, None] + NA_COLS)
    pick = (np.arange(2 * NA_COLS - 1)[:, None, None] == delta[None]).astype(np.float32)
    tab = jnp.einsum("hrk,kwc->hrwc", rpb, jnp.asarray(pick), precision=HIGHEST)
    tab = jnp.where(jnp.asarray(inside)[None, None], tab, NEG_BIG)
    out = []
    for d0 in range(NA_ROWS_MAX):
        rows = tab[:, d0:d0 + kr]
        out.append(rows.transpose(0, 2, 1, 3).reshape(NA_HEADS, GRID_W, kr * GRID_W))
    return jnp.stack(out, axis=0)


def _na_latent(q, k, v, ctx_k, ctx_v, rpb, n_prompt, dec_batch, dec_seq):
    d = q.shape[1]
    dh = d // NA_HEADS
    rows = dec_seq // GRID_W
    kr = min(NA_ROWS_MAX, rows)
    base = n_prompt // GRID_W
    past = ctx_k.shape[1]
    bias = _na_bias_table(rpb, kr)

    def row_start(r):
        return jnp.clip(r - kr // 2, 0, rows - kr)

    def win_spec(j):
        return pl.BlockSpec((GRID_W, d), lambda b, r: (base + b * rows + row_start(r) + j, 0))

    in_specs = ([pl.BlockSpec((GRID_W, d), lambda b, r: (base + b * rows + r, 0))]
                + [win_spec(j) for j in range(kr)] * 2
                + [pl.BlockSpec((None, past, d), lambda b, r: (b, 0, 0))] * 2
                + [pl.BlockSpec((None, NA_HEADS, GRID_W, kr * GRID_W),
                                lambda b, r: (row_start(r) - r + NA_ROWS_MAX - 1, 0, 0, 0))])
    return pl.pallas_call(
        functools.partial(_na_lat_kernel, dh=dh, scale=dh ** -0.5, kr=kr),
        grid=(dec_batch, rows),
        in_specs=in_specs,
        out_specs=pl.BlockSpec((GRID_W, d), lambda b, r: (b * rows + r, 0)),
        out_shape=jax.ShapeDtypeStruct((dec_batch * dec_seq, d), F32),
        compiler_params=_params(("arbitrary", "arbitrary"), 40),
        name="na_latent",
    )(q, *([k] * kr), *([v] * kr), ctx_k.reshape(dec_batch, past, d), ctx_v.reshape(dec_batch, past, d), bias)


def _hgrn_scan_kernel(*refs, seq_len, has_s0, want_state):
    q_ref, v_ref, z0_ref, z1_ref, lb_ref, tri_ref = refs[:6]
    rest = list(refs[6:])
    s0_ref = rest.pop(0) if has_s0 else None
    o_ref = rest.pop(0)
    sf_ref = rest.pop(0) if want_state else None
    ob_ref = rest.pop(0)
    c, r = HG_CHUNK, HG_TILE
    nc, nt = r // c, seq_len // r
    dk = HG_KEY_DIM
    tpos8 = lax.broadcasted_iota(jnp.int32, (1, SUBLANES, 1), 1)
    row = lax.broadcasted_iota(jnp.int32, (r, r), 0)
    col = lax.broadcasted_iota(jnp.int32, (r, r), 1)
    same16 = (row // (2 * SUBLANES)) == (col // (2 * SUBLANES))
    same32 = (row // c) == (col // c)
    nt_dims = (((1,), (1,)), ((), ()))
    tn_dims = (((0,), (0,)), ((), ()))

    def tile(dr, i, st):
        z_ref = (z0_ref, z1_ref)[dr]
        lb = lb_ref[dr:dr + 1, :]
        tri = tri_ref[dr]
        ti = i if dr == 0 else nt - 1 - i
        r0 = pl.multiple_of(ti * r, r)
        q = q_ref[pl.ds(r0, r), :]
        v = v_ref[pl.ds(r0, r), :]
        z = z_ref[pl.ds(r0, r), :]
        g = jnp.log(lb + (1.0 - lb) * jax.nn.sigmoid(z))
        k = (1.0 - lb) * jax.nn.sigmoid(-z)
        b = jnp.dot(tri, g, precision=HIGHEST, preferred_element_type=F32)
        b3, q3, k3, v3 = (a.reshape(nc, c, dk) for a in (b, q, k, v))
        nb = r // SUBLANES
        b8, q8, k8, v8 = (a.reshape(nb, SUBLANES, dk) for a in (b, q, k, v))
        oi = jnp.zeros((nb, SUBLANES, dk), F32)
        for s in range(SUBLANES):
            keep = (tpos8 >= s) if dr == 0 else (tpos8 <= s)
            e = jnp.exp(jnp.where(keep, b8 - b8[:, s:s + 1, :], NEG_BIG))
            a = jnp.sum(q8 * e * k8[:, s:s + 1, :], axis=-1, keepdims=True)
            oi = oi + a * v8[:, s:s + 1, :]
        att = None
        for span, same_group in ((2 * SUBLANES, same16), (c, same32)):
            half = span // 2
            bg = b.reshape(r // span, span, dk)
            pos = lax.broadcasted_iota(jnp.int32, (1, span, 1), 1)
            edge = bg[:, half - 1:half, :] if dr == 0 else bg[:, half:half + 1, :]
            is_query = (pos >= half) if dr == 0 else (pos < half)
            qf = q.reshape(bg.shape) * jnp.exp(jnp.where(is_query, bg - edge, NEG_BIG))
            kf = k.reshape(bg.shape) * jnp.exp(jnp.where(is_query, NEG_BIG, edge - bg))
            part = lax.dot_general(qf.reshape(r, dk).astype(BF16), kf.reshape(r, dk).astype(BF16), nt_dims,
                                   preferred_element_type=F32)
            part = jnp.where(same_group, part, 0.0)
            att = part if att is None else att + part
        oi = oi.reshape(r, dk) + jnp.dot(att.astype(BF16), v.astype(BF16), preferred_element_type=F32)
        oi = oi.reshape(nc, c, dk)
        bl = b3[:, c - 1:c, :] if dr == 0 else b3[:, 0:1, :]
        qt = (q3 * jnp.exp(b3)).astype(BF16)
        kt = (k3 * jnp.exp(bl - b3)).astype(BF16)
        vb = v3.astype(BF16)
        outs = [None] * nc
        for ci in (range(nc) if dr == 0 else range(nc - 1, -1, -1)):
            oc = lax.dot_general(qt[ci], st.astype(BF16), nt_dims, preferred_element_type=F32)
            outs[ci] = oi[ci] + oc
            kv = lax.dot_general(vb[ci], kt[ci], tn_dims, preferred_element_type=F32)
            st = st * jnp.exp(bl[ci]) + kv
        return r0, jnp.concatenate(outs, axis=0), st

    def step(i, carry):
        st_f, st_b = carry
        r_f, o_f, st_f = tile(0, i, st_f)
        r_b, o_b, st_b = tile(1, i, st_b)
        o_ref[pl.ds(r_f, r), :] = o_f
        ob_ref[pl.ds(r_b, r), :] = o_b
        return st_f, st_b

    zero = jnp.zeros((dk, dk), F32)
    st_f, st_b = lax.fori_loop(0, nt, step, (s0_ref[0], s0_ref[1]) if has_s0 else (zero, zero))
    o_ref[...] += ob_ref[...]
    if want_state:
        sf_ref[0] = st_f.T
        sf_ref[1] = st_b.T


def _hgrn_tri():
    t = np.arange(HG_TILE)
    same = (t[:, None] // HG_CHUNK) == (t[None, :] // HG_CHUNK)
    fwd = same & (t[None, :] <= t[:, None])
    bwd = same & (t[None, :] >= t[:, None])
    return jnp.asarray(np.stack([fwd, bwd]).astype(np.float32))


def _hgrn_scan(q, v, z0, z1, lb, row0, n_seq, seq_len, s0_t=None, want_state=False):
    d = q.shape[1]
    dk = HG_KEY_DIM
    blk0 = row0 // seq_len
    seq_spec = pl.BlockSpec((seq_len, dk), lambda s, h: (blk0 + s, h))
    st_spec = pl.BlockSpec((None, 2, None, dk, dk), lambda s, h: (s, 0, h, 0, 0))
    in_specs = [seq_spec] * 4 + [pl.BlockSpec((2, dk), lambda s, h: (0, h)),
                                 pl.BlockSpec((2, HG_TILE, HG_TILE), lambda s, h: (0, 0, 0))]
    args = [q, v, z0, z1, lb, _hgrn_tri()]
    if s0_t is not None:
        in_specs.append(st_spec)
        args.append(s0_t)
    out_specs = [pl.BlockSpec((seq_len, dk), lambda s, h: (s, h))]
    out_shape = [jax.ShapeDtypeStruct((n_seq * seq_len, d), F32)]
    if want_state:
        out_specs.append(st_spec)
        out_shape.append(jax.ShapeDtypeStruct((n_seq, 2, HG_HEADS, dk, dk), F32))
    res = pl.pallas_call(
        functools.partial(_hgrn_scan_kernel, seq_len=seq_len, has_s0=s0_t is not None, want_state=want_state),
        grid=(n_seq, HG_HEADS),
        in_specs=in_specs,
        out_specs=out_specs,
        out_shape=out_shape,
        scratch_shapes=[pltpu.VMEM((seq_len, dk), F32)],
        compiler_params=_params(("arbitrary", "arbitrary"), 32),
        name="hgrn_scan",
    )(*args)
    return res if want_state else (res[0], None)


def _route(idx, rank, counts, t):
    n = t * TOP_K
    rb = MOE_ROWS
    n_blocks = -(-n // rb) + N_EXPERTS
    n_rows = (n_blocks + N_EXPERTS) * rb
    experts = jnp.arange(N_EXPERTS, dtype=jnp.int32)
    counts = counts[0, :N_EXPERTS].astype(jnp.int32)
    padded = (counts + rb - 1) // rb * rb
    padded_end = jnp.cumsum(padded)
    start = padded_end - padded
    e = idx[:, :TOP_K]
    dest = jnp.sum(jnp.where(e[:, :, None] == experts, start, 0), axis=-1) + rank[:, :TOP_K]
    dest_km = dest.T.astype(jnp.int32)
    q = jnp.arange(rb, dtype=jnp.int32)
    spare = n_blocks * rb + experts[:, None] * rb + q[None, :]
    pad_rows = jnp.where(q[None, :] < (padded - counts)[:, None], (start + counts)[:, None] + q[None, :], spare)
    per_chunk = jnp.concatenate([dest_km.reshape(-1), pad_rows.reshape(-1)])
    sc_index = (jnp.arange(PACKED_CHUNKS, dtype=jnp.int32)[:, None] * n_rows + per_chunk[None, :]).reshape(1, -1)
    block_start = jnp.arange(n_blocks, dtype=jnp.int32) * rb
    block_expert = jnp.minimum(jnp.sum((padded_end[None, :] <= block_start[:, None]).astype(jnp.int32), axis=1),
                               N_EXPERTS - 1)
    n_used = (padded_end[-1] // rb).reshape(1)
    return sc_index, block_expert, n_used, n_rows


def _sc_mesh():
    return plsc.VectorSubcoreMesh(core_axis_name="c", subcore_axis_name="s")


def _sc_dispatch(h2c, sc_index, t, n_rows):
    win = SC_WINDOW
    tw = t // win
    nw = (t * TOP_K + N_EXPERTS * MOE_ROWS) // win
    n_chunks = h2c.shape[0]

    def scatter(x_hbm, i_hbm, o_hbm):
        def body(x_vmem, i_vmem):
            pltpu.sync_copy(x_vmem, o_hbm.at[i_vmem.at[0]])

        pltpu.emit_pipeline(
            body,
            grid=(n_chunks * nw,),
            in_specs=[pl.BlockSpec((win, CHUNK_W), index_map=lambda i: ((i // nw) * tw + (i % nw) % tw, 0)),
                      pl.BlockSpec((1, win), index_map=lambda i: (0, i))],
            out_specs=[],
            core_axis_name=("c", "s"),
            dimension_semantics=(pltpu.PARALLEL,),
        )(x_hbm, i_hbm)

    out = pl.kernel(scatter, out_type=jax.ShapeDtypeStruct((n_chunks * n_rows, CHUNK_W), h2c.dtype),
                    mesh=_sc_mesh(), name="moe_dispatch")(h2c.reshape(n_chunks * t, CHUNK_W), sc_index)
    return out.reshape(n_chunks, n_rows, CHUNK_W)


def _sc_collect(y_rows, sc_index, t, n_rows):
    win = SC_WINDOW
    nw = (t * TOP_K + N_EXPERTS * MOE_ROWS) // win
    aw = t * TOP_K // win
    n_chunks = y_rows.shape[0]

    def gather(y_hbm, i_hbm, o_hbm):
        def body(i_vmem, o_vmem):
            pltpu.sync_copy(y_hbm.at[i_vmem.at[0]], o_vmem)

        pltpu.emit_pipeline(
            body,
            grid=(n_chunks * aw,),
            in_specs=[pl.BlockSpec((1, win), index_map=lambda i: (0, (i // aw) * nw + i % aw))],
            out_specs=[pl.BlockSpec((win, CHUNK_W), index_map=lambda i: (i, 0))],
            core_axis_name=("c", "s"),
            dimension_semantics=(pltpu.PARALLEL,),
        )(i_hbm, o_hbm)

    out = pl.kernel(gather, out_type=jax.ShapeDtypeStruct((n_chunks * TOP_K * t, CHUNK_W), y_rows.dtype),
                    mesh=_sc_mesh(), name="moe_collect")(y_rows.reshape(n_chunks * n_rows, CHUNK_W), sc_index)
    return out.reshape(n_chunks, TOP_K, t, CHUNK_W)


def _moe_kernel(be_ref, nu_ref, x_ref, w1_ref, b1_ref, w2_ref, b2_ref, o_ref, w1b, w2b, *, dff):
    i = pl.program_id(0)
    e = be_ref[i]
    prev = be_ref[jnp.maximum(i - 1, 0)]

    @pl.when(jnp.logical_or(i == 0, e != prev))
    def _():
        w1b[...] = w1_ref[...].astype(BF16)
        w2b[...] = w2_ref[...].astype(BF16)

    @pl.when(i < nu_ref[0])
    def _():
        halves = [_unpack_bf16_pairs(x_ref[j]) for j in range(PACKED_CHUNKS)]
        xb = jnp.concatenate([h[0] for h in halves] + [h[1] for h in halves], axis=1).astype(BF16)
        gu = jnp.dot(xb, w1b[...], preferred_element_type=F32) + b1_ref[...]
        g = jnp.minimum(gu[:, :dff], SWIGLU_LIMIT)
        u = jnp.clip(gu[:, dff:], -SWIGLU_LIMIT, SWIGLU_LIMIT)
        a = (u + 1.0) * (g * jax.nn.sigmoid(SWIGLU_ALPHA * g))
        y = jnp.dot(a.astype(BF16), w2b[...], preferred_element_type=F32) + b2_ref[...]
        words = _pack_bf16_pairs(y)
        for j in range(PACKED_CHUNKS):
            o_ref[j] = words[:, j * CHUNK_W:(j + 1) * CHUNK_W]

    @pl.when(i >= nu_ref[0])
    def _():
        o_ref[...] = jnp.zeros(o_ref.shape, o_ref.dtype)


def _moe_experts(x_rows, block_expert, n_used, layer, w1, b1, w2, b2):
    _, n_rows, _ = x_rows.shape
    rb = MOE_ROWS
    n_blocks = n_rows // rb - N_EXPERTS
    depth, n_e, d, dff2 = w1.shape
    dff = dff2 // 2
    row_spec = pl.BlockSpec((PACKED_CHUNKS, rb, CHUNK_W), lambda i, be, nu: (0, jnp.minimum(i, nu[0] - 1), 0))
    grid_spec = pltpu.PrefetchScalarGridSpec(
        num_scalar_prefetch=2,
        grid=(n_blocks,),
        in_specs=[row_spec,
                  pl.BlockSpec((None, None, d, dff2), lambda i, be, nu: (layer, be[i], 0, 0)),
                  pl.BlockSpec((None, None, 1, dff2), lambda i, be, nu: (layer, be[i], 0, 0)),
                  pl.BlockSpec((None, None, dff, d), lambda i, be, nu: (layer, be[i], 0, 0)),
                  pl.BlockSpec((None, None, 1, d), lambda i, be, nu: (layer, be[i], 0, 0))],
        out_specs=pl.BlockSpec((PACKED_CHUNKS, rb, CHUNK_W), lambda i, be, nu: (0, i, 0)),
        scratch_shapes=[pltpu.VMEM((d, dff2), BF16), pltpu.VMEM((dff, d), BF16)],
    )
    return pl.pallas_call(
        functools.partial(_moe_kernel, dff=dff),
        grid_spec=grid_spec,
        out_shape=jax.ShapeDtypeStruct((PACKED_CHUNKS, n_rows, CHUNK_W), jnp.int32),
        compiler_params=_params(("arbitrary",), 56),
        name="moe_experts",
    )(block_expert, n_used, x_rows, w1, b1.reshape(depth, n_e, 1, dff2), w2, b2.reshape(depth, n_e, 1, d))


def _combine_kernel(y_ref, gw_ref, x_ref, mod_ref, o_ref, *, d):
    gw = gw_ref[...]
    for j in range(PACKED_CHUNKS):
        accs = [None, None]
        for k in range(TOP_K):
            for side, part in enumerate(_unpack_bf16_pairs(y_ref[j, k])):
                term = gw[:, k:k + 1] * part
                accs[side] = term if accs[side] is None else accs[side] + term
        for side, acc in enumerate(accs):
            c0 = side * (d // 2) + j * CHUNK_W
            o_ref[:, c0:c0 + CHUNK_W] = (x_ref[:, c0:c0 + CHUNK_W]
                                          + mod_ref[:, 5 * d + c0:5 * d + c0 + CHUNK_W] * acc)


def _moe_combine(y_slots, gw, x1, mods_l, geom):
    t, d = x1.shape
    tm = TOKEN_TILE
    grp = functools.partial(_group_of_tile, tile=tm, n_prompt=geom[0], dec_seq=geom[1])
    return pl.pallas_call(
        functools.partial(_combine_kernel, d=d),
        grid=(t // tm,),
        in_specs=[pl.BlockSpec((PACKED_CHUNKS, TOP_K, tm, CHUNK_W), lambda i: (0, 0, i, 0)),
                  pl.BlockSpec((tm, LANES), lambda i: (i, 0)),
                  pl.BlockSpec((tm, d), lambda i: (i, 0)),
                  pl.BlockSpec((None, 1, 6 * d), lambda i: (grp(i), 0, 0))],
        out_specs=pl.BlockSpec((tm, d), lambda i: (i, 0)),
        out_shape=jax.ShapeDtypeStruct((t, d), F32),
        compiler_params=_params(("arbitrary",), 24),
        name="moe_combine",
    )(y_slots, gw, x1, mods_l)


def _final_norm_kernel(x_ref, g_ref, o_ref):
    o_ref[...] = _rms(x_ref[...]) * g_ref[...]


def _final_norm(x, g):
    t, d = x.shape
    tm = TOKEN_TILE
    return pl.pallas_call(
        _final_norm_kernel,
        grid=(t // tm,),
        in_specs=[pl.BlockSpec((tm, d), lambda i: (i, 0)), pl.BlockSpec((1, d), lambda i: (0, 0))],
        out_specs=pl.BlockSpec((tm, d), lambda i: (i, 0)),
        out_shape=jax.ShapeDtypeStruct((t, d), F32),
        compiler_params=_params(("arbitrary",), 16),
        name="final_norm",
    )(x, g)


def kernel(x_prompt, x_sample, c, c_ctx, cache_na_k, cache_na_v, state_hgrn, norm_mix, norm_ffn, w_mod, b_mod,
           conv_w_in, conv_k, conv_w_out, na_w_qkv, na_rpb, na_w_o, hg_w_qig, hg_w_f, hg_lb, hg_norm, hg_w_o,
           moe_w_router, moe_b_router, moe_w1, moe_b1, moe_w2, moe_b2, final_norm):
    batch, seq, d = x_prompt.shape
    dec_batch, dec_seq, _ = x_sample.shape
    depth = w_mod.shape[0]
    n_prompt = batch * seq
    t = n_prompt + dec_batch * dec_seq
    geom = (n_prompt, dec_seq, seq)
    assert seq % TOKEN_TILE == 0 and dec_seq % TOKEN_TILE == 0 and seq % HG_TILE == 0
    assert n_prompt % dec_seq == 0 and dec_seq % GRID_W == 0 and 1 + dec_batch <= SUBLANES
    assert d == 2 * PACKED_CHUNKS * CHUNK_W and t % SC_WINDOW == 0 and MOE_ROWS % SC_WINDOW == 0

    x = jnp.concatenate([x_prompt.reshape(n_prompt, d), x_sample.reshape(dec_batch * dec_seq, d)], axis=0)
    cvecs = jnp.zeros((SUBLANES, d), F32).at[0].set(c_ctx).at[1:1 + dec_batch].set(c)
    mods = _adaln_all(cvecs, w_mod, b_mod)[:, :1 + dec_batch].reshape(depth, 1 + dec_batch, 1, 6 * d)

    lb_soft = jax.nn.softmax(hg_lb.astype(F32), axis=0)
    lower_bounds = jnp.cumsum(lb_soft, axis=0) - lb_soft[0]

    new_k, new_v, new_s = [], [], []
    for l in range(depth):
        kind, j = l % N_MIXERS, l // N_MIXERS
        g1 = norm_mix[l].reshape(1, d)
        g2 = norm_ffn[l].reshape(1, d)
        tail_args = (x, mods[l], g2, moe_w_router[l], moe_b_router[l], geom)
        if kind == 0:
            b, cu = _mixer_in(x, mods[l], g1, conv_w_in[j].astype(BF16), geom, conv=True)
            x1, h2c, idx, gw, rank, counts = _mixer_out("conv", (b, cu), conv_w_out[j].astype(BF16), *tail_args,
                                         extra=(conv_k[j],))
        elif kind == 1:
            q, k, v = _mixer_in(x, mods[l], g1, na_w_qkv[j].astype(BF16), geom)
            o_p = _na_context(q, k, v, batch, seq)
            o_s = _na_latent(q, k, v, cache_na_k[:, j], cache_na_v[:, j], na_rpb[j], n_prompt, dec_batch, dec_seq)
            new_k.append(k[:n_prompt].reshape(batch, seq, NA_HEADS, d // NA_HEADS))
            new_v.append(v[:n_prompt].reshape(batch, seq, NA_HEADS, d // NA_HEADS))
            x1, h2c, idx, gw, rank, counts = _mixer_out("plain", (jnp.concatenate([o_p, o_s], axis=0),),
                                         na_w_o[j].astype(BF16), *tail_args)
        else:
            w_in = jnp.concatenate([hg_w_qig[j], hg_w_f[j, 0], hg_w_f[j, 1]], axis=1).astype(BF16)
            q, v, gout, z0, z1 = _mixer_in(x, mods[l], g1, w_in, geom)
            lb = lower_bounds[l]
            o_p, s_p = _hgrn_scan(q, v, z0, z1, lb, 0, batch, seq, want_state=True)
            s0_t = jnp.swapaxes(state_hgrn[:, j].astype(F32), -1, -2)
            o_s, _ = _hgrn_scan(q, v, z0, z1, lb, n_prompt, dec_batch, dec_seq, s0_t=s0_t)
            new_s.append(s_p)
            x1, h2c, idx, gw, rank, counts = _mixer_out("hgrn", (jnp.concatenate([o_p, o_s], axis=0), gout),
                                         hg_w_o[j].astype(BF16), *tail_args, extra=(hg_norm[j].reshape(1, d),))
        sc_index, block_expert, n_used, n_rows = _route(idx, rank, counts, t)
        x_rows = _sc_dispatch(h2c, sc_index, t, n_rows)
        y_rows = _moe_experts(x_rows, block_expert, n_used, l, moe_w1, moe_b1, moe_w2, moe_b2)
        y_slots = _sc_collect(y_rows, sc_index, t, n_rows)
        x = _moe_combine(y_slots, gw, x1, mods[l], geom)

    y = _final_norm(x, final_norm.reshape(1, d))
    y_prompt = y[:n_prompt].reshape(batch, seq, d)
    y_sample = y[n_prompt:].reshape(dec_batch, dec_seq, d)
    new_na_k = jnp.stack(new_k, axis=1).astype(x_prompt.dtype)
    new_na_v = jnp.stack(new_v, axis=1).astype(x_prompt.dtype)
    new_hgrn_state = jnp.stack(new_s, axis=1).astype(x_prompt.dtype)
    return (y_prompt, y_sample, new_na_k, new_na_v, new_hgrn_state)
```

```python
import functools

import numpy as np
import jax
import jax.numpy as jnp
from jax import lax
from jax.experimental import pallas as pl
from jax.experimental.pallas import tpu as pltpu
from jax.experimental.pallas import tpu_sc as plsc

F32 = jnp.float32
BF16 = jnp.bfloat16
HIGHEST = lax.Precision.HIGHEST

N_MIXERS = 3
GRID_W = 64
CONV_WIDTH = 3
NA_HEADS = 16
NA_ROWS_MAX = 8
NA_COLS = 16
HG_HEADS = 8
HG_KEY_DIM = 128
HG_CHUNK = 32
N_EXPERTS = 32
TOP_K = 4
SWIGLU_LIMIT = 7.0
SWIGLU_ALPHA = 1.702
EPS = 1e-6

LANES = 128
SUBLANES = 8
VMEM_LIMIT_CAP = 60 * 1024 * 1024

NEG_BIG = -1e30
TOKEN_TILE = 256
MOE_ROWS = 512
HG_TILE = 128
HG_HEADS_PER_STEP = 2
PACKED_CHUNKS = 2
CHUNK_W = 256
SC_WINDOW = 128


def _params(sem, vmem_mb, flags=None):
    return pltpu.CompilerParams(dimension_semantics=sem, flags=flags,
                                vmem_limit_bytes=min(vmem_mb * 1024 * 1024, VMEM_LIMIT_CAP))


def _rms(x):
    return x * lax.rsqrt(jnp.mean(x * x, axis=-1, keepdims=True) + EPS)


def _modulate(x, g, shift, scale):
    return (_rms(x) * g) * (1.0 + scale) + shift


def _bf16_part(x):
    bits = lax.bitcast_convert_type(x, jnp.uint32) & jnp.uint32(0xFFFF0000)
    return lax.bitcast_convert_type(bits, F32)


def _pack_bf16_pairs(x):
    n = x.shape[1]
    bits = lax.bitcast_convert_type(x, jnp.uint32)
    top = (bits + (jnp.uint32(0x7FFF) + ((bits >> 16) & jnp.uint32(1)))) & jnp.uint32(0xFFFF0000)
    return lax.bitcast_convert_type(top[:, :n // 2] | (top[:, n // 2:] >> 16), jnp.int32)


def _unpack_bf16_pairs(words):
    bits = lax.bitcast_convert_type(words, jnp.uint32)
    return (lax.bitcast_convert_type(bits & jnp.uint32(0xFFFF0000), F32),
            lax.bitcast_convert_type(bits << 16, F32))


def _group_of_tile(i, tile, n_prompt, dec_seq):
    start = i * tile
    return jnp.where(start < n_prompt, 0, 1 + (start - n_prompt) // dec_seq)


def _adaln_kernel(c_ref, w_ref, b_ref, o_ref):
    cv = c_ref[...]
    s = cv * jax.nn.sigmoid(cv)
    o_ref[...] = jnp.dot(s, w_ref[...], precision=HIGHEST, preferred_element_type=F32) + b_ref[...]


def _adaln_all(cvecs, w_mod, b_mod):
    depth, d, n = w_mod.shape
    tn = 2048
    return pl.pallas_call(
        _adaln_kernel,
        grid=(depth, n // tn),
        in_specs=[pl.BlockSpec((SUBLANES, d), lambda l, j: (0, 0)),
                  pl.BlockSpec((None, d, tn), lambda l, j: (l, 0, j)),
                  pl.BlockSpec((None, 1, tn), lambda l, j: (l, 0, j))],
        out_specs=pl.BlockSpec((None, SUBLANES, tn), lambda l, j: (l, 0, j)),
        out_shape=jax.ShapeDtypeStruct((depth, SUBLANES, n), F32),
        compiler_params=_params(("arbitrary", "arbitrary"), 40),
        name="adaln",
    )(cvecs, w_mod, b_mod.reshape(depth, 1, n))


def _in_kernel(x_ref, mod_ref, g_ref, w_ref, *out_refs, d, conv):
    mod = mod_ref[...]
    h = _modulate(x_ref[...], g_ref[...], mod[:, 0:d], mod[:, d:2 * d]).astype(BF16)
    if conv:
        b_ref, cu_ref = out_refs
        b_ref[...] = jnp.dot(h, w_ref[:, 0:d], preferred_element_type=F32)
        c = jnp.dot(h, w_ref[:, d:2 * d], preferred_element_type=F32)
        u = jnp.dot(h, w_ref[:, 2 * d:3 * d], preferred_element_type=F32)
        cu_ref[...] = c * u
    else:
        for j, o_ref in enumerate(out_refs):
            o_ref[...] = jnp.dot(h, w_ref[:, j * d:(j + 1) * d], preferred_element_type=F32)


def _mixer_in(x, mods_l, g, w, geom, conv=False):
    t, d = x.shape
    n = w.shape[1]
    n_out = 2 if conv else n // d
    tm = TOKEN_TILE
    grp = functools.partial(_group_of_tile, tile=tm, n_prompt=geom[0], dec_seq=geom[1])
    return pl.pallas_call(
        functools.partial(_in_kernel, d=d, conv=conv),
        grid=(t // tm,),
        in_specs=[pl.BlockSpec((tm, d), lambda i: (i, 0)),
                  pl.BlockSpec((None, 1, 6 * d), lambda i: (grp(i), 0, 0)),
                  pl.BlockSpec((1, d), lambda i: (0, 0)),
                  pl.BlockSpec((d, n), lambda i: (0, 0))],
        out_specs=[pl.BlockSpec((tm, d), lambda i: (i, 0))] * n_out,
        out_shape=[jax.ShapeDtypeStruct((t, d), F32)] * n_out,
        compiler_params=_params(("arbitrary",), 48),
        name="mixer_in",
    )(x, mods_l, g, w)


def _tail(m, x_ref, mod, g2_ref, wr_ref, br_ref, outs, run_ref, d):
    x1_ref, h2_ref, idx_ref, gw_ref, rank_ref, cnt_ref = outs
    x1 = x_ref[...] + mod[:, 2 * d:3 * d] * m
    x1_ref[...] = x1
    h2 = _modulate(x1, g2_ref[...], mod[:, 3 * d:4 * d], mod[:, 4 * d:5 * d])
    words = _pack_bf16_pairs(h2)
    for j in range(PACKED_CHUNKS):
        h2_ref[j] = words[:, j * CHUNK_W:(j + 1) * CHUNK_W]
    h2_top = _bf16_part(h2)
    h2_hi = h2_top.astype(BF16)
    h2_lo = (h2 - h2_top).astype(BF16)
    logits = (jnp.dot(h2_hi, wr_ref[0], preferred_element_type=F32)
              + jnp.dot(h2_lo, wr_ref[0], preferred_element_type=F32)
              + jnp.dot(h2_hi, wr_ref[1], preferred_element_type=F32)) + br_ref[...]
    lane = lax.broadcasted_iota(jnp.int32, logits.shape, 1)
    vals, idxs = [], []
    for _ in range(TOP_K):
        mx = jnp.max(logits, axis=-1, keepdims=True)
        ix = jnp.argmax(logits, axis=-1, keepdims=True).astype(jnp.int32)
        vals.append(mx)
        idxs.append(ix)
        logits = jnp.where(lane == ix, NEG_BIG, logits)
    es = [jnp.exp(v - vals[0]) for v in vals]
    den = es[0] + es[1] + es[2] + es[3]

    @pl.when(pl.program_id(0) == 0)
    def _():
        run_ref[...] = jnp.zeros(run_ref.shape, F32)

    tm = lane.shape[0]
    earlier = (lax.broadcasted_iota(jnp.int32, (tm, tm), 0)
               > lax.broadcasted_iota(jnp.int32, (tm, tm), 1)).astype(BF16)
    seen = run_ref[...]
    idx_out = jnp.zeros(lane.shape, jnp.int32)
    gw_out = jnp.zeros(lane.shape, F32)
    rank_out = jnp.zeros(lane.shape, jnp.int32)
    for k in range(TOP_K):
        hit = lane == idxs[k]
        before = jnp.dot(earlier, hit.astype(BF16), preferred_element_type=F32) + seen
        rank_k = jnp.sum(jnp.where(hit, before, 0.0), axis=-1, keepdims=True)
        seen = seen + jnp.sum(hit.astype(F32), axis=0, keepdims=True)
        idx_out = jnp.where(lane == k, idxs[k], idx_out)
        gw_out = jnp.where(lane == k, es[k] / den, gw_out)
        rank_out = jnp.where(lane == k, rank_k.astype(jnp.int32), rank_out)
    run_ref[...] = seen
    cnt_ref[...] = seen
    idx_ref[...] = idx_out
    gw_ref[...] = gw_out
    rank_ref[...] = rank_out


def _plain_out_kernel(a_ref, w_ref, x_ref, mod_ref, g2_ref, wr_ref, br_ref, *rest, d):
    m = jnp.dot(a_ref[...].astype(BF16), w_ref[...], preferred_element_type=F32)
    _tail(m, x_ref, mod_ref[...], g2_ref, wr_ref, br_ref, rest[:-1], rest[-1], d)


def _conv_out_kernel(b_ref, cu_ref, prev_ref, next_ref, ck_ref, w_ref, x_ref, mod_ref, g2_ref, wr_ref,
                     br_ref, *rest, d, tm, n_prompt, seq, dec_seq):
    i = pl.program_id(0)
    cu = cu_ref[...]
    row = lax.broadcasted_iota(jnp.int32, (tm, 1), 0)
    start = i * tm
    in_prompt = start < n_prompt
    seq_len = jnp.where(in_prompt, seq, dec_seq)
    pos = lax.rem(jnp.where(in_prompt, start, start - n_prompt), seq_len) + row
    prev = jnp.where(row == 0, prev_ref[SUBLANES - 1:SUBLANES, :], pltpu.roll(cu, 1, 0))
    prev = jnp.where(pos == 0, 0.0, prev)
    nxt = jnp.where(row == tm - 1, next_ref[0:1, :], pltpu.roll(cu, tm - 1, 0))
    nxt = jnp.where(pos == seq_len - 1, 0.0, nxt)
    conv = prev * ck_ref[0:1, :] + cu * ck_ref[1:2, :] + nxt * ck_ref[2:3, :]
    m = jnp.dot((b_ref[...] * conv).astype(BF16), w_ref[...], preferred_element_type=F32)
    _tail(m, x_ref, mod_ref[...], g2_ref, wr_ref, br_ref, rest[:-1], rest[-1], d)


def _hgrn_out_kernel(o_ref, gout_ref, ng_ref, w_ref, x_ref, mod_ref, g2_ref, wr_ref, br_ref, *rest, d):
    parts = []
    for h in range(HG_HEADS):
        oh = o_ref[:, h * LANES:(h + 1) * LANES]
        parts.append(oh * lax.rsqrt(jnp.mean(oh * oh, axis=-1, keepdims=True) + EPS))
    gout = gout_ref[...]
    y = (jnp.concatenate(parts, axis=1) * ng_ref[...]) * (gout * jax.nn.sigmoid(gout))
    m = jnp.dot(y.astype(BF16), w_ref[...], preferred_element_type=F32)
    _tail(m, x_ref, mod_ref[...], g2_ref, wr_ref, br_ref, rest[:-1], rest[-1], d)


def _mixer_out(kind, acts, w_out, x, mods_l, g2, w_router, b_router, geom, extra=()):
    t, d = x.shape
    tm = TOKEN_TILE
    n_prompt, dec_seq, seq = geom
    grp = functools.partial(_group_of_tile, tile=tm, n_prompt=n_prompt, dec_seq=dec_seq)
    row_spec = pl.BlockSpec((tm, d), lambda i: (i, 0))
    wr = jnp.zeros((d, LANES), F32).at[:, :N_EXPERTS].set(w_router)
    wr_top = _bf16_part(wr)
    wr = jnp.stack([wr_top.astype(BF16), (wr - wr_top).astype(BF16)])
    br = jnp.full((1, LANES), NEG_BIG, F32).at[0, :N_EXPERTS].set(b_router)
    common_specs = [pl.BlockSpec((d, d), lambda i: (0, 0)),
                    row_spec,
                    pl.BlockSpec((None, 1, 6 * d), lambda i: (grp(i), 0, 0)),
                    pl.BlockSpec((1, d), lambda i: (0, 0)),
                    pl.BlockSpec((2, d, LANES), lambda i: (0, 0, 0)),
                    pl.BlockSpec((1, LANES), lambda i: (0, 0))]
    common_args = [w_out, x, mods_l, g2, wr, br]
    if kind == "plain":
        body = functools.partial(_plain_out_kernel, d=d)
        specs = [row_spec] + common_specs
        args = list(acts) + common_args
    elif kind == "conv":
        body = functools.partial(_conv_out_kernel, d=d, tm=tm, n_prompt=n_prompt, seq=seq, dec_seq=dec_seq)
        per = tm // SUBLANES
        last = t // SUBLANES - 1
        specs = [row_spec, row_spec,
                 pl.BlockSpec((SUBLANES, d), lambda i: (jnp.maximum(i * per - 1, 0), 0)),
                 pl.BlockSpec((SUBLANES, d), lambda i: (jnp.minimum((i + 1) * per, last), 0)),
                 pl.BlockSpec((CONV_WIDTH, d), lambda i: (0, 0))] + common_specs
        b, cu = acts
        args = [b, cu, cu, cu, extra[0]] + common_args
    else:
        body = functools.partial(_hgrn_out_kernel, d=d)
        specs = [row_spec, row_spec, pl.BlockSpec((1, d), lambda i: (0, 0))] + common_specs
        args = list(acts) + [extra[0]] + common_args
    lane_spec = pl.BlockSpec((tm, LANES), lambda i: (i, 0))
    return pl.pallas_call(
        body,
        grid=(t // tm,),
        in_specs=specs,
        out_specs=[row_spec, pl.BlockSpec((PACKED_CHUNKS, tm, CHUNK_W), lambda i: (0, i, 0)),
                   lane_spec, lane_spec, lane_spec, pl.BlockSpec((1, LANES), lambda i: (0, 0))],
        out_shape=[jax.ShapeDtypeStruct((t, d), F32), jax.ShapeDtypeStruct((PACKED_CHUNKS, t, CHUNK_W), jnp.int32),
                   jax.ShapeDtypeStruct((t, LANES), jnp.int32), jax.ShapeDtypeStruct((t, LANES), F32),
                   jax.ShapeDtypeStruct((t, LANES), jnp.int32), jax.ShapeDtypeStruct((1, LANES), F32)],
        scratch_shapes=[pltpu.VMEM((1, LANES), F32)],
        compiler_params=_params(("arbitrary",), 40),
        name="mixer_out_" + kind,
    )(*args)


def _softmax_rows(parts):
    mx = functools.reduce(jnp.maximum, [jnp.max(s, axis=-1, keepdims=True) for s in parts])
    es = [jnp.exp(s - mx) for s in parts]
    den = functools.reduce(lambda a, b: a + b, [jnp.sum(e, axis=-1, keepdims=True) for e in es])
    return [e / den for e in es]


def _head_pair_queries(q, dh):
    first = lax.broadcasted_iota(jnp.int32, (1, q.shape[1]), 1) < dh
    return first, (jnp.where(first, q, 0.0).astype(BF16), jnp.where(first, 0.0, q).astype(BF16))


def _na_ctx_kernel(q_ref, k_ref, v_ref, o_ref, *, dh, scale):
    nt = (((1,), (1,)), ((), ()))
    for pair in range(NA_HEADS * dh // LANES):
        sl = slice(pair * LANES, (pair + 1) * LANES)
        first, queries = _head_pair_queries(q_ref[:, sl], dh)
        n_q = q_ref.shape[0]
        q2 = jnp.concatenate(queries, axis=0)
        (p,) = _softmax_rows([lax.dot_general(q2, k_ref[:, sl].astype(BF16), nt,
                                              preferred_element_type=F32) * scale])
        o2 = jnp.dot(p.astype(BF16), v_ref[:, sl].astype(BF16), preferred_element_type=F32)
        o_ref[:, sl] = jnp.where(first, o2[:n_q], o2[n_q:])


def _na_context(q, k, v, batch, seq):
    d = q.shape[1]
    dh = d // NA_HEADS
    spec = pl.BlockSpec((seq, d), lambda b: (b, 0))
    return pl.pallas_call(
        functools.partial(_na_ctx_kernel, dh=dh, scale=dh ** -0.5),
        grid=(batch,),
        in_specs=[spec, spec, spec],
        out_specs=spec,
        out_shape=jax.ShapeDtypeStruct((batch * seq, d), F32),
        compiler_params=_params(("arbitrary",), 32),
        name="na_context",
    )(q, k, v)


def _na_lat_kernel(*refs, dh, scale, kr, n_batch):
    per = 1 + 2 * kr
    ck_ref, cv_ref, bias_ref, o_ref = refs[n_batch * per:]
    nt = (((1,), (1,)), ((), ()))
    heads_per_group = LANES // dh
    for pair in range(NA_HEADS // heads_per_group):
        sl = slice(pair * LANES, (pair + 1) * LANES)
        for b in range(n_batch):
            q_ref = refs[b * per]
            k_refs = refs[b * per + 1:b * per + 1 + kr]
            v_refs = refs[b * per + 1 + kr:(b + 1) * per]
            first, queries = _head_pair_queries(q_ref[:, sl], dh)
            n_q = q_ref.shape[0]
            q2 = jnp.concatenate(queries, axis=0)
            bias2 = jnp.concatenate([bias_ref[pair * heads_per_group + j] for j in range(heads_per_group)], axis=0)
            kw = jnp.concatenate([r[:, sl] for r in k_refs], axis=0).astype(BF16)
            vw = jnp.concatenate([r[:, sl] for r in v_refs], axis=0).astype(BF16)
            s_nb = lax.dot_general(q2, kw, nt, preferred_element_type=F32) * scale + bias2
            s_cx = lax.dot_general(q2, ck_ref[b, :, sl].astype(BF16), nt, preferred_element_type=F32) * scale
            p_nb, p_cx = _softmax_rows([s_nb, s_cx])
            o2 = (jnp.dot(p_nb.astype(BF16), vw, preferred_element_type=F32)
                  + jnp.dot(p_cx.astype(BF16), cv_ref[b, :, sl].astype(BF16), preferred_element_type=F32))
            o_ref[b, :, sl] = jnp.where(first, o2[:n_q], o2[n_q:])


def _na_bias_table(rpb, kr):
    col = np.arange(GRID_W)
    col_start = np.clip(col - NA_COLS // 2, 0, GRID_W - NA_COLS)
    delta = col[None, :] - col[:, None] + (NA_COLS - 1)
    inside = (col[None, :] >= col_start[:, None]) & (col[None, :] < col_start[:, None] + NA_COLS)
    pick = (np.arange(2 * NA_COLS - 1)[:, None, None] == delta[None]).astype(np.float32)
    tab = jnp.einsum("hrk,kwc->hrwc", rpb, jnp.asarray(pick), precision=HIGHEST)
    tab = jnp.where(jnp.asarray(inside)[None, None], tab, NEG_BIG)
    out = []
    for d0 in range(NA_ROWS_MAX):
        rows = tab[:, d0:d0 + kr]
        out.append(rows.transpose(0, 2, 1, 3).reshape(NA_HEADS, GRID_W, kr * GRID_W))
    return jnp.stack(out, axis=0)


def _na_latent(q, k, v, ctx_k, ctx_v, rpb, n_prompt, dec_batch, dec_seq):
    d = q.shape[1]
    dh = d // NA_HEADS
    rows = dec_seq // GRID_W
    kr = min(NA_ROWS_MAX, rows)
    base = n_prompt // GRID_W
    past = ctx_k.shape[1]
    bias = _na_bias_table(rpb, kr)

    def row_start(r):
        return jnp.clip(r - kr // 2, 0, rows - kr)

    def win_spec(b, j):
        return pl.BlockSpec((GRID_W, d), lambda r: (base + b * rows + row_start(r) + j, 0))

    in_specs, args = [], []
    for b in range(dec_batch):
        in_specs += ([pl.BlockSpec((GRID_W, d), lambda r, b=b: (base + b * rows + r, 0))]
                     + [win_spec(b, j) for j in range(kr)] * 2)
        args += [q] + [k] * kr + [v] * kr
    in_specs += ([pl.BlockSpec((dec_batch, past, d), lambda r: (0, 0, 0))] * 2
                 + [pl.BlockSpec((None, NA_HEADS, GRID_W, kr * GRID_W),
                                 lambda r: (row_start(r) - r + NA_ROWS_MAX - 1, 0, 0, 0))])
    args += [ctx_k.reshape(dec_batch, past, d), ctx_v.reshape(dec_batch, past, d), bias]
    out = pl.pallas_call(
        functools.partial(_na_lat_kernel, dh=dh, scale=dh ** -0.5, kr=kr, n_batch=dec_batch),
        grid=(rows,),
        in_specs=in_specs,
        out_specs=pl.BlockSpec((dec_batch, GRID_W, d), lambda r: (0, r, 0)),
        out_shape=jax.ShapeDtypeStruct((dec_batch, dec_seq, d), F32),
        compiler_params=_params(("arbitrary",), 40),
        name="na_latent",
    )(*args)
    return out.reshape(dec_batch * dec_seq, d)


def _hgrn_scan_kernel(*refs, seq_len, has_s0, want_state):
    q_ref, v_ref, z0_ref, z1_ref, lb_ref, tri_ref = refs[:6]
    rest = list(refs[6:])
    s0_ref = rest.pop(0) if has_s0 else None
    o_ref = rest.pop(0)
    sf_ref = rest.pop(0) if want_state else None
    ob_ref = rest.pop(0)
    c, r = HG_CHUNK, HG_TILE
    nc, nt = r // c, seq_len // r
    dk = HG_KEY_DIM
    tpos8 = lax.broadcasted_iota(jnp.int32, (1, SUBLANES, 1), 1)
    row = lax.broadcasted_iota(jnp.int32, (r, r), 0)
    col = lax.broadcasted_iota(jnp.int32, (r, r), 1)
    same16 = (row // (2 * SUBLANES)) == (col // (2 * SUBLANES))
    same32 = (row // c) == (col // c)
    nt_dims = (((1,), (1,)), ((), ()))
    tn_dims = (((0,), (0,)), ((), ()))

    def tile(dr, hh, i, st):
        z_ref = (z0_ref, z1_ref)[dr]
        cols = slice(hh * dk, (hh + 1) * dk)
        lb = lb_ref[dr:dr + 1, cols]
        tri = tri_ref[dr]
        ti = i if dr == 0 else nt - 1 - i
        r0 = pl.multiple_of(ti * r, r)
        q = q_ref[pl.ds(r0, r), cols]
        v = v_ref[pl.ds(r0, r), cols]
        z = z_ref[pl.ds(r0, r), cols]
        g = jnp.log(lb + (1.0 - lb) * jax.nn.sigmoid(z))
        k = (1.0 - lb) * jax.nn.sigmoid(-z)
        b = jnp.dot(tri, g, precision=HIGHEST, preferred_element_type=F32)
        b3, q3, k3, v3 = (a.reshape(nc, c, dk) for a in (b, q, k, v))
        nb = r // SUBLANES
        b8, q8, k8, v8 = (a.reshape(nb, SUBLANES, dk) for a in (b, q, k, v))
        oi = jnp.zeros((nb, SUBLANES, dk), F32)
        for s in range(SUBLANES):
            keep = (tpos8 >= s) if dr == 0 else (tpos8 <= s)
            e = jnp.exp(jnp.where(keep, b8 - b8[:, s:s + 1, :], NEG_BIG))
            a = jnp.sum(q8 * e * k8[:, s:s + 1, :], axis=-1, keepdims=True)
            oi = oi + a * v8[:, s:s + 1, :]
        att = None
        for span, same_group in ((2 * SUBLANES, same16), (c, same32)):
            half = span // 2
            bg = b.reshape(r // span, span, dk)
            pos = lax.broadcasted_iota(jnp.int32, (1, span, 1), 1)
            edge = bg[:, half - 1:half, :] if dr == 0 else bg[:, half:half + 1, :]
            is_query = (pos >= half) if dr == 0 else (pos < half)
            qf = q.reshape(bg.shape) * jnp.exp(jnp.where(is_query, bg - edge, NEG_BIG))
            kf = k.reshape(bg.shape) * jnp.exp(jnp.where(is_query, NEG_BIG, edge - bg))
            part = lax.dot_general(qf.reshape(r, dk).astype(BF16), kf.reshape(r, dk).astype(BF16), nt_dims,
                                   preferred_element_type=F32)
            part = jnp.where(same_group, part, 0.0)
            att = part if att is None else att + part
        oi = oi.reshape(r, dk) + jnp.dot(att.astype(BF16), v.astype(BF16), preferred_element_type=F32)
        oi = oi.reshape(nc, c, dk)
        bl = b3[:, c - 1:c, :] if dr == 0 else b3[:, 0:1, :]
        qt = (q3 * jnp.exp(b3)).astype(BF16)
        kt = (k3 * jnp.exp(bl - b3)).astype(BF16)
        vb = v3.astype(BF16)
        outs = [None] * nc
        for ci in (range(nc) if dr == 0 else range(nc - 1, -1, -1)):
            oc = lax.dot_general(qt[ci], st.astype(BF16), nt_dims, preferred_element_type=F32)
            outs[ci] = oi[ci] + oc
            kv = lax.dot_general(vb[ci], kt[ci], tn_dims, preferred_element_type=F32)
            st = st * jnp.exp(bl[ci]) + kv
        return r0, jnp.concatenate(outs, axis=0), st

    chains = [(dr, hh) for hh in range(HG_HEADS_PER_STEP) for dr in range(2)]

    def step(i, carry):
        new = []
        for (dr, hh), st in zip(chains, carry):
            r0, o, st = tile(dr, hh, i, st)
            (o_ref if dr == 0 else ob_ref)[pl.ds(r0, r), hh * dk:(hh + 1) * dk] = o
            new.append(st)
        return tuple(new)

    init = tuple(s0_ref[dr, hh] if has_s0 else jnp.zeros((dk, dk), F32) for dr, hh in chains)
    final = lax.fori_loop(0, nt, step, init)
    o_ref[...] += ob_ref[...]
    if want_state:
        for (dr, hh), st in zip(chains, final):
            sf_ref[dr, hh] = st.T


def _hgrn_tri():
    t = np.arange(HG_TILE)
    same = (t[:, None] // HG_CHUNK) == (t[None, :] // HG_CHUNK)
    fwd = same & (t[None, :] <= t[:, None])
    bwd = same & (t[None, :] >= t[:, None])
    return jnp.asarray(np.stack([fwd, bwd]).astype(np.float32))


def _hgrn_scan(q, v, z0, z1, lb, row0, n_seq, seq_len, s0_t=None, want_state=False):
    d = q.shape[1]
    dk = HG_KEY_DIM
    blk0 = row0 // seq_len
    hp = HG_HEADS_PER_STEP
    seq_spec = pl.BlockSpec((seq_len, hp * dk), lambda s, h: (blk0 + s, h))
    st_spec = pl.BlockSpec((None, 2, hp, dk, dk), lambda s, h: (s, 0, h, 0, 0))
    in_specs = [seq_spec] * 4 + [pl.BlockSpec((2, hp * dk), lambda s, h: (0, h)),
                                 pl.BlockSpec((2, HG_TILE, HG_TILE), lambda s, h: (0, 0, 0))]
    args = [q, v, z0, z1, lb, _hgrn_tri()]
    if s0_t is not None:
        in_specs.append(st_spec)
        args.append(s0_t)
    out_specs = [pl.BlockSpec((seq_len, hp * dk), lambda s, h: (s, h))]
    out_shape = [jax.ShapeDtypeStruct((n_seq * seq_len, d), F32)]
    if want_state:
        out_specs.append(st_spec)
        out_shape.append(jax.ShapeDtypeStruct((n_seq, 2, HG_HEADS, dk, dk), F32))
    res = pl.pallas_call(
        functools.partial(_hgrn_scan_kernel, seq_len=seq_len, has_s0=s0_t is not None, want_state=want_state),
        grid=(n_seq, HG_HEADS // hp),
        in_specs=in_specs,
        out_specs=out_specs,
        out_shape=out_shape,
        scratch_shapes=[pltpu.VMEM((seq_len, hp * dk), F32)],
        compiler_params=_params(("arbitrary", "arbitrary"), 32),
        name="hgrn_scan",
    )(*args)
    return res if want_state else (res[0], None)


def _route(idx, rank, counts, t):
    n = t * TOP_K
    rb = MOE_ROWS
    n_blocks = -(-n // rb) + N_EXPERTS
    n_rows = (n_blocks + N_EXPERTS) * rb
    experts = jnp.arange(N_EXPERTS, dtype=jnp.int32)
    counts = counts[0, :N_EXPERTS].astype(jnp.int32)
    padded = (counts + rb - 1) // rb * rb
    padded_end = jnp.cumsum(padded)
    start = padded_end - padded
    e = idx[:, :TOP_K]
    dest = jnp.sum(jnp.where(e[:, :, None] == experts, start, 0), axis=-1) + rank[:, :TOP_K]
    dest_km = dest.T.astype(jnp.int32)
    q = jnp.arange(rb, dtype=jnp.int32)
    spare = n_blocks * rb + experts[:, None] * rb + q[None, :]
    pad_rows = jnp.where(q[None, :] < (padded - counts)[:, None], (start + counts)[:, None] + q[None, :], spare)
    per_chunk = jnp.concatenate([dest_km.reshape(-1), pad_rows.reshape(-1)])
    sc_index = (jnp.arange(PACKED_CHUNKS, dtype=jnp.int32)[:, None] * n_rows + per_chunk[None, :]).reshape(1, -1)
    block_start = jnp.arange(n_blocks, dtype=jnp.int32) * rb
    block_expert = jnp.minimum(jnp.sum((padded_end[None, :] <= block_start[:, None]).astype(jnp.int32), axis=1),
                               N_EXPERTS - 1)
    n_used = (padded_end[-1] // rb).reshape(1)
    return sc_index, block_expert, n_used, n_rows


def _sc_mesh():
    return plsc.VectorSubcoreMesh(core_axis_name="c", subcore_axis_name="s")


def _sc_dispatch(h2c, sc_index, t, n_rows):
    win = SC_WINDOW
    tw = t // win
    nw = (t * TOP_K + N_EXPERTS * MOE_ROWS) // win
    n_chunks = h2c.shape[0]

    def scatter(x_hbm, i_hbm, o_hbm):
        def body(x_vmem, i_vmem):
            pltpu.sync_copy(x_vmem, o_hbm.at[i_vmem.at[0]])

        pltpu.emit_pipeline(
            body,
            grid=(n_chunks * nw,),
            in_specs=[pl.BlockSpec((win, CHUNK_W), index_map=lambda i: ((i // nw) * tw + (i % nw) % tw, 0)),
                      pl.BlockSpec((1, win), index_map=lambda i: (0, i))],
            out_specs=[],
            core_axis_name=("c", "s"),
            dimension_semantics=(pltpu.PARALLEL,),
        )(x_hbm, i_hbm)

    out = pl.kernel(scatter, out_type=jax.ShapeDtypeStruct((n_chunks * n_rows, CHUNK_W), h2c.dtype),
                    mesh=_sc_mesh(), name="moe_dispatch")(h2c.reshape(n_chunks * t, CHUNK_W), sc_index)
    return out.reshape(n_chunks, n_rows, CHUNK_W)


def _sc_collect(y_rows, sc_index, t, n_rows):
    win = SC_WINDOW
    nw = (t * TOP_K + N_EXPERTS * MOE_ROWS) // win
    aw = t * TOP_K // win
    n_chunks = y_rows.shape[0]

    def gather(y_hbm, i_hbm, o_hbm):
        def body(i_vmem, o_vmem):
            pltpu.sync_copy(y_hbm.at[i_vmem.at[0]], o_vmem)

        pltpu.emit_pipeline(
            body,
            grid=(n_chunks * aw,),
            in_specs=[pl.BlockSpec((1, win), index_map=lambda i: (0, (i // aw) * nw + i % aw))],
            out_specs=[pl.BlockSpec((win, CHUNK_W), index_map=lambda i: (i, 0))],
            core_axis_name=("c", "s"),
            dimension_semantics=(pltpu.PARALLEL,),
        )(i_hbm, o_hbm)

    out = pl.kernel(gather, out_type=jax.ShapeDtypeStruct((n_chunks * TOP_K * t, CHUNK_W), y_rows.dtype),
                    mesh=_sc_mesh(), name="moe_collect")(y_rows.reshape(n_chunks * n_rows, CHUNK_W), sc_index)
    return out.reshape(n_chunks, TOP_K, t, CHUNK_W)


def _moe_kernel(be_ref, nu_ref, x_ref, w1_ref, b1_ref, w2_ref, b2_ref, o_ref, w1b, w2b, *, dff):
    i = pl.program_id(0)
    e = be_ref[i]
    prev = be_ref[jnp.maximum(i - 1, 0)]

    @pl.when(jnp.logical_or(i == 0, e != prev))
    def _():
        w1b[...] = w1_ref[...].astype(BF16)
        w2b[...] = w2_ref[...].astype(BF16)

    @pl.when(i < nu_ref[0])
    def _():
        halves = [_unpack_bf16_pairs(x_ref[j]) for j in range(PACKED_CHUNKS)]
        xb = jnp.concatenate([h[0] for h in halves] + [h[1] for h in halves], axis=1).astype(BF16)
        gu = jnp.dot(xb, w1b[...], preferred_element_type=F32) + b1_ref[...]
        g = jnp.minimum(gu[:, :dff], SWIGLU_LIMIT)
        u = jnp.clip(gu[:, dff:], -SWIGLU_LIMIT, SWIGLU_LIMIT)
        a = (u + 1.0) * (g * jax.nn.sigmoid(SWIGLU_ALPHA * g))
        y = jnp.dot(a.astype(BF16), w2b[...], preferred_element_type=F32) + b2_ref[...]
        words = _pack_bf16_pairs(y)
        for j in range(PACKED_CHUNKS):
            o_ref[j] = words[:, j * CHUNK_W:(j + 1) * CHUNK_W]

    @pl.when(i >= nu_ref[0])
    def _():
        o_ref[...] = jnp.zeros(o_ref.shape, o_ref.dtype)


def _moe_experts(x_rows, block_expert, n_used, layer, w1, b1, w2, b2):
    _, n_rows, _ = x_rows.shape
    rb = MOE_ROWS
    n_blocks = n_rows // rb - N_EXPERTS
    depth, n_e, d, dff2 = w1.shape
    dff = dff2 // 2
    row_spec = pl.BlockSpec((PACKED_CHUNKS, rb, CHUNK_W), lambda i, be, nu: (0, jnp.minimum(i, nu[0] - 1), 0))
    grid_spec = pltpu.PrefetchScalarGridSpec(
        num_scalar_prefetch=2,
        grid=(n_blocks,),
        in_specs=[row_spec,
                  pl.BlockSpec((None, None, d, dff2), lambda i, be, nu: (layer, be[i], 0, 0)),
                  pl.BlockSpec((None, None, 1, dff2), lambda i, be, nu: (layer, be[i], 0, 0)),
                  pl.BlockSpec((None, None, dff, d), lambda i, be, nu: (layer, be[i], 0, 0)),
                  pl.BlockSpec((None, None, 1, d), lambda i, be, nu: (layer, be[i], 0, 0))],
        out_specs=pl.BlockSpec((PACKED_CHUNKS, rb, CHUNK_W), lambda i, be, nu: (0, i, 0)),
        scratch_shapes=[pltpu.VMEM((d, dff2), BF16), pltpu.VMEM((dff, d), BF16)],
    )
    return pl.pallas_call(
        functools.partial(_moe_kernel, dff=dff),
        grid_spec=grid_spec,
        out_shape=jax.ShapeDtypeStruct((PACKED_CHUNKS, n_rows, CHUNK_W), jnp.int32),
        compiler_params=_params(("arbitrary",), 56),
        name="moe_experts",
    )(block_expert, n_used, x_rows, w1, b1.reshape(depth, n_e, 1, dff2), w2, b2.reshape(depth, n_e, 1, d))


def _combine_kernel(y_ref, gw_ref, x_ref, mod_ref, *rest, d):
    o_ref = rest[-1]
    gw = gw_ref[...]
    pieces = {}
    for j in range(PACKED_CHUNKS):
        accs = [None, None]
        for k in range(TOP_K):
            for side, part in enumerate(_unpack_bf16_pairs(y_ref[j, k])):
                term = gw[:, k:k + 1] * part
                accs[side] = term if accs[side] is None else accs[side] + term
        for side, acc in enumerate(accs):
            c0 = side * (d // 2) + j * CHUNK_W
            pieces[c0] = x_ref[:, c0:c0 + CHUNK_W] + mod_ref[:, 5 * d + c0:5 * d + c0 + CHUNK_W] * acc
    if len(rest) == 1:
        for c0, piece in pieces.items():
            o_ref[:, c0:c0 + CHUNK_W] = piece
    else:
        x2 = jnp.concatenate([pieces[c0] for c0 in sorted(pieces)], axis=1)
        o_ref[...] = _rms(x2) * rest[0][...]


def _moe_combine(y_slots, gw, x1, mods_l, geom, final_g=None):
    t, d = x1.shape
    tm = TOKEN_TILE
    grp = functools.partial(_group_of_tile, tile=tm, n_prompt=geom[0], dec_seq=geom[1])
    in_specs = [pl.BlockSpec((PACKED_CHUNKS, TOP_K, tm, CHUNK_W), lambda i: (0, 0, i, 0)),
                pl.BlockSpec((tm, LANES), lambda i: (i, 0)),
                pl.BlockSpec((tm, d), lambda i: (i, 0)),
                pl.BlockSpec((None, 1, 6 * d), lambda i: (grp(i), 0, 0))]
    args = [y_slots, gw, x1, mods_l]
    if final_g is not None:
        in_specs.append(pl.BlockSpec((1, d), lambda i: (0, 0)))
        args.append(final_g)
    return pl.pallas_call(
        functools.partial(_combine_kernel, d=d),
        grid=(t // tm,),
        in_specs=in_specs,
        out_specs=pl.BlockSpec((tm, d), lambda i: (i, 0)),
        out_shape=jax.ShapeDtypeStruct((t, d), F32),
        compiler_params=_params(("arbitrary",), 24),
        name="moe_combine",
    )(*args)


def kernel(x_prompt, x_sample, c, c_ctx, cache_na_k, cache_na_v, state_hgrn, norm_mix, norm_ffn, w_mod, b_mod,
           conv_w_in, conv_k, conv_w_out, na_w_qkv, na_rpb, na_w_o, hg_w_qig, hg_w_f, hg_lb, hg_norm, hg_w_o,
           moe_w_router, moe_b_router, moe_w1, moe_b1, moe_w2, moe_b2, final_norm):
    batch, seq, d = x_prompt.shape
    dec_batch, dec_seq, _ = x_sample.shape
    depth = w_mod.shape[0]
    n_prompt = batch * seq
    t = n_prompt + dec_batch * dec_seq
    geom = (n_prompt, dec_seq, seq)
    assert seq % TOKEN_TILE == 0 and dec_seq % TOKEN_TILE == 0 and seq % HG_TILE == 0
    assert n_prompt % dec_seq == 0 and dec_seq % GRID_W == 0 and 1 + dec_batch <= SUBLANES
    assert d == 2 * PACKED_CHUNKS * CHUNK_W and t % SC_WINDOW == 0 and MOE_ROWS % SC_WINDOW == 0

    x = jnp.concatenate([x_prompt.reshape(n_prompt, d), x_sample.reshape(dec_batch * dec_seq, d)], axis=0)
    cvecs = jnp.zeros((SUBLANES, d), F32).at[0].set(c_ctx).at[1:1 + dec_batch].set(c)
    mods = _adaln_all(cvecs, w_mod, b_mod)[:, :1 + dec_batch].reshape(depth, 1 + dec_batch, 1, 6 * d)

    lb_soft = jax.nn.softmax(hg_lb.astype(F32), axis=0)
    lower_bounds = jnp.cumsum(lb_soft, axis=0) - lb_soft[0]

    new_k, new_v, new_s = [], [], []
    for l in range(depth):
        kind, j = l % N_MIXERS, l // N_MIXERS
        g1 = norm_mix[l].reshape(1, d)
        g2 = norm_ffn[l].reshape(1, d)
        tail_args = (x, mods[l], g2, moe_w_router[l], moe_b_router[l], geom)
        if kind == 0:
            b, cu = _mixer_in(x, mods[l], g1, conv_w_in[j].astype(BF16), geom, conv=True)
            x1, h2c, idx, gw, rank, counts = _mixer_out("conv", (b, cu), conv_w_out[j].astype(BF16), *tail_args,
                                         extra=(conv_k[j],))
        elif kind == 1:
            q, k, v = _mixer_in(x, mods[l], g1, na_w_qkv[j].astype(BF16), geom)
            o_p = _na_context(q, k, v, batch, seq)
            o_s = _na_latent(q, k, v, cache_na_k[:, j], cache_na_v[:, j], na_rpb[j], n_prompt, dec_batch, dec_seq)
            new_k.append(k[:n_prompt].reshape(batch, seq, NA_HEADS, d // NA_HEADS))
            new_v.append(v[:n_prompt].reshape(batch, seq, NA_HEADS, d // NA_HEADS))
            x1, h2c, idx, gw, rank, counts = _mixer_out("plain", (jnp.concatenate([o_p, o_s], axis=0),),
                                         na_w_o[j].astype(BF16), *tail_args)
        else:
            w_in = jnp.concatenate([hg_w_qig[j], hg_w_f[j, 0], hg_w_f[j, 1]], axis=1).astype(BF16)
            q, v, gout, z0, z1 = _mixer_in(x, mods[l], g1, w_in, geom)
            lb = lower_bounds[l]
            o_p, s_p = _hgrn_scan(q, v, z0, z1, lb, 0, batch, seq, want_state=True)
            s0_t = jnp.swapaxes(state_hgrn[:, j].astype(F32), -1, -2)
            o_s, _ = _hgrn_scan(q, v, z0, z1, lb, n_prompt, dec_batch, dec_seq, s0_t=s0_t)
            new_s.append(s_p)
            x1, h2c, idx, gw, rank, counts = _mixer_out("hgrn", (jnp.concatenate([o_p, o_s], axis=0), gout),
                                         hg_w_o[j].astype(BF16), *tail_args, extra=(hg_norm[j].reshape(1, d),))
        sc_index, block_expert, n_used, n_rows = _route(idx, rank, counts, t)
        x_rows = _sc_dispatch(h2c, sc_index, t, n_rows)
        y_rows = _moe_experts(x_rows, block_expert, n_used, l, moe_w1, moe_b1, moe_w2, moe_b2)
        y_slots = _sc_collect(y_rows, sc_index, t, n_rows)
        x = _moe_combine(y_slots, gw, x1, mods[l], geom,
                         final_g=final_norm.reshape(1, d) if l == depth - 1 else None)

    y = x
    y_prompt = y[:n_prompt].reshape(batch, seq, d)
    y_sample = y[n_prompt:].reshape(dec_batch, dec_seq, d)
    new_na_k = jnp.stack(new_k, axis=1).astype(x_prompt.dtype)
    new_na_v = jnp.stack(new_v, axis=1).astype(x_prompt.dtype)
    new_hgrn_state = jnp.stack(new_s, axis=1).astype(x_prompt.dtype)
    return (y_prompt, y_sample, new_na_k, new_na_v, new_hgrn_state)
```

```python
import functools

import numpy as np
import jax
import jax.numpy as jnp
from jax import lax
from jax.experimental import pallas as pl
from jax.experimental.pallas import tpu as pltpu
from jax.experimental.pallas import tpu_sc as plsc

F32 = jnp.float32
BF16 = jnp.bfloat16
HIGHEST = lax.Precision.HIGHEST

N_MIXERS = 3
GRID_W = 64
CONV_WIDTH = 3
NA_HEADS = 16
NA_ROWS_MAX = 8
NA_COLS = 16
HG_HEADS = 8
HG_KEY_DIM = 128
HG_CHUNK = 32
N_EXPERTS = 32
TOP_K = 4
SWIGLU_LIMIT = 7.0
SWIGLU_ALPHA = 1.702
EPS = 1e-6

LANES = 128
SUBLANES = 8
VMEM_LIMIT_CAP = 60 * 1024 * 1024

NEG_BIG = -1e30
TOKEN_TILE = 256
MOE_ROWS = 512
HG_TILE = 128
HG_HEADS_PER_STEP = 2
PACKED_CHUNKS = 2
CHUNK_W = 256
SC_WINDOW = 128


def _params(sem, vmem_mb, flags=None):
    return pltpu.CompilerParams(dimension_semantics=sem, flags=flags,
                                vmem_limit_bytes=min(vmem_mb * 1024 * 1024, VMEM_LIMIT_CAP))


def _rms(x):
    return x * lax.rsqrt(jnp.mean(x * x, axis=-1, keepdims=True) + EPS)


def _modulate(x, g, shift, scale):
    return (_rms(x) * g) * (1.0 + scale) + shift


def _bf16_part(x):
    bits = lax.bitcast_convert_type(x, jnp.uint32) & jnp.uint32(0xFFFF0000)
    return lax.bitcast_convert_type(bits, F32)


def _pack_bf16_pairs(x):
    n = x.shape[1]
    bits = lax.bitcast_convert_type(x, jnp.uint32)
    top = (bits + (jnp.uint32(0x7FFF) + ((bits >> 16) & jnp.uint32(1)))) & jnp.uint32(0xFFFF0000)
    return lax.bitcast_convert_type(top[:, :n // 2] | (top[:, n // 2:] >> 16), jnp.int32)


def _unpack_bf16_pairs(words):
    bits = lax.bitcast_convert_type(words, jnp.uint32)
    return (lax.bitcast_convert_type(bits & jnp.uint32(0xFFFF0000), F32),
            lax.bitcast_convert_type(bits << 16, F32))


def _group_of_tile(i, tile, n_prompt, dec_seq):
    start = i * tile
    return jnp.where(start < n_prompt, 0, 1 + (start - n_prompt) // dec_seq)


def _adaln_kernel(c_ref, w_ref, b_ref, o_ref):
    cv = c_ref[...]
    s = cv * jax.nn.sigmoid(cv)
    o_ref[...] = jnp.dot(s, w_ref[...], precision=HIGHEST, preferred_element_type=F32) + b_ref[...]


def _adaln_all(cvecs, w_mod, b_mod):
    depth, d, n = w_mod.shape
    tn = 2048
    return pl.pallas_call(
        _adaln_kernel,
        grid=(depth, n // tn),
        in_specs=[pl.BlockSpec((SUBLANES, d), lambda l, j: (0, 0)),
                  pl.BlockSpec((None, d, tn), lambda l, j: (l, 0, j)),
                  pl.BlockSpec((None, 1, tn), lambda l, j: (l, 0, j))],
        out_specs=pl.BlockSpec((None, SUBLANES, tn), lambda l, j: (l, 0, j)),
        out_shape=jax.ShapeDtypeStruct((depth, SUBLANES, n), F32),
        compiler_params=_params(("arbitrary", "arbitrary"), 40),
        name="adaln",
    )(cvecs, w_mod, b_mod.reshape(depth, 1, n))


def _in_kernel(x_ref, mod_ref, g_ref, w_ref, *out_refs, d, conv):
    mod = mod_ref[...]
    h = _modulate(x_ref[...], g_ref[...], mod[:, 0:d], mod[:, d:2 * d]).astype(BF16)
    if conv:
        b_ref, cu_ref = out_refs
        b_ref[...] = jnp.dot(h, w_ref[:, 0:d], preferred_element_type=F32)
        c = jnp.dot(h, w_ref[:, d:2 * d], preferred_element_type=F32)
        u = jnp.dot(h, w_ref[:, 2 * d:3 * d], preferred_element_type=F32)
        cu_ref[...] = c * u
    else:
        for j, o_ref in enumerate(out_refs):
            o_ref[...] = jnp.dot(h, w_ref[:, j * d:(j + 1) * d], preferred_element_type=F32)


def _mixer_in(x, mods_l, g, w, geom, conv=False, rows=None):
    d = x.shape[1]
    row0, t = rows if rows is not None else (0, x.shape[0])
    n = w.shape[1]
    n_out = 2 if conv else n // d
    tm = TOKEN_TILE
    t0 = row0 // tm
    grp = functools.partial(_group_of_tile, tile=tm, n_prompt=geom[0], dec_seq=geom[1])
    return pl.pallas_call(
        functools.partial(_in_kernel, d=d, conv=conv),
        grid=(t // tm,),
        in_specs=[pl.BlockSpec((tm, d), lambda i: (t0 + i, 0)),
                  pl.BlockSpec((None, 1, 6 * d), lambda i: (grp(t0 + i), 0, 0)),
                  pl.BlockSpec((1, d), lambda i: (0, 0)),
                  pl.BlockSpec((d, n), lambda i: (0, 0))],
        out_specs=[pl.BlockSpec((tm, d), lambda i: (i, 0))] * n_out,
        out_shape=[jax.ShapeDtypeStruct((t, d), F32)] * n_out,
        compiler_params=_params(("arbitrary",), 48),
        name="mixer_in",
    )(x, mods_l, g, w)


def _tail(m, x_ref, mod, g2_ref, wr_ref, br_ref, outs, run_ref, d):
    x1_ref, h2_ref, idx_ref, gw_ref, rank_ref, cnt_ref = outs
    x1 = x_ref[...] + mod[:, 2 * d:3 * d] * m
    x1_ref[...] = x1
    h2 = _modulate(x1, g2_ref[...], mod[:, 3 * d:4 * d], mod[:, 4 * d:5 * d])
    words = _pack_bf16_pairs(h2)
    for j in range(PACKED_CHUNKS):
        h2_ref[j] = words[:, j * CHUNK_W:(j + 1) * CHUNK_W]
    h2_top = _bf16_part(h2)
    h2_hi = h2_top.astype(BF16)
    h2_lo = (h2 - h2_top).astype(BF16)
    logits = (jnp.dot(h2_hi, wr_ref[0], preferred_element_type=F32)
              + jnp.dot(h2_lo, wr_ref[0], preferred_element_type=F32)
              + jnp.dot(h2_hi, wr_ref[1], preferred_element_type=F32)) + br_ref[...]
    lane = lax.broadcasted_iota(jnp.int32, logits.shape, 1)
    vals, idxs = [], []
    for _ in range(TOP_K):
        mx = jnp.max(logits, axis=-1, keepdims=True)
        ix = jnp.argmax(logits, axis=-1, keepdims=True).astype(jnp.int32)
        vals.append(mx)
        idxs.append(ix)
        logits = jnp.where(lane == ix, NEG_BIG, logits)
    es = [jnp.exp(v - vals[0]) for v in vals]
    den = es[0] + es[1] + es[2] + es[3]

    @pl.when(pl.program_id(0) == 0)
    def _():
        run_ref[...] = jnp.zeros(run_ref.shape, F32)

    tm = lane.shape[0]
    earlier = (lax.broadcasted_iota(jnp.int32, (tm, tm), 0)
               > lax.broadcasted_iota(jnp.int32, (tm, tm), 1)).astype(BF16)
    seen = run_ref[...]
    idx_out = jnp.zeros(lane.shape, jnp.int32)
    gw_out = jnp.zeros(lane.shape, F32)
    rank_out = jnp.zeros(lane.shape, jnp.int32)
    for k in range(TOP_K):
        hit = lane == idxs[k]
        before = jnp.dot(earlier, hit.astype(BF16), preferred_element_type=F32) + seen
        rank_k = jnp.sum(jnp.where(hit, before, 0.0), axis=-1, keepdims=True)
        seen = seen + jnp.sum(hit.astype(F32), axis=0, keepdims=True)
        idx_out = jnp.where(lane == k, idxs[k], idx_out)
        gw_out = jnp.where(lane == k, es[k] / den, gw_out)
        rank_out = jnp.where(lane == k, rank_k.astype(jnp.int32), rank_out)
    run_ref[...] = seen
    cnt_ref[...] = seen
    idx_ref[...] = idx_out
    gw_ref[...] = gw_out
    rank_ref[...] = rank_out


def _plain_out_kernel(a_ref, w_ref, x_ref, mod_ref, g2_ref, wr_ref, br_ref, *rest, d):
    m = jnp.dot(a_ref[...].astype(BF16), w_ref[...], preferred_element_type=F32)
    _tail(m, x_ref, mod_ref[...], g2_ref, wr_ref, br_ref, rest[:-1], rest[-1], d)


def _conv_out_kernel(b_ref, cu_ref, prev_ref, next_ref, ck_ref, w_ref, x_ref, mod_ref, g2_ref, wr_ref,
                     br_ref, *rest, d, tm, n_prompt, seq, dec_seq):
    i = pl.program_id(0)
    cu = cu_ref[...]
    row = lax.broadcasted_iota(jnp.int32, (tm, 1), 0)
    start = i * tm
    in_prompt = start < n_prompt
    seq_len = jnp.where(in_prompt, seq, dec_seq)
    pos = lax.rem(jnp.where(in_prompt, start, start - n_prompt), seq_len) + row
    prev = jnp.where(row == 0, prev_ref[SUBLANES - 1:SUBLANES, :], pltpu.roll(cu, 1, 0))
    prev = jnp.where(pos == 0, 0.0, prev)
    nxt = jnp.where(row == tm - 1, next_ref[0:1, :], pltpu.roll(cu, tm - 1, 0))
    nxt = jnp.where(pos == seq_len - 1, 0.0, nxt)
    conv = prev * ck_ref[0:1, :] + cu * ck_ref[1:2, :] + nxt * ck_ref[2:3, :]
    m = jnp.dot((b_ref[...] * conv).astype(BF16), w_ref[...], preferred_element_type=F32)
    _tail(m, x_ref, mod_ref[...], g2_ref, wr_ref, br_ref, rest[:-1], rest[-1], d)


def _hgrn_out_kernel(o_ref, gout_ref, ng_ref, w_ref, x_ref, mod_ref, g2_ref, wr_ref, br_ref, *rest, d):
    parts = []
    for h in range(HG_HEADS):
        oh = o_ref[:, h * LANES:(h + 1) * LANES]
        parts.append(oh * lax.rsqrt(jnp.mean(oh * oh, axis=-1, keepdims=True) + EPS))
    gout = gout_ref[...]
    y = (jnp.concatenate(parts, axis=1) * ng_ref[...]) * (gout * jax.nn.sigmoid(gout))
    m = jnp.dot(y.astype(BF16), w_ref[...], preferred_element_type=F32)
    _tail(m, x_ref, mod_ref[...], g2_ref, wr_ref, br_ref, rest[:-1], rest[-1], d)


def _mixer_out(kind, acts, w_out, x, mods_l, g2, w_router, b_router, geom, extra=()):
    t, d = x.shape
    tm = TOKEN_TILE
    n_prompt, dec_seq, seq = geom
    grp = functools.partial(_group_of_tile, tile=tm, n_prompt=n_prompt, dec_seq=dec_seq)
    row_spec = pl.BlockSpec((tm, d), lambda i: (i, 0))
    wr = jnp.zeros((d, LANES), F32).at[:, :N_EXPERTS].set(w_router)
    wr_top = _bf16_part(wr)
    wr = jnp.stack([wr_top.astype(BF16), (wr - wr_top).astype(BF16)])
    br = jnp.full((1, LANES), NEG_BIG, F32).at[0, :N_EXPERTS].set(b_router)
    common_specs = [pl.BlockSpec((d, d), lambda i: (0, 0)),
                    row_spec,
                    pl.BlockSpec((None, 1, 6 * d), lambda i: (grp(i), 0, 0)),
                    pl.BlockSpec((1, d), lambda i: (0, 0)),
                    pl.BlockSpec((2, d, LANES), lambda i: (0, 0, 0)),
                    pl.BlockSpec((1, LANES), lambda i: (0, 0))]
    common_args = [w_out, x, mods_l, g2, wr, br]
    if kind == "plain":
        body = functools.partial(_plain_out_kernel, d=d)
        specs = [row_spec] + common_specs
        args = list(acts) + common_args
    elif kind == "conv":
        body = functools.partial(_conv_out_kernel, d=d, tm=tm, n_prompt=n_prompt, seq=seq, dec_seq=dec_seq)
        per = tm // SUBLANES
        last = t // SUBLANES - 1
        specs = [row_spec, row_spec,
                 pl.BlockSpec((SUBLANES, d), lambda i: (jnp.maximum(i * per - 1, 0), 0)),
                 pl.BlockSpec((SUBLANES, d), lambda i: (jnp.minimum((i + 1) * per, last), 0)),
                 pl.BlockSpec((CONV_WIDTH, d), lambda i: (0, 0))] + common_specs
        b, cu = acts
        args = [b, cu, cu, cu, extra[0]] + common_args
    else:
        body = functools.partial(_hgrn_out_kernel, d=d)
        specs = [row_spec, row_spec, pl.BlockSpec((1, d), lambda i: (0, 0))] + common_specs
        args = list(acts) + [extra[0]] + common_args
    lane_spec = pl.BlockSpec((tm, LANES), lambda i: (i, 0))
    return pl.pallas_call(
        body,
        grid=(t // tm,),
        in_specs=specs,
        out_specs=[row_spec, pl.BlockSpec((PACKED_CHUNKS, tm, CHUNK_W), lambda i: (0, i, 0)),
                   lane_spec, lane_spec, lane_spec, pl.BlockSpec((1, LANES), lambda i: (0, 0))],
        out_shape=[jax.ShapeDtypeStruct((t, d), F32), jax.ShapeDtypeStruct((PACKED_CHUNKS, t, CHUNK_W), jnp.int32),
                   jax.ShapeDtypeStruct((t, LANES), jnp.int32), jax.ShapeDtypeStruct((t, LANES), F32),
                   jax.ShapeDtypeStruct((t, LANES), jnp.int32), jax.ShapeDtypeStruct((1, LANES), F32)],
        scratch_shapes=[pltpu.VMEM((1, LANES), F32)],
        compiler_params=_params(("arbitrary",), 40),
        name="mixer_out_" + kind,
    )(*args)


def _softmax_rows(parts):
    mx = functools.reduce(jnp.maximum, [jnp.max(s, axis=-1, keepdims=True) for s in parts])
    es = [jnp.exp(s - mx) for s in parts]
    den = functools.reduce(lambda a, b: a + b, [jnp.sum(e, axis=-1, keepdims=True) for e in es])
    return [e / den for e in es]


def _head_pair_queries(q, dh):
    first = lax.broadcasted_iota(jnp.int32, (1, q.shape[1]), 1) < dh
    return first, (jnp.where(first, q, 0.0).astype(BF16), jnp.where(first, 0.0, q).astype(BF16))


def _na_ctx_kernel(q_ref, k_ref, v_ref, o_ref, *, dh, scale):
    nt = (((1,), (1,)), ((), ()))
    for pair in range(NA_HEADS * dh // LANES):
        sl = slice(pair * LANES, (pair + 1) * LANES)
        first, queries = _head_pair_queries(q_ref[:, sl], dh)
        n_q = q_ref.shape[0]
        q2 = jnp.concatenate(queries, axis=0)
        (p,) = _softmax_rows([lax.dot_general(q2, k_ref[:, sl].astype(BF16), nt,
                                              preferred_element_type=F32) * scale])
        o2 = jnp.dot(p.astype(BF16), v_ref[:, sl].astype(BF16), preferred_element_type=F32)
        o_ref[:, sl] = jnp.where(first, o2[:n_q], o2[n_q:])


def _na_context(q, k, v, batch, seq, t_total, dec_seq):
    d = q.shape[1]
    dh = d // NA_HEADS
    per = dec_seq // seq
    spec = pl.BlockSpec((seq, d), lambda b: (b, 0))
    return pl.pallas_call(
        functools.partial(_na_ctx_kernel, dh=dh, scale=dh ** -0.5),
        grid=(batch,),
        in_specs=[spec, spec, spec],
        out_specs=pl.BlockSpec((None, seq, d), lambda b: (b // per, b % per, 0)),
        out_shape=jax.ShapeDtypeStruct((t_total // dec_seq, dec_seq, d), F32),
        compiler_params=_params(("arbitrary",), 32),
        name="na_context",
    )(q, k, v)


def _na_lat_kernel(*refs, dh, scale, kr, n_batch):
    per = 1 + 2 * kr
    ck_ref, cv_ref, bias_ref, _, o_ref = refs[n_batch * per:]
    nt = (((1,), (1,)), ((), ()))
    heads_per_group = LANES // dh
    for pair in range(NA_HEADS // heads_per_group):
        sl = slice(pair * LANES, (pair + 1) * LANES)
        for b in range(n_batch):
            q_ref = refs[b * per]
            k_refs = refs[b * per + 1:b * per + 1 + kr]
            v_refs = refs[b * per + 1 + kr:(b + 1) * per]
            first, queries = _head_pair_queries(q_ref[:, sl], dh)
            n_q = q_ref.shape[0]
            q2 = jnp.concatenate(queries, axis=0)
            bias2 = jnp.concatenate([bias_ref[pair * heads_per_group + j] for j in range(heads_per_group)], axis=0)
            kw = jnp.concatenate([r[:, sl] for r in k_refs], axis=0).astype(BF16)
            vw = jnp.concatenate([r[:, sl] for r in v_refs], axis=0).astype(BF16)
            s_nb = lax.dot_general(q2, kw, nt, preferred_element_type=F32) * scale + bias2
            s_cx = lax.dot_general(q2, ck_ref[b, :, sl].astype(BF16), nt, preferred_element_type=F32) * scale
            p_nb, p_cx = _softmax_rows([s_nb, s_cx])
            o2 = (jnp.dot(p_nb.astype(BF16), vw, preferred_element_type=F32)
                  + jnp.dot(p_cx.astype(BF16), cv_ref[b, :, sl].astype(BF16), preferred_element_type=F32))
            o_ref[b, :, sl] = jnp.where(first, o2[:n_q], o2[n_q:])


def _na_bias_table(rpb, kr):
    col = np.arange(GRID_W)
    col_start = np.clip(col - NA_COLS // 2, 0, GRID_W - NA_COLS)
    delta = col[None, :] - col[:, None] + (NA_COLS - 1)
    inside = (col[None, :] >= col_start[:, None]) & (col[None, :] < col_start[:, None] + NA_COLS)
    pick = (np.arange(2 * NA_COLS - 1)[:, None, None] == delta[None]).astype(np.float32)
    tab = jnp.einsum("hrk,kwc->hrwc", rpb, jnp.asarray(pick), precision=HIGHEST)
    tab = jnp.where(jnp.asarray(inside)[None, None], tab, NEG_BIG)
    out = []
    for d0 in range(NA_ROWS_MAX):
        rows = tab[:, d0:d0 + kr]
        out.append(rows.transpose(0, 2, 1, 3).reshape(NA_HEADS, GRID_W, kr * GRID_W))
    return jnp.stack(out, axis=0)


def _na_latent(q, k, v, ctx_k, ctx_v, rpb, out_buf, dec_batch, dec_seq):
    d = q.shape[1]
    dh = d // NA_HEADS
    rows = dec_seq // GRID_W
    kr = min(NA_ROWS_MAX, rows)
    base = 0
    past = ctx_k.shape[1]
    out_slab = (out_buf.shape[0] - dec_batch) // dec_batch
    bias = _na_bias_table(rpb, kr)

    def row_start(r):
        return jnp.clip(r - kr // 2, 0, rows - kr)

    def win_spec(b, j):
        return pl.BlockSpec((GRID_W, d), lambda r: (base + b * rows + row_start(r) + j, 0))

    in_specs, args = [], []
    for b in range(dec_batch):
        in_specs += ([pl.BlockSpec((GRID_W, d), lambda r, b=b: (base + b * rows + r, 0))]
                     + [win_spec(b, j) for j in range(kr)] * 2)
        args += [q] + [k] * kr + [v] * kr
    in_specs += ([pl.BlockSpec((dec_batch, past, d), lambda r: (0, 0, 0))] * 2
                 + [pl.BlockSpec((None, NA_HEADS, GRID_W, kr * GRID_W),
                                 lambda r: (row_start(r) - r + NA_ROWS_MAX - 1, 0, 0, 0))])
    in_specs.append(pl.BlockSpec(memory_space=pl.ANY))
    args += [ctx_k.reshape(dec_batch, past, d), ctx_v.reshape(dec_batch, past, d), bias, out_buf]
    return pl.pallas_call(
        functools.partial(_na_lat_kernel, dh=dh, scale=dh ** -0.5, kr=kr, n_batch=dec_batch),
        grid=(rows,),
        in_specs=in_specs,
        out_specs=pl.BlockSpec((dec_batch, GRID_W, d), lambda r: (out_slab, r, 0)),
        out_shape=jax.ShapeDtypeStruct(out_buf.shape, F32),
        input_output_aliases={len(args) - 1: 0},
        compiler_params=_params(("arbitrary",), 40),
        name="na_latent",
    )(*args)


def _hgrn_scan_kernel(*refs, seq_len, has_s0, has_buf, want_state):
    q_ref, v_ref, z0_ref, z1_ref, lb_ref, tri_ref = refs[:6]
    rest = list(refs[6:])
    s0_ref = rest.pop(0) if has_s0 else None
    if has_buf:
        rest.pop(0)
    o_ref = rest.pop(0)
    sf_ref = rest.pop(0) if want_state else None
    ob_ref = rest.pop(0)
    c, r = HG_CHUNK, HG_TILE
    nc, nt = r // c, seq_len // r
    dk = HG_KEY_DIM
    tpos8 = lax.broadcasted_iota(jnp.int32, (1, SUBLANES, 1), 1)
    row = lax.broadcasted_iota(jnp.int32, (r, r), 0)
    col = lax.broadcasted_iota(jnp.int32, (r, r), 1)
    same16 = (row // (2 * SUBLANES)) == (col // (2 * SUBLANES))
    same32 = (row // c) == (col // c)
    nt_dims = (((1,), (1,)), ((), ()))
    tn_dims = (((0,), (0,)), ((), ()))

    def tile(dr, hh, i, st):
        z_ref = (z0_ref, z1_ref)[dr]
        cols = slice(hh * dk, (hh + 1) * dk)
        lb = lb_ref[dr:dr + 1, cols]
        tri = tri_ref[dr]
        ti = i if dr == 0 else nt - 1 - i
        r0 = pl.multiple_of(ti * r, r)
        q = q_ref[pl.ds(r0, r), cols]
        v = v_ref[pl.ds(r0, r), cols]
        z = z_ref[pl.ds(r0, r), cols]
        g = jnp.log(lb + (1.0 - lb) * jax.nn.sigmoid(z))
        k = (1.0 - lb) * jax.nn.sigmoid(-z)
        b = jnp.dot(tri, g, precision=HIGHEST, preferred_element_type=F32)
        b3, q3, k3, v3 = (a.reshape(nc, c, dk) for a in (b, q, k, v))
        nb = r // SUBLANES
        b8, q8, k8, v8 = (a.reshape(nb, SUBLANES, dk) for a in (b, q, k, v))
        oi = jnp.zeros((nb, SUBLANES, dk), F32)
        for s in range(SUBLANES):
            keep = (tpos8 >= s) if dr == 0 else (tpos8 <= s)
            e = jnp.exp(jnp.where(keep, b8 - b8[:, s:s + 1, :], NEG_BIG))
            a = jnp.sum(q8 * e * k8[:, s:s + 1, :], axis=-1, keepdims=True)
            oi = oi + a * v8[:, s:s + 1, :]
        att = None
        for span, same_group in ((2 * SUBLANES, same16), (c, same32)):
            half = span // 2
            bg = b.reshape(r // span, span, dk)
            pos = lax.broadcasted_iota(jnp.int32, (1, span, 1), 1)
            edge = bg[:, half - 1:half, :] if dr == 0 else bg[:, half:half + 1, :]
            is_query = (pos >= half) if dr == 0 else (pos < half)
            qf = q.reshape(bg.shape) * jnp.exp(jnp.where(is_query, bg - edge, NEG_BIG))
            kf = k.reshape(bg.shape) * jnp.exp(jnp.where(is_query, NEG_BIG, edge - bg))
            part = lax.dot_general(qf.reshape(r, dk).astype(BF16), kf.reshape(r, dk).astype(BF16), nt_dims,
                                   preferred_element_type=F32)
            part = jnp.where(same_group, part, 0.0)
            att = part if att is None else att + part
        oi = oi.reshape(r, dk) + jnp.dot(att.astype(BF16), v.astype(BF16), preferred_element_type=F32)
        oi = oi.reshape(nc, c, dk)
        bl = b3[:, c - 1:c, :] if dr == 0 else b3[:, 0:1, :]
        qt = (q3 * jnp.exp(b3)).astype(BF16)
        kt = (k3 * jnp.exp(bl - b3)).astype(BF16)
        vb = v3.astype(BF16)
        outs = [None] * nc
        for ci in (range(nc) if dr == 0 else range(nc - 1, -1, -1)):
            oc = lax.dot_general(qt[ci], st.astype(BF16), nt_dims, preferred_element_type=F32)
            outs[ci] = oi[ci] + oc
            kv = lax.dot_general(vb[ci], kt[ci], tn_dims, preferred_element_type=F32)
            st = st * jnp.exp(bl[ci]) + kv
        return r0, jnp.concatenate(outs, axis=0), st

    chains = [(dr, hh) for hh in range(HG_HEADS_PER_STEP) for dr in range(2)]

    def step(i, carry):
        new = []
        for (dr, hh), st in zip(chains, carry):
            r0, o, st = tile(dr, hh, i, st)
            (o_ref if dr == 0 else ob_ref)[pl.ds(r0, r), hh * dk:(hh + 1) * dk] = o
            new.append(st)
        return tuple(new)

    init = tuple(s0_ref[dr, hh] if has_s0 else jnp.zeros((dk, dk), F32) for dr, hh in chains)
    final = lax.fori_loop(0, nt, step, init)
    o_ref[...] += ob_ref[...]
    if want_state:
        for (dr, hh), st in zip(chains, final):
            sf_ref[dr, hh] = st.T


def _hgrn_tri():
    t = np.arange(HG_TILE)
    same = (t[:, None] // HG_CHUNK) == (t[None, :] // HG_CHUNK)
    fwd = same & (t[None, :] <= t[:, None])
    bwd = same & (t[None, :] >= t[:, None])
    return jnp.asarray(np.stack([fwd, bwd]).astype(np.float32))


def _hgrn_scan(q, v, z0, z1, lb, row0, n_seq, seq_len, s0_t=None, want_state=False, out_buf=None):
    t_total, d = q.shape
    dk = HG_KEY_DIM
    blk0 = row0 // seq_len
    hp = HG_HEADS_PER_STEP
    seq_spec = pl.BlockSpec((seq_len, hp * dk), lambda s, h: (blk0 + s, h))
    st_spec = pl.BlockSpec((None, 2, hp, dk, dk), lambda s, h: (s, 0, h, 0, 0))
    in_specs = [seq_spec] * 4 + [pl.BlockSpec((2, hp * dk), lambda s, h: (0, h)),
                                 pl.BlockSpec((2, HG_TILE, HG_TILE), lambda s, h: (0, 0, 0))]
    args = [q, v, z0, z1, lb, _hgrn_tri()]
    if s0_t is not None:
        in_specs.append(st_spec)
        args.append(s0_t)
    aliases = {}
    if out_buf is not None:
        in_specs.append(pl.BlockSpec(memory_space=pl.ANY))
        args.append(out_buf)
        aliases = {len(args) - 1: 0}
    out_specs = [seq_spec]
    out_shape = [jax.ShapeDtypeStruct((t_total, d), F32)]
    if want_state:
        out_specs.append(st_spec)
        out_shape.append(jax.ShapeDtypeStruct((n_seq, 2, HG_HEADS, dk, dk), F32))
    res = pl.pallas_call(
        functools.partial(_hgrn_scan_kernel, seq_len=seq_len, has_s0=s0_t is not None,
                          has_buf=out_buf is not None, want_state=want_state),
        grid=(n_seq, HG_HEADS // hp),
        in_specs=in_specs,
        out_specs=out_specs,
        out_shape=out_shape,
        input_output_aliases=aliases,
        scratch_shapes=[pltpu.VMEM((seq_len, hp * dk), F32)],
        compiler_params=_params(("arbitrary", "arbitrary"), 32),
        name="hgrn_scan",
    )(*args)
    return res if want_state else (res[0], None)


def _route(idx, rank, counts, t):
    n = t * TOP_K
    rb = MOE_ROWS
    n_blocks = -(-n // rb) + N_EXPERTS
    n_rows = (n_blocks + N_EXPERTS) * rb
    experts = jnp.arange(N_EXPERTS, dtype=jnp.int32)
    counts = counts[0, :N_EXPERTS].astype(jnp.int32)
    padded = (counts + rb - 1) // rb * rb
    padded_end = jnp.cumsum(padded)
    start = padded_end - padded
    e = idx[:, :TOP_K]
    dest = jnp.sum(jnp.where(e[:, :, None] == experts, start, 0), axis=-1) + rank[:, :TOP_K]
    dest_km = dest.T.astype(jnp.int32)
    q = jnp.arange(rb, dtype=jnp.int32)
    spare = n_blocks * rb + experts[:, None] * rb + q[None, :]
    pad_rows = jnp.where(q[None, :] < (padded - counts)[:, None], (start + counts)[:, None] + q[None, :], spare)
    per_chunk = jnp.concatenate([dest_km.reshape(-1), pad_rows.reshape(-1)])
    sc_index = (jnp.arange(PACKED_CHUNKS, dtype=jnp.int32)[:, None] * n_rows + per_chunk[None, :]).reshape(1, -1)
    block_start = jnp.arange(n_blocks, dtype=jnp.int32) * rb
    block_expert = jnp.minimum(jnp.sum((padded_end[None, :] <= block_start[:, None]).astype(jnp.int32), axis=1),
                               N_EXPERTS - 1)
    n_used = (padded_end[-1] // rb).reshape(1)
    return sc_index, block_expert, n_used, n_rows


def _sc_mesh():
    return plsc.VectorSubcoreMesh(core_axis_name="c", subcore_axis_name="s")


def _sc_dispatch(h2c, sc_index, t, n_rows):
    win = SC_WINDOW
    tw = t // win
    nw = (t * TOP_K + N_EXPERTS * MOE_ROWS) // win
    n_chunks = h2c.shape[0]

    def scatter(x_hbm, i_hbm, o_hbm):
        def body(x_vmem, i_vmem):
            pltpu.sync_copy(x_vmem, o_hbm.at[i_vmem.at[0]])

        pltpu.emit_pipeline(
            body,
            grid=(n_chunks * nw,),
            in_specs=[pl.BlockSpec((win, CHUNK_W), index_map=lambda i: ((i // nw) * tw + (i % nw) % tw, 0)),
                      pl.BlockSpec((1, win), index_map=lambda i: (0, i))],
            out_specs=[],
            core_axis_name=("c", "s"),
            dimension_semantics=(pltpu.PARALLEL,),
        )(x_hbm, i_hbm)

    out = pl.kernel(scatter, out_type=jax.ShapeDtypeStruct((n_chunks * n_rows, CHUNK_W), h2c.dtype),
                    mesh=_sc_mesh(), name="moe_dispatch")(h2c.reshape(n_chunks * t, CHUNK_W), sc_index)
    return out.reshape(n_chunks, n_rows, CHUNK_W)


def _sc_collect(y_rows, sc_index, t, n_rows):
    win = SC_WINDOW
    nw = (t * TOP_K + N_EXPERTS * MOE_ROWS) // win
    aw = t * TOP_K // win
    n_chunks = y_rows.shape[0]

    def gather(y_hbm, i_hbm, o_hbm):
        def body(i_vmem, o_vmem):
            pltpu.sync_copy(y_hbm.at[i_vmem.at[0]], o_vmem)

        pltpu.emit_pipeline(
            body,
            grid=(n_chunks * aw,),
            in_specs=[pl.BlockSpec((1, win), index_map=lambda i: (0, (i // aw) * nw + i % aw))],
            out_specs=[pl.BlockSpec((win, CHUNK_W), index_map=lambda i: (i, 0))],
            core_axis_name=("c", "s"),
            dimension_semantics=(pltpu.PARALLEL,),
        )(i_hbm, o_hbm)

    out = pl.kernel(gather, out_type=jax.ShapeDtypeStruct((n_chunks * TOP_K * t, CHUNK_W), y_rows.dtype),
                    mesh=_sc_mesh(), name="moe_collect")(y_rows.reshape(n_chunks * n_rows, CHUNK_W), sc_index)
    return out.reshape(n_chunks, TOP_K, t, CHUNK_W)


def _moe_kernel(be_ref, nu_ref, x_ref, w1_ref, b1_ref, w2_ref, b2_ref, o_ref, w1b, w2b, *, dff):
    i = pl.program_id(0)
    e = be_ref[i]
    prev = be_ref[jnp.maximum(i - 1, 0)]

    @pl.when(jnp.logical_or(i == 0, e != prev))
    def _():
        w1b[...] = w1_ref[...].astype(BF16)
        w2b[...] = w2_ref[...].astype(BF16)

    @pl.when(i < nu_ref[0])
    def _():
        halves = [_unpack_bf16_pairs(x_ref[j]) for j in range(PACKED_CHUNKS)]
        xb = jnp.concatenate([h[0] for h in halves] + [h[1] for h in halves], axis=1).astype(BF16)
        gu = jnp.dot(xb, w1b[...], preferred_element_type=F32) + b1_ref[...]
        g = jnp.minimum(gu[:, :dff], SWIGLU_LIMIT)
        u = jnp.clip(gu[:, dff:], -SWIGLU_LIMIT, SWIGLU_LIMIT)
        a = (u + 1.0) * (g * jax.nn.sigmoid(SWIGLU_ALPHA * g))
        y = jnp.dot(a.astype(BF16), w2b[...], preferred_element_type=F32) + b2_ref[...]
        words = _pack_bf16_pairs(y)
        for j in range(PACKED_CHUNKS):
            o_ref[j] = words[:, j * CHUNK_W:(j + 1) * CHUNK_W]

    @pl.when(i >= nu_ref[0])
    def _():
        o_ref[...] = jnp.zeros(o_ref.shape, o_ref.dtype)


def _moe_experts(x_rows, block_expert, n_used, layer, w1, b1, w2, b2):
    _, n_rows, _ = x_rows.shape
    rb = MOE_ROWS
    n_blocks = n_rows // rb - N_EXPERTS
    depth, n_e, d, dff2 = w1.shape
    dff = dff2 // 2
    row_spec = pl.BlockSpec((PACKED_CHUNKS, rb, CHUNK_W), lambda i, be, nu: (0, jnp.minimum(i, nu[0] - 1), 0))
    grid_spec = pltpu.PrefetchScalarGridSpec(
        num_scalar_prefetch=2,
        grid=(n_blocks,),
        in_specs=[row_spec,
                  pl.BlockSpec((None, None, d, dff2), lambda i, be, nu: (layer, be[i], 0, 0)),
                  pl.BlockSpec((None, None, 1, dff2), lambda i, be, nu: (layer, be[i], 0, 0)),
                  pl.BlockSpec((None, None, dff, d), lambda i, be, nu: (layer, be[i], 0, 0)),
                  pl.BlockSpec((None, None, 1, d), lambda i, be, nu: (layer, be[i], 0, 0))],
        out_specs=pl.BlockSpec((PACKED_CHUNKS, rb, CHUNK_W), lambda i, be, nu: (0, i, 0)),
        scratch_shapes=[pltpu.VMEM((d, dff2), BF16), pltpu.VMEM((dff, d), BF16)],
    )
    return pl.pallas_call(
        functools.partial(_moe_kernel, dff=dff),
        grid_spec=grid_spec,
        out_shape=jax.ShapeDtypeStruct((PACKED_CHUNKS, n_rows, CHUNK_W), jnp.int32),
        compiler_params=_params(("arbitrary",), 56),
        name="moe_experts",
    )(block_expert, n_used, x_rows, w1, b1.reshape(depth, n_e, 1, dff2), w2, b2.reshape(depth, n_e, 1, d))


def _combine_kernel(y_ref, gw_ref, x_ref, mod_ref, *rest, d):
    o_ref = rest[-1]
    gw = gw_ref[...]
    pieces = {}
    for j in range(PACKED_CHUNKS):
        accs = [None, None]
        for k in range(TOP_K):
            for side, part in enumerate(_unpack_bf16_pairs(y_ref[j, k])):
                term = gw[:, k:k + 1] * part
                accs[side] = term if accs[side] is None else accs[side] + term
        for side, acc in enumerate(accs):
            c0 = side * (d // 2) + j * CHUNK_W
            pieces[c0] = x_ref[:, c0:c0 + CHUNK_W] + mod_ref[:, 5 * d + c0:5 * d + c0 + CHUNK_W] * acc
    if len(rest) == 1:
        for c0, piece in pieces.items():
            o_ref[:, c0:c0 + CHUNK_W] = piece
    else:
        x2 = jnp.concatenate([pieces[c0] for c0 in sorted(pieces)], axis=1)
        o_ref[...] = _rms(x2) * rest[0][...]


def _moe_combine(y_slots, gw, x1, mods_l, geom, final_g=None, rows=None):
    d = x1.shape[1]
    row0, t = rows if rows is not None else (0, x1.shape[0])
    tm = TOKEN_TILE
    t0 = row0 // tm
    grp = functools.partial(_group_of_tile, tile=tm, n_prompt=geom[0], dec_seq=geom[1])
    in_specs = [pl.BlockSpec((PACKED_CHUNKS, TOP_K, tm, CHUNK_W), lambda i: (0, 0, t0 + i, 0)),
                pl.BlockSpec((tm, LANES), lambda i: (t0 + i, 0)),
                pl.BlockSpec((tm, d), lambda i: (t0 + i, 0)),
                pl.BlockSpec((None, 1, 6 * d), lambda i: (grp(t0 + i), 0, 0))]
    args = [y_slots, gw, x1, mods_l]
    if final_g is not None:
        in_specs.append(pl.BlockSpec((1, d), lambda i: (0, 0)))
        args.append(final_g)
    return pl.pallas_call(
        functools.partial(_combine_kernel, d=d),
        grid=(t // tm,),
        in_specs=in_specs,
        out_specs=pl.BlockSpec((tm, d), lambda i: (i, 0)),
        out_shape=jax.ShapeDtypeStruct((t, d), F32),
        compiler_params=_params(("arbitrary",), 24),
        name="moe_combine",
    )(*args)


def kernel(x_prompt, x_sample, c, c_ctx, cache_na_k, cache_na_v, state_hgrn, norm_mix, norm_ffn, w_mod, b_mod,
           conv_w_in, conv_k, conv_w_out, na_w_qkv, na_rpb, na_w_o, hg_w_qig, hg_w_f, hg_lb, hg_norm, hg_w_o,
           moe_w_router, moe_b_router, moe_w1, moe_b1, moe_w2, moe_b2, final_norm):
    batch, seq, d = x_prompt.shape
    dec_batch, dec_seq, _ = x_sample.shape
    depth = w_mod.shape[0]
    n_prompt = batch * seq
    t = n_prompt + dec_batch * dec_seq
    geom = (n_prompt, dec_seq, seq)
    assert seq % TOKEN_TILE == 0 and dec_seq % TOKEN_TILE == 0 and seq % HG_TILE == 0
    assert n_prompt % (dec_seq * dec_batch) == 0 and dec_seq % GRID_W == 0 and 1 + dec_batch <= SUBLANES
    assert d == 2 * PACKED_CHUNKS * CHUNK_W and t % SC_WINDOW == 0 and MOE_ROWS % SC_WINDOW == 0

    x = jnp.concatenate([x_prompt.reshape(n_prompt, d), x_sample.reshape(dec_batch * dec_seq, d)], axis=0)
    cvecs = jnp.zeros((SUBLANES, d), F32).at[0].set(c_ctx).at[1:1 + dec_batch].set(c)
    mods = _adaln_all(cvecs, w_mod, b_mod)[:, :1 + dec_batch].reshape(depth, 1 + dec_batch, 1, 6 * d)

    lb_soft = jax.nn.softmax(hg_lb.astype(F32), axis=0)
    lower_bounds = jnp.cumsum(lb_soft, axis=0) - lb_soft[0]

    new_k, new_v, new_s = [], [], []
    for l in range(depth):
        kind, j = l % N_MIXERS, l // N_MIXERS
        g1 = norm_mix[l].reshape(1, d)
        g2 = norm_ffn[l].reshape(1, d)
        tail_args = (x, mods[l], g2, moe_w_router[l], moe_b_router[l], geom)
        if kind == 0:
            b, cu = _mixer_in(x, mods[l], g1, conv_w_in[j].astype(BF16), geom, conv=True)
            x1, h2c, idx, gw, rank, counts = _mixer_out("conv", (b, cu), conv_w_out[j].astype(BF16), *tail_args,
                                         extra=(conv_k[j],))
        elif kind == 1:
            w_qkv = na_w_qkv[j].astype(BF16)
            qp, kp, vp = _mixer_in(x, mods[l], g1, w_qkv, geom, rows=(0, n_prompt))
            qs, ks, vs = _mixer_in(x, mods[l], g1, w_qkv, geom, rows=(n_prompt, t - n_prompt))
            o = _na_context(qp, kp, vp, batch, seq, t, dec_seq)
            o = _na_latent(qs, ks, vs, cache_na_k[:, j], cache_na_v[:, j], na_rpb[j], o, dec_batch, dec_seq)
            new_k.append(kp.reshape(batch, seq, NA_HEADS, d // NA_HEADS))
            new_v.append(vp.reshape(batch, seq, NA_HEADS, d // NA_HEADS))
            x1, h2c, idx, gw, rank, counts = _mixer_out("plain", (o.reshape(t, d),),
                                         na_w_o[j].astype(BF16), *tail_args)
        else:
            w_in = jnp.concatenate([hg_w_qig[j], hg_w_f[j, 0], hg_w_f[j, 1]], axis=1).astype(BF16)
            q, v, gout, z0, z1 = _mixer_in(x, mods[l], g1, w_in, geom)
            lb = lower_bounds[l]
            o, s_p = _hgrn_scan(q, v, z0, z1, lb, 0, batch, seq, want_state=True)
            s0_t = jnp.swapaxes(state_hgrn[:, j].astype(F32), -1, -2)
            o, _ = _hgrn_scan(q, v, z0, z1, lb, n_prompt, dec_batch, dec_seq, s0_t=s0_t, out_buf=o)
            new_s.append(s_p)
            x1, h2c, idx, gw, rank, counts = _mixer_out("hgrn", (o, gout),
                                         hg_w_o[j].astype(BF16), *tail_args, extra=(hg_norm[j].reshape(1, d),))
        sc_index, block_expert, n_used, n_rows = _route(idx, rank, counts, t)
        x_rows = _sc_dispatch(h2c, sc_index, t, n_rows)
        y_rows = _moe_experts(x_rows, block_expert, n_used, l, moe_w1, moe_b1, moe_w2, moe_b2)
        y_slots = _sc_collect(y_rows, sc_index, t, n_rows)
        if l < depth - 1:
            x = _moe_combine(y_slots, gw, x1, mods[l], geom)

    final_g = final_norm.reshape(1, d)
    last = (y_slots, gw, x1, mods[depth - 1], geom)
    y_prompt = _moe_combine(*last, final_g=final_g, rows=(0, n_prompt)).reshape(batch, seq, d)
    y_sample = _moe_combine(*last, final_g=final_g, rows=(n_prompt, t - n_prompt)).reshape(dec_batch, dec_seq, d)
    new_na_k = jnp.stack(new_k, axis=1).astype(x_prompt.dtype)
    new_na_v = jnp.stack(new_v, axis=1).astype(x_prompt.dtype)
    new_hgrn_state = jnp.stack(new_s, axis=1).astype(x_prompt.dtype)
    return (y_prompt, y_sample, new_na_k, new_na_v, new_hgrn_state)
```

```python
import functools

import numpy as np
import jax
import jax.numpy as jnp
from jax import lax
from jax.experimental import pallas as pl
from jax.experimental.pallas import tpu as pltpu
from jax.experimental.pallas import tpu_sc as plsc

F32 = jnp.float32
BF16 = jnp.bfloat16
HIGHEST = lax.Precision.HIGHEST

N_MIXERS = 3
GRID_W = 64
CONV_WIDTH = 3
NA_HEADS = 16
NA_ROWS_MAX = 8
NA_COLS = 16
HG_HEADS = 8
HG_KEY_DIM = 128
HG_CHUNK = 32
N_EXPERTS = 32
TOP_K = 4
SWIGLU_LIMIT = 7.0
SWIGLU_ALPHA = 1.702
EPS = 1e-6

LANES = 128
SUBLANES = 8
VMEM_LIMIT_CAP = 60 * 1024 * 1024

NEG_BIG = -1e30
TOKEN_TILE = 256
MOE_ROWS = 512
HG_TILE = 128
HG_HEADS_PER_STEP = 2
PACKED_CHUNKS = 2
CHUNK_W = 256
SC_WINDOW = 128


def _params(sem, vmem_mb, flags=None):
    return pltpu.CompilerParams(dimension_semantics=sem, flags=flags,
                                vmem_limit_bytes=min(vmem_mb * 1024 * 1024, VMEM_LIMIT_CAP))


def _rms(x):
    return x * lax.rsqrt(jnp.mean(x * x, axis=-1, keepdims=True) + EPS)


def _modulate(x, g, shift, scale):
    return (_rms(x) * g) * (1.0 + scale) + shift


def _bf16_part(x):
    bits = lax.bitcast_convert_type(x, jnp.uint32) & jnp.uint32(0xFFFF0000)
    return lax.bitcast_convert_type(bits, F32)


def _pack_bf16_pairs(x):
    n = x.shape[1]
    bits = lax.bitcast_convert_type(x, jnp.uint32)
    top = (bits + (jnp.uint32(0x7FFF) + ((bits >> 16) & jnp.uint32(1)))) & jnp.uint32(0xFFFF0000)
    return lax.bitcast_convert_type(top[:, :n // 2] | (top[:, n // 2:] >> 16), jnp.int32)


def _unpack_bf16_pairs(words):
    bits = lax.bitcast_convert_type(words, jnp.uint32)
    return (lax.bitcast_convert_type(bits & jnp.uint32(0xFFFF0000), F32),
            lax.bitcast_convert_type(bits << 16, F32))


def _group_of_tile(i, tile, n_prompt, dec_seq):
    start = i * tile
    return jnp.where(start < n_prompt, 0, 1 + (start - n_prompt) // dec_seq)


def _adaln_kernel(c_ref, w_ref, b_ref, o_ref):
    cv = c_ref[...]
    s = cv * jax.nn.sigmoid(cv)
    o_ref[...] = jnp.dot(s, w_ref[...], precision=HIGHEST, preferred_element_type=F32) + b_ref[...]


def _adaln_all(cvecs, w_mod, b_mod):
    depth, d, n = w_mod.shape
    tn = 2048
    return pl.pallas_call(
        _adaln_kernel,
        grid=(depth, n // tn),
        in_specs=[pl.BlockSpec((SUBLANES, d), lambda l, j: (0, 0)),
                  pl.BlockSpec((None, d, tn), lambda l, j: (l, 0, j)),
                  pl.BlockSpec((None, 1, tn), lambda l, j: (l, 0, j))],
        out_specs=pl.BlockSpec((None, SUBLANES, tn), lambda l, j: (l, 0, j)),
        out_shape=jax.ShapeDtypeStruct((depth, SUBLANES, n), F32),
        compiler_params=_params(("arbitrary", "arbitrary"), 40),
        name="adaln",
    )(cvecs, w_mod, b_mod.reshape(depth, 1, n))


def _project(x, mod, g_ref, w_ref, out_refs, d, conv):
    h = _modulate(x, g_ref[...], mod[:, 0:d], mod[:, d:2 * d]).astype(BF16)
    if conv:
        b_ref, cu_ref = out_refs
        b_ref[...] = jnp.dot(h, w_ref[:, 0:d], preferred_element_type=F32)
        c = jnp.dot(h, w_ref[:, d:2 * d], preferred_element_type=F32)
        u = jnp.dot(h, w_ref[:, 2 * d:3 * d], preferred_element_type=F32)
        cu_ref[...] = c * u
    else:
        for j, o_ref in enumerate(out_refs):
            o_ref[...] = jnp.dot(h, w_ref[:, j * d:(j + 1) * d], preferred_element_type=F32)


def _in_kernel(x_ref, mod_ref, g_ref, w_ref, *out_refs, d, conv):
    _project(x_ref[...], mod_ref[...], g_ref, w_ref, out_refs, d, conv)


def _mixer_in(x, mods_l, g, w, geom, conv=False, rows=None):
    d = x.shape[1]
    row0, t = rows if rows is not None else (0, x.shape[0])
    n = w.shape[1]
    n_out = 2 if conv else n // d
    tm = TOKEN_TILE
    t0 = row0 // tm
    grp = functools.partial(_group_of_tile, tile=tm, n_prompt=geom[0], dec_seq=geom[1])
    return pl.pallas_call(
        functools.partial(_in_kernel, d=d, conv=conv),
        grid=(t // tm,),
        in_specs=[pl.BlockSpec((tm, d), lambda i: (t0 + i, 0)),
                  pl.BlockSpec((None, 1, 6 * d), lambda i: (grp(t0 + i), 0, 0)),
                  pl.BlockSpec((1, d), lambda i: (0, 0)),
                  pl.BlockSpec((d, n), lambda i: (0, 0))],
        out_specs=[pl.BlockSpec((tm, d), lambda i: (i, 0))] * n_out,
        out_shape=[jax.ShapeDtypeStruct((t, d), F32)] * n_out,
        compiler_params=_params(("arbitrary",), 48),
        name="mixer_in",
    )(x, mods_l, g, w)


def _tail(m, x_ref, mod, g2_ref, wr_ref, br_ref, outs, run_ref, d):
    x1_ref, h2_ref, idx_ref, gw_ref, rank_ref, cnt_ref = outs
    x1 = x_ref[...] + mod[:, 2 * d:3 * d] * m
    x1_ref[...] = x1
    h2 = _modulate(x1, g2_ref[...], mod[:, 3 * d:4 * d], mod[:, 4 * d:5 * d])
    words = _pack_bf16_pairs(h2)
    for j in range(PACKED_CHUNKS):
        h2_ref[j] = words[:, j * CHUNK_W:(j + 1) * CHUNK_W]
    h2_top = _bf16_part(h2)
    h2_hi = h2_top.astype(BF16)
    h2_lo = (h2 - h2_top).astype(BF16)
    logits = (jnp.dot(h2_hi, wr_ref[0], preferred_element_type=F32)
              + jnp.dot(h2_lo, wr_ref[0], preferred_element_type=F32)
              + jnp.dot(h2_hi, wr_ref[1], preferred_element_type=F32)) + br_ref[...]
    lane = lax.broadcasted_iota(jnp.int32, logits.shape, 1)
    vals, idxs = [], []
    for _ in range(TOP_K):
        mx = jnp.max(logits, axis=-1, keepdims=True)
        ix = jnp.argmax(logits, axis=-1, keepdims=True).astype(jnp.int32)
        vals.append(mx)
        idxs.append(ix)
        logits = jnp.where(lane == ix, NEG_BIG, logits)
    es = [jnp.exp(v - vals[0]) for v in vals]
    den = es[0] + es[1] + es[2] + es[3]

    @pl.when(pl.program_id(0) == 0)
    def _():
        run_ref[...] = jnp.zeros(run_ref.shape, F32)

    tm = lane.shape[0]
    earlier = (lax.broadcasted_iota(jnp.int32, (tm, tm), 0)
               > lax.broadcasted_iota(jnp.int32, (tm, tm), 1)).astype(BF16)
    seen = run_ref[...]
    idx_out = jnp.zeros(lane.shape, jnp.int32)
    gw_out = jnp.zeros(lane.shape, F32)
    rank_out = jnp.zeros(lane.shape, jnp.int32)
    for k in range(TOP_K):
        hit = lane == idxs[k]
        before = jnp.dot(earlier, hit.astype(BF16), preferred_element_type=F32) + seen
        rank_k = jnp.sum(jnp.where(hit, before, 0.0), axis=-1, keepdims=True)
        seen = seen + jnp.sum(hit.astype(F32), axis=0, keepdims=True)
        idx_out = jnp.where(lane == k, idxs[k], idx_out)
        gw_out = jnp.where(lane == k, es[k] / den, gw_out)
        rank_out = jnp.where(lane == k, rank_k.astype(jnp.int32), rank_out)
    run_ref[...] = seen
    cnt_ref[...] = seen
    idx_ref[...] = idx_out
    gw_ref[...] = gw_out
    rank_ref[...] = rank_out


def _plain_out_kernel(a_ref, w_ref, x_ref, mod_ref, g2_ref, wr_ref, br_ref, *rest, d):
    m = jnp.dot(a_ref[...].astype(BF16), w_ref[...], preferred_element_type=F32)
    _tail(m, x_ref, mod_ref[...], g2_ref, wr_ref, br_ref, rest[:-1], rest[-1], d)


def _conv_out_kernel(b_ref, cu_ref, prev_ref, next_ref, ck_ref, w_ref, x_ref, mod_ref, g2_ref, wr_ref,
                     br_ref, *rest, d, tm, n_prompt, seq, dec_seq):
    i = pl.program_id(0)
    cu = cu_ref[...]
    row = lax.broadcasted_iota(jnp.int32, (tm, 1), 0)
    start = i * tm
    in_prompt = start < n_prompt
    seq_len = jnp.where(in_prompt, seq, dec_seq)
    pos = lax.rem(jnp.where(in_prompt, start, start - n_prompt), seq_len) + row
    prev = jnp.where(row == 0, prev_ref[SUBLANES - 1:SUBLANES, :], pltpu.roll(cu, 1, 0))
    prev = jnp.where(pos == 0, 0.0, prev)
    nxt = jnp.where(row == tm - 1, next_ref[0:1, :], pltpu.roll(cu, tm - 1, 0))
    nxt = jnp.where(pos == seq_len - 1, 0.0, nxt)
    conv = prev * ck_ref[0:1, :] + cu * ck_ref[1:2, :] + nxt * ck_ref[2:3, :]
    m = jnp.dot((b_ref[...] * conv).astype(BF16), w_ref[...], preferred_element_type=F32)
    _tail(m, x_ref, mod_ref[...], g2_ref, wr_ref, br_ref, rest[:-1], rest[-1], d)


def _hgrn_out_kernel(o_ref, gout_ref, ng_ref, w_ref, x_ref, mod_ref, g2_ref, wr_ref, br_ref, *rest, d):
    parts = []
    for h in range(HG_HEADS):
        oh = o_ref[:, h * LANES:(h + 1) * LANES]
        parts.append(oh * lax.rsqrt(jnp.mean(oh * oh, axis=-1, keepdims=True) + EPS))
    gout = gout_ref[...]
    y = (jnp.concatenate(parts, axis=1) * ng_ref[...]) * (gout * jax.nn.sigmoid(gout))
    m = jnp.dot(y.astype(BF16), w_ref[...], preferred_element_type=F32)
    _tail(m, x_ref, mod_ref[...], g2_ref, wr_ref, br_ref, rest[:-1], rest[-1], d)


def _mixer_out(kind, acts, w_out, x, mods_l, g2, w_router, b_router, geom, extra=()):
    t, d = x.shape
    tm = TOKEN_TILE
    n_prompt, dec_seq, seq = geom
    grp = functools.partial(_group_of_tile, tile=tm, n_prompt=n_prompt, dec_seq=dec_seq)
    row_spec = pl.BlockSpec((tm, d), lambda i: (i, 0))
    wr = jnp.zeros((d, LANES), F32).at[:, :N_EXPERTS].set(w_router)
    wr_top = _bf16_part(wr)
    wr = jnp.stack([wr_top.astype(BF16), (wr - wr_top).astype(BF16)])
    br = jnp.full((1, LANES), NEG_BIG, F32).at[0, :N_EXPERTS].set(b_router)
    common_specs = [pl.BlockSpec((d, d), lambda i: (0, 0)),
                    row_spec,
                    pl.BlockSpec((None, 1, 6 * d), lambda i: (grp(i), 0, 0)),
                    pl.BlockSpec((1, d), lambda i: (0, 0)),
                    pl.BlockSpec((2, d, LANES), lambda i: (0, 0, 0)),
                    pl.BlockSpec((1, LANES), lambda i: (0, 0))]
    common_args = [w_out, x, mods_l, g2, wr, br]
    if kind == "plain":
        body = functools.partial(_plain_out_kernel, d=d)
        specs = [row_spec] + common_specs
        args = list(acts) + common_args
    elif kind == "conv":
        body = functools.partial(_conv_out_kernel, d=d, tm=tm, n_prompt=n_prompt, seq=seq, dec_seq=dec_seq)
        per = tm // SUBLANES
        last = t // SUBLANES - 1
        specs = [row_spec, row_spec,
                 pl.BlockSpec((SUBLANES, d), lambda i: (jnp.maximum(i * per - 1, 0), 0)),
                 pl.BlockSpec((SUBLANES, d), lambda i: (jnp.minimum((i + 1) * per, last), 0)),
                 pl.BlockSpec((CONV_WIDTH, d), lambda i: (0, 0))] + common_specs
        b, cu = acts
        args = [b, cu, cu, cu, extra[0]] + common_args
    else:
        body = functools.partial(_hgrn_out_kernel, d=d)
        specs = [row_spec, row_spec, pl.BlockSpec((1, d), lambda i: (0, 0))] + common_specs
        args = list(acts) + [extra[0]] + common_args
    lane_spec = pl.BlockSpec((tm, LANES), lambda i: (i, 0))
    return pl.pallas_call(
        body,
        grid=(t // tm,),
        in_specs=specs,
        out_specs=[row_spec, pl.BlockSpec((PACKED_CHUNKS, tm, CHUNK_W), lambda i: (0, i, 0)),
                   lane_spec, lane_spec, lane_spec, pl.BlockSpec((1, LANES), lambda i: (0, 0))],
        out_shape=[jax.ShapeDtypeStruct((t, d), F32), jax.ShapeDtypeStruct((PACKED_CHUNKS, t, CHUNK_W), jnp.int32),
                   jax.ShapeDtypeStruct((t, LANES), jnp.int32), jax.ShapeDtypeStruct((t, LANES), F32),
                   jax.ShapeDtypeStruct((t, LANES), jnp.int32), jax.ShapeDtypeStruct((1, LANES), F32)],
        scratch_shapes=[pltpu.VMEM((1, LANES), F32)],
        compiler_params=_params(("arbitrary",), 40),
        name="mixer_out_" + kind,
    )(*args)


def _softmax_rows(parts):
    mx = functools.reduce(jnp.maximum, [jnp.max(s, axis=-1, keepdims=True) for s in parts])
    es = [jnp.exp(s - mx) for s in parts]
    den = functools.reduce(lambda a, b: a + b, [jnp.sum(e, axis=-1, keepdims=True) for e in es])
    return [e / den for e in es]


def _head_pair_queries(q, dh):
    first = lax.broadcasted_iota(jnp.int32, (1, q.shape[1]), 1) < dh
    return first, (jnp.where(first, q, 0.0).astype(BF16), jnp.where(first, 0.0, q).astype(BF16))


def _na_ctx_kernel(q_ref, k_ref, v_ref, o_ref, *, dh, scale):
    nt = (((1,), (1,)), ((), ()))
    for pair in range(NA_HEADS * dh // LANES):
        sl = slice(pair * LANES, (pair + 1) * LANES)
        first, queries = _head_pair_queries(q_ref[:, sl], dh)
        n_q = q_ref.shape[0]
        q2 = jnp.concatenate(queries, axis=0)
        (p,) = _softmax_rows([lax.dot_general(q2, k_ref[:, sl].astype(BF16), nt,
                                              preferred_element_type=F32) * scale])
        o2 = jnp.dot(p.astype(BF16), v_ref[:, sl].astype(BF16), preferred_element_type=F32)
        o_ref[:, sl] = jnp.where(first, o2[:n_q], o2[n_q:])


def _na_context(q, k, v, batch, seq, t_total, dec_seq):
    d = q.shape[1]
    dh = d // NA_HEADS
    per = dec_seq // seq
    spec = pl.BlockSpec((seq, d), lambda b: (b, 0))
    return pl.pallas_call(
        functools.partial(_na_ctx_kernel, dh=dh, scale=dh ** -0.5),
        grid=(batch,),
        in_specs=[spec, spec, spec],
        out_specs=pl.BlockSpec((None, seq, d), lambda b: (b // per, b % per, 0)),
        out_shape=jax.ShapeDtypeStruct((t_total // dec_seq, dec_seq, d), F32),
        compiler_params=_params(("arbitrary",), 32),
        name="na_context",
    )(q, k, v)


def _na_lat_kernel(*refs, dh, scale, kr, n_batch):
    per = 1 + 2 * kr
    ck_ref, cv_ref, bias_ref, _, o_ref = refs[n_batch * per:]
    nt = (((1,), (1,)), ((), ()))
    heads_per_group = LANES // dh
    for pair in range(NA_HEADS // heads_per_group):
        sl = slice(pair * LANES, (pair + 1) * LANES)
        for b in range(n_batch):
            q_ref = refs[b * per]
            k_refs = refs[b * per + 1:b * per + 1 + kr]
            v_refs = refs[b * per + 1 + kr:(b + 1) * per]
            first, queries = _head_pair_queries(q_ref[:, sl], dh)
            n_q = q_ref.shape[0]
            q2 = jnp.concatenate(queries, axis=0)
            bias2 = jnp.concatenate([bias_ref[pair * heads_per_group + j] for j in range(heads_per_group)], axis=0)
            kw = jnp.concatenate([r[:, sl] for r in k_refs], axis=0).astype(BF16)
            vw = jnp.concatenate([r[:, sl] for r in v_refs], axis=0).astype(BF16)
            s_nb = lax.dot_general(q2, kw, nt, preferred_element_type=F32) * scale + bias2
            s_cx = lax.dot_general(q2, ck_ref[b, :, sl].astype(BF16), nt, preferred_element_type=F32) * scale
            p_nb, p_cx = _softmax_rows([s_nb, s_cx])
            o2 = (jnp.dot(p_nb.astype(BF16), vw, preferred_element_type=F32)
                  + jnp.dot(p_cx.astype(BF16), cv_ref[b, :, sl].astype(BF16), preferred_element_type=F32))
            o_ref[b, :, sl] = jnp.where(first, o2[:n_q], o2[n_q:])


def _na_bias_table(rpb, kr):
    col = np.arange(GRID_W)
    col_start = np.clip(col - NA_COLS // 2, 0, GRID_W - NA_COLS)
    delta = col[None, :] - col[:, None] + (NA_COLS - 1)
    inside = (col[None, :] >= col_start[:, None]) & (col[None, :] < col_start[:, None] + NA_COLS)
    pick = (np.arange(2 * NA_COLS - 1)[:, None, None] == delta[None]).astype(np.float32)
    tab = jnp.einsum("hrk,kwc->hrwc", rpb, jnp.asarray(pick), precision=HIGHEST)
    tab = jnp.where(jnp.asarray(inside)[None, None], tab, NEG_BIG)
    out = []
    for d0 in range(NA_ROWS_MAX):
        rows = tab[:, d0:d0 + kr]
        out.append(rows.transpose(0, 2, 1, 3).reshape(NA_HEADS, GRID_W, kr * GRID_W))
    return jnp.stack(out, axis=0)


def _na_latent(q, k, v, ctx_k, ctx_v, rpb, out_buf, dec_batch, dec_seq):
    d = q.shape[1]
    dh = d // NA_HEADS
    rows = dec_seq // GRID_W
    kr = min(NA_ROWS_MAX, rows)
    base = 0
    past = ctx_k.shape[1]
    out_slab = (out_buf.shape[0] - dec_batch) // dec_batch
    bias = _na_bias_table(rpb, kr)

    def row_start(r):
        return jnp.clip(r - kr // 2, 0, rows - kr)

    def win_spec(b, j):
        return pl.BlockSpec((GRID_W, d), lambda r: (base + b * rows + row_start(r) + j, 0))

    in_specs, args = [], []
    for b in range(dec_batch):
        in_specs += ([pl.BlockSpec((GRID_W, d), lambda r, b=b: (base + b * rows + r, 0))]
                     + [win_spec(b, j) for j in range(kr)] * 2)
        args += [q] + [k] * kr + [v] * kr
    in_specs += ([pl.BlockSpec((dec_batch, past, d), lambda r: (0, 0, 0))] * 2
                 + [pl.BlockSpec((None, NA_HEADS, GRID_W, kr * GRID_W),
                                 lambda r: (row_start(r) - r + NA_ROWS_MAX - 1, 0, 0, 0))])
    in_specs.append(pl.BlockSpec(memory_space=pl.ANY))
    args += [ctx_k.reshape(dec_batch, past, d), ctx_v.reshape(dec_batch, past, d), bias, out_buf]
    return pl.pallas_call(
        functools.partial(_na_lat_kernel, dh=dh, scale=dh ** -0.5, kr=kr, n_batch=dec_batch),
        grid=(rows,),
        in_specs=in_specs,
        out_specs=pl.BlockSpec((dec_batch, GRID_W, d), lambda r: (out_slab, r, 0)),
        out_shape=jax.ShapeDtypeStruct(out_buf.shape, F32),
        input_output_aliases={len(args) - 1: 0},
        compiler_params=_params(("arbitrary",), 40),
        name="na_latent",
    )(*args)


def _hgrn_scan_kernel(*refs, seq_len, has_s0, has_buf, want_state):
    q_ref, v_ref, z0_ref, z1_ref, lb_ref, tri_ref = refs[:6]
    rest = list(refs[6:])
    s0_ref = rest.pop(0) if has_s0 else None
    if has_buf:
        rest.pop(0)
    o_ref = rest.pop(0)
    sf_ref = rest.pop(0) if want_state else None
    ob_ref = rest.pop(0)
    c, r = HG_CHUNK, HG_TILE
    nc, nt = r // c, seq_len // r
    dk = HG_KEY_DIM
    tpos8 = lax.broadcasted_iota(jnp.int32, (1, SUBLANES, 1), 1)
    row = lax.broadcasted_iota(jnp.int32, (r, r), 0)
    col = lax.broadcasted_iota(jnp.int32, (r, r), 1)
    same16 = (row // (2 * SUBLANES)) == (col // (2 * SUBLANES))
    same32 = (row // c) == (col // c)
    nt_dims = (((1,), (1,)), ((), ()))
    tn_dims = (((0,), (0,)), ((), ()))

    def tile(dr, hh, i, st):
        z_ref = (z0_ref, z1_ref)[dr]
        cols = slice(hh * dk, (hh + 1) * dk)
        lb = lb_ref[dr:dr + 1, cols]
        tri = tri_ref[dr]
        ti = i if dr == 0 else nt - 1 - i
        r0 = pl.multiple_of(ti * r, r)
        q = q_ref[pl.ds(r0, r), cols]
        v = v_ref[pl.ds(r0, r), cols]
        z = z_ref[pl.ds(r0, r), cols]
        g = jnp.log(lb + (1.0 - lb) * jax.nn.sigmoid(z))
        k = (1.0 - lb) * jax.nn.sigmoid(-z)
        b = jnp.dot(tri, g, precision=HIGHEST, preferred_element_type=F32)
        b3, q3, k3, v3 = (a.reshape(nc, c, dk) for a in (b, q, k, v))
        nb = r // SUBLANES
        b8, q8, k8, v8 = (a.reshape(nb, SUBLANES, dk) for a in (b, q, k, v))
        oi = jnp.zeros((nb, SUBLANES, dk), F32)
        for s in range(SUBLANES):
            keep = (tpos8 >= s) if dr == 0 else (tpos8 <= s)
            e = jnp.exp(jnp.where(keep, b8 - b8[:, s:s + 1, :], NEG_BIG))
            a = jnp.sum(q8 * e * k8[:, s:s + 1, :], axis=-1, keepdims=True)
            oi = oi + a * v8[:, s:s + 1, :]
        att = None
        for span, same_group in ((2 * SUBLANES, same16), (c, same32)):
            half = span // 2
            bg = b.reshape(r // span, span, dk)
            pos = lax.broadcasted_iota(jnp.int32, (1, span, 1), 1)
            edge = bg[:, half - 1:half, :] if dr == 0 else bg[:, half:half + 1, :]
            is_query = (pos >= half) if dr == 0 else (pos < half)
            qf = q.reshape(bg.shape) * jnp.exp(jnp.where(is_query, bg - edge, NEG_BIG))
            kf = k.reshape(bg.shape) * jnp.exp(jnp.where(is_query, NEG_BIG, edge - bg))
            part = lax.dot_general(qf.reshape(r, dk).astype(BF16), kf.reshape(r, dk).astype(BF16), nt_dims,
                                   preferred_element_type=F32)
            part = jnp.where(same_group, part, 0.0)
            att = part if att is None else att + part
        oi = oi.reshape(r, dk) + jnp.dot(att.astype(BF16), v.astype(BF16), preferred_element_type=F32)
        oi = oi.reshape(nc, c, dk)
        bl = b3[:, c - 1:c, :] if dr == 0 else b3[:, 0:1, :]
        qt = (q3 * jnp.exp(b3)).astype(BF16)
        kt = (k3 * jnp.exp(bl - b3)).astype(BF16)
        vb = v3.astype(BF16)
        outs = [None] * nc
        for ci in (range(nc) if dr == 0 else range(nc - 1, -1, -1)):
            oc = lax.dot_general(qt[ci], st.astype(BF16), nt_dims, preferred_element_type=F32)
            outs[ci] = oi[ci] + oc
            kv = lax.dot_general(vb[ci], kt[ci], tn_dims, preferred_element_type=F32)
            st = st * jnp.exp(bl[ci]) + kv
        return r0, jnp.concatenate(outs, axis=0), st

    chains = [(dr, hh) for hh in range(HG_HEADS_PER_STEP) for dr in range(2)]

    def step(i, carry):
        new = []
        for (dr, hh), st in zip(chains, carry):
            r0, o, st = tile(dr, hh, i, st)
            (o_ref if dr == 0 else ob_ref)[pl.ds(r0, r), hh * dk:(hh + 1) * dk] = o
            new.append(st)
        return tuple(new)

    init = tuple(s0_ref[dr, hh] if has_s0 else jnp.zeros((dk, dk), F32) for dr, hh in chains)
    final = lax.fori_loop(0, nt, step, init)
    o_ref[...] += ob_ref[...]
    if want_state:
        for (dr, hh), st in zip(chains, final):
            sf_ref[dr, hh] = st.T


def _hgrn_tri():
    t = np.arange(HG_TILE)
    same = (t[:, None] // HG_CHUNK) == (t[None, :] // HG_CHUNK)
    fwd = same & (t[None, :] <= t[:, None])
    bwd = same & (t[None, :] >= t[:, None])
    return jnp.asarray(np.stack([fwd, bwd]).astype(np.float32))


def _hgrn_scan(q, v, z0, z1, lb, row0, n_seq, seq_len, s0_t=None, want_state=False, out_buf=None):
    t_total, d = q.shape
    dk = HG_KEY_DIM
    blk0 = row0 // seq_len
    hp = HG_HEADS_PER_STEP
    seq_spec = pl.BlockSpec((seq_len, hp * dk), lambda s, h: (blk0 + s, h))
    st_spec = pl.BlockSpec((None, 2, hp, dk, dk), lambda s, h: (s, 0, h, 0, 0))
    in_specs = [seq_spec] * 4 + [pl.BlockSpec((2, hp * dk), lambda s, h: (0, h)),
                                 pl.BlockSpec((2, HG_TILE, HG_TILE), lambda s, h: (0, 0, 0))]
    args = [q, v, z0, z1, lb, _hgrn_tri()]
    if s0_t is not None:
        in_specs.append(st_spec)
        args.append(s0_t)
    aliases = {}
    if out_buf is not None:
        in_specs.append(pl.BlockSpec(memory_space=pl.ANY))
        args.append(out_buf)
        aliases = {len(args) - 1: 0}
    out_specs = [seq_spec]
    out_shape = [jax.ShapeDtypeStruct((t_total, d), F32)]
    if want_state:
        out_specs.append(st_spec)
        out_shape.append(jax.ShapeDtypeStruct((n_seq, 2, HG_HEADS, dk, dk), F32))
    res = pl.pallas_call(
        functools.partial(_hgrn_scan_kernel, seq_len=seq_len, has_s0=s0_t is not None,
                          has_buf=out_buf is not None, want_state=want_state),
        grid=(n_seq, HG_HEADS // hp),
        in_specs=in_specs,
        out_specs=out_specs,
        out_shape=out_shape,
        input_output_aliases=aliases,
        scratch_shapes=[pltpu.VMEM((seq_len, hp * dk), F32)],
        compiler_params=_params(("arbitrary", "arbitrary"), 32),
        name="hgrn_scan",
    )(*args)
    return res if want_state else (res[0], None)


def _route(idx, rank, counts, t):
    n = t * TOP_K
    rb = MOE_ROWS
    n_blocks = -(-n // rb) + N_EXPERTS
    n_rows = (n_blocks + N_EXPERTS) * rb
    experts = jnp.arange(N_EXPERTS, dtype=jnp.int32)
    counts = counts[0, :N_EXPERTS].astype(jnp.int32)
    padded = (counts + rb - 1) // rb * rb
    padded_end = jnp.cumsum(padded)
    start = padded_end - padded
    e = idx[:, :TOP_K]
    dest = jnp.sum(jnp.where(e[:, :, None] == experts, start, 0), axis=-1) + rank[:, :TOP_K]
    dest_km = dest.T.astype(jnp.int32)
    q = jnp.arange(rb, dtype=jnp.int32)
    spare = n_blocks * rb + experts[:, None] * rb + q[None, :]
    pad_rows = jnp.where(q[None, :] < (padded - counts)[:, None], (start + counts)[:, None] + q[None, :], spare)
    per_chunk = jnp.concatenate([dest_km.reshape(-1), pad_rows.reshape(-1)])
    sc_index = (jnp.arange(PACKED_CHUNKS, dtype=jnp.int32)[:, None] * n_rows + per_chunk[None, :]).reshape(1, -1)
    block_start = jnp.arange(n_blocks, dtype=jnp.int32) * rb
    block_expert = jnp.minimum(jnp.sum((padded_end[None, :] <= block_start[:, None]).astype(jnp.int32), axis=1),
                               N_EXPERTS - 1)
    n_used = (padded_end[-1] // rb).reshape(1)
    return sc_index, block_expert, n_used, n_rows


def _sc_mesh():
    return plsc.VectorSubcoreMesh(core_axis_name="c", subcore_axis_name="s")


def _sc_dispatch(h2c, sc_index, t, n_rows):
    win = SC_WINDOW
    tw = t // win
    nw = (t * TOP_K + N_EXPERTS * MOE_ROWS) // win
    n_chunks = h2c.shape[0]

    def scatter(x_hbm, i_hbm, o_hbm):
        def body(x_vmem, i_vmem):
            pltpu.sync_copy(x_vmem, o_hbm.at[i_vmem.at[0]])

        pltpu.emit_pipeline(
            body,
            grid=(n_chunks * nw,),
            in_specs=[pl.BlockSpec((win, CHUNK_W), index_map=lambda i: ((i // nw) * tw + (i % nw) % tw, 0)),
                      pl.BlockSpec((1, win), index_map=lambda i: (0, i))],
            out_specs=[],
            core_axis_name=("c", "s"),
            dimension_semantics=(pltpu.PARALLEL,),
        )(x_hbm, i_hbm)

    out = pl.kernel(scatter, out_type=jax.ShapeDtypeStruct((n_chunks * n_rows, CHUNK_W), h2c.dtype),
                    mesh=_sc_mesh(), name="moe_dispatch")(h2c.reshape(n_chunks * t, CHUNK_W), sc_index)
    return out.reshape(n_chunks, n_rows, CHUNK_W)


def _sc_collect(y_rows, sc_index, t, n_rows):
    win = SC_WINDOW
    nw = (t * TOP_K + N_EXPERTS * MOE_ROWS) // win
    aw = t * TOP_K // win
    n_chunks = y_rows.shape[0]

    def gather(y_hbm, i_hbm, o_hbm):
        def body(i_vmem, o_vmem):
            pltpu.sync_copy(y_hbm.at[i_vmem.at[0]], o_vmem)

        pltpu.emit_pipeline(
            body,
            grid=(n_chunks * aw,),
            in_specs=[pl.BlockSpec((1, win), index_map=lambda i: (0, (i // aw) * nw + i % aw))],
            out_specs=[pl.BlockSpec((win, CHUNK_W), index_map=lambda i: (i, 0))],
            core_axis_name=("c", "s"),
            dimension_semantics=(pltpu.PARALLEL,),
        )(i_hbm, o_hbm)

    out = pl.kernel(gather, out_type=jax.ShapeDtypeStruct((n_chunks * TOP_K * t, CHUNK_W), y_rows.dtype),
                    mesh=_sc_mesh(), name="moe_collect")(y_rows.reshape(n_chunks * n_rows, CHUNK_W), sc_index)
    return out.reshape(n_chunks, TOP_K, t, CHUNK_W)


def _moe_kernel(be_ref, nu_ref, x_ref, w1_ref, b1_ref, w2_ref, b2_ref, o_ref, w1b, w2b, *, dff):
    i = pl.program_id(0)
    e = be_ref[i]
    prev = be_ref[jnp.maximum(i - 1, 0)]

    @pl.when(jnp.logical_or(i == 0, e != prev))
    def _():
        w1b[...] = w1_ref[...].astype(BF16)
        w2b[...] = w2_ref[...].astype(BF16)

    @pl.when(i < nu_ref[0])
    def _():
        halves = [_unpack_bf16_pairs(x_ref[j]) for j in range(PACKED_CHUNKS)]
        xb = jnp.concatenate([h[0] for h in halves] + [h[1] for h in halves], axis=1).astype(BF16)
        gu = jnp.dot(xb, w1b[...], preferred_element_type=F32) + b1_ref[...]
        g = jnp.minimum(gu[:, :dff], SWIGLU_LIMIT)
        u = jnp.clip(gu[:, dff:], -SWIGLU_LIMIT, SWIGLU_LIMIT)
        a = (u + 1.0) * (g * jax.nn.sigmoid(SWIGLU_ALPHA * g))
        y = jnp.dot(a.astype(BF16), w2b[...], preferred_element_type=F32) + b2_ref[...]
        words = _pack_bf16_pairs(y)
        for j in range(PACKED_CHUNKS):
            o_ref[j] = words[:, j * CHUNK_W:(j + 1) * CHUNK_W]

    @pl.when(i >= nu_ref[0])
    def _():
        o_ref[...] = jnp.zeros(o_ref.shape, o_ref.dtype)


def _moe_experts(x_rows, block_expert, n_used, layer, w1, b1, w2, b2):
    _, n_rows, _ = x_rows.shape
    rb = MOE_ROWS
    n_blocks = n_rows // rb - N_EXPERTS
    depth, n_e, d, dff2 = w1.shape
    dff = dff2 // 2
    row_spec = pl.BlockSpec((PACKED_CHUNKS, rb, CHUNK_W), lambda i, be, nu: (0, jnp.minimum(i, nu[0] - 1), 0))
    grid_spec = pltpu.PrefetchScalarGridSpec(
        num_scalar_prefetch=2,
        grid=(n_blocks,),
        in_specs=[row_spec,
                  pl.BlockSpec((None, None, d, dff2), lambda i, be, nu: (layer, be[i], 0, 0)),
                  pl.BlockSpec((None, None, 1, dff2), lambda i, be, nu: (layer, be[i], 0, 0)),
                  pl.BlockSpec((None, None, dff, d), lambda i, be, nu: (layer, be[i], 0, 0)),
                  pl.BlockSpec((None, None, 1, d), lambda i, be, nu: (layer, be[i], 0, 0))],
        out_specs=pl.BlockSpec((PACKED_CHUNKS, rb, CHUNK_W), lambda i, be, nu: (0, i, 0)),
        scratch_shapes=[pltpu.VMEM((d, dff2), BF16), pltpu.VMEM((dff, d), BF16)],
    )
    return pl.pallas_call(
        functools.partial(_moe_kernel, dff=dff),
        grid_spec=grid_spec,
        out_shape=jax.ShapeDtypeStruct((PACKED_CHUNKS, n_rows, CHUNK_W), jnp.int32),
        compiler_params=_params(("arbitrary",), 56),
        name="moe_experts",
    )(block_expert, n_used, x_rows, w1, b1.reshape(depth, n_e, 1, dff2), w2, b2.reshape(depth, n_e, 1, d))


def _combined_rows(y_ref, gw_ref, x_ref, mod_ref, d):
    gw = gw_ref[...]
    pieces = {}
    for j in range(PACKED_CHUNKS):
        accs = [None, None]
        for k in range(TOP_K):
            for side, part in enumerate(_unpack_bf16_pairs(y_ref[j, k])):
                term = gw[:, k:k + 1] * part
                accs[side] = term if accs[side] is None else accs[side] + term
        for side, acc in enumerate(accs):
            c0 = side * (d // 2) + j * CHUNK_W
            pieces[c0] = x_ref[:, c0:c0 + CHUNK_W] + mod_ref[:, 5 * d + c0:5 * d + c0 + CHUNK_W] * acc
    return jnp.concatenate([pieces[c0] for c0 in sorted(pieces)], axis=1)


def _combine_final_kernel(y_ref, gw_ref, x_ref, mod_ref, g_ref, o_ref, *, d):
    o_ref[...] = _rms(_combined_rows(y_ref, gw_ref, x_ref, mod_ref, d)) * g_ref[...]


def _combine_in_kernel(y_ref, gw_ref, x_ref, mod_ref, modn_ref, g_ref, w_ref, *rest, d, conv, has_buf):
    x2_ref, *out_refs = rest[1:] if has_buf else rest
    x2 = _combined_rows(y_ref, gw_ref, x_ref, mod_ref, d)
    x2_ref[...] = x2
    _project(x2, modn_ref[...], g_ref, w_ref, out_refs, d, conv)


def _combine_specs(y_slots, gw, x1, mods_l, geom, rows):
    d = x1.shape[1]
    row0, t = rows if rows is not None else (0, x1.shape[0])
    tm = TOKEN_TILE
    t0 = row0 // tm
    grp = functools.partial(_group_of_tile, tile=tm, n_prompt=geom[0], dec_seq=geom[1])
    in_specs = [pl.BlockSpec((PACKED_CHUNKS, TOP_K, tm, CHUNK_W), lambda i: (0, 0, t0 + i, 0)),
                pl.BlockSpec((tm, LANES), lambda i: (t0 + i, 0)),
                pl.BlockSpec((tm, d), lambda i: (t0 + i, 0)),
                pl.BlockSpec((None, 1, 6 * d), lambda i: (grp(t0 + i), 0, 0))]
    return in_specs, [y_slots, gw, x1, mods_l], t, t0, grp


def _combine_final(y_slots, gw, x1, mods_l, geom, final_g, rows):
    d = x1.shape[1]
    tm = TOKEN_TILE
    in_specs, args, t, _, _ = _combine_specs(y_slots, gw, x1, mods_l, geom, rows)
    return pl.pallas_call(
        functools.partial(_combine_final_kernel, d=d),
        grid=(t // tm,),
        in_specs=in_specs + [pl.BlockSpec((1, d), lambda i: (0, 0))],
        out_specs=pl.BlockSpec((tm, d), lambda i: (i, 0)),
        out_shape=jax.ShapeDtypeStruct((t, d), F32),
        compiler_params=_params(("arbitrary",), 24),
        name="moe_combine_final",
    )(*args, final_g)


def _combine_mixer_in(y_slots, gw, x1, mods_l, mods_next, g, w, geom, conv=False, rows=None, x2_buf=None):
    t_total, d = x1.shape
    tm = TOKEN_TILE
    n = w.shape[1]
    n_out = 2 if conv else n // d
    in_specs, args, t, t0, grp = _combine_specs(y_slots, gw, x1, mods_l, geom, rows)
    in_specs += [pl.BlockSpec((None, 1, 6 * d), lambda i: (grp(t0 + i), 0, 0)),
                 pl.BlockSpec((1, d), lambda i: (0, 0)),
                 pl.BlockSpec((d, n), lambda i: (0, 0))]
    args += [mods_next, g, w]
    aliases = {}
    if x2_buf is not None:
        in_specs.append(pl.BlockSpec(memory_space=pl.ANY))
        args.append(x2_buf)
        aliases = {len(args) - 1: 0}
    res = pl.pallas_call(
        functools.partial(_combine_in_kernel, d=d, conv=conv, has_buf=x2_buf is not None),
        grid=(t // tm,),
        in_specs=in_specs,
        out_specs=[pl.BlockSpec((tm, d), lambda i: (t0 + i, 0))] + [pl.BlockSpec((tm, d), lambda i: (i, 0))] * n_out,
        out_shape=[jax.ShapeDtypeStruct((t_total, d), F32)] + [jax.ShapeDtypeStruct((t, d), F32)] * n_out,
        input_output_aliases=aliases,
        compiler_params=_params(("arbitrary",), 56),
        name="combine_mixer_in",
    )(*args)
    return res[0], res[1:]


def kernel(x_prompt, x_sample, c, c_ctx, cache_na_k, cache_na_v, state_hgrn, norm_mix, norm_ffn, w_mod, b_mod,
           conv_w_in, conv_k, conv_w_out, na_w_qkv, na_rpb, na_w_o, hg_w_qig, hg_w_f, hg_lb, hg_norm, hg_w_o,
           moe_w_router, moe_b_router, moe_w1, moe_b1, moe_w2, moe_b2, final_norm):
    batch, seq, d = x_prompt.shape
    dec_batch, dec_seq, _ = x_sample.shape
    depth = w_mod.shape[0]
    n_prompt = batch * seq
    t = n_prompt + dec_batch * dec_seq
    geom = (n_prompt, dec_seq, seq)
    assert seq % TOKEN_TILE == 0 and dec_seq % TOKEN_TILE == 0 and seq % HG_TILE == 0
    assert n_prompt % (dec_seq * dec_batch) == 0 and dec_seq % GRID_W == 0 and 1 + dec_batch <= SUBLANES
    assert d == 2 * PACKED_CHUNKS * CHUNK_W and t % SC_WINDOW == 0 and MOE_ROWS % SC_WINDOW == 0

    x = jnp.concatenate([x_prompt.reshape(n_prompt, d), x_sample.reshape(dec_batch * dec_seq, d)], axis=0)
    cvecs = jnp.zeros((SUBLANES, d), F32).at[0].set(c_ctx).at[1:1 + dec_batch].set(c)
    mods = _adaln_all(cvecs, w_mod, b_mod)[:, :1 + dec_batch].reshape(depth, 1 + dec_batch, 1, 6 * d)

    lb_soft = jax.nn.softmax(hg_lb.astype(F32), axis=0)
    lower_bounds = jnp.cumsum(lb_soft, axis=0) - lb_soft[0]

    new_k, new_v, new_s = [], [], []
    pending = None
    for l in range(depth):
        kind, j = l % N_MIXERS, l // N_MIXERS
        g1 = norm_mix[l].reshape(1, d)
        g2 = norm_ffn[l].reshape(1, d)

        def project(x, w, conv=False, rows=None, x2_buf=None):
            if pending is None:
                return x, _mixer_in(x, mods[l], g1, w, geom, conv=conv, rows=rows)
            return _combine_mixer_in(*pending, mods[l], g1, w, geom, conv=conv, rows=rows, x2_buf=x2_buf)

        if kind == 0:
            x, (b, cu) = project(x, conv_w_in[j].astype(BF16), conv=True)
        elif kind == 1:
            w_qkv = na_w_qkv[j].astype(BF16)
            x, (qp, kp, vp) = project(x, w_qkv, rows=(0, n_prompt))
            x, (qs, ks, vs) = project(x, w_qkv, rows=(n_prompt, t - n_prompt), x2_buf=x)
        else:
            w_in = jnp.concatenate([hg_w_qig[j], hg_w_f[j, 0], hg_w_f[j, 1]], axis=1).astype(BF16)
            x, (q, v, gout, z0, z1) = project(x, w_in)
        tail_args = (x, mods[l], g2, moe_w_router[l], moe_b_router[l], geom)
        if kind == 0:
            x1, h2c, idx, gw, rank, counts = _mixer_out("conv", (b, cu), conv_w_out[j].astype(BF16), *tail_args,
                                         extra=(conv_k[j],))
        elif kind == 1:
            o = _na_context(qp, kp, vp, batch, seq, t, dec_seq)
            o = _na_latent(qs, ks, vs, cache_na_k[:, j], cache_na_v[:, j], na_rpb[j], o, dec_batch, dec_seq)
            new_k.append(kp.reshape(batch, seq, NA_HEADS, d // NA_HEADS))
            new_v.append(vp.reshape(batch, seq, NA_HEADS, d // NA_HEADS))
            x1, h2c, idx, gw, rank, counts = _mixer_out("plain", (o.reshape(t, d),),
                                         na_w_o[j].astype(BF16), *tail_args)
        else:
            lb = lower_bounds[l]
            o, s_p = _hgrn_scan(q, v, z0, z1, lb, 0, batch, seq, want_state=True)
            s0_t = jnp.swapaxes(state_hgrn[:, j].astype(F32), -1, -2)
            o, _ = _hgrn_scan(q, v, z0, z1, lb, n_prompt, dec_batch, dec_seq, s0_t=s0_t, out_buf=o)
            new_s.append(s_p)
            x1, h2c, idx, gw, rank, counts = _mixer_out("hgrn", (o, gout),
                                         hg_w_o[j].astype(BF16), *tail_args, extra=(hg_norm[j].reshape(1, d),))
        sc_index, block_expert, n_used, n_rows = _route(idx, rank, counts, t)
        x_rows = _sc_dispatch(h2c, sc_index, t, n_rows)
        y_rows = _moe_experts(x_rows, block_expert, n_used, l, moe_w1, moe_b1, moe_w2, moe_b2)
        y_slots = _sc_collect(y_rows, sc_index, t, n_rows)
        pending = (y_slots, gw, x1, mods[l])

    final_g = final_norm.reshape(1, d)
    y_prompt = _combine_final(*pending, geom, final_g, (0, n_prompt)).reshape(batch, seq, d)
    y_sample = _combine_final(*pending, geom, final_g, (n_prompt, t - n_prompt)).reshape(dec_batch, dec_seq, d)
    new_na_k = jnp.stack(new_k, axis=1).astype(x_prompt.dtype)
    new_na_v = jnp.stack(new_v, axis=1).astype(x_prompt.dtype)
    new_hgrn_state = jnp.stack(new_s, axis=1).astype(x_prompt.dtype)
    return (y_prompt, y_sample, new_na_k, new_na_v, new_hgrn_state)
```

```python
import functools

import numpy as np
import jax
import jax.numpy as jnp
from jax import lax
from jax.experimental import pallas as pl
from jax.experimental.pallas import tpu as pltpu
from jax.experimental.pallas import tpu_sc as plsc

F32 = jnp.float32
BF16 = jnp.bfloat16
HIGHEST = lax.Precision.HIGHEST

N_MIXERS = 3
GRID_W = 64
CONV_WIDTH = 3
NA_HEADS = 16
NA_ROWS_MAX = 8
NA_COLS = 16
HG_HEADS = 8
HG_KEY_DIM = 128
HG_CHUNK = 32
N_EXPERTS = 32
TOP_K = 4
SWIGLU_LIMIT = 7.0
SWIGLU_ALPHA = 1.702
EPS = 1e-6

LANES = 128
SUBLANES = 8
VMEM_LIMIT_CAP = 60 * 1024 * 1024

NEG_BIG = -1e30
TOKEN_TILE = 256
MOE_ROWS = 256
HG_TILE = 128
HG_HEADS_PER_STEP = 2
PACKED_CHUNKS = 2
CHUNK_W = 256
SC_WINDOW = 128


def _params(sem, vmem_mb, flags=None):
    return pltpu.CompilerParams(dimension_semantics=sem, flags=flags,
                                vmem_limit_bytes=min(vmem_mb * 1024 * 1024, VMEM_LIMIT_CAP))


def _rms(x):
    return x * lax.rsqrt(jnp.mean(x * x, axis=-1, keepdims=True) + EPS)


def _modulate(x, g, shift, scale):
    return (_rms(x) * g) * (1.0 + scale) + shift


def _bf16_part(x):
    bits = lax.bitcast_convert_type(x, jnp.uint32) & jnp.uint32(0xFFFF0000)
    return lax.bitcast_convert_type(bits, F32)


def _pack_bf16_pairs(x):
    n = x.shape[1]
    bits = lax.bitcast_convert_type(x, jnp.uint32)
    top = (bits + (jnp.uint32(0x7FFF) + ((bits >> 16) & jnp.uint32(1)))) & jnp.uint32(0xFFFF0000)
    return lax.bitcast_convert_type(top[:, :n // 2] | (top[:, n // 2:] >> 16), jnp.int32)


def _unpack_bf16_pairs(words):
    bits = lax.bitcast_convert_type(words, jnp.uint32)
    return (lax.bitcast_convert_type(bits & jnp.uint32(0xFFFF0000), F32),
            lax.bitcast_convert_type(bits << 16, F32))


def _group_of_tile(i, tile, n_prompt, dec_seq):
    start = i * tile
    return jnp.where(start < n_prompt, 0, 1 + (start - n_prompt) // dec_seq)


def _adaln_kernel(c_ref, w_ref, b_ref, o_ref):
    cv = c_ref[...]
    s = cv * jax.nn.sigmoid(cv)
    o_ref[...] = jnp.dot(s, w_ref[...], precision=HIGHEST, preferred_element_type=F32) + b_ref[...]


def _adaln_all(cvecs, w_mod, b_mod):
    depth, d, n = w_mod.shape
    tn = 2048
    return pl.pallas_call(
        _adaln_kernel,
        grid=(depth, n // tn),
        in_specs=[pl.BlockSpec((SUBLANES, d), lambda l, j: (0, 0)),
                  pl.BlockSpec((None, d, tn), lambda l, j: (l, 0, j)),
                  pl.BlockSpec((None, 1, tn), lambda l, j: (l, 0, j))],
        out_specs=pl.BlockSpec((None, SUBLANES, tn), lambda l, j: (l, 0, j)),
        out_shape=jax.ShapeDtypeStruct((depth, SUBLANES, n), F32),
        compiler_params=_params(("arbitrary", "arbitrary"), 40),
        name="adaln",
    )(cvecs, w_mod, b_mod.reshape(depth, 1, n))


def _project(x, mod, g_ref, w_ref, out_refs, d, conv):
    h = _modulate(x, g_ref[...], mod[:, 0:d], mod[:, d:2 * d]).astype(BF16)
    if conv:
        b_ref, cu_ref = out_refs
        b_ref[...] = jnp.dot(h, w_ref[:, 0:d], preferred_element_type=F32)
        c = jnp.dot(h, w_ref[:, d:2 * d], preferred_element_type=F32)
        u = jnp.dot(h, w_ref[:, 2 * d:3 * d], preferred_element_type=F32)
        cu_ref[...] = c * u
    else:
        for j, o_ref in enumerate(out_refs):
            o_ref[...] = jnp.dot(h, w_ref[:, j * d:(j + 1) * d], preferred_element_type=F32)


def _in_kernel(x_ref, mod_ref, g_ref, w_ref, *out_refs, d, conv):
    _project(x_ref[...], mod_ref[...], g_ref, w_ref, out_refs, d, conv)


def _mixer_in(x, mods_l, g, w, geom, conv=False, rows=None):
    d = x.shape[1]
    row0, t = rows if rows is not None else (0, x.shape[0])
    n = w.shape[1]
    n_out = 2 if conv else n // d
    tm = TOKEN_TILE
    t0 = row0 // tm
    grp = functools.partial(_group_of_tile, tile=tm, n_prompt=geom[0], dec_seq=geom[1])
    return pl.pallas_call(
        functools.partial(_in_kernel, d=d, conv=conv),
        grid=(t // tm,),
        in_specs=[pl.BlockSpec((tm, d), lambda i: (t0 + i, 0)),
                  pl.BlockSpec((None, 1, 6 * d), lambda i: (grp(t0 + i), 0, 0)),
                  pl.BlockSpec((1, d), lambda i: (0, 0)),
                  pl.BlockSpec((d, n), lambda i: (0, 0))],
        out_specs=[pl.BlockSpec((tm, d), lambda i: (i, 0))] * n_out,
        out_shape=[jax.ShapeDtypeStruct((t, d), F32)] * n_out,
        compiler_params=_params(("arbitrary",), 48),
        name="mixer_in",
    )(x, mods_l, g, w)


def _tail(m, x_ref, mod, g2_ref, wr_ref, br_ref, outs, run_ref, d):
    x1_ref, h2_ref, idx_ref, gw_ref, rank_ref, cnt_ref = outs
    x1 = x_ref[...] + mod[:, 2 * d:3 * d] * m
    x1_ref[...] = x1
    h2 = _modulate(x1, g2_ref[...], mod[:, 3 * d:4 * d], mod[:, 4 * d:5 * d])
    words = _pack_bf16_pairs(h2)
    for j in range(PACKED_CHUNKS):
        h2_ref[j] = words[:, j * CHUNK_W:(j + 1) * CHUNK_W]
    h2_top = _bf16_part(h2)
    h2_hi = h2_top.astype(BF16)
    h2_lo = (h2 - h2_top).astype(BF16)
    logits = (jnp.dot(h2_hi, wr_ref[0], preferred_element_type=F32)
              + jnp.dot(h2_lo, wr_ref[0], preferred_element_type=F32)
              + jnp.dot(h2_hi, wr_ref[1], preferred_element_type=F32)) + br_ref[...]
    lane = lax.broadcasted_iota(jnp.int32, logits.shape, 1)
    vals, idxs = [], []
    for _ in range(TOP_K):
        mx = jnp.max(logits, axis=-1, keepdims=True)
        ix = jnp.argmax(logits, axis=-1, keepdims=True).astype(jnp.int32)
        vals.append(mx)
        idxs.append(ix)
        logits = jnp.where(lane == ix, NEG_BIG, logits)
    es = [jnp.exp(v - vals[0]) for v in vals]
    den = es[0] + es[1] + es[2] + es[3]

    @pl.when(pl.program_id(0) == 0)
    def _():
        run_ref[...] = jnp.zeros(run_ref.shape, F32)

    tm = lane.shape[0]
    earlier = (lax.broadcasted_iota(jnp.int32, (tm, tm), 0)
               > lax.broadcasted_iota(jnp.int32, (tm, tm), 1)).astype(BF16)
    seen = run_ref[...]
    idx_out = jnp.zeros(lane.shape, jnp.int32)
    gw_out = jnp.zeros(lane.shape, F32)
    rank_out = jnp.zeros(lane.shape, jnp.int32)
    for k in range(TOP_K):
        hit = lane == idxs[k]
        before = jnp.dot(earlier, hit.astype(BF16), preferred_element_type=F32) + seen
        rank_k = jnp.sum(jnp.where(hit, before, 0.0), axis=-1, keepdims=True)
        seen = seen + jnp.sum(hit.astype(F32), axis=0, keepdims=True)
        idx_out = jnp.where(lane == k, idxs[k], idx_out)
        gw_out = jnp.where(lane == k, es[k] / den, gw_out)
        rank_out = jnp.where(lane == k, rank_k.astype(jnp.int32), rank_out)
    run_ref[...] = seen
    cnt_ref[...] = seen
    idx_ref[...] = idx_out
    gw_ref[...] = gw_out
    rank_ref[...] = rank_out


def _plain_out_kernel(a_ref, w_ref, x_ref, mod_ref, g2_ref, wr_ref, br_ref, *rest, d):
    m = jnp.dot(a_ref[...].astype(BF16), w_ref[...], preferred_element_type=F32)
    _tail(m, x_ref, mod_ref[...], g2_ref, wr_ref, br_ref, rest[:-1], rest[-1], d)


def _conv_out_kernel(b_ref, cu_ref, prev_ref, next_ref, ck_ref, w_ref, x_ref, mod_ref, g2_ref, wr_ref,
                     br_ref, *rest, d, tm, n_prompt, seq, dec_seq):
    i = pl.program_id(0)
    cu = cu_ref[...]
    row = lax.broadcasted_iota(jnp.int32, (tm, 1), 0)
    start = i * tm
    in_prompt = start < n_prompt
    seq_len = jnp.where(in_prompt, seq, dec_seq)
    pos = lax.rem(jnp.where(in_prompt, start, start - n_prompt), seq_len) + row
    prev = jnp.where(row == 0, prev_ref[SUBLANES - 1:SUBLANES, :], pltpu.roll(cu, 1, 0))
    prev = jnp.where(pos == 0, 0.0, prev)
    nxt = jnp.where(row == tm - 1, next_ref[0:1, :], pltpu.roll(cu, tm - 1, 0))
    nxt = jnp.where(pos == seq_len - 1, 0.0, nxt)
    conv = prev * ck_ref[0:1, :] + cu * ck_ref[1:2, :] + nxt * ck_ref[2:3, :]
    m = jnp.dot((b_ref[...] * conv).astype(BF16), w_ref[...], preferred_element_type=F32)
    _tail(m, x_ref, mod_ref[...], g2_ref, wr_ref, br_ref, rest[:-1], rest[-1], d)


def _hgrn_out_kernel(o_ref, gout_ref, ng_ref, w_ref, x_ref, mod_ref, g2_ref, wr_ref, br_ref, *rest, d):
    parts = []
    for h in range(HG_HEADS):
        oh = o_ref[:, h * LANES:(h + 1) * LANES]
        parts.append(oh * lax.rsqrt(jnp.mean(oh * oh, axis=-1, keepdims=True) + EPS))
    gout = gout_ref[...]
    y = (jnp.concatenate(parts, axis=1) * ng_ref[...]) * (gout * jax.nn.sigmoid(gout))
    m = jnp.dot(y.astype(BF16), w_ref[...], preferred_element_type=F32)
    _tail(m, x_ref, mod_ref[...], g2_ref, wr_ref, br_ref, rest[:-1], rest[-1], d)


def _mixer_out(kind, acts, w_out, x, mods_l, g2, w_router, b_router, geom, extra=()):
    t, d = x.shape
    tm = TOKEN_TILE
    n_prompt, dec_seq, seq = geom
    grp = functools.partial(_group_of_tile, tile=tm, n_prompt=n_prompt, dec_seq=dec_seq)
    row_spec = pl.BlockSpec((tm, d), lambda i: (i, 0))
    wr = jnp.zeros((d, LANES), F32).at[:, :N_EXPERTS].set(w_router)
    wr_top = _bf16_part(wr)
    wr = jnp.stack([wr_top.astype(BF16), (wr - wr_top).astype(BF16)])
    br = jnp.full((1, LANES), NEG_BIG, F32).at[0, :N_EXPERTS].set(b_router)
    common_specs = [pl.BlockSpec((d, d), lambda i: (0, 0)),
                    row_spec,
                    pl.BlockSpec((None, 1, 6 * d), lambda i: (grp(i), 0, 0)),
                    pl.BlockSpec((1, d), lambda i: (0, 0)),
                    pl.BlockSpec((2, d, LANES), lambda i: (0, 0, 0)),
                    pl.BlockSpec((1, LANES), lambda i: (0, 0))]
    common_args = [w_out, x, mods_l, g2, wr, br]
    if kind == "plain":
        body = functools.partial(_plain_out_kernel, d=d)
        specs = [row_spec] + common_specs
        args = list(acts) + common_args
    elif kind == "conv":
        body = functools.partial(_conv_out_kernel, d=d, tm=tm, n_prompt=n_prompt, seq=seq, dec_seq=dec_seq)
        per = tm // SUBLANES
        last = t // SUBLANES - 1
        specs = [row_spec, row_spec,
                 pl.BlockSpec((SUBLANES, d), lambda i: (jnp.maximum(i * per - 1, 0), 0)),
                 pl.BlockSpec((SUBLANES, d), lambda i: (jnp.minimum((i + 1) * per, last), 0)),
                 pl.BlockSpec((CONV_WIDTH, d), lambda i: (0, 0))] + common_specs
        b, cu = acts
        args = [b, cu, cu, cu, extra[0]] + common_args
    else:
        body = functools.partial(_hgrn_out_kernel, d=d)
        specs = [row_spec, row_spec, pl.BlockSpec((1, d), lambda i: (0, 0))] + common_specs
        args = list(acts) + [extra[0]] + common_args
    lane_spec = pl.BlockSpec((tm, LANES), lambda i: (i, 0))
    return pl.pallas_call(
        body,
        grid=(t // tm,),
        in_specs=specs,
        out_specs=[row_spec, pl.BlockSpec((PACKED_CHUNKS, tm, CHUNK_W), lambda i: (0, i, 0)),
                   lane_spec, lane_spec, lane_spec, pl.BlockSpec((1, LANES), lambda i: (0, 0))],
        out_shape=[jax.ShapeDtypeStruct((t, d), F32), jax.ShapeDtypeStruct((PACKED_CHUNKS, t, CHUNK_W), jnp.int32),
                   jax.ShapeDtypeStruct((t, LANES), jnp.int32), jax.ShapeDtypeStruct((t, LANES), F32),
                   jax.ShapeDtypeStruct((t, LANES), jnp.int32), jax.ShapeDtypeStruct((1, LANES), F32)],
        scratch_shapes=[pltpu.VMEM((1, LANES), F32)],
        compiler_params=_params(("arbitrary",), 40),
        name="mixer_out_" + kind,
    )(*args)


def _softmax_rows(parts):
    mx = functools.reduce(jnp.maximum, [jnp.max(s, axis=-1, keepdims=True) for s in parts])
    es = [jnp.exp(s - mx) for s in parts]
    den = functools.reduce(lambda a, b: a + b, [jnp.sum(e, axis=-1, keepdims=True) for e in es])
    return [e / den for e in es]


def _head_pair_queries(q, dh):
    first = lax.broadcasted_iota(jnp.int32, (1, q.shape[1]), 1) < dh
    return first, (jnp.where(first, q, 0.0).astype(BF16), jnp.where(first, 0.0, q).astype(BF16))


def _na_ctx_kernel(q_ref, k_ref, v_ref, o_ref, *, dh, scale):
    nt = (((1,), (1,)), ((), ()))
    for pair in range(NA_HEADS * dh // LANES):
        sl = slice(pair * LANES, (pair + 1) * LANES)
        first, queries = _head_pair_queries(q_ref[:, sl], dh)
        n_q = q_ref.shape[0]
        q2 = jnp.concatenate(queries, axis=0)
        (p,) = _softmax_rows([lax.dot_general(q2, k_ref[:, sl].astype(BF16), nt,
                                              preferred_element_type=F32) * scale])
        o2 = jnp.dot(p.astype(BF16), v_ref[:, sl].astype(BF16), preferred_element_type=F32)
        o_ref[:, sl] = jnp.where(first, o2[:n_q], o2[n_q:])


def _na_context(q, k, v, batch, seq, t_total, dec_seq):
    d = q.shape[1]
    dh = d // NA_HEADS
    per = dec_seq // seq
    spec = pl.BlockSpec((seq, d), lambda b: (b, 0))
    return pl.pallas_call(
        functools.partial(_na_ctx_kernel, dh=dh, scale=dh ** -0.5),
        grid=(batch,),
        in_specs=[spec, spec, spec],
        out_specs=pl.BlockSpec((None, seq, d), lambda b: (b // per, b % per, 0)),
        out_shape=jax.ShapeDtypeStruct((t_total // dec_seq, dec_seq, d), F32),
        compiler_params=_params(("arbitrary",), 32),
        name="na_context",
    )(q, k, v)


def _na_lat_kernel(*refs, dh, scale, kr, n_batch):
    per = 1 + 2 * kr
    ck_ref, cv_ref, bias_ref, _, o_ref = refs[n_batch * per:]
    nt = (((1,), (1,)), ((), ()))
    heads_per_group = LANES // dh
    for pair in range(NA_HEADS // heads_per_group):
        sl = slice(pair * LANES, (pair + 1) * LANES)
        for b in range(n_batch):
            q_ref = refs[b * per]
            k_refs = refs[b * per + 1:b * per + 1 + kr]
            v_refs = refs[b * per + 1 + kr:(b + 1) * per]
            first, queries = _head_pair_queries(q_ref[:, sl], dh)
            n_q = q_ref.shape[0]
            q2 = jnp.concatenate(queries, axis=0)
            bias2 = jnp.concatenate([bias_ref[pair * heads_per_group + j] for j in range(heads_per_group)], axis=0)
            kw = jnp.concatenate([r[:, sl] for r in k_refs], axis=0).astype(BF16)
            vw = jnp.concatenate([r[:, sl] for r in v_refs], axis=0).astype(BF16)
            s_nb = lax.dot_general(q2, kw, nt, preferred_element_type=F32) * scale + bias2
            s_cx = lax.dot_general(q2, ck_ref[b, :, sl].astype(BF16), nt, preferred_element_type=F32) * scale
            p_nb, p_cx = _softmax_rows([s_nb, s_cx])
            o2 = (jnp.dot(p_nb.astype(BF16), vw, preferred_element_type=F32)
                  + jnp.dot(p_cx.astype(BF16), cv_ref[b, :, sl].astype(BF16), preferred_element_type=F32))
            o_ref[b, :, sl] = jnp.where(first, o2[:n_q], o2[n_q:])


def _na_bias_table(rpb, kr):
    col = np.arange(GRID_W)
    col_start = np.clip(col - NA_COLS // 2, 0, GRID_W - NA_COLS)
    delta = col[None, :] - col[:, None] + (NA_COLS - 1)
    inside = (col[None, :] >= col_start[:, None]) & (col[None, :] < col_start[:, None] + NA_COLS)
    pick = (np.arange(2 * NA_COLS - 1)[:, None, None] == delta[None]).astype(np.float32)
    tab = jnp.einsum("hrk,kwc->hrwc", rpb, jnp.asarray(pick), precision=HIGHEST)
    tab = jnp.where(jnp.asarray(inside)[None, None], tab, NEG_BIG)
    out = []
    for d0 in range(NA_ROWS_MAX):
        rows = tab[:, d0:d0 + kr]
        out.append(rows.transpose(0, 2, 1, 3).reshape(NA_HEADS, GRID_W, kr * GRID_W))
    return jnp.stack(out, axis=0)


def _na_latent(q, k, v, ctx_k, ctx_v, rpb, out_buf, dec_batch, dec_seq):
    d = q.shape[1]
    dh = d // NA_HEADS
    rows = dec_seq // GRID_W
    kr = min(NA_ROWS_MAX, rows)
    base = 0
    past = ctx_k.shape[1]
    out_slab = (out_buf.shape[0] - dec_batch) // dec_batch
    bias = _na_bias_table(rpb, kr)

    def row_start(r):
        return jnp.clip(r - kr // 2, 0, rows - kr)

    def win_spec(b, j):
        return pl.BlockSpec((GRID_W, d), lambda r: (base + b * rows + row_start(r) + j, 0))

    in_specs, args = [], []
    for b in range(dec_batch):
        in_specs += ([pl.BlockSpec((GRID_W, d), lambda r, b=b: (base + b * rows + r, 0))]
                     + [win_spec(b, j) for j in range(kr)] * 2)
        args += [q] + [k] * kr + [v] * kr
    in_specs += ([pl.BlockSpec((dec_batch, past, d), lambda r: (0, 0, 0))] * 2
                 + [pl.BlockSpec((None, NA_HEADS, GRID_W, kr * GRID_W),
                                 lambda r: (row_start(r) - r + NA_ROWS_MAX - 1, 0, 0, 0))])
    in_specs.append(pl.BlockSpec(memory_space=pl.ANY))
    args += [ctx_k.reshape(dec_batch, past, d), ctx_v.reshape(dec_batch, past, d), bias, out_buf]
    return pl.pallas_call(
        functools.partial(_na_lat_kernel, dh=dh, scale=dh ** -0.5, kr=kr, n_batch=dec_batch),
        grid=(rows,),
        in_specs=in_specs,
        out_specs=pl.BlockSpec((dec_batch, GRID_W, d), lambda r: (out_slab, r, 0)),
        out_shape=jax.ShapeDtypeStruct(out_buf.shape, F32),
        input_output_aliases={len(args) - 1: 0},
        compiler_params=_params(("arbitrary",), 40),
        name="na_latent",
    )(*args)


def _hgrn_scan_kernel(*refs, seq_len, has_s0, has_buf, want_state):
    q_ref, v_ref, z0_ref, z1_ref, lb_ref, tri_ref = refs[:6]
    rest = list(refs[6:])
    s0_ref = rest.pop(0) if has_s0 else None
    if has_buf:
        rest.pop(0)
    o_ref = rest.pop(0)
    sf_ref = rest.pop(0) if want_state else None
    ob_ref = rest.pop(0)
    c, r = HG_CHUNK, HG_TILE
    nc, nt = r // c, seq_len // r
    dk = HG_KEY_DIM
    tpos8 = lax.broadcasted_iota(jnp.int32, (1, SUBLANES, 1), 1)
    row = lax.broadcasted_iota(jnp.int32, (r, r), 0)
    col = lax.broadcasted_iota(jnp.int32, (r, r), 1)
    same16 = (row // (2 * SUBLANES)) == (col // (2 * SUBLANES))
    same32 = (row // c) == (col // c)
    nt_dims = (((1,), (1,)), ((), ()))
    tn_dims = (((0,), (0,)), ((), ()))

    def tile(dr, hh, i, st):
        z_ref = (z0_ref, z1_ref)[dr]
        cols = slice(hh * dk, (hh + 1) * dk)
        lb = lb_ref[dr:dr + 1, cols]
        tri = tri_ref[dr]
        ti = i if dr == 0 else nt - 1 - i
        r0 = pl.multiple_of(ti * r, r)
        q = q_ref[pl.ds(r0, r), cols]
        v = v_ref[pl.ds(r0, r), cols]
        z = z_ref[pl.ds(r0, r), cols]
        g = jnp.log(lb + (1.0 - lb) * jax.nn.sigmoid(z))
        k = (1.0 - lb) * jax.nn.sigmoid(-z)
        b = jnp.dot(tri, g, precision=HIGHEST, preferred_element_type=F32)
        b3, q3, k3, v3 = (a.reshape(nc, c, dk) for a in (b, q, k, v))
        nb = r // SUBLANES
        b8, q8, k8, v8 = (a.reshape(nb, SUBLANES, dk) for a in (b, q, k, v))
        oi = jnp.zeros((nb, SUBLANES, dk), F32)
        for s in range(SUBLANES):
            keep = (tpos8 >= s) if dr == 0 else (tpos8 <= s)
            e = jnp.exp(jnp.where(keep, b8 - b8[:, s:s + 1, :], NEG_BIG))
            a = jnp.sum(q8 * e * k8[:, s:s + 1, :], axis=-1, keepdims=True)
            oi = oi + a * v8[:, s:s + 1, :]
        att = None
        for span, same_group in ((2 * SUBLANES, same16), (c, same32)):
            half = span // 2
            bg = b.reshape(r // span, span, dk)
            pos = lax.broadcasted_iota(jnp.int32, (1, span, 1), 1)
            edge = bg[:, half - 1:half, :] if dr == 0 else bg[:, half:half + 1, :]
            is_query = (pos >= half) if dr == 0 else (pos < half)
            qf = q.reshape(bg.shape) * jnp.exp(jnp.where(is_query, bg - edge, NEG_BIG))
            kf = k.reshape(bg.shape) * jnp.exp(jnp.where(is_query, NEG_BIG, edge - bg))
            part = lax.dot_general(qf.reshape(r, dk).astype(BF16), kf.reshape(r, dk).astype(BF16), nt_dims,
                                   preferred_element_type=F32)
            part = jnp.where(same_group, part, 0.0)
            att = part if att is None else att + part
        oi = oi.reshape(r, dk) + jnp.dot(att.astype(BF16), v.astype(BF16), preferred_element_type=F32)
        oi = oi.reshape(nc, c, dk)
        bl = b3[:, c - 1:c, :] if dr == 0 else b3[:, 0:1, :]
        qt = (q3 * jnp.exp(b3)).astype(BF16)
        kt = (k3 * jnp.exp(bl - b3)).astype(BF16)
        vb = v3.astype(BF16)
        outs = [None] * nc
        for ci in (range(nc) if dr == 0 else range(nc - 1, -1, -1)):
            oc = lax.dot_general(qt[ci], st.astype(BF16), nt_dims, preferred_element_type=F32)
            outs[ci] = oi[ci] + oc
            kv = lax.dot_general(vb[ci], kt[ci], tn_dims, preferred_element_type=F32)
            st = st * jnp.exp(bl[ci]) + kv
        return r0, jnp.concatenate(outs, axis=0), st

    chains = [(dr, hh) for hh in range(HG_HEADS_PER_STEP) for dr in range(2)]

    def step(i, carry):
        new = []
        for (dr, hh), st in zip(chains, carry):
            r0, o, st = tile(dr, hh, i, st)
            (o_ref if dr == 0 else ob_ref)[pl.ds(r0, r), hh * dk:(hh + 1) * dk] = o
            new.append(st)
        return tuple(new)

    init = tuple(s0_ref[dr, hh] if has_s0 else jnp.zeros((dk, dk), F32) for dr, hh in chains)
    final = lax.fori_loop(0, nt, step, init)
    o_ref[...] += ob_ref[...]
    if want_state:
        for (dr, hh), st in zip(chains, final):
            sf_ref[dr, hh] = st.T


def _hgrn_tri():
    t = np.arange(HG_TILE)
    same = (t[:, None] // HG_CHUNK) == (t[None, :] // HG_CHUNK)
    fwd = same & (t[None, :] <= t[:, None])
    bwd = same & (t[None, :] >= t[:, None])
    return jnp.asarray(np.stack([fwd, bwd]).astype(np.float32))


def _hgrn_scan(q, v, z0, z1, lb, row0, n_seq, seq_len, s0_t=None, want_state=False, out_buf=None):
    t_total, d = q.shape
    dk = HG_KEY_DIM
    blk0 = row0 // seq_len
    hp = HG_HEADS_PER_STEP
    seq_spec = pl.BlockSpec((seq_len, hp * dk), lambda s, h: (blk0 + s, h))
    st_spec = pl.BlockSpec((None, 2, hp, dk, dk), lambda s, h: (s, 0, h, 0, 0))
    in_specs = [seq_spec] * 4 + [pl.BlockSpec((2, hp * dk), lambda s, h: (0, h)),
                                 pl.BlockSpec((2, HG_TILE, HG_TILE), lambda s, h: (0, 0, 0))]
    args = [q, v, z0, z1, lb, _hgrn_tri()]
    if s0_t is not None:
        in_specs.append(st_spec)
        args.append(s0_t)
    aliases = {}
    if out_buf is not None:
        in_specs.append(pl.BlockSpec(memory_space=pl.ANY))
        args.append(out_buf)
        aliases = {len(args) - 1: 0}
    out_specs = [seq_spec]
    out_shape = [jax.ShapeDtypeStruct((t_total, d), F32)]
    if want_state:
        out_specs.append(st_spec)
        out_shape.append(jax.ShapeDtypeStruct((n_seq, 2, HG_HEADS, dk, dk), F32))
    res = pl.pallas_call(
        functools.partial(_hgrn_scan_kernel, seq_len=seq_len, has_s0=s0_t is not None,
                          has_buf=out_buf is not None, want_state=want_state),
        grid=(n_seq, HG_HEADS // hp),
        in_specs=in_specs,
        out_specs=out_specs,
        out_shape=out_shape,
        input_output_aliases=aliases,
        scratch_shapes=[pltpu.VMEM((seq_len, hp * dk), F32)],
        compiler_params=_params(("arbitrary", "arbitrary"), 32),
        name="hgrn_scan",
    )(*args)
    return res if want_state else (res[0], None)


def _route(idx, rank, counts, t):
    n = t * TOP_K
    rb = MOE_ROWS
    n_blocks = -(-n // rb) + N_EXPERTS
    n_steps = (n_blocks + N_EXPERTS + 1) // 2
    n_rows_x = (n_blocks + N_EXPERTS) * rb
    n_rows_y = n_steps * 2 * rb
    experts = jnp.arange(N_EXPERTS, dtype=jnp.int32)
    counts = counts[0, :N_EXPERTS].astype(jnp.int32)
    blocks = (counts + rb - 1) // rb
    block_end = jnp.cumsum(blocks)
    block_start = block_end - blocks
    steps = (blocks + 1) // 2
    step_end = jnp.cumsum(steps)
    step_start = step_end - steps
    onehot = idx[:, :TOP_K, None] == experts
    pos = rank[:, :TOP_K]
    x_row = jnp.sum(jnp.where(onehot, block_start * rb, 0), axis=-1) + pos
    y_row = (jnp.sum(jnp.where(onehot, step_start, 0), axis=-1) + pos // (2 * rb)) * (2 * rb) + pos % (2 * rb)
    q = jnp.arange(rb, dtype=jnp.int32)
    spare = n_blocks * rb + experts[:, None] * rb + q[None, :]
    pad_rows = jnp.where(q[None, :] < (blocks * rb - counts)[:, None],
                         (block_start * rb + counts)[:, None] + q[None, :], spare)
    chunk = jnp.arange(PACKED_CHUNKS, dtype=jnp.int32)[:, None]
    x_list = jnp.concatenate([x_row.T.reshape(-1), pad_rows.reshape(-1)]).astype(jnp.int32)
    x_index = (chunk * n_rows_x + x_list[None, :]).reshape(1, -1)
    y_index = (chunk * n_rows_y + y_row.T.reshape(-1).astype(jnp.int32)[None, :]).reshape(1, -1)
    i = jnp.arange(n_steps, dtype=jnp.int32)
    e_i = jnp.minimum(jnp.sum((step_end[None, :] <= i[:, None]).astype(jnp.int32), axis=1), N_EXPERTS - 1)
    local = i - step_start[e_i]
    in_use = i < step_end[-1]
    first = jnp.where(in_use, block_start[e_i] + 2 * local, n_blocks)
    second = jnp.where(in_use & (2 * local + 1 < blocks[e_i]), first + 1, first)
    return x_index, y_index, (e_i, first, second, in_use.astype(jnp.int32)), n_rows_x, n_rows_y


def _sc_mesh():
    return plsc.VectorSubcoreMesh(core_axis_name="c", subcore_axis_name="s")


def _sc_dispatch(h2c, sc_index, t, n_rows):
    win = SC_WINDOW
    tw = t // win
    nw = (t * TOP_K + N_EXPERTS * MOE_ROWS) // win
    n_chunks = h2c.shape[0]

    def scatter(x_hbm, i_hbm, o_hbm):
        def body(x_vmem, i_vmem):
            pltpu.sync_copy(x_vmem, o_hbm.at[i_vmem.at[0]])

        pltpu.emit_pipeline(
            body,
            grid=(n_chunks * nw,),
            in_specs=[pl.BlockSpec((win, CHUNK_W), index_map=lambda i: ((i // nw) * tw + (i % nw) % tw, 0)),
                      pl.BlockSpec((1, win), index_map=lambda i: (0, i))],
            out_specs=[],
            core_axis_name=("c", "s"),
            dimension_semantics=(pltpu.PARALLEL,),
        )(x_hbm, i_hbm)

    out = pl.kernel(scatter, out_type=jax.ShapeDtypeStruct((n_chunks * n_rows, CHUNK_W), h2c.dtype),
                    mesh=_sc_mesh(), name="moe_dispatch")(h2c.reshape(n_chunks * t, CHUNK_W), sc_index)
    return out.reshape(n_chunks, n_rows, CHUNK_W)


def _sc_collect(y_rows, sc_index, t, n_rows):
    win = SC_WINDOW
    n_chunks = y_rows.shape[0]
    nw = sc_index.shape[1] // n_chunks // win
    aw = t * TOP_K // win

    def gather(y_hbm, i_hbm, o_hbm):
        def body(i_vmem, o_vmem):
            pltpu.sync_copy(y_hbm.at[i_vmem.at[0]], o_vmem)

        pltpu.emit_pipeline(
            body,
            grid=(n_chunks * aw,),
            in_specs=[pl.BlockSpec((1, win), index_map=lambda i: (0, (i // aw) * nw + i % aw))],
            out_specs=[pl.BlockSpec((win, CHUNK_W), index_map=lambda i: (i, 0))],
            core_axis_name=("c", "s"),
            dimension_semantics=(pltpu.PARALLEL,),
        )(i_hbm, o_hbm)

    out = pl.kernel(gather, out_type=jax.ShapeDtypeStruct((n_chunks * TOP_K * t, CHUNK_W), y_rows.dtype),
                    mesh=_sc_mesh(), name="moe_collect")(y_rows.reshape(n_chunks * n_rows, CHUNK_W), sc_index)
    return out.reshape(n_chunks, TOP_K, t, CHUNK_W)


def _moe_kernel(se_ref, s1_ref, s2_ref, su_ref, xa_ref, xb_ref, w1_ref, b1_ref, w2_ref, b2_ref, o_ref, w1b, w2b,
                *, dff):
    i = pl.program_id(0)
    rb = xa_ref.shape[1]
    e = se_ref[i]
    prev = se_ref[jnp.maximum(i - 1, 0)]
    in_use = su_ref[i] == 1
    two = s2_ref[i] != s1_ref[i]

    @pl.when(jnp.logical_or(i == 0, e != prev))
    def _():
        w1b[...] = w1_ref[...].astype(BF16)
        w2b[...] = w2_ref[...].astype(BF16)

    def ffn(words):
        halves = [_unpack_bf16_pairs(w) for w in words]
        xb = jnp.concatenate([h[0] for h in halves] + [h[1] for h in halves], axis=1).astype(BF16)
        gu = jnp.dot(xb, w1b[...], preferred_element_type=F32) + b1_ref[...]
        g = jnp.minimum(gu[:, :dff], SWIGLU_LIMIT)
        u = jnp.clip(gu[:, dff:], -SWIGLU_LIMIT, SWIGLU_LIMIT)
        a = (u + 1.0) * (g * jax.nn.sigmoid(SWIGLU_ALPHA * g))
        return _pack_bf16_pairs(jnp.dot(a.astype(BF16), w2b[...], preferred_element_type=F32) + b2_ref[...])

    @pl.when(jnp.logical_and(in_use, two))
    def _():
        out = ffn([jnp.concatenate([xa_ref[j], xb_ref[j]], axis=0) for j in range(PACKED_CHUNKS)])
        for j in range(PACKED_CHUNKS):
            o_ref[j] = out[:, j * CHUNK_W:(j + 1) * CHUNK_W]

    @pl.when(jnp.logical_and(in_use, jnp.logical_not(two)))
    def _():
        out = ffn([xa_ref[j] for j in range(PACKED_CHUNKS)])
        for j in range(PACKED_CHUNKS):
            o_ref[j, 0:rb, :] = out[:, j * CHUNK_W:(j + 1) * CHUNK_W]
        o_ref[:, rb:, :] = jnp.zeros((PACKED_CHUNKS, rb, CHUNK_W), o_ref.dtype)

    @pl.when(jnp.logical_not(in_use))
    def _():
        o_ref[...] = jnp.zeros(o_ref.shape, o_ref.dtype)


def _moe_experts(x_rows, steps, n_rows_y, layer, w1, b1, w2, b2):
    rb = MOE_ROWS
    n_steps = n_rows_y // (2 * rb)
    depth, n_e, d, dff2 = w1.shape
    dff = dff2 // 2

    def weights(shape):
        return pl.BlockSpec((None, None) + shape, lambda i, se, s1, s2, su: (layer, se[i], 0, 0))

    grid_spec = pltpu.PrefetchScalarGridSpec(
        num_scalar_prefetch=4,
        grid=(n_steps,),
        in_specs=[pl.BlockSpec((PACKED_CHUNKS, rb, CHUNK_W), lambda i, se, s1, s2, su: (0, s1[i], 0)),
                  pl.BlockSpec((PACKED_CHUNKS, rb, CHUNK_W), lambda i, se, s1, s2, su: (0, s2[i], 0)),
                  weights((d, dff2)), weights((1, dff2)), weights((dff, d)), weights((1, d))],
        out_specs=pl.BlockSpec((PACKED_CHUNKS, 2 * rb, CHUNK_W), lambda i, se, s1, s2, su: (0, i, 0)),
        scratch_shapes=[pltpu.VMEM((d, dff2), BF16), pltpu.VMEM((dff, d), BF16)],
    )
    return pl.pallas_call(
        functools.partial(_moe_kernel, dff=dff),
        grid_spec=grid_spec,
        out_shape=jax.ShapeDtypeStruct((PACKED_CHUNKS, n_rows_y, CHUNK_W), jnp.int32),
        compiler_params=_params(("arbitrary",), 56),
        name="moe_experts",
    )(*steps, x_rows, x_rows, w1, b1.reshape(depth, n_e, 1, dff2), w2, b2.reshape(depth, n_e, 1, d))


def _combined_rows(y_ref, gw_ref, x_ref, mod_ref, d):
    gw = gw_ref[...]
    pieces = {}
    for j in range(PACKED_CHUNKS):
        accs = [None, None]
        for k in range(TOP_K):
            for side, part in enumerate(_unpack_bf16_pairs(y_ref[j, k])):
                term = gw[:, k:k + 1] * part
                accs[side] = term if accs[side] is None else accs[side] + term
        for side, acc in enumerate(accs):
            c0 = side * (d // 2) + j * CHUNK_W
            pieces[c0] = x_ref[:, c0:c0 + CHUNK_W] + mod_ref[:, 5 * d + c0:5 * d + c0 + CHUNK_W] * acc
    return jnp.concatenate([pieces[c0] for c0 in sorted(pieces)], axis=1)


def _combine_final_kernel(y_ref, gw_ref, x_ref, mod_ref, g_ref, o_ref, *, d):
    o_ref[...] = _rms(_combined_rows(y_ref, gw_ref, x_ref, mod_ref, d)) * g_ref[...]


def _combine_in_kernel(y_ref, gw_ref, x_ref, mod_ref, modn_ref, g_ref, w_ref, *rest, d, conv, has_buf):
    x2_ref, *out_refs = rest[1:] if has_buf else rest
    x2 = _combined_rows(y_ref, gw_ref, x_ref, mod_ref, d)
    x2_ref[...] = x2
    _project(x2, modn_ref[...], g_ref, w_ref, out_refs, d, conv)


def _combine_specs(y_slots, gw, x1, mods_l, geom, rows):
    d = x1.shape[1]
    row0, t = rows if rows is not None else (0, x1.shape[0])
    tm = TOKEN_TILE
    t0 = row0 // tm
    grp = functools.partial(_group_of_tile, tile=tm, n_prompt=geom[0], dec_seq=geom[1])
    in_specs = [pl.BlockSpec((PACKED_CHUNKS, TOP_K, tm, CHUNK_W), lambda i: (0, 0, t0 + i, 0)),
                pl.BlockSpec((tm, LANES), lambda i: (t0 + i, 0)),
                pl.BlockSpec((tm, d), lambda i: (t0 + i, 0)),
                pl.BlockSpec((None, 1, 6 * d), lambda i: (grp(t0 + i), 0, 0))]
    return in_specs, [y_slots, gw, x1, mods_l], t, t0, grp


def _combine_final(y_slots, gw, x1, mods_l, geom, final_g, rows):
    d = x1.shape[1]
    tm = TOKEN_TILE
    in_specs, args, t, _, _ = _combine_specs(y_slots, gw, x1, mods_l, geom, rows)
    return pl.pallas_call(
        functools.partial(_combine_final_kernel, d=d),
        grid=(t // tm,),
        in_specs=in_specs + [pl.BlockSpec((1, d), lambda i: (0, 0))],
        out_specs=pl.BlockSpec((tm, d), lambda i: (i, 0)),
        out_shape=jax.ShapeDtypeStruct((t, d), F32),
        compiler_params=_params(("arbitrary",), 24),
        name="moe_combine_final",
    )(*args, final_g)


def _combine_mixer_in(y_slots, gw, x1, mods_l, mods_next, g, w, geom, conv=False, rows=None, x2_buf=None):
    t_total, d = x1.shape
    tm = TOKEN_TILE
    n = w.shape[1]
    n_out = 2 if conv else n // d
    in_specs, args, t, t0, grp = _combine_specs(y_slots, gw, x1, mods_l, geom, rows)
    in_specs += [pl.BlockSpec((None, 1, 6 * d), lambda i: (grp(t0 + i), 0, 0)),
                 pl.BlockSpec((1, d), lambda i: (0, 0)),
                 pl.BlockSpec((d, n), lambda i: (0, 0))]
    args += [mods_next, g, w]
    aliases = {}
    if x2_buf is not None:
        in_specs.append(pl.BlockSpec(memory_space=pl.ANY))
        args.append(x2_buf)
        aliases = {len(args) - 1: 0}
    res = pl.pallas_call(
        functools.partial(_combine_in_kernel, d=d, conv=conv, has_buf=x2_buf is not None),
        grid=(t // tm,),
        in_specs=in_specs,
        out_specs=[pl.BlockSpec((tm, d), lambda i: (t0 + i, 0))] + [pl.BlockSpec((tm, d), lambda i: (i, 0))] * n_out,
        out_shape=[jax.ShapeDtypeStruct((t_total, d), F32)] + [jax.ShapeDtypeStruct((t, d), F32)] * n_out,
        input_output_aliases=aliases,
        compiler_params=_params(("arbitrary",), 56),
        name="combine_mixer_in",
    )(*args)
    return res[0], res[1:]


def kernel(x_prompt, x_sample, c, c_ctx, cache_na_k, cache_na_v, state_hgrn, norm_mix, norm_ffn, w_mod, b_mod,
           conv_w_in, conv_k, conv_w_out, na_w_qkv, na_rpb, na_w_o, hg_w_qig, hg_w_f, hg_lb, hg_norm, hg_w_o,
           moe_w_router, moe_b_router, moe_w1, moe_b1, moe_w2, moe_b2, final_norm):
    batch, seq, d = x_prompt.shape
    dec_batch, dec_seq, _ = x_sample.shape
    depth = w_mod.shape[0]
    n_prompt = batch * seq
    t = n_prompt + dec_batch * dec_seq
    geom = (n_prompt, dec_seq, seq)
    assert seq % TOKEN_TILE == 0 and dec_seq % TOKEN_TILE == 0 and seq % HG_TILE == 0
    assert n_prompt % (dec_seq * dec_batch) == 0 and dec_seq % GRID_W == 0 and 1 + dec_batch <= SUBLANES
    assert d == 2 * PACKED_CHUNKS * CHUNK_W and t % SC_WINDOW == 0 and MOE_ROWS % SC_WINDOW == 0

    x = jnp.concatenate([x_prompt.reshape(n_prompt, d), x_sample.reshape(dec_batch * dec_seq, d)], axis=0)
    cvecs = jnp.zeros((SUBLANES, d), F32).at[0].set(c_ctx).at[1:1 + dec_batch].set(c)
    mods = _adaln_all(cvecs, w_mod, b_mod)[:, :1 + dec_batch].reshape(depth, 1 + dec_batch, 1, 6 * d)

    lb_soft = jax.nn.softmax(hg_lb.astype(F32), axis=0)
    lower_bounds = jnp.cumsum(lb_soft, axis=0) - lb_soft[0]

    new_k, new_v, new_s = [], [], []
    pending = None
    for l in range(depth):
        kind, j = l % N_MIXERS, l // N_MIXERS
        g1 = norm_mix[l].reshape(1, d)
        g2 = norm_ffn[l].reshape(1, d)

        def project(x, w, conv=False, rows=None, x2_buf=None):
            if pending is None:
                return x, _mixer_in(x, mods[l], g1, w, geom, conv=conv, rows=rows)
            return _combine_mixer_in(*pending, mods[l], g1, w, geom, conv=conv, rows=rows, x2_buf=x2_buf)

        if kind == 0:
            x, (b, cu) = project(x, conv_w_in[j].astype(BF16), conv=True)
        elif kind == 1:
            w_qkv = na_w_qkv[j].astype(BF16)
            x, (qp, kp, vp) = project(x, w_qkv, rows=(0, n_prompt))
            x, (qs, ks, vs) = project(x, w_qkv, rows=(n_prompt, t - n_prompt), x2_buf=x)
        else:
            w_in = jnp.concatenate([hg_w_qig[j], hg_w_f[j, 0], hg_w_f[j, 1]], axis=1).astype(BF16)
            x, (q, v, gout, z0, z1) = project(x, w_in)
        tail_args = (x, mods[l], g2, moe_w_router[l], moe_b_router[l], geom)
        if kind == 0:
            x1, h2c, idx, gw, rank, counts = _mixer_out("conv", (b, cu), conv_w_out[j].astype(BF16), *tail_args,
                                         extra=(conv_k[j],))
        elif kind == 1:
            o = _na_context(qp, kp, vp, batch, seq, t, dec_seq)
            o = _na_latent(qs, ks, vs, cache_na_k[:, j], cache_na_v[:, j], na_rpb[j], o, dec_batch, dec_seq)
            new_k.append(kp.reshape(batch, seq, NA_HEADS, d // NA_HEADS))
            new_v.append(vp.reshape(batch, seq, NA_HEADS, d // NA_HEADS))
            x1, h2c, idx, gw, rank, counts = _mixer_out("plain", (o.reshape(t, d),),
                                         na_w_o[j].astype(BF16), *tail_args)
        else:
            lb = lower_bounds[l]
            o, s_p = _hgrn_scan(q, v, z0, z1, lb, 0, batch, seq, want_state=True)
            s0_t = jnp.swapaxes(state_hgrn[:, j].astype(F32), -1, -2)
            o, _ = _hgrn_scan(q, v, z0, z1, lb, n_prompt, dec_batch, dec_seq, s0_t=s0_t, out_buf=o)
            new_s.append(s_p)
            x1, h2c, idx, gw, rank, counts = _mixer_out("hgrn", (o, gout),
                                         hg_w_o[j].astype(BF16), *tail_args, extra=(hg_norm[j].reshape(1, d),))
        x_index, y_index, steps, n_rows_x, n_rows_y = _route(idx, rank, counts, t)
        x_rows = _sc_dispatch(h2c, x_index, t, n_rows_x)
        y_rows = _moe_experts(x_rows, steps, n_rows_y, l, moe_w1, moe_b1, moe_w2, moe_b2)
        y_slots = _sc_collect(y_rows, y_index, t, n_rows_y)
        pending = (y_slots, gw, x1, mods[l])

    final_g = final_norm.reshape(1, d)
    y_prompt = _combine_final(*pending, geom, final_g, (0, n_prompt)).reshape(batch, seq, d)
    y_sample = _combine_final(*pending, geom, final_g, (n_prompt, t - n_prompt)).reshape(dec_batch, dec_seq, d)
    new_na_k = jnp.stack(new_k, axis=1).astype(x_prompt.dtype)
    new_na_v = jnp.stack(new_v, axis=1).astype(x_prompt.dtype)
    new_hgrn_state = jnp.stack(new_s, axis=1).astype(x_prompt.dtype)
    return (y_prompt, y_sample, new_na_k, new_na_v, new_hgrn_state)
```

```python
import functools

import numpy as np
import jax
import jax.numpy as jnp
from jax import lax
from jax.experimental import pallas as pl
from jax.experimental.pallas import tpu as pltpu
from jax.experimental.pallas import tpu_sc as plsc

F32 = jnp.float32
BF16 = jnp.bfloat16
HIGHEST = lax.Precision.HIGHEST

N_MIXERS = 3
GRID_W = 64
CONV_WIDTH = 3
NA_HEADS = 16
NA_ROWS_MAX = 8
NA_COLS = 16
HG_HEADS = 8
HG_KEY_DIM = 128
HG_CHUNK = 32
N_EXPERTS = 32
TOP_K = 4
SWIGLU_LIMIT = 7.0
SWIGLU_ALPHA = 1.702
EPS = 1e-6

LANES = 128
SUBLANES = 8
VMEM_LIMIT_CAP = 60 * 1024 * 1024

NEG_BIG = -1e30
TOKEN_TILE = 256
MOE_ROWS = 256
HG_TILE = 128
HG_HEADS_PER_STEP = 2
PACKED_CHUNKS = 2
CHUNK_W = 256
SC_WINDOW = 128


def _params(sem, vmem_mb, flags=None):
    return pltpu.CompilerParams(dimension_semantics=sem, flags=flags,
                                vmem_limit_bytes=min(vmem_mb * 1024 * 1024, VMEM_LIMIT_CAP))


def _rms(x):
    return x * lax.rsqrt(jnp.mean(x * x, axis=-1, keepdims=True) + EPS)


def _modulate(x, g, shift, scale):
    return (_rms(x) * g) * (1.0 + scale) + shift


def _bf16_part(x):
    bits = lax.bitcast_convert_type(x, jnp.uint32) & jnp.uint32(0xFFFF0000)
    return lax.bitcast_convert_type(bits, F32)


def _pack_bf16_pairs(x):
    n = x.shape[1]
    bits = lax.bitcast_convert_type(x, jnp.uint32)
    top = (bits + (jnp.uint32(0x7FFF) + ((bits >> 16) & jnp.uint32(1)))) & jnp.uint32(0xFFFF0000)
    return lax.bitcast_convert_type(top[:, :n // 2] | (top[:, n // 2:] >> 16), jnp.int32)


def _unpack_bf16_pairs(words):
    bits = lax.bitcast_convert_type(words, jnp.uint32)
    return (lax.bitcast_convert_type(bits & jnp.uint32(0xFFFF0000), F32),
            lax.bitcast_convert_type(bits << 16, F32))


def _group_of_tile(i, tile, n_prompt, dec_seq):
    start = i * tile
    return jnp.where(start < n_prompt, 0, 1 + (start - n_prompt) // dec_seq)


def _adaln_kernel(c_ref, w_ref, b_ref, o_ref):
    cv = c_ref[...]
    s = cv * jax.nn.sigmoid(cv)
    o_ref[...] = jnp.dot(s, w_ref[...], precision=HIGHEST, preferred_element_type=F32) + b_ref[...]


def _adaln_all(cvecs, w_mod, b_mod):
    depth, d, n = w_mod.shape
    tn = 2048
    return pl.pallas_call(
        _adaln_kernel,
        grid=(depth, n // tn),
        in_specs=[pl.BlockSpec((SUBLANES, d), lambda l, j: (0, 0)),
                  pl.BlockSpec((None, d, tn), lambda l, j: (l, 0, j)),
                  pl.BlockSpec((None, 1, tn), lambda l, j: (l, 0, j))],
        out_specs=pl.BlockSpec((None, SUBLANES, tn), lambda l, j: (l, 0, j)),
        out_shape=jax.ShapeDtypeStruct((depth, SUBLANES, n), F32),
        compiler_params=_params(("arbitrary", "arbitrary"), 40),
        name="adaln",
    )(cvecs, w_mod, b_mod.reshape(depth, 1, n))


def _project(x, mod, g_ref, w_ref, out_refs, d, conv):
    h = _modulate(x, g_ref[...], mod[:, 0:d], mod[:, d:2 * d]).astype(BF16)
    if conv:
        b_ref, cu_ref = out_refs
        b_ref[...] = jnp.dot(h, w_ref[:, 0:d], preferred_element_type=F32)
        c = jnp.dot(h, w_ref[:, d:2 * d], preferred_element_type=F32)
        u = jnp.dot(h, w_ref[:, 2 * d:3 * d], preferred_element_type=F32)
        cu_ref[...] = c * u
    else:
        for j, o_ref in enumerate(out_refs):
            o_ref[...] = jnp.dot(h, w_ref[:, j * d:(j + 1) * d], preferred_element_type=F32)


def _in_kernel(x_ref, mod_ref, g_ref, w_ref, *out_refs, d, conv):
    _project(x_ref[...], mod_ref[...], g_ref, w_ref, out_refs, d, conv)


def _mixer_in(x, mods_l, g, w, geom, conv=False, rows=None):
    d = x.shape[1]
    row0, t = rows if rows is not None else (0, x.shape[0])
    n = w.shape[1]
    n_out = 2 if conv else n // d
    tm = TOKEN_TILE
    t0 = row0 // tm
    grp = functools.partial(_group_of_tile, tile=tm, n_prompt=geom[0], dec_seq=geom[1])
    return pl.pallas_call(
        functools.partial(_in_kernel, d=d, conv=conv),
        grid=(t // tm,),
        in_specs=[pl.BlockSpec((tm, d), lambda i: (t0 + i, 0)),
                  pl.BlockSpec((None, 1, 6 * d), lambda i: (grp(t0 + i), 0, 0)),
                  pl.BlockSpec((1, d), lambda i: (0, 0)),
                  pl.BlockSpec((d, n), lambda i: (0, 0))],
        out_specs=[pl.BlockSpec((tm, d), lambda i: (i, 0))] * n_out,
        out_shape=[jax.ShapeDtypeStruct((t, d), F32)] * n_out,
        compiler_params=_params(("arbitrary",), 48),
        name="mixer_in",
    )(x, mods_l, g, w)


def _tail(m, x_ref, mod, g2_ref, wr_ref, br_ref, outs, run_ref, d):
    x1_ref, h2_ref, idx_ref, gw_ref, rank_ref, cnt_ref = outs
    x1 = x_ref[...] + mod[:, 2 * d:3 * d] * m
    x1_ref[...] = x1
    h2 = _modulate(x1, g2_ref[...], mod[:, 3 * d:4 * d], mod[:, 4 * d:5 * d])
    words = _pack_bf16_pairs(h2)
    for j in range(PACKED_CHUNKS):
        h2_ref[j] = words[:, j * CHUNK_W:(j + 1) * CHUNK_W]
    h2_top = _bf16_part(h2)
    h2_hi = h2_top.astype(BF16)
    h2_lo = (h2 - h2_top).astype(BF16)
    logits = (jnp.dot(h2_hi, wr_ref[0], preferred_element_type=F32)
              + jnp.dot(h2_lo, wr_ref[0], preferred_element_type=F32)
              + jnp.dot(h2_hi, wr_ref[1], preferred_element_type=F32)) + br_ref[...]
    lane = lax.broadcasted_iota(jnp.int32, logits.shape, 1)
    vals, idxs = [], []
    for _ in range(TOP_K):
        mx = jnp.max(logits, axis=-1, keepdims=True)
        ix = jnp.argmax(logits, axis=-1, keepdims=True).astype(jnp.int32)
        vals.append(mx)
        idxs.append(ix)
        logits = jnp.where(lane == ix, NEG_BIG, logits)
    es = [jnp.exp(v - vals[0]) for v in vals]
    den = es[0] + es[1] + es[2] + es[3]

    @pl.when(pl.program_id(0) == 0)
    def _():
        run_ref[...] = jnp.zeros(run_ref.shape, F32)

    tm = lane.shape[0]
    earlier = (lax.broadcasted_iota(jnp.int32, (tm, tm), 0)
               > lax.broadcasted_iota(jnp.int32, (tm, tm), 1)).astype(BF16)
    seen = run_ref[...]
    idx_out = jnp.zeros(lane.shape, jnp.int32)
    gw_out = jnp.zeros(lane.shape, F32)
    rank_out = jnp.zeros(lane.shape, jnp.int32)
    for k in range(TOP_K):
        hit = lane == idxs[k]
        before = jnp.dot(earlier, hit.astype(BF16), preferred_element_type=F32) + seen
        rank_k = jnp.sum(jnp.where(hit, before, 0.0), axis=-1, keepdims=True)
        seen = seen + jnp.sum(hit.astype(F32), axis=0, keepdims=True)
        idx_out = jnp.where(lane == k, idxs[k], idx_out)
        gw_out = jnp.where(lane == k, es[k] / den, gw_out)
        rank_out = jnp.where(lane == k, rank_k.astype(jnp.int32), rank_out)
    run_ref[...] = seen
    cnt_ref[...] = seen
    idx_ref[...] = idx_out
    gw_ref[...] = gw_out
    rank_ref[...] = rank_out


def _plain_out_kernel(a_ref, w_ref, x_ref, mod_ref, g2_ref, wr_ref, br_ref, *rest, d):
    m = jnp.dot(a_ref[...].astype(BF16), w_ref[...], preferred_element_type=F32)
    _tail(m, x_ref, mod_ref[...], g2_ref, wr_ref, br_ref, rest[:-1], rest[-1], d)


def _conv_out_kernel(b_ref, cu_ref, prev_ref, next_ref, ck_ref, w_ref, x_ref, mod_ref, g2_ref, wr_ref,
                     br_ref, *rest, d, tm, n_prompt, seq, dec_seq):
    i = pl.program_id(0)
    cu = cu_ref[...]
    row = lax.broadcasted_iota(jnp.int32, (tm, 1), 0)
    start = i * tm
    in_prompt = start < n_prompt
    seq_len = jnp.where(in_prompt, seq, dec_seq)
    pos = lax.rem(jnp.where(in_prompt, start, start - n_prompt), seq_len) + row
    prev = jnp.where(row == 0, prev_ref[SUBLANES - 1:SUBLANES, :], pltpu.roll(cu, 1, 0))
    prev = jnp.where(pos == 0, 0.0, prev)
    nxt = jnp.where(row == tm - 1, next_ref[0:1, :], pltpu.roll(cu, tm - 1, 0))
    nxt = jnp.where(pos == seq_len - 1, 0.0, nxt)
    conv = prev * ck_ref[0:1, :] + cu * ck_ref[1:2, :] + nxt * ck_ref[2:3, :]
    m = jnp.dot((b_ref[...] * conv).astype(BF16), w_ref[...], preferred_element_type=F32)
    _tail(m, x_ref, mod_ref[...], g2_ref, wr_ref, br_ref, rest[:-1], rest[-1], d)


def _hgrn_out_kernel(o_ref, gout_ref, ng_ref, w_ref, x_ref, mod_ref, g2_ref, wr_ref, br_ref, *rest, d):
    parts = []
    for h in range(HG_HEADS):
        oh = o_ref[:, h * LANES:(h + 1) * LANES]
        parts.append(oh * lax.rsqrt(jnp.mean(oh * oh, axis=-1, keepdims=True) + EPS))
    gout = gout_ref[...]
    y = (jnp.concatenate(parts, axis=1) * ng_ref[...]) * (gout * jax.nn.sigmoid(gout))
    m = jnp.dot(y.astype(BF16), w_ref[...], preferred_element_type=F32)
    _tail(m, x_ref, mod_ref[...], g2_ref, wr_ref, br_ref, rest[:-1], rest[-1], d)


def _mixer_out(kind, acts, w_out, x, mods_l, g2, w_router, b_router, geom, extra=()):
    t, d = x.shape
    tm = TOKEN_TILE
    n_prompt, dec_seq, seq = geom
    grp = functools.partial(_group_of_tile, tile=tm, n_prompt=n_prompt, dec_seq=dec_seq)
    row_spec = pl.BlockSpec((tm, d), lambda i: (i, 0))
    wr = jnp.zeros((d, LANES), F32).at[:, :N_EXPERTS].set(w_router)
    wr_top = _bf16_part(wr)
    wr = jnp.stack([wr_top.astype(BF16), (wr - wr_top).astype(BF16)])
    br = jnp.full((1, LANES), NEG_BIG, F32).at[0, :N_EXPERTS].set(b_router)
    common_specs = [pl.BlockSpec((d, d), lambda i: (0, 0)),
                    row_spec,
                    pl.BlockSpec((None, 1, 6 * d), lambda i: (grp(i), 0, 0)),
                    pl.BlockSpec((1, d), lambda i: (0, 0)),
                    pl.BlockSpec((2, d, LANES), lambda i: (0, 0, 0)),
                    pl.BlockSpec((1, LANES), lambda i: (0, 0))]
    common_args = [w_out, x, mods_l, g2, wr, br]
    if kind == "plain":
        body = functools.partial(_plain_out_kernel, d=d)
        specs = [row_spec] + common_specs
        args = list(acts) + common_args
    elif kind == "conv":
        body = functools.partial(_conv_out_kernel, d=d, tm=tm, n_prompt=n_prompt, seq=seq, dec_seq=dec_seq)
        per = tm // SUBLANES
        last = t // SUBLANES - 1
        specs = [row_spec, row_spec,
                 pl.BlockSpec((SUBLANES, d), lambda i: (jnp.maximum(i * per - 1, 0), 0)),
                 pl.BlockSpec((SUBLANES, d), lambda i: (jnp.minimum((i + 1) * per, last), 0)),
                 pl.BlockSpec((CONV_WIDTH, d), lambda i: (0, 0))] + common_specs
        b, cu = acts
        args = [b, cu, cu, cu, extra[0]] + common_args
    else:
        body = functools.partial(_hgrn_out_kernel, d=d)
        specs = [row_spec, row_spec, pl.BlockSpec((1, d), lambda i: (0, 0))] + common_specs
        args = list(acts) + [extra[0]] + common_args
    lane_spec = pl.BlockSpec((tm, LANES), lambda i: (i, 0))
    return pl.pallas_call(
        body,
        grid=(t // tm,),
        in_specs=specs,
        out_specs=[row_spec, pl.BlockSpec((PACKED_CHUNKS, tm, CHUNK_W), lambda i: (0, i, 0)),
                   lane_spec, lane_spec, lane_spec, pl.BlockSpec((1, LANES), lambda i: (0, 0))],
        out_shape=[jax.ShapeDtypeStruct((t, d), F32), jax.ShapeDtypeStruct((PACKED_CHUNKS, t, CHUNK_W), jnp.int32),
                   jax.ShapeDtypeStruct((t, LANES), jnp.int32), jax.ShapeDtypeStruct((t, LANES), F32),
                   jax.ShapeDtypeStruct((t, LANES), jnp.int32), jax.ShapeDtypeStruct((1, LANES), F32)],
        scratch_shapes=[pltpu.VMEM((1, LANES), F32)],
        compiler_params=_params(("arbitrary",), 40),
        name="mixer_out_" + kind,
    )(*args)


def _softmax_rows(parts):
    mx = functools.reduce(jnp.maximum, [jnp.max(s, axis=-1, keepdims=True) for s in parts])
    es = [jnp.exp(s - mx) for s in parts]
    den = functools.reduce(lambda a, b: a + b, [jnp.sum(e, axis=-1, keepdims=True) for e in es])
    return [e / den for e in es]


def _head_pair_queries(q, dh):
    first = lax.broadcasted_iota(jnp.int32, (1, q.shape[1]), 1) < dh
    return first, (jnp.where(first, q, 0.0).astype(BF16), jnp.where(first, 0.0, q).astype(BF16))


def _na_ctx_kernel(q_ref, k_ref, v_ref, o_ref, *, dh, scale):
    nt = (((1,), (1,)), ((), ()))
    for pair in range(NA_HEADS * dh // LANES):
        sl = slice(pair * LANES, (pair + 1) * LANES)
        first, queries = _head_pair_queries(q_ref[:, sl], dh)
        n_q = q_ref.shape[0]
        q2 = jnp.concatenate(queries, axis=0)
        (p,) = _softmax_rows([lax.dot_general(q2, k_ref[:, sl].astype(BF16), nt,
                                              preferred_element_type=F32) * scale])
        o2 = jnp.dot(p.astype(BF16), v_ref[:, sl].astype(BF16), preferred_element_type=F32)
        o_ref[:, sl] = jnp.where(first, o2[:n_q], o2[n_q:])


def _na_context(q, k, v, batch, seq, t_total, dec_seq):
    d = q.shape[1]
    dh = d // NA_HEADS
    per = dec_seq // seq
    spec = pl.BlockSpec((seq, d), lambda b: (b, 0))
    return pl.pallas_call(
        functools.partial(_na_ctx_kernel, dh=dh, scale=dh ** -0.5),
        grid=(batch,),
        in_specs=[spec, spec, spec],
        out_specs=pl.BlockSpec((None, seq, d), lambda b: (b // per, b % per, 0)),
        out_shape=jax.ShapeDtypeStruct((t_total // dec_seq, dec_seq, d), F32),
        compiler_params=_params(("arbitrary",), 32),
        name="na_context",
    )(q, k, v)


def _na_lat_kernel(*refs, dh, scale, kr, n_batch):
    per = 1 + 2 * kr
    ck_ref, cv_ref, bias_ref, _, o_ref = refs[n_batch * per:]
    nt = (((1,), (1,)), ((), ()))
    heads_per_group = LANES // dh
    for pair in range(NA_HEADS // heads_per_group):
        sl = slice(pair * LANES, (pair + 1) * LANES)
        for b in range(n_batch):
            q_ref = refs[b * per]
            k_refs = refs[b * per + 1:b * per + 1 + kr]
            v_refs = refs[b * per + 1 + kr:(b + 1) * per]
            first, queries = _head_pair_queries(q_ref[:, sl], dh)
            n_q = q_ref.shape[0]
            q2 = jnp.concatenate(queries, axis=0)
            bias2 = jnp.concatenate([bias_ref[pair * heads_per_group + j] for j in range(heads_per_group)], axis=0)
            kw = jnp.concatenate([r[:, sl] for r in k_refs], axis=0).astype(BF16)
            vw = jnp.concatenate([r[:, sl] for r in v_refs], axis=0).astype(BF16)
            s_nb = lax.dot_general(q2, kw, nt, preferred_element_type=F32) * scale + bias2
            s_cx = lax.dot_general(q2, ck_ref[b, :, sl].astype(BF16), nt, preferred_element_type=F32) * scale
            p_nb, p_cx = _softmax_rows([s_nb, s_cx])
            o2 = (jnp.dot(p_nb.astype(BF16), vw, preferred_element_type=F32)
                  + jnp.dot(p_cx.astype(BF16), cv_ref[b, :, sl].astype(BF16), preferred_element_type=F32))
            o_ref[b, :, sl] = jnp.where(first, o2[:n_q], o2[n_q:])


def _na_bias_table(rpb, kr):
    col = np.arange(GRID_W)
    col_start = np.clip(col - NA_COLS // 2, 0, GRID_W - NA_COLS)
    delta = col[None, :] - col[:, None] + (NA_COLS - 1)
    inside = (col[None, :] >= col_start[:, None]) & (col[None, :] < col_start[:, None] + NA_COLS)
    pick = (np.arange(2 * NA_COLS - 1)[:, None, None] == delta[None]).astype(np.float32)
    tab = jnp.einsum("hrk,kwc->hrwc", rpb, jnp.asarray(pick), precision=HIGHEST)
    tab = jnp.where(jnp.asarray(inside)[None, None], tab, NEG_BIG)
    out = []
    for d0 in range(NA_ROWS_MAX):
        rows = tab[:, d0:d0 + kr]
        out.append(rows.transpose(0, 2, 1, 3).reshape(NA_HEADS, GRID_W, kr * GRID_W))
    return jnp.stack(out, axis=0)


def _na_latent(q, k, v, ctx_k, ctx_v, rpb, out_buf, dec_batch, dec_seq):
    d = q.shape[1]
    dh = d // NA_HEADS
    rows = dec_seq // GRID_W
    kr = min(NA_ROWS_MAX, rows)
    base = 0
    past = ctx_k.shape[1]
    out_slab = (out_buf.shape[0] - dec_batch) // dec_batch
    bias = _na_bias_table(rpb, kr)

    def row_start(r):
        return jnp.clip(r - kr // 2, 0, rows - kr)

    def win_spec(b, j):
        return pl.BlockSpec((GRID_W, d), lambda r: (base + b * rows + row_start(r) + j, 0))

    in_specs, args = [], []
    for b in range(dec_batch):
        in_specs += ([pl.BlockSpec((GRID_W, d), lambda r, b=b: (base + b * rows + r, 0))]
                     + [win_spec(b, j) for j in range(kr)] * 2)
        args += [q] + [k] * kr + [v] * kr
    in_specs += ([pl.BlockSpec((dec_batch, past, d), lambda r: (0, 0, 0))] * 2
                 + [pl.BlockSpec((None, NA_HEADS, GRID_W, kr * GRID_W),
                                 lambda r: (row_start(r) - r + NA_ROWS_MAX - 1, 0, 0, 0))])
    in_specs.append(pl.BlockSpec(memory_space=pl.ANY))
    args += [ctx_k.reshape(dec_batch, past, d), ctx_v.reshape(dec_batch, past, d), bias, out_buf]
    return pl.pallas_call(
        functools.partial(_na_lat_kernel, dh=dh, scale=dh ** -0.5, kr=kr, n_batch=dec_batch),
        grid=(rows,),
        in_specs=in_specs,
        out_specs=pl.BlockSpec((dec_batch, GRID_W, d), lambda r: (out_slab, r, 0)),
        out_shape=jax.ShapeDtypeStruct(out_buf.shape, F32),
        input_output_aliases={len(args) - 1: 0},
        compiler_params=_params(("arbitrary",), 40),
        name="na_latent",
    )(*args)


def _hgrn_scan_kernel(*refs, seq_len, has_s0, has_buf, want_state):
    q_ref, v_ref, z0_ref, z1_ref, lb_ref, tri_ref = refs[:6]
    rest = list(refs[6:])
    s0_ref = rest.pop(0) if has_s0 else None
    if has_buf:
        rest.pop(0)
    o_ref = rest.pop(0)
    sf_ref = rest.pop(0) if want_state else None
    ob_ref = rest.pop(0)
    c, r = HG_CHUNK, HG_TILE
    nc, nt = r // c, seq_len // r
    dk = HG_KEY_DIM
    tpos8 = lax.broadcasted_iota(jnp.int32, (1, SUBLANES, 1), 1)
    row = lax.broadcasted_iota(jnp.int32, (r, r), 0)
    col = lax.broadcasted_iota(jnp.int32, (r, r), 1)
    same16 = (row // (2 * SUBLANES)) == (col // (2 * SUBLANES))
    same32 = (row // c) == (col // c)
    nt_dims = (((1,), (1,)), ((), ()))
    tn_dims = (((0,), (0,)), ((), ()))

    def tile(dr, hh, i, st):
        z_ref = (z0_ref, z1_ref)[dr]
        cols = slice(hh * dk, (hh + 1) * dk)
        lb = lb_ref[dr:dr + 1, cols]
        tri = tri_ref[dr]
        ti = i if dr == 0 else nt - 1 - i
        r0 = pl.multiple_of(ti * r, r)
        q = q_ref[pl.ds(r0, r), cols]
        v = v_ref[pl.ds(r0, r), cols]
        z = z_ref[pl.ds(r0, r), cols]
        g = jnp.log(lb + (1.0 - lb) * jax.nn.sigmoid(z))
        k = (1.0 - lb) * jax.nn.sigmoid(-z)
        g_hi = _bf16_part(g)
        g_mid = _bf16_part(g - g_hi)
        g_lo = (g - g_hi) - g_mid
        b = (jnp.dot(tri, g_hi.astype(BF16), preferred_element_type=F32)
             + jnp.dot(tri, g_mid.astype(BF16), preferred_element_type=F32)
             + jnp.dot(tri, g_lo.astype(BF16), preferred_element_type=F32))
        b3, q3, k3, v3 = (a.reshape(nc, c, dk) for a in (b, q, k, v))
        nb = r // SUBLANES
        b8, q8, k8, v8 = (a.reshape(nb, SUBLANES, dk) for a in (b, q, k, v))
        oi = jnp.zeros((nb, SUBLANES, dk), F32)
        for s in range(SUBLANES):
            keep = (tpos8 >= s) if dr == 0 else (tpos8 <= s)
            e = jnp.exp(jnp.where(keep, b8 - b8[:, s:s + 1, :], NEG_BIG))
            a = jnp.sum(q8 * e * k8[:, s:s + 1, :], axis=-1, keepdims=True)
            oi = oi + a * v8[:, s:s + 1, :]
        att = None
        for span, same_group in ((2 * SUBLANES, same16), (c, same32)):
            half = span // 2
            bg = b.reshape(r // span, span, dk)
            pos = lax.broadcasted_iota(jnp.int32, (1, span, 1), 1)
            edge = bg[:, half - 1:half, :] if dr == 0 else bg[:, half:half + 1, :]
            is_query = (pos >= half) if dr == 0 else (pos < half)
            qf = q.reshape(bg.shape) * jnp.exp(jnp.where(is_query, bg - edge, NEG_BIG))
            kf = k.reshape(bg.shape) * jnp.exp(jnp.where(is_query, NEG_BIG, edge - bg))
            part = lax.dot_general(qf.reshape(r, dk).astype(BF16), kf.reshape(r, dk).astype(BF16), nt_dims,
                                   preferred_element_type=F32)
            part = jnp.where(same_group, part, 0.0)
            att = part if att is None else att + part
        oi = oi.reshape(r, dk) + jnp.dot(att.astype(BF16), v.astype(BF16), preferred_element_type=F32)
        oi = oi.reshape(nc, c, dk)
        bl = b3[:, c - 1:c, :] if dr == 0 else b3[:, 0:1, :]
        qt = (q3 * jnp.exp(b3)).astype(BF16)
        kt = (k3 * jnp.exp(bl - b3)).astype(BF16)
        vb = v3.astype(BF16)
        outs = [None] * nc
        for ci in (range(nc) if dr == 0 else range(nc - 1, -1, -1)):
            oc = lax.dot_general(qt[ci], st.astype(BF16), nt_dims, preferred_element_type=F32)
            outs[ci] = oi[ci] + oc
            kv = lax.dot_general(vb[ci], kt[ci], tn_dims, preferred_element_type=F32)
            st = st * jnp.exp(bl[ci]) + kv
        return r0, jnp.concatenate(outs, axis=0), st

    chains = [(dr, hh) for hh in range(HG_HEADS_PER_STEP) for dr in range(2)]

    def step(i, carry):
        new = []
        for (dr, hh), st in zip(chains, carry):
            r0, o, st = tile(dr, hh, i, st)
            (o_ref if dr == 0 else ob_ref)[pl.ds(r0, r), hh * dk:(hh + 1) * dk] = o
            new.append(st)
        return tuple(new)

    init = tuple(s0_ref[dr, hh] if has_s0 else jnp.zeros((dk, dk), F32) for dr, hh in chains)
    final = lax.fori_loop(0, nt, step, init)
    o_ref[...] += ob_ref[...]
    if want_state:
        for (dr, hh), st in zip(chains, final):
            sf_ref[dr, hh] = st.T


def _hgrn_tri():
    t = np.arange(HG_TILE)
    same = (t[:, None] // HG_CHUNK) == (t[None, :] // HG_CHUNK)
    fwd = same & (t[None, :] <= t[:, None])
    bwd = same & (t[None, :] >= t[:, None])
    return jnp.asarray(np.stack([fwd, bwd]).astype(np.float32)).astype(BF16)


def _hgrn_scan(q, v, z0, z1, lb, row0, n_seq, seq_len, s0_t=None, want_state=False, out_buf=None):
    t_total, d = q.shape
    dk = HG_KEY_DIM
    blk0 = row0 // seq_len
    hp = HG_HEADS_PER_STEP
    seq_spec = pl.BlockSpec((seq_len, hp * dk), lambda s, h: (blk0 + s, h))
    st_spec = pl.BlockSpec((None, 2, hp, dk, dk), lambda s, h: (s, 0, h, 0, 0))
    in_specs = [seq_spec] * 4 + [pl.BlockSpec((2, hp * dk), lambda s, h: (0, h)),
                                 pl.BlockSpec((2, HG_TILE, HG_TILE), lambda s, h: (0, 0, 0))]
    args = [q, v, z0, z1, lb, _hgrn_tri()]
    if s0_t is not None:
        in_specs.append(st_spec)
        args.append(s0_t)
    aliases = {}
    if out_buf is not None:
        in_specs.append(pl.BlockSpec(memory_space=pl.ANY))
        args.append(out_buf)
        aliases = {len(args) - 1: 0}
    out_specs = [seq_spec]
    out_shape = [jax.ShapeDtypeStruct((t_total, d), F32)]
    if want_state:
        out_specs.append(st_spec)
        out_shape.append(jax.ShapeDtypeStruct((n_seq, 2, HG_HEADS, dk, dk), F32))
    res = pl.pallas_call(
        functools.partial(_hgrn_scan_kernel, seq_len=seq_len, has_s0=s0_t is not None,
                          has_buf=out_buf is not None, want_state=want_state),
        grid=(n_seq, HG_HEADS // hp),
        in_specs=in_specs,
        out_specs=out_specs,
        out_shape=out_shape,
        input_output_aliases=aliases,
        scratch_shapes=[pltpu.VMEM((seq_len, hp * dk), F32)],
        compiler_params=_params(("arbitrary", "arbitrary"), 32),
        name="hgrn_scan",
    )(*args)
    return res if want_state else (res[0], None)


def _route(idx, rank, counts, t):
    n = t * TOP_K
    rb = MOE_ROWS
    n_blocks = -(-n // rb) + N_EXPERTS
    n_steps = (n_blocks + N_EXPERTS + 1) // 2
    n_rows_x = (n_blocks + N_EXPERTS) * rb
    n_rows_y = n_steps * 2 * rb
    experts = jnp.arange(N_EXPERTS, dtype=jnp.int32)
    counts = counts[0, :N_EXPERTS].astype(jnp.int32)
    blocks = (counts + rb - 1) // rb
    block_end = jnp.cumsum(blocks)
    block_start = block_end - blocks
    steps = (blocks + 1) // 2
    step_end = jnp.cumsum(steps)
    step_start = step_end - steps
    onehot = idx[:, :TOP_K, None] == experts
    pos = rank[:, :TOP_K]
    x_row = jnp.sum(jnp.where(onehot, block_start * rb, 0), axis=-1) + pos
    y_row = (jnp.sum(jnp.where(onehot, step_start, 0), axis=-1) + pos // (2 * rb)) * (2 * rb) + pos % (2 * rb)
    q = jnp.arange(rb, dtype=jnp.int32)
    spare = n_blocks * rb + experts[:, None] * rb + q[None, :]
    pad_rows = jnp.where(q[None, :] < (blocks * rb - counts)[:, None],
                         (block_start * rb + counts)[:, None] + q[None, :], spare)
    chunk = jnp.arange(PACKED_CHUNKS, dtype=jnp.int32)[:, None]
    x_list = jnp.concatenate([x_row.T.reshape(-1), pad_rows.reshape(-1)]).astype(jnp.int32)
    x_index = (chunk * n_rows_x + x_list[None, :]).reshape(1, -1)
    y_index = (chunk * n_rows_y + y_row.T.reshape(-1).astype(jnp.int32)[None, :]).reshape(1, -1)
    i = jnp.arange(n_steps, dtype=jnp.int32)[:, None]
    mine = (step_start[None, :] <= i) & (i < step_end[None, :])
    in_use = jnp.any(mine, axis=1)
    e_i = jnp.where(in_use, jnp.sum(jnp.where(mine, experts, 0), axis=1), N_EXPERTS - 1)
    local = i[:, 0] - jnp.sum(jnp.where(mine, step_start, 0), axis=1)
    first = jnp.where(in_use, jnp.sum(jnp.where(mine, block_start, 0), axis=1) + 2 * local, n_blocks)
    two = in_use & (2 * local + 1 < jnp.sum(jnp.where(mine, blocks, 0), axis=1))
    steps_table = jnp.stack([e_i, first, first + two, in_use.astype(jnp.int32)]).astype(jnp.int32)
    return x_index, y_index, steps_table, n_rows_x, n_rows_y


def _sc_mesh():
    return plsc.VectorSubcoreMesh(core_axis_name="c", subcore_axis_name="s")


def _sc_dispatch(h2c, sc_index, t, n_rows):
    win = SC_WINDOW
    tw = t // win
    nw = (t * TOP_K + N_EXPERTS * MOE_ROWS) // win
    n_chunks = h2c.shape[0]

    def scatter(x_hbm, i_hbm, o_hbm):
        def body(x_vmem, i_vmem):
            pltpu.sync_copy(x_vmem, o_hbm.at[i_vmem.at[0]])

        pltpu.emit_pipeline(
            body,
            grid=(n_chunks * nw,),
            in_specs=[pl.BlockSpec((win, CHUNK_W), index_map=lambda i: ((i // nw) * tw + (i % nw) % tw, 0)),
                      pl.BlockSpec((1, win), index_map=lambda i: (0, i))],
            out_specs=[],
            core_axis_name=("c", "s"),
            dimension_semantics=(pltpu.PARALLEL,),
        )(x_hbm, i_hbm)

    out = pl.kernel(scatter, out_type=jax.ShapeDtypeStruct((n_chunks * n_rows, CHUNK_W), h2c.dtype),
                    mesh=_sc_mesh(), name="moe_dispatch")(h2c.reshape(n_chunks * t, CHUNK_W), sc_index)
    return out.reshape(n_chunks, n_rows, CHUNK_W)


def _sc_collect(y_rows, sc_index, t, n_rows):
    win = SC_WINDOW
    n_chunks = y_rows.shape[0]
    nw = sc_index.shape[1] // n_chunks // win
    aw = t * TOP_K // win

    def gather(y_hbm, i_hbm, o_hbm):
        def body(i_vmem, o_vmem):
            pltpu.sync_copy(y_hbm.at[i_vmem.at[0]], o_vmem)

        pltpu.emit_pipeline(
            body,
            grid=(n_chunks * aw,),
            in_specs=[pl.BlockSpec((1, win), index_map=lambda i: (0, (i // aw) * nw + i % aw))],
            out_specs=[pl.BlockSpec((win, CHUNK_W), index_map=lambda i: (i, 0))],
            core_axis_name=("c", "s"),
            dimension_semantics=(pltpu.PARALLEL,),
        )(i_hbm, o_hbm)

    out = pl.kernel(gather, out_type=jax.ShapeDtypeStruct((n_chunks * TOP_K * t, CHUNK_W), y_rows.dtype),
                    mesh=_sc_mesh(), name="moe_collect")(y_rows.reshape(n_chunks * n_rows, CHUNK_W), sc_index)
    return out.reshape(n_chunks, TOP_K, t, CHUNK_W)


def _moe_kernel(st_ref, xa_ref, xb_ref, w1a_ref, w1c_ref, b1_ref, w2a_ref, w2c_ref, b2_ref, o_ref, w1b, w2b, *, dff):
    i = pl.program_id(0)
    rb = xa_ref.shape[1]
    e = st_ref[0, i]
    prev = st_ref[0, jnp.maximum(i - 1, 0)]
    in_use = st_ref[3, i] == 1
    two = st_ref[2, i] != st_ref[1, i]

    @pl.when(jnp.logical_or(i == 0, e != prev))
    def _():
        h1, h2 = w1a_ref.shape[0], w2a_ref.shape[0]
        w1b[0:h1, :] = w1a_ref[...].astype(BF16)
        w1b[h1:, :] = w1c_ref[...].astype(BF16)
        w2b[0:h2, :] = w2a_ref[...].astype(BF16)
        w2b[h2:, :] = w2c_ref[...].astype(BF16)

    def ffn(words):
        halves = [_unpack_bf16_pairs(w) for w in words]
        xb = jnp.concatenate([h[0] for h in halves] + [h[1] for h in halves], axis=1).astype(BF16)
        gu = jnp.dot(xb, w1b[...], preferred_element_type=F32) + b1_ref[...]
        g = jnp.minimum(gu[:, :dff], SWIGLU_LIMIT)
        u = jnp.clip(gu[:, dff:], -SWIGLU_LIMIT, SWIGLU_LIMIT)
        a = (u + 1.0) * (g * jax.nn.sigmoid(SWIGLU_ALPHA * g))
        return _pack_bf16_pairs(jnp.dot(a.astype(BF16), w2b[...], preferred_element_type=F32) + b2_ref[...])

    @pl.when(jnp.logical_and(in_use, two))
    def _():
        out = ffn([jnp.concatenate([xa_ref[j], xb_ref[j]], axis=0) for j in range(PACKED_CHUNKS)])
        for j in range(PACKED_CHUNKS):
            o_ref[j] = out[:, j * CHUNK_W:(j + 1) * CHUNK_W]

    @pl.when(jnp.logical_and(in_use, jnp.logical_not(two)))
    def _():
        out = ffn([xa_ref[j] for j in range(PACKED_CHUNKS)])
        for j in range(PACKED_CHUNKS):
            o_ref[j, 0:rb, :] = out[:, j * CHUNK_W:(j + 1) * CHUNK_W]
        o_ref[:, rb:, :] = jnp.zeros((PACKED_CHUNKS, rb, CHUNK_W), o_ref.dtype)

    @pl.when(jnp.logical_not(in_use))
    def _():
        o_ref[...] = jnp.zeros(o_ref.shape, o_ref.dtype)


def _moe_experts(x_rows, steps, n_rows_y, layer, w1, b1, w2, b2):
    rb = MOE_ROWS
    n_steps = n_rows_y // (2 * rb)
    depth, n_e, d, dff2 = w1.shape
    dff = dff2 // 2

    def weights(shape, half=0):
        return pl.BlockSpec((None, None) + shape, lambda i, st: (layer, st[0, i], half, 0))

    grid_spec = pltpu.PrefetchScalarGridSpec(
        num_scalar_prefetch=1,
        grid=(n_steps,),
        in_specs=[pl.BlockSpec((PACKED_CHUNKS, rb, CHUNK_W), lambda i, st: (0, st[1, i], 0)),
                  pl.BlockSpec((PACKED_CHUNKS, rb, CHUNK_W), lambda i, st: (0, st[2, i], 0)),
                  weights((d // 2, dff2), 0), weights((d // 2, dff2), 1), weights((1, dff2)),
                  weights((dff // 2, d), 0), weights((dff // 2, d), 1), weights((1, d))],
        out_specs=pl.BlockSpec((PACKED_CHUNKS, 2 * rb, CHUNK_W), lambda i, st: (0, i, 0)),
        scratch_shapes=[pltpu.VMEM((d, dff2), BF16), pltpu.VMEM((dff, d), BF16)],
    )
    return pl.pallas_call(
        functools.partial(_moe_kernel, dff=dff),
        grid_spec=grid_spec,
        out_shape=jax.ShapeDtypeStruct((PACKED_CHUNKS, n_rows_y, CHUNK_W), jnp.int32),
        compiler_params=_params(("arbitrary",), 56),
        name="moe_experts",
    )(steps, x_rows, x_rows, w1, w1, b1.reshape(depth, n_e, 1, dff2), w2, w2, b2.reshape(depth, n_e, 1, d))


def _combined_rows(y_ref, gw_ref, x_ref, mod_ref, d):
    gw = gw_ref[...]
    pieces = {}
    for j in range(PACKED_CHUNKS):
        accs = [None, None]
        for k in range(TOP_K):
            for side, part in enumerate(_unpack_bf16_pairs(y_ref[j, k])):
                term = gw[:, k:k + 1] * part
                accs[side] = term if accs[side] is None else accs[side] + term
        for side, acc in enumerate(accs):
            c0 = side * (d // 2) + j * CHUNK_W
            pieces[c0] = x_ref[:, c0:c0 + CHUNK_W] + mod_ref[:, 5 * d + c0:5 * d + c0 + CHUNK_W] * acc
    return jnp.concatenate([pieces[c0] for c0 in sorted(pieces)], axis=1)


def _combine_final_kernel(y_ref, gw_ref, x_ref, mod_ref, g_ref, o_ref, *, d):
    o_ref[...] = _rms(_combined_rows(y_ref, gw_ref, x_ref, mod_ref, d)) * g_ref[...]


def _combine_in_kernel(y_ref, gw_ref, x_ref, mod_ref, modn_ref, g_ref, w_ref, *rest, d, conv, has_buf):
    x2_ref, *out_refs = rest[1:] if has_buf else rest
    x2 = _combined_rows(y_ref, gw_ref, x_ref, mod_ref, d)
    x2_ref[...] = x2
    _project(x2, modn_ref[...], g_ref, w_ref, out_refs, d, conv)


def _combine_specs(y_slots, gw, x1, mods_l, geom, rows):
    d = x1.shape[1]
    row0, t = rows if rows is not None else (0, x1.shape[0])
    tm = TOKEN_TILE
    t0 = row0 // tm
    grp = functools.partial(_group_of_tile, tile=tm, n_prompt=geom[0], dec_seq=geom[1])
    in_specs = [pl.BlockSpec((PACKED_CHUNKS, TOP_K, tm, CHUNK_W), lambda i: (0, 0, t0 + i, 0)),
                pl.BlockSpec((tm, LANES), lambda i: (t0 + i, 0)),
                pl.BlockSpec((tm, d), lambda i: (t0 + i, 0)),
                pl.BlockSpec((None, 1, 6 * d), lambda i: (grp(t0 + i), 0, 0))]
    return in_specs, [y_slots, gw, x1, mods_l], t, t0, grp


def _combine_final(y_slots, gw, x1, mods_l, geom, final_g, rows):
    d = x1.shape[1]
    tm = TOKEN_TILE
    in_specs, args, t, _, _ = _combine_specs(y_slots, gw, x1, mods_l, geom, rows)
    return pl.pallas_call(
        functools.partial(_combine_final_kernel, d=d),
        grid=(t // tm,),
        in_specs=in_specs + [pl.BlockSpec((1, d), lambda i: (0, 0))],
        out_specs=pl.BlockSpec((tm, d), lambda i: (i, 0)),
        out_shape=jax.ShapeDtypeStruct((t, d), F32),
        compiler_params=_params(("arbitrary",), 24),
        name="moe_combine_final",
    )(*args, final_g)


def _combine_mixer_in(y_slots, gw, x1, mods_l, mods_next, g, w, geom, conv=False, rows=None, x2_buf=None):
    t_total, d = x1.shape
    tm = TOKEN_TILE
    n = w.shape[1]
    n_out = 2 if conv else n // d
    in_specs, args, t, t0, grp = _combine_specs(y_slots, gw, x1, mods_l, geom, rows)
    in_specs += [pl.BlockSpec((None, 1, 6 * d), lambda i: (grp(t0 + i), 0, 0)),
                 pl.BlockSpec((1, d), lambda i: (0, 0)),
                 pl.BlockSpec((d, n), lambda i: (0, 0))]
    args += [mods_next, g, w]
    aliases = {}
    if x2_buf is not None:
        in_specs.append(pl.BlockSpec(memory_space=pl.ANY))
        args.append(x2_buf)
        aliases = {len(args) - 1: 0}
    res = pl.pallas_call(
        functools.partial(_combine_in_kernel, d=d, conv=conv, has_buf=x2_buf is not None),
        grid=(t // tm,),
        in_specs=in_specs,
        out_specs=[pl.BlockSpec((tm, d), lambda i: (t0 + i, 0))] + [pl.BlockSpec((tm, d), lambda i: (i, 0))] * n_out,
        out_shape=[jax.ShapeDtypeStruct((t_total, d), F32)] + [jax.ShapeDtypeStruct((t, d), F32)] * n_out,
        input_output_aliases=aliases,
        compiler_params=_params(("arbitrary",), 56),
        name="combine_mixer_in",
    )(*args)
    return res[0], res[1:]


def kernel(x_prompt, x_sample, c, c_ctx, cache_na_k, cache_na_v, state_hgrn, norm_mix, norm_ffn, w_mod, b_mod,
           conv_w_in, conv_k, conv_w_out, na_w_qkv, na_rpb, na_w_o, hg_w_qig, hg_w_f, hg_lb, hg_norm, hg_w_o,
           moe_w_router, moe_b_router, moe_w1, moe_b1, moe_w2, moe_b2, final_norm):
    batch, seq, d = x_prompt.shape
    dec_batch, dec_seq, _ = x_sample.shape
    depth = w_mod.shape[0]
    n_prompt = batch * seq
    t = n_prompt + dec_batch * dec_seq
    geom = (n_prompt, dec_seq, seq)
    assert seq % TOKEN_TILE == 0 and dec_seq % TOKEN_TILE == 0 and seq % HG_TILE == 0
    assert n_prompt % (dec_seq * dec_batch) == 0 and dec_seq % GRID_W == 0 and 1 + dec_batch <= SUBLANES
    assert d == 2 * PACKED_CHUNKS * CHUNK_W and t % SC_WINDOW == 0 and MOE_ROWS % SC_WINDOW == 0

    x = jnp.concatenate([x_prompt.reshape(n_prompt, d), x_sample.reshape(dec_batch * dec_seq, d)], axis=0)
    cvecs = jnp.zeros((SUBLANES, d), F32).at[0].set(c_ctx).at[1:1 + dec_batch].set(c)
    mods = _adaln_all(cvecs, w_mod, b_mod)[:, :1 + dec_batch].reshape(depth, 1 + dec_batch, 1, 6 * d)

    lb_soft = jax.nn.softmax(hg_lb.astype(F32), axis=0)
    lower_bounds = jnp.cumsum(lb_soft, axis=0) - lb_soft[0]

    new_k, new_v, new_s = [], [], []
    pending = None
    for l in range(depth):
        kind, j = l % N_MIXERS, l // N_MIXERS
        g1 = norm_mix[l].reshape(1, d)
        g2 = norm_ffn[l].reshape(1, d)

        def project(x, w, conv=False, rows=None, x2_buf=None):
            if pending is None:
                return x, _mixer_in(x, mods[l], g1, w, geom, conv=conv, rows=rows)
            return _combine_mixer_in(*pending, mods[l], g1, w, geom, conv=conv, rows=rows, x2_buf=x2_buf)

        if kind == 0:
            x, (b, cu) = project(x, conv_w_in[j].astype(BF16), conv=True)
        elif kind == 1:
            w_qkv = na_w_qkv[j].astype(BF16)
            x, (qp, kp, vp) = project(x, w_qkv, rows=(0, n_prompt))
            x, (qs, ks, vs) = project(x, w_qkv, rows=(n_prompt, t - n_prompt), x2_buf=x)
        else:
            w_in = jnp.concatenate([hg_w_qig[j], hg_w_f[j, 0], hg_w_f[j, 1]], axis=1).astype(BF16)
            x, (q, v, gout, z0, z1) = project(x, w_in)
        tail_args = (x, mods[l], g2, moe_w_router[l], moe_b_router[l], geom)
        if kind == 0:
            x1, h2c, idx, gw, rank, counts = _mixer_out("conv", (b, cu), conv_w_out[j].astype(BF16), *tail_args,
                                         extra=(conv_k[j],))
        elif kind == 1:
            o = _na_context(qp, kp, vp, batch, seq, t, dec_seq)
            o = _na_latent(qs, ks, vs, cache_na_k[:, j], cache_na_v[:, j], na_rpb[j], o, dec_batch, dec_seq)
            new_k.append(kp.reshape(batch, seq, NA_HEADS, d // NA_HEADS))
            new_v.append(vp.reshape(batch, seq, NA_HEADS, d // NA_HEADS))
            x1, h2c, idx, gw, rank, counts = _mixer_out("plain", (o.reshape(t, d),),
                                         na_w_o[j].astype(BF16), *tail_args)
        else:
            lb = lower_bounds[l]
            o, s_p = _hgrn_scan(q, v, z0, z1, lb, 0, batch, seq, want_state=True)
            s0_t = jnp.swapaxes(state_hgrn[:, j].astype(F32), -1, -2)
            o, _ = _hgrn_scan(q, v, z0, z1, lb, n_prompt, dec_batch, dec_seq, s0_t=s0_t, out_buf=o)
            new_s.append(s_p)
            x1, h2c, idx, gw, rank, counts = _mixer_out("hgrn", (o, gout),
                                         hg_w_o[j].astype(BF16), *tail_args, extra=(hg_norm[j].reshape(1, d),))
        x_index, y_index, steps, n_rows_x, n_rows_y = _route(idx, rank, counts, t)
        x_rows = _sc_dispatch(h2c, x_index, t, n_rows_x)
        y_rows = _moe_experts(x_rows, steps, n_rows_y, l, moe_w1, moe_b1, moe_w2, moe_b2)
        y_slots = _sc_collect(y_rows, y_index, t, n_rows_y)
        pending = (y_slots, gw, x1, mods[l])

    final_g = final_norm.reshape(1, d)
    y_prompt = _combine_final(*pending, geom, final_g, (0, n_prompt)).reshape(batch, seq, d)
    y_sample = _combine_final(*pending, geom, final_g, (n_prompt, t - n_prompt)).reshape(dec_batch, dec_seq, d)
    new_na_k = jnp.stack(new_k, axis=1).astype(x_prompt.dtype)
    new_na_v = jnp.stack(new_v, axis=1).astype(x_prompt.dtype)
    new_hgrn_state = jnp.stack(new_s, axis=1).astype(x_prompt.dtype)
    return (y_prompt, y_sample, new_na_k, new_na_v, new_hgrn_state)
```

```python
import functools

import numpy as np
import jax
import jax.numpy as jnp
from jax import lax
from jax.experimental import pallas as pl
from jax.experimental.pallas import tpu as pltpu
from jax.experimental.pallas import tpu_sc as plsc

F32 = jnp.float32
BF16 = jnp.bfloat16
HIGHEST = lax.Precision.HIGHEST

N_MIXERS = 3
GRID_W = 64
CONV_WIDTH = 3
NA_HEADS = 16
NA_ROWS_MAX = 8
NA_COLS = 16
HG_HEADS = 8
HG_KEY_DIM = 128
HG_CHUNK = 32
N_EXPERTS = 32
TOP_K = 4
SWIGLU_LIMIT = 7.0
SWIGLU_ALPHA = 1.702
EPS = 1e-6

LANES = 128
SUBLANES = 8
VMEM_LIMIT_CAP = 60 * 1024 * 1024

NEG_BIG = -1e30
TOKEN_TILE = 256
MOE_ROWS = 512
HG_TILE = 128
HG_HEADS_PER_STEP = 2
PACKED_CHUNKS = 2
CHUNK_W = 256
SC_WINDOW = 128


def _params(sem, vmem_mb, flags=None):
    return pltpu.CompilerParams(dimension_semantics=sem, flags=flags,
                                vmem_limit_bytes=min(vmem_mb * 1024 * 1024, VMEM_LIMIT_CAP))


def _rms(x):
    return x * lax.rsqrt(jnp.mean(x * x, axis=-1, keepdims=True) + EPS)


def _modulate(x, g, shift, scale):
    return (_rms(x) * g) * (1.0 + scale) + shift


def _bf16_part(x):
    bits = lax.bitcast_convert_type(x, jnp.uint32) & jnp.uint32(0xFFFF0000)
    return lax.bitcast_convert_type(bits, F32)


def _pack_bf16_pairs(x):
    n = x.shape[1]
    bits = lax.bitcast_convert_type(x, jnp.uint32)
    top = (bits + (jnp.uint32(0x7FFF) + ((bits >> 16) & jnp.uint32(1)))) & jnp.uint32(0xFFFF0000)
    return lax.bitcast_convert_type(top[:, :n // 2] | (top[:, n // 2:] >> 16), jnp.int32)


def _unpack_bf16_pairs(words):
    bits = lax.bitcast_convert_type(words, jnp.uint32)
    return (lax.bitcast_convert_type(bits & jnp.uint32(0xFFFF0000), F32),
            lax.bitcast_convert_type(bits << 16, F32))


def _group_of_tile(i, tile, n_prompt, dec_seq):
    start = i * tile
    return jnp.where(start < n_prompt, 0, 1 + (start - n_prompt) // dec_seq)


def _adaln_kernel(c_ref, w_ref, b_ref, o_ref):
    cv = c_ref[...]
    s = cv * jax.nn.sigmoid(cv)
    o_ref[...] = jnp.dot(s, w_ref[...], precision=HIGHEST, preferred_element_type=F32) + b_ref[...]


def _adaln_all(cvecs, w_mod, b_mod):
    depth, d, n = w_mod.shape
    tn = 2048
    return pl.pallas_call(
        _adaln_kernel,
        grid=(depth, n // tn),
        in_specs=[pl.BlockSpec((SUBLANES, d), lambda l, j: (0, 0)),
                  pl.BlockSpec((None, d, tn), lambda l, j: (l, 0, j)),
                  pl.BlockSpec((None, 1, tn), lambda l, j: (l, 0, j))],
        out_specs=pl.BlockSpec((None, SUBLANES, tn), lambda l, j: (l, 0, j)),
        out_shape=jax.ShapeDtypeStruct((depth, SUBLANES, n), F32),
        compiler_params=_params(("arbitrary", "arbitrary"), 40),
        name="adaln",
    )(cvecs, w_mod, b_mod.reshape(depth, 1, n))


def _project(x, mod, g_ref, w_ref, out_refs, d, conv):
    h = _modulate(x, g_ref[...], mod[:, 0:d], mod[:, d:2 * d]).astype(BF16)
    if conv:
        b_ref, cu_ref = out_refs
        b_ref[...] = jnp.dot(h, w_ref[:, 0:d], preferred_element_type=F32)
        c = jnp.dot(h, w_ref[:, d:2 * d], preferred_element_type=F32)
        u = jnp.dot(h, w_ref[:, 2 * d:3 * d], preferred_element_type=F32)
        cu_ref[...] = c * u
    else:
        for j, o_ref in enumerate(out_refs):
            o_ref[...] = jnp.dot(h, w_ref[:, j * d:(j + 1) * d], preferred_element_type=F32)


def _in_kernel(x_ref, mod_ref, g_ref, w_ref, *out_refs, d, conv):
    _project(x_ref[...], mod_ref[...], g_ref, w_ref, out_refs, d, conv)


def _mixer_in(x, mods_l, g, w, geom, conv=False, rows=None):
    d = x.shape[1]
    row0, t = rows if rows is not None else (0, x.shape[0])
    n = w.shape[1]
    n_out = 2 if conv else n // d
    tm = TOKEN_TILE
    t0 = row0 // tm
    grp = functools.partial(_group_of_tile, tile=tm, n_prompt=geom[0], dec_seq=geom[1])
    return pl.pallas_call(
        functools.partial(_in_kernel, d=d, conv=conv),
        grid=(t // tm,),
        in_specs=[pl.BlockSpec((tm, d), lambda i: (t0 + i, 0)),
                  pl.BlockSpec((None, 1, 6 * d), lambda i: (grp(t0 + i), 0, 0)),
                  pl.BlockSpec((1, d), lambda i: (0, 0)),
                  pl.BlockSpec((d, n), lambda i: (0, 0))],
        out_specs=[pl.BlockSpec((tm, d), lambda i: (i, 0))] * n_out,
        out_shape=[jax.ShapeDtypeStruct((t, d), F32)] * n_out,
        compiler_params=_params(("arbitrary",), 48),
        name="mixer_in",
    )(x, mods_l, g, w)


def _tail(m, x_ref, mod, g2_ref, wr_ref, br_ref, outs, run_ref, d):
    x1_ref, h2_ref, idx_ref, gw_ref, rank_ref, cnt_ref = outs
    x1 = x_ref[...] + mod[:, 2 * d:3 * d] * m
    x1_ref[...] = x1
    h2 = _modulate(x1, g2_ref[...], mod[:, 3 * d:4 * d], mod[:, 4 * d:5 * d])
    words = _pack_bf16_pairs(h2)
    for j in range(PACKED_CHUNKS):
        h2_ref[j] = words[:, j * CHUNK_W:(j + 1) * CHUNK_W]
    h2_top = _bf16_part(h2)
    h2_hi = h2_top.astype(BF16)
    h2_lo = (h2 - h2_top).astype(BF16)
    logits = (jnp.dot(h2_hi, wr_ref[0], preferred_element_type=F32)
              + jnp.dot(h2_lo, wr_ref[0], preferred_element_type=F32)
              + jnp.dot(h2_hi, wr_ref[1], preferred_element_type=F32)) + br_ref[...]
    lane = lax.broadcasted_iota(jnp.int32, logits.shape, 1)
    vals, idxs = [], []
    for _ in range(TOP_K):
        mx = jnp.max(logits, axis=-1, keepdims=True)
        ix = jnp.argmax(logits, axis=-1, keepdims=True).astype(jnp.int32)
        vals.append(mx)
        idxs.append(ix)
        logits = jnp.where(lane == ix, NEG_BIG, logits)
    es = [jnp.exp(v - vals[0]) for v in vals]
    den = es[0] + es[1] + es[2] + es[3]

    @pl.when(pl.program_id(0) == 0)
    def _():
        run_ref[...] = jnp.zeros(run_ref.shape, F32)

    tm = lane.shape[0]
    earlier = (lax.broadcasted_iota(jnp.int32, (tm, tm), 0)
               > lax.broadcasted_iota(jnp.int32, (tm, tm), 1)).astype(BF16)
    seen = run_ref[...]
    idx_out = jnp.zeros(lane.shape, jnp.int32)
    gw_out = jnp.zeros(lane.shape, F32)
    rank_out = jnp.zeros(lane.shape, jnp.int32)
    for k in range(TOP_K):
        hit = lane == idxs[k]
        before = jnp.dot(earlier, hit.astype(BF16), preferred_element_type=F32) + seen
        rank_k = jnp.sum(jnp.where(hit, before, 0.0), axis=-1, keepdims=True)
        seen = seen + jnp.sum(hit.astype(F32), axis=0, keepdims=True)
        idx_out = jnp.where(lane == k, idxs[k], idx_out)
        gw_out = jnp.where(lane == k, es[k] / den, gw_out)
        rank_out = jnp.where(lane == k, rank_k.astype(jnp.int32), rank_out)
    run_ref[...] = seen
    cnt_ref[...] = seen
    idx_ref[...] = idx_out
    gw_ref[...] = gw_out
    rank_ref[...] = rank_out


def _plain_out_kernel(a_ref, w_ref, x_ref, mod_ref, g2_ref, wr_ref, br_ref, *rest, d):
    m = jnp.dot(a_ref[...].astype(BF16), w_ref[...], preferred_element_type=F32)
    _tail(m, x_ref, mod_ref[...], g2_ref, wr_ref, br_ref, rest[:-1], rest[-1], d)


def _conv_out_kernel(b_ref, cu_ref, prev_ref, next_ref, ck_ref, w_ref, x_ref, mod_ref, g2_ref, wr_ref,
                     br_ref, *rest, d, tm, n_prompt, seq, dec_seq):
    i = pl.program_id(0)
    cu = cu_ref[...]
    row = lax.broadcasted_iota(jnp.int32, (tm, 1), 0)
    start = i * tm
    in_prompt = start < n_prompt
    seq_len = jnp.where(in_prompt, seq, dec_seq)
    pos = lax.rem(jnp.where(in_prompt, start, start - n_prompt), seq_len) + row
    prev = jnp.where(row == 0, prev_ref[SUBLANES - 1:SUBLANES, :], pltpu.roll(cu, 1, 0))
    prev = jnp.where(pos == 0, 0.0, prev)
    nxt = jnp.where(row == tm - 1, next_ref[0:1, :], pltpu.roll(cu, tm - 1, 0))
    nxt = jnp.where(pos == seq_len - 1, 0.0, nxt)
    conv = prev * ck_ref[0:1, :] + cu * ck_ref[1:2, :] + nxt * ck_ref[2:3, :]
    m = jnp.dot((b_ref[...] * conv).astype(BF16), w_ref[...], preferred_element_type=F32)
    _tail(m, x_ref, mod_ref[...], g2_ref, wr_ref, br_ref, rest[:-1], rest[-1], d)


def _hgrn_out_kernel(o_ref, gout_ref, ng_ref, w_ref, x_ref, mod_ref, g2_ref, wr_ref, br_ref, *rest, d):
    parts = []
    for h in range(HG_HEADS):
        oh = o_ref[:, h * LANES:(h + 1) * LANES]
        parts.append(oh * lax.rsqrt(jnp.mean(oh * oh, axis=-1, keepdims=True) + EPS))
    gout = gout_ref[...]
    y = (jnp.concatenate(parts, axis=1) * ng_ref[...]) * (gout * jax.nn.sigmoid(gout))
    m = jnp.dot(y.astype(BF16), w_ref[...], preferred_element_type=F32)
    _tail(m, x_ref, mod_ref[...], g2_ref, wr_ref, br_ref, rest[:-1], rest[-1], d)


def _mixer_out(kind, acts, w_out, x, mods_l, g2, w_router, b_router, geom, extra=()):
    t, d = x.shape
    tm = TOKEN_TILE
    n_prompt, dec_seq, seq = geom
    grp = functools.partial(_group_of_tile, tile=tm, n_prompt=n_prompt, dec_seq=dec_seq)
    row_spec = pl.BlockSpec((tm, d), lambda i: (i, 0))
    wr = jnp.zeros((d, LANES), F32).at[:, :N_EXPERTS].set(w_router)
    wr_top = _bf16_part(wr)
    wr = jnp.stack([wr_top.astype(BF16), (wr - wr_top).astype(BF16)])
    br = jnp.full((1, LANES), NEG_BIG, F32).at[0, :N_EXPERTS].set(b_router)
    common_specs = [pl.BlockSpec((d, d), lambda i: (0, 0)),
                    row_spec,
                    pl.BlockSpec((None, 1, 6 * d), lambda i: (grp(i), 0, 0)),
                    pl.BlockSpec((1, d), lambda i: (0, 0)),
                    pl.BlockSpec((2, d, LANES), lambda i: (0, 0, 0)),
                    pl.BlockSpec((1, LANES), lambda i: (0, 0))]
    common_args = [w_out, x, mods_l, g2, wr, br]
    if kind == "plain":
        body = functools.partial(_plain_out_kernel, d=d)
        specs = [row_spec] + common_specs
        args = list(acts) + common_args
    elif kind == "conv":
        body = functools.partial(_conv_out_kernel, d=d, tm=tm, n_prompt=n_prompt, seq=seq, dec_seq=dec_seq)
        per = tm // SUBLANES
        last = t // SUBLANES - 1
        specs = [row_spec, row_spec,
                 pl.BlockSpec((SUBLANES, d), lambda i: (jnp.maximum(i * per - 1, 0), 0)),
                 pl.BlockSpec((SUBLANES, d), lambda i: (jnp.minimum((i + 1) * per, last), 0)),
                 pl.BlockSpec((CONV_WIDTH, d), lambda i: (0, 0))] + common_specs
        b, cu = acts
        args = [b, cu, cu, cu, extra[0]] + common_args
    else:
        body = functools.partial(_hgrn_out_kernel, d=d)
        specs = [row_spec, row_spec, pl.BlockSpec((1, d), lambda i: (0, 0))] + common_specs
        args = list(acts) + [extra[0]] + common_args
    lane_spec = pl.BlockSpec((tm, LANES), lambda i: (i, 0))
    return pl.pallas_call(
        body,
        grid=(t // tm,),
        in_specs=specs,
        out_specs=[row_spec, pl.BlockSpec((PACKED_CHUNKS, tm, CHUNK_W), lambda i: (0, i, 0)),
                   lane_spec, lane_spec, lane_spec, pl.BlockSpec((1, LANES), lambda i: (0, 0))],
        out_shape=[jax.ShapeDtypeStruct((t, d), F32), jax.ShapeDtypeStruct((PACKED_CHUNKS, t, CHUNK_W), jnp.int32),
                   jax.ShapeDtypeStruct((t, LANES), jnp.int32), jax.ShapeDtypeStruct((t, LANES), F32),
                   jax.ShapeDtypeStruct((t, LANES), jnp.int32), jax.ShapeDtypeStruct((1, LANES), F32)],
        scratch_shapes=[pltpu.VMEM((1, LANES), F32)],
        compiler_params=_params(("arbitrary",), 40),
        name="mixer_out_" + kind,
    )(*args)


def _softmax_rows(parts):
    mx = functools.reduce(jnp.maximum, [jnp.max(s, axis=-1, keepdims=True) for s in parts])
    es = [jnp.exp(s - mx) for s in parts]
    den = functools.reduce(lambda a, b: a + b, [jnp.sum(e, axis=-1, keepdims=True) for e in es])
    return [e / den for e in es]


def _head_pair_queries(q, dh):
    first = lax.broadcasted_iota(jnp.int32, (1, q.shape[1]), 1) < dh
    return first, (jnp.where(first, q, 0.0).astype(BF16), jnp.where(first, 0.0, q).astype(BF16))


def _na_ctx_kernel(q_ref, k_ref, v_ref, o_ref, *, dh, scale):
    nt = (((1,), (1,)), ((), ()))
    for pair in range(NA_HEADS * dh // LANES):
        sl = slice(pair * LANES, (pair + 1) * LANES)
        first, queries = _head_pair_queries(q_ref[:, sl], dh)
        n_q = q_ref.shape[0]
        q2 = jnp.concatenate(queries, axis=0)
        (p,) = _softmax_rows([lax.dot_general(q2, k_ref[:, sl].astype(BF16), nt,
                                              preferred_element_type=F32) * scale])
        o2 = jnp.dot(p.astype(BF16), v_ref[:, sl].astype(BF16), preferred_element_type=F32)
        o_ref[:, sl] = jnp.where(first, o2[:n_q], o2[n_q:])


def _na_context(q, k, v, batch, seq, t_total, dec_seq):
    d = q.shape[1]
    dh = d // NA_HEADS
    per = dec_seq // seq
    spec = pl.BlockSpec((seq, d), lambda b: (b, 0))
    return pl.pallas_call(
        functools.partial(_na_ctx_kernel, dh=dh, scale=dh ** -0.5),
        grid=(batch,),
        in_specs=[spec, spec, spec],
        out_specs=pl.BlockSpec((None, seq, d), lambda b: (b // per, b % per, 0)),
        out_shape=jax.ShapeDtypeStruct((t_total // dec_seq, dec_seq, d), F32),
        compiler_params=_params(("arbitrary",), 32),
        name="na_context",
    )(q, k, v)


def _na_lat_kernel(*refs, dh, scale, kr, n_batch):
    per = 1 + 2 * kr
    ck_ref, cv_ref, bias_ref, _, o_ref = refs[n_batch * per:]
    nt = (((1,), (1,)), ((), ()))
    heads_per_group = LANES // dh
    for pair in range(NA_HEADS // heads_per_group):
        sl = slice(pair * LANES, (pair + 1) * LANES)
        for b in range(n_batch):
            q_ref = refs[b * per]
            k_refs = refs[b * per + 1:b * per + 1 + kr]
            v_refs = refs[b * per + 1 + kr:(b + 1) * per]
            first, queries = _head_pair_queries(q_ref[:, sl], dh)
            n_q = q_ref.shape[0]
            q2 = jnp.concatenate(queries, axis=0)
            bias2 = jnp.concatenate([bias_ref[pair * heads_per_group + j] for j in range(heads_per_group)], axis=0)
            kw = jnp.concatenate([r[:, sl] for r in k_refs], axis=0).astype(BF16)
            vw = jnp.concatenate([r[:, sl] for r in v_refs], axis=0).astype(BF16)
            s_nb = lax.dot_general(q2, kw, nt, preferred_element_type=F32) * scale + bias2
            s_cx = lax.dot_general(q2, ck_ref[b, :, sl].astype(BF16), nt, preferred_element_type=F32) * scale
            p_nb, p_cx = _softmax_rows([s_nb, s_cx])
            o2 = (jnp.dot(p_nb.astype(BF16), vw, preferred_element_type=F32)
                  + jnp.dot(p_cx.astype(BF16), cv_ref[b, :, sl].astype(BF16), preferred_element_type=F32))
            o_ref[b, :, sl] = jnp.where(first, o2[:n_q], o2[n_q:])


def _na_bias_table(rpb, kr):
    col = np.arange(GRID_W)
    col_start = np.clip(col - NA_COLS // 2, 0, GRID_W - NA_COLS)
    delta = col[None, :] - col[:, None] + (NA_COLS - 1)
    inside = (col[None, :] >= col_start[:, None]) & (col[None, :] < col_start[:, None] + NA_COLS)
    pick = (np.arange(2 * NA_COLS - 1)[:, None, None] == delta[None]).astype(np.float32)
    tab = jnp.einsum("hrk,kwc->hrwc", rpb, jnp.asarray(pick), precision=HIGHEST)
    tab = jnp.where(jnp.asarray(inside)[None, None], tab, NEG_BIG)
    out = []
    for d0 in range(NA_ROWS_MAX):
        rows = tab[:, d0:d0 + kr]
        out.append(rows.transpose(0, 2, 1, 3).reshape(NA_HEADS, GRID_W, kr * GRID_W))
    return jnp.stack(out, axis=0)


def _na_latent(q, k, v, ctx_k, ctx_v, rpb, out_buf, dec_batch, dec_seq):
    d = q.shape[1]
    dh = d // NA_HEADS
    rows = dec_seq // GRID_W
    kr = min(NA_ROWS_MAX, rows)
    base = 0
    past = ctx_k.shape[1]
    out_slab = (out_buf.shape[0] - dec_batch) // dec_batch
    bias = _na_bias_table(rpb, kr)

    def row_start(r):
        return jnp.clip(r - kr // 2, 0, rows - kr)

    def win_spec(b, j):
        return pl.BlockSpec((GRID_W, d), lambda r: (base + b * rows + row_start(r) + j, 0))

    in_specs, args = [], []
    for b in range(dec_batch):
        in_specs += ([pl.BlockSpec((GRID_W, d), lambda r, b=b: (base + b * rows + r, 0))]
                     + [win_spec(b, j) for j in range(kr)] * 2)
        args += [q] + [k] * kr + [v] * kr
    in_specs += ([pl.BlockSpec((dec_batch, past, d), lambda r: (0, 0, 0))] * 2
                 + [pl.BlockSpec((None, NA_HEADS, GRID_W, kr * GRID_W),
                                 lambda r: (row_start(r) - r + NA_ROWS_MAX - 1, 0, 0, 0))])
    in_specs.append(pl.BlockSpec(memory_space=pl.ANY))
    args += [ctx_k.reshape(dec_batch, past, d), ctx_v.reshape(dec_batch, past, d), bias, out_buf]
    return pl.pallas_call(
        functools.partial(_na_lat_kernel, dh=dh, scale=dh ** -0.5, kr=kr, n_batch=dec_batch),
        grid=(rows,),
        in_specs=in_specs,
        out_specs=pl.BlockSpec((dec_batch, GRID_W, d), lambda r: (out_slab, r, 0)),
        out_shape=jax.ShapeDtypeStruct(out_buf.shape, F32),
        input_output_aliases={len(args) - 1: 0},
        compiler_params=_params(("arbitrary",), 40),
        name="na_latent",
    )(*args)


def _hgrn_scan_kernel(*refs, seq_len, has_s0, has_buf, want_state):
    q_ref, v_ref, z0_ref, z1_ref, lb_ref, tri_ref = refs[:6]
    rest = list(refs[6:])
    s0_ref = rest.pop(0) if has_s0 else None
    if has_buf:
        rest.pop(0)
    o_ref = rest.pop(0)
    sf_ref = rest.pop(0) if want_state else None
    ob_ref = rest.pop(0)
    c, r = HG_CHUNK, HG_TILE
    nc, nt = r // c, seq_len // r
    dk = HG_KEY_DIM
    tpos8 = lax.broadcasted_iota(jnp.int32, (1, SUBLANES, 1), 1)
    row = lax.broadcasted_iota(jnp.int32, (r, r), 0)
    col = lax.broadcasted_iota(jnp.int32, (r, r), 1)
    same16 = (row // (2 * SUBLANES)) == (col // (2 * SUBLANES))
    same32 = (row // c) == (col // c)
    nt_dims = (((1,), (1,)), ((), ()))
    tn_dims = (((0,), (0,)), ((), ()))

    def tile(dr, hh, i, st):
        z_ref = (z0_ref, z1_ref)[dr]
        cols = slice(hh * dk, (hh + 1) * dk)
        lb = lb_ref[dr:dr + 1, cols]
        tri = tri_ref[dr]
        ti = i if dr == 0 else nt - 1 - i
        r0 = pl.multiple_of(ti * r, r)
        q = q_ref[pl.ds(r0, r), cols]
        v = v_ref[pl.ds(r0, r), cols]
        z = z_ref[pl.ds(r0, r), cols]
        g = jnp.log(lb + (1.0 - lb) * jax.nn.sigmoid(z))
        k = (1.0 - lb) * jax.nn.sigmoid(-z)
        g_hi = _bf16_part(g)
        g_mid = _bf16_part(g - g_hi)
        g_lo = (g - g_hi) - g_mid
        b = (jnp.dot(tri, g_hi.astype(BF16), preferred_element_type=F32)
             + jnp.dot(tri, g_mid.astype(BF16), preferred_element_type=F32)
             + jnp.dot(tri, g_lo.astype(BF16), preferred_element_type=F32))
        b3, q3, k3, v3 = (a.reshape(nc, c, dk) for a in (b, q, k, v))
        nb = r // SUBLANES
        b8, q8, k8, v8 = (a.reshape(nb, SUBLANES, dk) for a in (b, q, k, v))
        oi = jnp.zeros((nb, SUBLANES, dk), F32)
        for s in range(SUBLANES):
            keep = (tpos8 >= s) if dr == 0 else (tpos8 <= s)
            e = jnp.exp(jnp.where(keep, b8 - b8[:, s:s + 1, :], NEG_BIG))
            a = jnp.sum(q8 * e * k8[:, s:s + 1, :], axis=-1, keepdims=True)
            oi = oi + a * v8[:, s:s + 1, :]
        att = None
        for span, same_group in ((2 * SUBLANES, same16), (c, same32)):
            half = span // 2
            bg = b.reshape(r // span, span, dk)
            pos = lax.broadcasted_iota(jnp.int32, (1, span, 1), 1)
            edge = bg[:, half - 1:half, :] if dr == 0 else bg[:, half:half + 1, :]
            is_query = (pos >= half) if dr == 0 else (pos < half)
            qf = q.reshape(bg.shape) * jnp.exp(jnp.where(is_query, bg - edge, NEG_BIG))
            kf = k.reshape(bg.shape) * jnp.exp(jnp.where(is_query, NEG_BIG, edge - bg))
            part = lax.dot_general(qf.reshape(r, dk).astype(BF16), kf.reshape(r, dk).astype(BF16), nt_dims,
                                   preferred_element_type=F32)
            part = jnp.where(same_group, part, 0.0)
            att = part if att is None else att + part
        oi = oi.reshape(r, dk) + jnp.dot(att.astype(BF16), v.astype(BF16), preferred_element_type=F32)
        oi = oi.reshape(nc, c, dk)
        bl = b3[:, c - 1:c, :] if dr == 0 else b3[:, 0:1, :]
        qt = (q3 * jnp.exp(b3)).astype(BF16)
        kt = (k3 * jnp.exp(bl - b3)).astype(BF16)
        vb = v3.astype(BF16)
        outs = [None] * nc
        for ci in (range(nc) if dr == 0 else range(nc - 1, -1, -1)):
            oc = lax.dot_general(qt[ci], st.astype(BF16), nt_dims, preferred_element_type=F32)
            outs[ci] = oi[ci] + oc
            kv = lax.dot_general(vb[ci], kt[ci], tn_dims, preferred_element_type=F32)
            st = st * jnp.exp(bl[ci]) + kv
        return r0, jnp.concatenate(outs, axis=0), st

    chains = [(dr, hh) for hh in range(HG_HEADS_PER_STEP) for dr in range(2)]

    def step(i, carry):
        new = []
        for (dr, hh), st in zip(chains, carry):
            r0, o, st = tile(dr, hh, i, st)
            (o_ref if dr == 0 else ob_ref)[pl.ds(r0, r), hh * dk:(hh + 1) * dk] = o
            new.append(st)
        return tuple(new)

    init = tuple(s0_ref[dr, hh] if has_s0 else jnp.zeros((dk, dk), F32) for dr, hh in chains)
    final = lax.fori_loop(0, nt, step, init)
    o_ref[...] += ob_ref[...]
    if want_state:
        for (dr, hh), st in zip(chains, final):
            sf_ref[dr, hh] = st.T


def _hgrn_tri():
    t = np.arange(HG_TILE)
    same = (t[:, None] // HG_CHUNK) == (t[None, :] // HG_CHUNK)
    fwd = same & (t[None, :] <= t[:, None])
    bwd = same & (t[None, :] >= t[:, None])
    return jnp.asarray(np.stack([fwd, bwd]).astype(np.float32)).astype(BF16)


def _hgrn_scan(q, v, z0, z1, lb, row0, n_seq, seq_len, s0_t=None, want_state=False, out_buf=None):
    t_total, d = q.shape
    dk = HG_KEY_DIM
    blk0 = row0 // seq_len
    hp = HG_HEADS_PER_STEP
    seq_spec = pl.BlockSpec((seq_len, hp * dk), lambda s, h: (blk0 + s, h))
    st_spec = pl.BlockSpec((None, 2, hp, dk, dk), lambda s, h: (s, 0, h, 0, 0))
    in_specs = [seq_spec] * 4 + [pl.BlockSpec((2, hp * dk), lambda s, h: (0, h)),
                                 pl.BlockSpec((2, HG_TILE, HG_TILE), lambda s, h: (0, 0, 0))]
    args = [q, v, z0, z1, lb, _hgrn_tri()]
    if s0_t is not None:
        in_specs.append(st_spec)
        args.append(s0_t)
    aliases = {}
    if out_buf is not None:
        in_specs.append(pl.BlockSpec(memory_space=pl.ANY))
        args.append(out_buf)
        aliases = {len(args) - 1: 0}
    out_specs = [seq_spec]
    out_shape = [jax.ShapeDtypeStruct((t_total, d), F32)]
    if want_state:
        out_specs.append(st_spec)
        out_shape.append(jax.ShapeDtypeStruct((n_seq, 2, HG_HEADS, dk, dk), F32))
    res = pl.pallas_call(
        functools.partial(_hgrn_scan_kernel, seq_len=seq_len, has_s0=s0_t is not None,
                          has_buf=out_buf is not None, want_state=want_state),
        grid=(n_seq, HG_HEADS // hp),
        in_specs=in_specs,
        out_specs=out_specs,
        out_shape=out_shape,
        input_output_aliases=aliases,
        scratch_shapes=[pltpu.VMEM((seq_len, hp * dk), F32)],
        compiler_params=_params(("arbitrary", "arbitrary"), 32),
        name="hgrn_scan",
    )(*args)
    return res if want_state else (res[0], None)


def _route(idx, rank, counts, t):
    n = t * TOP_K
    rb = MOE_ROWS
    n_blocks = -(-n // rb) + N_EXPERTS
    n_rows = (n_blocks + N_EXPERTS) * rb
    experts = jnp.arange(N_EXPERTS, dtype=jnp.int32)
    counts = counts[0, :N_EXPERTS].astype(jnp.int32)
    padded = (counts + rb - 1) // rb * rb
    padded_end = jnp.cumsum(padded)
    start = padded_end - padded
    e = idx[:, :TOP_K]
    dest = jnp.sum(jnp.where(e[:, :, None] == experts, start, 0), axis=-1) + rank[:, :TOP_K]
    dest_km = dest.T.astype(jnp.int32)
    q = jnp.arange(rb, dtype=jnp.int32)
    spare = n_blocks * rb + experts[:, None] * rb + q[None, :]
    pad_rows = jnp.where(q[None, :] < (padded - counts)[:, None], (start + counts)[:, None] + q[None, :], spare)
    per_chunk = jnp.concatenate([dest_km.reshape(-1), pad_rows.reshape(-1)])
    sc_index = (jnp.arange(PACKED_CHUNKS, dtype=jnp.int32)[:, None] * n_rows + per_chunk[None, :]).reshape(1, -1)
    block_start = jnp.arange(n_blocks, dtype=jnp.int32) * rb
    block_expert = jnp.minimum(jnp.sum((padded_end[None, :] <= block_start[:, None]).astype(jnp.int32), axis=1),
                               N_EXPERTS - 1)
    n_used = (padded_end[-1] // rb).reshape(1)
    return sc_index, block_expert, n_used, n_rows


def _sc_mesh():
    return plsc.VectorSubcoreMesh(core_axis_name="c", subcore_axis_name="s")


def _sc_dispatch(h2c, sc_index, t, n_rows):
    win = SC_WINDOW
    tw = t // win
    nw = (t * TOP_K + N_EXPERTS * MOE_ROWS) // win
    n_chunks = h2c.shape[0]

    def scatter(x_hbm, i_hbm, o_hbm):
        def body(x_vmem, i_vmem):
            pltpu.sync_copy(x_vmem, o_hbm.at[i_vmem.at[0]])

        pltpu.emit_pipeline(
            body,
            grid=(n_chunks * nw,),
            in_specs=[pl.BlockSpec((win, CHUNK_W), index_map=lambda i: ((i // nw) * tw + (i % nw) % tw, 0)),
                      pl.BlockSpec((1, win), index_map=lambda i: (0, i))],
            out_specs=[],
            core_axis_name=("c", "s"),
            dimension_semantics=(pltpu.PARALLEL,),
        )(x_hbm, i_hbm)

    out = pl.kernel(scatter, out_type=jax.ShapeDtypeStruct((n_chunks * n_rows, CHUNK_W), h2c.dtype),
                    mesh=_sc_mesh(), name="moe_dispatch")(h2c.reshape(n_chunks * t, CHUNK_W), sc_index)
    return out.reshape(n_chunks, n_rows, CHUNK_W)


def _sc_collect(y_rows, sc_index, t, n_rows):
    win = SC_WINDOW
    n_chunks = y_rows.shape[0]
    nw = (t * TOP_K + N_EXPERTS * MOE_ROWS) // win
    aw = t * TOP_K // win

    def gather(y_hbm, i_hbm, o_hbm):
        def body(i_vmem, o_vmem):
            pltpu.sync_copy(y_hbm.at[i_vmem.at[0]], o_vmem)

        pltpu.emit_pipeline(
            body,
            grid=(n_chunks * aw,),
            in_specs=[pl.BlockSpec((1, win), index_map=lambda i: (0, (i // aw) * nw + i % aw))],
            out_specs=[pl.BlockSpec((win, CHUNK_W), index_map=lambda i: (i, 0))],
            core_axis_name=("c", "s"),
            dimension_semantics=(pltpu.PARALLEL,),
        )(i_hbm, o_hbm)

    out = pl.kernel(gather, out_type=jax.ShapeDtypeStruct((n_chunks * TOP_K * t, CHUNK_W), y_rows.dtype),
                    mesh=_sc_mesh(), name="moe_collect")(y_rows.reshape(n_chunks * n_rows, CHUNK_W), sc_index)
    return out.reshape(n_chunks, TOP_K, t, CHUNK_W)


def _moe_kernel(be_ref, nu_ref, x_ref, w1_ref, b1_ref, w2_ref, b2_ref, o_ref, w1b, w2b, *, dff):
    i = pl.program_id(0)
    e = be_ref[i]
    prev = be_ref[jnp.maximum(i - 1, 0)]

    @pl.when(jnp.logical_or(i == 0, e != prev))
    def _():
        w1b[...] = w1_ref[...].astype(BF16)
        w2b[...] = w2_ref[...].astype(BF16)

    @pl.when(i < nu_ref[0])
    def _():
        halves = [_unpack_bf16_pairs(x_ref[j]) for j in range(PACKED_CHUNKS)]
        xb = jnp.concatenate([h[0] for h in halves] + [h[1] for h in halves], axis=1).astype(BF16)
        gu = jnp.dot(xb, w1b[...], preferred_element_type=F32) + b1_ref[...]
        g = jnp.minimum(gu[:, :dff], SWIGLU_LIMIT)
        u = jnp.clip(gu[:, dff:], -SWIGLU_LIMIT, SWIGLU_LIMIT)
        a = (u + 1.0) * (g * jax.nn.sigmoid(SWIGLU_ALPHA * g))
        y = jnp.dot(a.astype(BF16), w2b[...], preferred_element_type=F32) + b2_ref[...]
        words = _pack_bf16_pairs(y)
        for j in range(PACKED_CHUNKS):
            o_ref[j] = words[:, j * CHUNK_W:(j + 1) * CHUNK_W]

    @pl.when(i >= nu_ref[0])
    def _():
        o_ref[...] = jnp.zeros(o_ref.shape, o_ref.dtype)


def _moe_experts(x_rows, block_expert, n_used, layer, w1, b1, w2, b2):
    _, n_rows, _ = x_rows.shape
    rb = MOE_ROWS
    n_blocks = n_rows // rb - N_EXPERTS
    depth, n_e, d, dff2 = w1.shape
    dff = dff2 // 2
    row_spec = pl.BlockSpec((PACKED_CHUNKS, rb, CHUNK_W), lambda i, be, nu: (0, jnp.minimum(i, nu[0] - 1), 0))
    grid_spec = pltpu.PrefetchScalarGridSpec(
        num_scalar_prefetch=2,
        grid=(n_blocks,),
        in_specs=[row_spec,
                  pl.BlockSpec((None, None, d, dff2), lambda i, be, nu: (layer, be[i], 0, 0)),
                  pl.BlockSpec((None, None, 1, dff2), lambda i, be, nu: (layer, be[i], 0, 0)),
                  pl.BlockSpec((None, None, dff, d), lambda i, be, nu: (layer, be[i], 0, 0)),
                  pl.BlockSpec((None, None, 1, d), lambda i, be, nu: (layer, be[i], 0, 0))],
        out_specs=pl.BlockSpec((PACKED_CHUNKS, rb, CHUNK_W), lambda i, be, nu: (0, i, 0)),
        scratch_shapes=[pltpu.VMEM((d, dff2), BF16), pltpu.VMEM((dff, d), BF16)],
    )
    return pl.pallas_call(
        functools.partial(_moe_kernel, dff=dff),
        grid_spec=grid_spec,
        out_shape=jax.ShapeDtypeStruct((PACKED_CHUNKS, n_rows, CHUNK_W), jnp.int32),
        compiler_params=_params(("arbitrary",), 56),
        name="moe_experts",
    )(block_expert, n_used, x_rows, w1, b1.reshape(depth, n_e, 1, dff2), w2, b2.reshape(depth, n_e, 1, d))


def _combined_rows(y_ref, gw_ref, x_ref, mod_ref, d):
    gw = gw_ref[...]
    pieces = {}
    for j in range(PACKED_CHUNKS):
        accs = [None, None]
        for k in range(TOP_K):
            for side, part in enumerate(_unpack_bf16_pairs(y_ref[j, k])):
                term = gw[:, k:k + 1] * part
                accs[side] = term if accs[side] is None else accs[side] + term
        for side, acc in enumerate(accs):
            c0 = side * (d // 2) + j * CHUNK_W
            pieces[c0] = x_ref[:, c0:c0 + CHUNK_W] + mod_ref[:, 5 * d + c0:5 * d + c0 + CHUNK_W] * acc
    return jnp.concatenate([pieces[c0] for c0 in sorted(pieces)], axis=1)


def _combine_final_kernel(y_ref, gw_ref, x_ref, mod_ref, g_ref, o_ref, *, d):
    o_ref[...] = _rms(_combined_rows(y_ref, gw_ref, x_ref, mod_ref, d)) * g_ref[...]


def _combine_in_kernel(y_ref, gw_ref, x_ref, mod_ref, modn_ref, g_ref, w_ref, *rest, d, conv, has_buf):
    x2_ref, *out_refs = rest[1:] if has_buf else rest
    x2 = _combined_rows(y_ref, gw_ref, x_ref, mod_ref, d)
    x2_ref[...] = x2
    _project(x2, modn_ref[...], g_ref, w_ref, out_refs, d, conv)


def _combine_specs(y_slots, gw, x1, mods_l, geom, rows):
    d = x1.shape[1]
    row0, t = rows if rows is not None else (0, x1.shape[0])
    tm = TOKEN_TILE
    t0 = row0 // tm
    grp = functools.partial(_group_of_tile, tile=tm, n_prompt=geom[0], dec_seq=geom[1])
    in_specs = [pl.BlockSpec((PACKED_CHUNKS, TOP_K, tm, CHUNK_W), lambda i: (0, 0, t0 + i, 0)),
                pl.BlockSpec((tm, LANES), lambda i: (t0 + i, 0)),
                pl.BlockSpec((tm, d), lambda i: (t0 + i, 0)),
                pl.BlockSpec((None, 1, 6 * d), lambda i: (grp(t0 + i), 0, 0))]
    return in_specs, [y_slots, gw, x1, mods_l], t, t0, grp


def _combine_final(y_slots, gw, x1, mods_l, geom, final_g, rows):
    d = x1.shape[1]
    tm = TOKEN_TILE
    in_specs, args, t, _, _ = _combine_specs(y_slots, gw, x1, mods_l, geom, rows)
    return pl.pallas_call(
        functools.partial(_combine_final_kernel, d=d),
        grid=(t // tm,),
        in_specs=in_specs + [pl.BlockSpec((1, d), lambda i: (0, 0))],
        out_specs=pl.BlockSpec((tm, d), lambda i: (i, 0)),
        out_shape=jax.ShapeDtypeStruct((t, d), F32),
        compiler_params=_params(("arbitrary",), 24),
        name="moe_combine_final",
    )(*args, final_g)


def _combine_mixer_in(y_slots, gw, x1, mods_l, mods_next, g, w, geom, conv=False, rows=None, x2_buf=None):
    t_total, d = x1.shape
    tm = TOKEN_TILE
    n = w.shape[1]
    n_out = 2 if conv else n // d
    in_specs, args, t, t0, grp = _combine_specs(y_slots, gw, x1, mods_l, geom, rows)
    in_specs += [pl.BlockSpec((None, 1, 6 * d), lambda i: (grp(t0 + i), 0, 0)),
                 pl.BlockSpec((1, d), lambda i: (0, 0)),
                 pl.BlockSpec((d, n), lambda i: (0, 0))]
    args += [mods_next, g, w]
    aliases = {}
    if x2_buf is not None:
        in_specs.append(pl.BlockSpec(memory_space=pl.ANY))
        args.append(x2_buf)
        aliases = {len(args) - 1: 0}
    res = pl.pallas_call(
        functools.partial(_combine_in_kernel, d=d, conv=conv, has_buf=x2_buf is not None),
        grid=(t // tm,),
        in_specs=in_specs,
        out_specs=[pl.BlockSpec((tm, d), lambda i: (t0 + i, 0))] + [pl.BlockSpec((tm, d), lambda i: (i, 0))] * n_out,
        out_shape=[jax.ShapeDtypeStruct((t_total, d), F32)] + [jax.ShapeDtypeStruct((t, d), F32)] * n_out,
        input_output_aliases=aliases,
        compiler_params=_params(("arbitrary",), 56),
        name="combine_mixer_in",
    )(*args)
    return res[0], res[1:]


def kernel(x_prompt, x_sample, c, c_ctx, cache_na_k, cache_na_v, state_hgrn, norm_mix, norm_ffn, w_mod, b_mod,
           conv_w_in, conv_k, conv_w_out, na_w_qkv, na_rpb, na_w_o, hg_w_qig, hg_w_f, hg_lb, hg_norm, hg_w_o,
           moe_w_router, moe_b_router, moe_w1, moe_b1, moe_w2, moe_b2, final_norm):
    batch, seq, d = x_prompt.shape
    dec_batch, dec_seq, _ = x_sample.shape
    depth = w_mod.shape[0]
    n_prompt = batch * seq
    t = n_prompt + dec_batch * dec_seq
    geom = (n_prompt, dec_seq, seq)
    assert seq % TOKEN_TILE == 0 and dec_seq % TOKEN_TILE == 0 and seq % HG_TILE == 0
    assert n_prompt % (dec_seq * dec_batch) == 0 and dec_seq % GRID_W == 0 and 1 + dec_batch <= SUBLANES
    assert d == 2 * PACKED_CHUNKS * CHUNK_W and t % SC_WINDOW == 0 and MOE_ROWS % SC_WINDOW == 0

    x = jnp.concatenate([x_prompt.reshape(n_prompt, d), x_sample.reshape(dec_batch * dec_seq, d)], axis=0)
    cvecs = jnp.zeros((SUBLANES, d), F32).at[0].set(c_ctx).at[1:1 + dec_batch].set(c)
    mods = _adaln_all(cvecs, w_mod, b_mod)[:, :1 + dec_batch].reshape(depth, 1 + dec_batch, 1, 6 * d)

    lb_soft = jax.nn.softmax(hg_lb.astype(F32), axis=0)
    lower_bounds = jnp.cumsum(lb_soft, axis=0) - lb_soft[0]

    new_k, new_v, new_s = [], [], []
    pending = None
    for l in range(depth):
        kind, j = l % N_MIXERS, l // N_MIXERS
        g1 = norm_mix[l].reshape(1, d)
        g2 = norm_ffn[l].reshape(1, d)

        def project(x, w, conv=False, rows=None, x2_buf=None):
            if pending is None:
                return x, _mixer_in(x, mods[l], g1, w, geom, conv=conv, rows=rows)
            return _combine_mixer_in(*pending, mods[l], g1, w, geom, conv=conv, rows=rows, x2_buf=x2_buf)

        if kind == 0:
            x, (b, cu) = project(x, conv_w_in[j].astype(BF16), conv=True)
        elif kind == 1:
            w_qkv = na_w_qkv[j].astype(BF16)
            x, (qp, kp, vp) = project(x, w_qkv, rows=(0, n_prompt))
            x, (qs, ks, vs) = project(x, w_qkv, rows=(n_prompt, t - n_prompt), x2_buf=x)
        else:
            w_in = jnp.concatenate([hg_w_qig[j], hg_w_f[j, 0], hg_w_f[j, 1]], axis=1).astype(BF16)
            x, (q, v, gout, z0, z1) = project(x, w_in)
        tail_args = (x, mods[l], g2, moe_w_router[l], moe_b_router[l], geom)
        if kind == 0:
            x1, h2c, idx, gw, rank, counts = _mixer_out("conv", (b, cu), conv_w_out[j].astype(BF16), *tail_args,
                                         extra=(conv_k[j],))
        elif kind == 1:
            o = _na_context(qp, kp, vp, batch, seq, t, dec_seq)
            o = _na_latent(qs, ks, vs, cache_na_k[:, j], cache_na_v[:, j], na_rpb[j], o, dec_batch, dec_seq)
            new_k.append(kp.reshape(batch, seq, NA_HEADS, d // NA_HEADS))
            new_v.append(vp.reshape(batch, seq, NA_HEADS, d // NA_HEADS))
            x1, h2c, idx, gw, rank, counts = _mixer_out("plain", (o.reshape(t, d),),
                                         na_w_o[j].astype(BF16), *tail_args)
        else:
            lb = lower_bounds[l]
            o, s_p = _hgrn_scan(q, v, z0, z1, lb, 0, batch, seq, want_state=True)
            s0_t = jnp.swapaxes(state_hgrn[:, j].astype(F32), -1, -2)
            o, _ = _hgrn_scan(q, v, z0, z1, lb, n_prompt, dec_batch, dec_seq, s0_t=s0_t, out_buf=o)
            new_s.append(s_p)
            x1, h2c, idx, gw, rank, counts = _mixer_out("hgrn", (o, gout),
                                         hg_w_o[j].astype(BF16), *tail_args, extra=(hg_norm[j].reshape(1, d),))
        sc_index, block_expert, n_used, n_rows = _route(idx, rank, counts, t)
        x_rows = _sc_dispatch(h2c, sc_index, t, n_rows)
        y_rows = _moe_experts(x_rows, block_expert, n_used, l, moe_w1, moe_b1, moe_w2, moe_b2)
        y_slots = _sc_collect(y_rows, sc_index, t, n_rows)
        pending = (y_slots, gw, x1, mods[l])

    final_g = final_norm.reshape(1, d)
    y_prompt = _combine_final(*pending, geom, final_g, (0, n_prompt)).reshape(batch, seq, d)
    y_sample = _combine_final(*pending, geom, final_g, (n_prompt, t - n_prompt)).reshape(dec_batch, dec_seq, d)
    new_na_k = jnp.stack(new_k, axis=1).astype(x_prompt.dtype)
    new_na_v = jnp.stack(new_v, axis=1).astype(x_prompt.dtype)
    new_hgrn_state = jnp.stack(new_s, axis=1).astype(x_prompt.dtype)
    return (y_prompt, y_sample, new_na_k, new_na_v, new_hgrn_state)
```

```python
import functools

import numpy as np
import jax
import jax.numpy as jnp
from jax import lax
from jax.experimental import pallas as pl
from jax.experimental.pallas import tpu as pltpu
from jax.experimental.pallas import tpu_sc as plsc

F32 = jnp.float32
BF16 = jnp.bfloat16
HIGHEST = lax.Precision.HIGHEST

N_MIXERS = 3
GRID_W = 64
CONV_WIDTH = 3
NA_HEADS = 16
NA_ROWS_MAX = 8
NA_COLS = 16
HG_HEADS = 8
HG_KEY_DIM = 128
HG_CHUNK = 32
N_EXPERTS = 32
TOP_K = 4
SWIGLU_LIMIT = 7.0
SWIGLU_ALPHA = 1.702
EPS = 1e-6

LANES = 128
SUBLANES = 8
VMEM_LIMIT_CAP = 60 * 1024 * 1024

NEG_BIG = -1e30
TOKEN_TILE = 256
MOE_ROWS = 512
HG_TILE = 128
HG_HEADS_SHORT = 4
HG_HEADS_LONG = 2
PACKED_CHUNKS = 2
CHUNK_W = 256
SC_WINDOW = 128


def _params(sem, vmem_mb, flags=None):
    return pltpu.CompilerParams(dimension_semantics=sem, flags=flags,
                                vmem_limit_bytes=min(vmem_mb * 1024 * 1024, VMEM_LIMIT_CAP))


def _rms(x):
    return x * lax.rsqrt(jnp.mean(x * x, axis=-1, keepdims=True) + EPS)


def _modulate(x, g, shift, scale):
    return (_rms(x) * g) * (1.0 + scale) + shift


def _bf16_part(x):
    bits = lax.bitcast_convert_type(x, jnp.uint32) & jnp.uint32(0xFFFF0000)
    return lax.bitcast_convert_type(bits, F32)


def _pack_bf16_pairs(x):
    n = x.shape[1]
    bits = lax.bitcast_convert_type(x, jnp.uint32)
    top = (bits + (jnp.uint32(0x7FFF) + ((bits >> 16) & jnp.uint32(1)))) & jnp.uint32(0xFFFF0000)
    return lax.bitcast_convert_type(top[:, :n // 2] | (top[:, n // 2:] >> 16), jnp.int32)


def _unpack_bf16_pairs(words):
    bits = lax.bitcast_convert_type(words, jnp.uint32)
    return (lax.bitcast_convert_type(bits & jnp.uint32(0xFFFF0000), F32),
            lax.bitcast_convert_type(bits << 16, F32))


def _group_of_tile(i, tile, n_prompt, dec_seq):
    start = i * tile
    return jnp.where(start < n_prompt, 0, 1 + (start - n_prompt) // dec_seq)


def _adaln_kernel(c_ref, w_ref, b_ref, o_ref):
    cv = c_ref[...]
    s = cv * jax.nn.sigmoid(cv)
    o_ref[...] = jnp.dot(s, w_ref[...], precision=HIGHEST, preferred_element_type=F32) + b_ref[...]


def _adaln_all(cvecs, w_mod, b_mod):
    depth, d, n = w_mod.shape
    tn = 2048
    return pl.pallas_call(
        _adaln_kernel,
        grid=(depth, n // tn),
        in_specs=[pl.BlockSpec((SUBLANES, d), lambda l, j: (0, 0)),
                  pl.BlockSpec((None, d, tn), lambda l, j: (l, 0, j)),
                  pl.BlockSpec((None, 1, tn), lambda l, j: (l, 0, j))],
        out_specs=pl.BlockSpec((None, SUBLANES, tn), lambda l, j: (l, 0, j)),
        out_shape=jax.ShapeDtypeStruct((depth, SUBLANES, n), F32),
        compiler_params=_params(("arbitrary", "arbitrary"), 40),
        name="adaln",
    )(cvecs, w_mod, b_mod.reshape(depth, 1, n))


def _project(x, mod, g_ref, w_ref, out_refs, d, conv):
    h = _modulate(x, g_ref[...], mod[:, 0:d], mod[:, d:2 * d]).astype(BF16)
    if conv:
        b_ref, cu_ref = out_refs
        b_ref[...] = jnp.dot(h, w_ref[:, 0:d], preferred_element_type=F32)
        c = jnp.dot(h, w_ref[:, d:2 * d], preferred_element_type=F32)
        u = jnp.dot(h, w_ref[:, 2 * d:3 * d], preferred_element_type=F32)
        cu_ref[...] = c * u
    else:
        for j, o_ref in enumerate(out_refs):
            o_ref[...] = jnp.dot(h, w_ref[:, j * d:(j + 1) * d], preferred_element_type=F32)


def _in_kernel(x_ref, mod_ref, g_ref, w_ref, *out_refs, d, conv):
    _project(x_ref[...], mod_ref[...], g_ref, w_ref, out_refs, d, conv)


def _mixer_in(x, mods_l, g, w, geom, conv=False, rows=None):
    d = x.shape[1]
    row0, t = rows if rows is not None else (0, x.shape[0])
    n = w.shape[1]
    n_out = 2 if conv else n // d
    tm = TOKEN_TILE
    t0 = row0 // tm
    grp = functools.partial(_group_of_tile, tile=tm, n_prompt=geom[0], dec_seq=geom[1])
    return pl.pallas_call(
        functools.partial(_in_kernel, d=d, conv=conv),
        grid=(t // tm,),
        in_specs=[pl.BlockSpec((tm, d), lambda i: (t0 + i, 0)),
                  pl.BlockSpec((None, 1, 6 * d), lambda i: (grp(t0 + i), 0, 0)),
                  pl.BlockSpec((1, d), lambda i: (0, 0)),
                  pl.BlockSpec((d, n), lambda i: (0, 0))],
        out_specs=[pl.BlockSpec((tm, d), lambda i: (i, 0))] * n_out,
        out_shape=[jax.ShapeDtypeStruct((t, d), F32)] * n_out,
        compiler_params=_params(("arbitrary",), 48),
        name="mixer_in",
    )(x, mods_l, g, w)


def _tail(m, x_ref, mod, g2_ref, wr_ref, br_ref, outs, run_ref, d):
    x1_ref, h2_ref, idx_ref, gw_ref, rank_ref, cnt_ref = outs
    x1 = x_ref[...] + mod[:, 2 * d:3 * d] * m
    x1_ref[...] = x1
    h2 = _modulate(x1, g2_ref[...], mod[:, 3 * d:4 * d], mod[:, 4 * d:5 * d])
    words = _pack_bf16_pairs(h2)
    for j in range(PACKED_CHUNKS):
        h2_ref[j] = words[:, j * CHUNK_W:(j + 1) * CHUNK_W]
    h2_top = _bf16_part(h2)
    h2_hi = h2_top.astype(BF16)
    h2_lo = (h2 - h2_top).astype(BF16)
    logits = (jnp.dot(h2_hi, wr_ref[0], preferred_element_type=F32)
              + jnp.dot(h2_lo, wr_ref[0], preferred_element_type=F32)
              + jnp.dot(h2_hi, wr_ref[1], preferred_element_type=F32)) + br_ref[...]
    lane = lax.broadcasted_iota(jnp.int32, logits.shape, 1)
    vals, idxs = [], []
    for _ in range(TOP_K):
        mx = jnp.max(logits, axis=-1, keepdims=True)
        ix = jnp.argmax(logits, axis=-1, keepdims=True).astype(jnp.int32)
        vals.append(mx)
        idxs.append(ix)
        logits = jnp.where(lane == ix, NEG_BIG, logits)
    es = [jnp.exp(v - vals[0]) for v in vals]
    den = es[0] + es[1] + es[2] + es[3]

    @pl.when(pl.program_id(0) == 0)
    def _():
        run_ref[...] = jnp.zeros(run_ref.shape, F32)

    tm = lane.shape[0]
    earlier = (lax.broadcasted_iota(jnp.int32, (tm, tm), 0)
               > lax.broadcasted_iota(jnp.int32, (tm, tm), 1)).astype(BF16)
    seen = run_ref[...]
    idx_out = jnp.zeros(lane.shape, jnp.int32)
    gw_out = jnp.zeros(lane.shape, F32)
    rank_out = jnp.zeros(lane.shape, jnp.int32)
    for k in range(TOP_K):
        hit = lane == idxs[k]
        before = jnp.dot(earlier, hit.astype(BF16), preferred_element_type=F32) + seen
        rank_k = jnp.sum(jnp.where(hit, before, 0.0), axis=-1, keepdims=True)
        seen = seen + jnp.sum(hit.astype(F32), axis=0, keepdims=True)
        idx_out = jnp.where(lane == k, idxs[k], idx_out)
        gw_out = jnp.where(lane == k, es[k] / den, gw_out)
        rank_out = jnp.where(lane == k, rank_k.astype(jnp.int32), rank_out)
    run_ref[...] = seen
    cnt_ref[...] = seen
    idx_ref[...] = idx_out
    gw_ref[...] = gw_out
    rank_ref[...] = rank_out


def _plain_out_kernel(a_ref, w_ref, x_ref, mod_ref, g2_ref, wr_ref, br_ref, *rest, d):
    m = jnp.dot(a_ref[...].astype(BF16), w_ref[...], preferred_element_type=F32)
    _tail(m, x_ref, mod_ref[...], g2_ref, wr_ref, br_ref, rest[:-1], rest[-1], d)


def _conv_out_kernel(b_ref, cu_ref, prev_ref, next_ref, ck_ref, w_ref, x_ref, mod_ref, g2_ref, wr_ref,
                     br_ref, *rest, d, tm, n_prompt, seq, dec_seq):
    i = pl.program_id(0)
    cu = cu_ref[...]
    row = lax.broadcasted_iota(jnp.int32, (tm, 1), 0)
    start = i * tm
    in_prompt = start < n_prompt
    seq_len = jnp.where(in_prompt, seq, dec_seq)
    pos = lax.rem(jnp.where(in_prompt, start, start - n_prompt), seq_len) + row
    prev = jnp.where(row == 0, prev_ref[SUBLANES - 1:SUBLANES, :], pltpu.roll(cu, 1, 0))
    prev = jnp.where(pos == 0, 0.0, prev)
    nxt = jnp.where(row == tm - 1, next_ref[0:1, :], pltpu.roll(cu, tm - 1, 0))
    nxt = jnp.where(pos == seq_len - 1, 0.0, nxt)
    conv = prev * ck_ref[0:1, :] + cu * ck_ref[1:2, :] + nxt * ck_ref[2:3, :]
    m = jnp.dot((b_ref[...] * conv).astype(BF16), w_ref[...], preferred_element_type=F32)
    _tail(m, x_ref, mod_ref[...], g2_ref, wr_ref, br_ref, rest[:-1], rest[-1], d)


def _hgrn_out_kernel(o_ref, gout_ref, ng_ref, w_ref, x_ref, mod_ref, g2_ref, wr_ref, br_ref, *rest, d):
    parts = []
    for h in range(HG_HEADS):
        oh = o_ref[:, h * LANES:(h + 1) * LANES]
        parts.append(oh * lax.rsqrt(jnp.mean(oh * oh, axis=-1, keepdims=True) + EPS))
    gout = gout_ref[...]
    y = (jnp.concatenate(parts, axis=1) * ng_ref[...]) * (gout * jax.nn.sigmoid(gout))
    m = jnp.dot(y.astype(BF16), w_ref[...], preferred_element_type=F32)
    _tail(m, x_ref, mod_ref[...], g2_ref, wr_ref, br_ref, rest[:-1], rest[-1], d)


def _mixer_out(kind, acts, w_out, x, mods_l, g2, w_router, b_router, geom, extra=()):
    t, d = x.shape
    tm = TOKEN_TILE
    n_prompt, dec_seq, seq = geom
    grp = functools.partial(_group_of_tile, tile=tm, n_prompt=n_prompt, dec_seq=dec_seq)
    row_spec = pl.BlockSpec((tm, d), lambda i: (i, 0))
    wr = jnp.zeros((d, LANES), F32).at[:, :N_EXPERTS].set(w_router)
    wr_top = _bf16_part(wr)
    wr = jnp.stack([wr_top.astype(BF16), (wr - wr_top).astype(BF16)])
    br = jnp.full((1, LANES), NEG_BIG, F32).at[0, :N_EXPERTS].set(b_router)
    common_specs = [pl.BlockSpec((d, d), lambda i: (0, 0)),
                    row_spec,
                    pl.BlockSpec((None, 1, 6 * d), lambda i: (grp(i), 0, 0)),
                    pl.BlockSpec((1, d), lambda i: (0, 0)),
                    pl.BlockSpec((2, d, LANES), lambda i: (0, 0, 0)),
                    pl.BlockSpec((1, LANES), lambda i: (0, 0))]
    common_args = [w_out, x, mods_l, g2, wr, br]
    if kind == "plain":
        body = functools.partial(_plain_out_kernel, d=d)
        specs = [row_spec] + common_specs
        args = list(acts) + common_args
    elif kind == "conv":
        body = functools.partial(_conv_out_kernel, d=d, tm=tm, n_prompt=n_prompt, seq=seq, dec_seq=dec_seq)
        per = tm // SUBLANES
        last = t // SUBLANES - 1
        specs = [row_spec, row_spec,
                 pl.BlockSpec((SUBLANES, d), lambda i: (jnp.maximum(i * per - 1, 0), 0)),
                 pl.BlockSpec((SUBLANES, d), lambda i: (jnp.minimum((i + 1) * per, last), 0)),
                 pl.BlockSpec((CONV_WIDTH, d), lambda i: (0, 0))] + common_specs
        b, cu = acts
        args = [b, cu, cu, cu, extra[0]] + common_args
    else:
        body = functools.partial(_hgrn_out_kernel, d=d)
        specs = [row_spec, row_spec, pl.BlockSpec((1, d), lambda i: (0, 0))] + common_specs
        args = list(acts) + [extra[0]] + common_args
    lane_spec = pl.BlockSpec((tm, LANES), lambda i: (i, 0))
    return pl.pallas_call(
        body,
        grid=(t // tm,),
        in_specs=specs,
        out_specs=[row_spec, pl.BlockSpec((PACKED_CHUNKS, tm, CHUNK_W), lambda i: (0, i, 0)),
                   lane_spec, lane_spec, lane_spec, pl.BlockSpec((1, LANES), lambda i: (0, 0))],
        out_shape=[jax.ShapeDtypeStruct((t, d), F32), jax.ShapeDtypeStruct((PACKED_CHUNKS, t, CHUNK_W), jnp.int32),
                   jax.ShapeDtypeStruct((t, LANES), jnp.int32), jax.ShapeDtypeStruct((t, LANES), F32),
                   jax.ShapeDtypeStruct((t, LANES), jnp.int32), jax.ShapeDtypeStruct((1, LANES), F32)],
        scratch_shapes=[pltpu.VMEM((1, LANES), F32)],
        compiler_params=_params(("arbitrary",), 40),
        name="mixer_out_" + kind,
    )(*args)


def _softmax_rows(parts):
    mx = functools.reduce(jnp.maximum, [jnp.max(s, axis=-1, keepdims=True) for s in parts])
    es = [jnp.exp(s - mx) for s in parts]
    den = functools.reduce(lambda a, b: a + b, [jnp.sum(e, axis=-1, keepdims=True) for e in es])
    return [e / den for e in es]


def _head_pair_queries(q, dh):
    first = lax.broadcasted_iota(jnp.int32, (1, q.shape[1]), 1) < dh
    return first, (jnp.where(first, q, 0.0).astype(BF16), jnp.where(first, 0.0, q).astype(BF16))


def _na_ctx_kernel(q_ref, k_ref, v_ref, o_ref, *, dh, scale):
    nt = (((1,), (1,)), ((), ()))
    for pair in range(NA_HEADS * dh // LANES):
        sl = slice(pair * LANES, (pair + 1) * LANES)
        first, queries = _head_pair_queries(q_ref[:, sl], dh)
        n_q = q_ref.shape[0]
        q2 = jnp.concatenate(queries, axis=0)
        (p,) = _softmax_rows([lax.dot_general(q2, k_ref[:, sl].astype(BF16), nt,
                                              preferred_element_type=F32) * scale])
        o2 = jnp.dot(p.astype(BF16), v_ref[:, sl].astype(BF16), preferred_element_type=F32)
        o_ref[:, sl] = jnp.where(first, o2[:n_q], o2[n_q:])


def _na_context(q, k, v, batch, seq, t_total, dec_seq):
    d = q.shape[1]
    dh = d // NA_HEADS
    per = dec_seq // seq
    spec = pl.BlockSpec((seq, d), lambda b: (b, 0))
    return pl.pallas_call(
        functools.partial(_na_ctx_kernel, dh=dh, scale=dh ** -0.5),
        grid=(batch,),
        in_specs=[spec, spec, spec],
        out_specs=pl.BlockSpec((None, seq, d), lambda b: (b // per, b % per, 0)),
        out_shape=jax.ShapeDtypeStruct((t_total // dec_seq, dec_seq, d), F32),
        compiler_params=_params(("arbitrary",), 32),
        name="na_context",
    )(q, k, v)


def _na_lat_kernel(*refs, dh, scale, kr, n_batch):
    per = 1 + 2 * kr
    ck_ref, cv_ref, bias_ref, _, o_ref = refs[n_batch * per:]
    nt = (((1,), (1,)), ((), ()))
    heads_per_group = LANES // dh
    for pair in range(NA_HEADS // heads_per_group):
        sl = slice(pair * LANES, (pair + 1) * LANES)
        for b in range(n_batch):
            q_ref = refs[b * per]
            k_refs = refs[b * per + 1:b * per + 1 + kr]
            v_refs = refs[b * per + 1 + kr:(b + 1) * per]
            first, queries = _head_pair_queries(q_ref[:, sl], dh)
            n_q = q_ref.shape[0]
            q2 = jnp.concatenate(queries, axis=0)
            bias2 = jnp.concatenate([bias_ref[pair * heads_per_group + j] for j in range(heads_per_group)], axis=0)
            kw = jnp.concatenate([r[:, sl] for r in k_refs], axis=0).astype(BF16)
            vw = jnp.concatenate([r[:, sl] for r in v_refs], axis=0).astype(BF16)
            s_nb = lax.dot_general(q2, kw, nt, preferred_element_type=F32) * scale + bias2
            s_cx = lax.dot_general(q2, ck_ref[b, :, sl].astype(BF16), nt, preferred_element_type=F32) * scale
            p_nb, p_cx = _softmax_rows([s_nb, s_cx])
            o2 = (jnp.dot(p_nb.astype(BF16), vw, preferred_element_type=F32)
                  + jnp.dot(p_cx.astype(BF16), cv_ref[b, :, sl].astype(BF16), preferred_element_type=F32))
            o_ref[b, :, sl] = jnp.where(first, o2[:n_q], o2[n_q:])


def _na_bias_table(rpb, kr):
    col = np.arange(GRID_W)
    col_start = np.clip(col - NA_COLS // 2, 0, GRID_W - NA_COLS)
    delta = col[None, :] - col[:, None] + (NA_COLS - 1)
    inside = (col[None, :] >= col_start[:, None]) & (col[None, :] < col_start[:, None] + NA_COLS)
    pick = (np.arange(2 * NA_COLS - 1)[:, None, None] == delta[None]).astype(np.float32)
    tab = jnp.einsum("hrk,kwc->hrwc", rpb, jnp.asarray(pick), precision=HIGHEST)
    tab = jnp.where(jnp.asarray(inside)[None, None], tab, NEG_BIG)
    out = []
    for d0 in range(NA_ROWS_MAX):
        rows = tab[:, d0:d0 + kr]
        out.append(rows.transpose(0, 2, 1, 3).reshape(NA_HEADS, GRID_W, kr * GRID_W))
    return jnp.stack(out, axis=0)


def _na_latent(q, k, v, ctx_k, ctx_v, rpb, out_buf, dec_batch, dec_seq):
    d = q.shape[1]
    dh = d // NA_HEADS
    rows = dec_seq // GRID_W
    kr = min(NA_ROWS_MAX, rows)
    base = 0
    past = ctx_k.shape[1]
    out_slab = (out_buf.shape[0] - dec_batch) // dec_batch
    bias = _na_bias_table(rpb, kr)

    def row_start(r):
        return jnp.clip(r - kr // 2, 0, rows - kr)

    def win_spec(b, j):
        return pl.BlockSpec((GRID_W, d), lambda r: (base + b * rows + row_start(r) + j, 0))

    in_specs, args = [], []
    for b in range(dec_batch):
        in_specs += ([pl.BlockSpec((GRID_W, d), lambda r, b=b: (base + b * rows + r, 0))]
                     + [win_spec(b, j) for j in range(kr)] * 2)
        args += [q] + [k] * kr + [v] * kr
    in_specs += ([pl.BlockSpec((dec_batch, past, d), lambda r: (0, 0, 0))] * 2
                 + [pl.BlockSpec((None, NA_HEADS, GRID_W, kr * GRID_W),
                                 lambda r: (row_start(r) - r + NA_ROWS_MAX - 1, 0, 0, 0))])
    in_specs.append(pl.BlockSpec(memory_space=pl.ANY))
    args += [ctx_k.reshape(dec_batch, past, d), ctx_v.reshape(dec_batch, past, d), bias, out_buf]
    return pl.pallas_call(
        functools.partial(_na_lat_kernel, dh=dh, scale=dh ** -0.5, kr=kr, n_batch=dec_batch),
        grid=(rows,),
        in_specs=in_specs,
        out_specs=pl.BlockSpec((dec_batch, GRID_W, d), lambda r: (out_slab, r, 0)),
        out_shape=jax.ShapeDtypeStruct(out_buf.shape, F32),
        input_output_aliases={len(args) - 1: 0},
        compiler_params=_params(("arbitrary",), 40),
        name="na_latent",
    )(*args)


def _hgrn_scan_kernel(*refs, seq_len, heads, has_s0, has_buf, want_state):
    q_ref, v_ref, z0_ref, z1_ref, lb_ref, tri_ref = refs[:6]
    rest = list(refs[6:])
    s0_ref = rest.pop(0) if has_s0 else None
    if has_buf:
        rest.pop(0)
    o_ref = rest.pop(0)
    sf_ref = rest.pop(0) if want_state else None
    ob_ref = rest.pop(0)
    c, r = HG_CHUNK, HG_TILE
    nc, nt = r // c, seq_len // r
    dk = HG_KEY_DIM
    tpos8 = lax.broadcasted_iota(jnp.int32, (1, SUBLANES, 1), 1)
    row = lax.broadcasted_iota(jnp.int32, (r, r), 0)
    col = lax.broadcasted_iota(jnp.int32, (r, r), 1)
    same16 = (row // (2 * SUBLANES)) == (col // (2 * SUBLANES))
    same32 = (row // c) == (col // c)
    nt_dims = (((1,), (1,)), ((), ()))
    tn_dims = (((0,), (0,)), ((), ()))

    def tile(dr, hh, i, st):
        z_ref = (z0_ref, z1_ref)[dr]
        cols = slice(hh * dk, (hh + 1) * dk)
        lb = lb_ref[dr:dr + 1, cols]
        tri = tri_ref[dr]
        ti = i if dr == 0 else nt - 1 - i
        r0 = pl.multiple_of(ti * r, r)
        q = q_ref[pl.ds(r0, r), cols]
        v = v_ref[pl.ds(r0, r), cols]
        z = z_ref[pl.ds(r0, r), cols]
        g = jnp.log(lb + (1.0 - lb) * jax.nn.sigmoid(z))
        k = (1.0 - lb) * jax.nn.sigmoid(-z)
        g_hi = _bf16_part(g)
        g_mid = _bf16_part(g - g_hi)
        g_lo = (g - g_hi) - g_mid
        b = (jnp.dot(tri, g_hi.astype(BF16), preferred_element_type=F32)
             + jnp.dot(tri, g_mid.astype(BF16), preferred_element_type=F32)
             + jnp.dot(tri, g_lo.astype(BF16), preferred_element_type=F32))
        b3, q3, k3, v3 = (a.reshape(nc, c, dk) for a in (b, q, k, v))
        nb = r // SUBLANES
        b8, q8, k8, v8 = (a.reshape(nb, SUBLANES, dk) for a in (b, q, k, v))
        oi = jnp.zeros((nb, SUBLANES, dk), F32)
        for s in range(SUBLANES):
            keep = (tpos8 >= s) if dr == 0 else (tpos8 <= s)
            e = jnp.exp(jnp.where(keep, b8 - b8[:, s:s + 1, :], NEG_BIG))
            a = jnp.sum(q8 * e * k8[:, s:s + 1, :], axis=-1, keepdims=True)
            oi = oi + a * v8[:, s:s + 1, :]
        att = None
        for span, same_group in ((2 * SUBLANES, same16), (c, same32)):
            half = span // 2
            bg = b.reshape(r // span, span, dk)
            pos = lax.broadcasted_iota(jnp.int32, (1, span, 1), 1)
            edge = bg[:, half - 1:half, :] if dr == 0 else bg[:, half:half + 1, :]
            is_query = (pos >= half) if dr == 0 else (pos < half)
            qf = q.reshape(bg.shape) * jnp.exp(jnp.where(is_query, bg - edge, NEG_BIG))
            kf = k.reshape(bg.shape) * jnp.exp(jnp.where(is_query, NEG_BIG, edge - bg))
            part = lax.dot_general(qf.reshape(r, dk).astype(BF16), kf.reshape(r, dk).astype(BF16), nt_dims,
                                   preferred_element_type=F32)
            part = jnp.where(same_group, part, 0.0)
            att = part if att is None else att + part
        oi = oi.reshape(r, dk) + jnp.dot(att.astype(BF16), v.astype(BF16), preferred_element_type=F32)
        oi = oi.reshape(nc, c, dk)
        bl = b3[:, c - 1:c, :] if dr == 0 else b3[:, 0:1, :]
        qt = (q3 * jnp.exp(b3)).astype(BF16)
        kt = (k3 * jnp.exp(bl - b3)).astype(BF16)
        vb = v3.astype(BF16)
        outs = [None] * nc
        for ci in (range(nc) if dr == 0 else range(nc - 1, -1, -1)):
            oc = lax.dot_general(qt[ci], st.astype(BF16), nt_dims, preferred_element_type=F32)
            outs[ci] = oi[ci] + oc
            kv = lax.dot_general(vb[ci], kt[ci], tn_dims, preferred_element_type=F32)
            st = st * jnp.exp(bl[ci]) + kv
        return r0, jnp.concatenate(outs, axis=0), st

    chains = [(dr, hh) for hh in range(heads) for dr in range(2)]

    def step(i, carry):
        new = []
        for (dr, hh), st in zip(chains, carry):
            r0, o, st = tile(dr, hh, i, st)
            (o_ref if dr == 0 else ob_ref)[pl.ds(r0, r), hh * dk:(hh + 1) * dk] = o
            new.append(st)
        return tuple(new)

    init = tuple(s0_ref[dr, hh] if has_s0 else jnp.zeros((dk, dk), F32) for dr, hh in chains)
    final = lax.fori_loop(0, nt, step, init)
    o_ref[...] += ob_ref[...]
    if want_state:
        for (dr, hh), st in zip(chains, final):
            sf_ref[dr, hh] = st.T


def _hgrn_tri():
    t = np.arange(HG_TILE)
    same = (t[:, None] // HG_CHUNK) == (t[None, :] // HG_CHUNK)
    fwd = same & (t[None, :] <= t[:, None])
    bwd = same & (t[None, :] >= t[:, None])
    return jnp.asarray(np.stack([fwd, bwd]).astype(np.float32)).astype(BF16)


def _hgrn_scan(q, v, z0, z1, lb, row0, n_seq, seq_len, hp, s0_t=None, want_state=False, out_buf=None):
    t_total, d = q.shape
    dk = HG_KEY_DIM
    blk0 = row0 // seq_len
    seq_spec = pl.BlockSpec((seq_len, hp * dk), lambda s, h: (blk0 + s, h))
    st_spec = pl.BlockSpec((None, 2, hp, dk, dk), lambda s, h: (s, 0, h, 0, 0))
    in_specs = [seq_spec] * 4 + [pl.BlockSpec((2, hp * dk), lambda s, h: (0, h)),
                                 pl.BlockSpec((2, HG_TILE, HG_TILE), lambda s, h: (0, 0, 0))]
    args = [q, v, z0, z1, lb, _hgrn_tri()]
    if s0_t is not None:
        in_specs.append(st_spec)
        args.append(s0_t)
    aliases = {}
    if out_buf is not None:
        in_specs.append(pl.BlockSpec(memory_space=pl.ANY))
        args.append(out_buf)
        aliases = {len(args) - 1: 0}
    out_specs = [seq_spec]
    out_shape = [jax.ShapeDtypeStruct((t_total, d), F32)]
    if want_state:
        out_specs.append(st_spec)
        out_shape.append(jax.ShapeDtypeStruct((n_seq, 2, HG_HEADS, dk, dk), F32))
    res = pl.pallas_call(
        functools.partial(_hgrn_scan_kernel, seq_len=seq_len, heads=hp, has_s0=s0_t is not None,
                          has_buf=out_buf is not None, want_state=want_state),
        grid=(n_seq, HG_HEADS // hp),
        in_specs=in_specs,
        out_specs=out_specs,
        out_shape=out_shape,
        input_output_aliases=aliases,
        scratch_shapes=[pltpu.VMEM((seq_len, hp * dk), F32)],
        compiler_params=_params(("arbitrary", "arbitrary"), 32),
        name="hgrn_scan",
    )(*args)
    return res if want_state else (res[0], None)


def _route(idx, rank, counts, t):
    n = t * TOP_K
    rb = MOE_ROWS
    n_blocks = -(-n // rb) + N_EXPERTS
    n_rows = (n_blocks + N_EXPERTS) * rb
    experts = jnp.arange(N_EXPERTS, dtype=jnp.int32)
    counts = counts[0, :N_EXPERTS].astype(jnp.int32)
    padded = (counts + rb - 1) // rb * rb
    padded_end = jnp.cumsum(padded)
    start = padded_end - padded
    e = idx[:, :TOP_K]
    dest = jnp.sum(jnp.where(e[:, :, None] == experts, start, 0), axis=-1) + rank[:, :TOP_K]
    dest_km = dest.T.astype(jnp.int32)
    q = jnp.arange(rb, dtype=jnp.int32)
    spare = n_blocks * rb + experts[:, None] * rb + q[None, :]
    pad_rows = jnp.where(q[None, :] < (padded - counts)[:, None], (start + counts)[:, None] + q[None, :], spare)
    per_chunk = jnp.concatenate([dest_km.reshape(-1), pad_rows.reshape(-1)])
    sc_index = (jnp.arange(PACKED_CHUNKS, dtype=jnp.int32)[:, None] * n_rows + per_chunk[None, :]).reshape(1, -1)
    block_start = jnp.arange(n_blocks, dtype=jnp.int32) * rb
    block_expert = jnp.minimum(jnp.sum((padded_end[None, :] <= block_start[:, None]).astype(jnp.int32), axis=1),
                               N_EXPERTS - 1)
    n_used = (padded_end[-1] // rb).reshape(1)
    return sc_index, block_expert, n_used, n_rows


def _sc_mesh():
    return plsc.VectorSubcoreMesh(core_axis_name="c", subcore_axis_name="s")


def _sc_dispatch(h2c, sc_index, t, n_rows):
    win = SC_WINDOW
    tw = t // win
    nw = (t * TOP_K + N_EXPERTS * MOE_ROWS) // win
    n_chunks = h2c.shape[0]

    def scatter(x_hbm, i_hbm, o_hbm):
        def body(x_vmem, i_vmem):
            pltpu.sync_copy(x_vmem, o_hbm.at[i_vmem.at[0]])

        pltpu.emit_pipeline(
            body,
            grid=(n_chunks * nw,),
            in_specs=[pl.BlockSpec((win, CHUNK_W), index_map=lambda i: ((i // nw) * tw + (i % nw) % tw, 0)),
                      pl.BlockSpec((1, win), index_map=lambda i: (0, i))],
            out_specs=[],
            core_axis_name=("c", "s"),
            dimension_semantics=(pltpu.PARALLEL,),
        )(x_hbm, i_hbm)

    out = pl.kernel(scatter, out_type=jax.ShapeDtypeStruct((n_chunks * n_rows, CHUNK_W), h2c.dtype),
                    mesh=_sc_mesh(), name="moe_dispatch")(h2c.reshape(n_chunks * t, CHUNK_W), sc_index)
    return out.reshape(n_chunks, n_rows, CHUNK_W)


def _sc_collect(y_rows, sc_index, t, n_rows):
    win = SC_WINDOW
    n_chunks = y_rows.shape[0]
    nw = (t * TOP_K + N_EXPERTS * MOE_ROWS) // win
    aw = t * TOP_K // win

    def gather(y_hbm, i_hbm, o_hbm):
        def body(i_vmem, o_vmem):
            pltpu.sync_copy(y_hbm.at[i_vmem.at[0]], o_vmem)

        pltpu.emit_pipeline(
            body,
            grid=(n_chunks * aw,),
            in_specs=[pl.BlockSpec((1, win), index_map=lambda i: (0, (i // aw) * nw + i % aw))],
            out_specs=[pl.BlockSpec((win, CHUNK_W), index_map=lambda i: (i, 0))],
            core_axis_name=("c", "s"),
            dimension_semantics=(pltpu.PARALLEL,),
        )(i_hbm, o_hbm)

    out = pl.kernel(gather, out_type=jax.ShapeDtypeStruct((n_chunks * TOP_K * t, CHUNK_W), y_rows.dtype),
                    mesh=_sc_mesh(), name="moe_collect")(y_rows.reshape(n_chunks * n_rows, CHUNK_W), sc_index)
    return out.reshape(n_chunks, TOP_K, t, CHUNK_W)


def _moe_kernel(be_ref, nu_ref, x_ref, w1_ref, b1_ref, w2_ref, b2_ref, o_ref, w1b, w2b, *, dff):
    i = pl.program_id(0)
    e = be_ref[i]
    prev = be_ref[jnp.maximum(i - 1, 0)]

    @pl.when(jnp.logical_or(i == 0, e != prev))
    def _():
        w1b[...] = w1_ref[...].astype(BF16)
        w2b[...] = w2_ref[...].astype(BF16)

    @pl.when(i < nu_ref[0])
    def _():
        halves = [_unpack_bf16_pairs(x_ref[j]) for j in range(PACKED_CHUNKS)]
        xb = jnp.concatenate([h[0] for h in halves] + [h[1] for h in halves], axis=1).astype(BF16)
        gu = jnp.dot(xb, w1b[...], preferred_element_type=F32) + b1_ref[...]
        g = jnp.minimum(gu[:, :dff], SWIGLU_LIMIT)
        u = jnp.clip(gu[:, dff:], -SWIGLU_LIMIT, SWIGLU_LIMIT)
        a = (u + 1.0) * (g * jax.nn.sigmoid(SWIGLU_ALPHA * g))
        y = jnp.dot(a.astype(BF16), w2b[...], preferred_element_type=F32) + b2_ref[...]
        words = _pack_bf16_pairs(y)
        for j in range(PACKED_CHUNKS):
            o_ref[j] = words[:, j * CHUNK_W:(j + 1) * CHUNK_W]

    @pl.when(i >= nu_ref[0])
    def _():
        o_ref[...] = jnp.zeros(o_ref.shape, o_ref.dtype)


def _moe_experts(x_rows, block_expert, n_used, layer, w1, b1, w2, b2):
    _, n_rows, _ = x_rows.shape
    rb = MOE_ROWS
    n_blocks = n_rows // rb - N_EXPERTS
    depth, n_e, d, dff2 = w1.shape
    dff = dff2 // 2
    row_spec = pl.BlockSpec((PACKED_CHUNKS, rb, CHUNK_W), lambda i, be, nu: (0, jnp.minimum(i, nu[0] - 1), 0))
    grid_spec = pltpu.PrefetchScalarGridSpec(
        num_scalar_prefetch=2,
        grid=(n_blocks,),
        in_specs=[row_spec,
                  pl.BlockSpec((None, None, d, dff2), lambda i, be, nu: (layer, be[i], 0, 0)),
                  pl.BlockSpec((None, None, 1, dff2), lambda i, be, nu: (layer, be[i], 0, 0)),
                  pl.BlockSpec((None, None, dff, d), lambda i, be, nu: (layer, be[i], 0, 0)),
                  pl.BlockSpec((None, None, 1, d), lambda i, be, nu: (layer, be[i], 0, 0))],
        out_specs=pl.BlockSpec((PACKED_CHUNKS, rb, CHUNK_W), lambda i, be, nu: (0, i, 0)),
        scratch_shapes=[pltpu.VMEM((d, dff2), BF16), pltpu.VMEM((dff, d), BF16)],
    )
    return pl.pallas_call(
        functools.partial(_moe_kernel, dff=dff),
        grid_spec=grid_spec,
        out_shape=jax.ShapeDtypeStruct((PACKED_CHUNKS, n_rows, CHUNK_W), jnp.int32),
        compiler_params=_params(("arbitrary",), 56),
        name="moe_experts",
    )(block_expert, n_used, x_rows, w1, b1.reshape(depth, n_e, 1, dff2), w2, b2.reshape(depth, n_e, 1, d))


def _combined_rows(y_ref, gw_ref, x_ref, mod_ref, d):
    gw = gw_ref[...]
    pieces = {}
    for j in range(PACKED_CHUNKS):
        accs = [None, None]
        for k in range(TOP_K):
            for side, part in enumerate(_unpack_bf16_pairs(y_ref[j, k])):
                term = gw[:, k:k + 1] * part
                accs[side] = term if accs[side] is None else accs[side] + term
        for side, acc in enumerate(accs):
            c0 = side * (d // 2) + j * CHUNK_W
            pieces[c0] = x_ref[:, c0:c0 + CHUNK_W] + mod_ref[:, 5 * d + c0:5 * d + c0 + CHUNK_W] * acc
    return jnp.concatenate([pieces[c0] for c0 in sorted(pieces)], axis=1)


def _combine_final_kernel(y_ref, gw_ref, x_ref, mod_ref, g_ref, o_ref, *, d):
    o_ref[...] = _rms(_combined_rows(y_ref, gw_ref, x_ref, mod_ref, d)) * g_ref[...]


def _combine_in_kernel(y_ref, gw_ref, x_ref, mod_ref, modn_ref, g_ref, w_ref, *rest, d, conv, has_buf):
    x2_ref, *out_refs = rest[1:] if has_buf else rest
    x2 = _combined_rows(y_ref, gw_ref, x_ref, mod_ref, d)
    x2_ref[...] = x2
    _project(x2, modn_ref[...], g_ref, w_ref, out_refs, d, conv)


def _combine_specs(y_slots, gw, x1, mods_l, geom, rows):
    d = x1.shape[1]
    row0, t = rows if rows is not None else (0, x1.shape[0])
    tm = TOKEN_TILE
    t0 = row0 // tm
    grp = functools.partial(_group_of_tile, tile=tm, n_prompt=geom[0], dec_seq=geom[1])
    in_specs = [pl.BlockSpec((PACKED_CHUNKS, TOP_K, tm, CHUNK_W), lambda i: (0, 0, t0 + i, 0)),
                pl.BlockSpec((tm, LANES), lambda i: (t0 + i, 0)),
                pl.BlockSpec((tm, d), lambda i: (t0 + i, 0)),
                pl.BlockSpec((None, 1, 6 * d), lambda i: (grp(t0 + i), 0, 0))]
    return in_specs, [y_slots, gw, x1, mods_l], t, t0, grp


def _combine_final(y_slots, gw, x1, mods_l, geom, final_g, rows):
    d = x1.shape[1]
    tm = TOKEN_TILE
    in_specs, args, t, _, _ = _combine_specs(y_slots, gw, x1, mods_l, geom, rows)
    return pl.pallas_call(
        functools.partial(_combine_final_kernel, d=d),
        grid=(t // tm,),
        in_specs=in_specs + [pl.BlockSpec((1, d), lambda i: (0, 0))],
        out_specs=pl.BlockSpec((tm, d), lambda i: (i, 0)),
        out_shape=jax.ShapeDtypeStruct((t, d), F32),
        compiler_params=_params(("arbitrary",), 24),
        name="moe_combine_final",
    )(*args, final_g)


def _combine_mixer_in(y_slots, gw, x1, mods_l, mods_next, g, w, geom, conv=False, rows=None, x2_buf=None):
    t_total, d = x1.shape
    tm = TOKEN_TILE
    n = w.shape[1]
    n_out = 2 if conv else n // d
    in_specs, args, t, t0, grp = _combine_specs(y_slots, gw, x1, mods_l, geom, rows)
    in_specs += [pl.BlockSpec((None, 1, 6 * d), lambda i: (grp(t0 + i), 0, 0)),
                 pl.BlockSpec((1, d), lambda i: (0, 0)),
                 pl.BlockSpec((d, n), lambda i: (0, 0))]
    args += [mods_next, g, w]
    aliases = {}
    if x2_buf is not None:
        in_specs.append(pl.BlockSpec(memory_space=pl.ANY))
        args.append(x2_buf)
        aliases = {len(args) - 1: 0}
    res = pl.pallas_call(
        functools.partial(_combine_in_kernel, d=d, conv=conv, has_buf=x2_buf is not None),
        grid=(t // tm,),
        in_specs=in_specs,
        out_specs=[pl.BlockSpec((tm, d), lambda i: (t0 + i, 0))] + [pl.BlockSpec((tm, d), lambda i: (i, 0))] * n_out,
        out_shape=[jax.ShapeDtypeStruct((t_total, d), F32)] + [jax.ShapeDtypeStruct((t, d), F32)] * n_out,
        input_output_aliases=aliases,
        compiler_params=_params(("arbitrary",), 56),
        name="combine_mixer_in",
    )(*args)
    return res[0], res[1:]


def kernel(x_prompt, x_sample, c, c_ctx, cache_na_k, cache_na_v, state_hgrn, norm_mix, norm_ffn, w_mod, b_mod,
           conv_w_in, conv_k, conv_w_out, na_w_qkv, na_rpb, na_w_o, hg_w_qig, hg_w_f, hg_lb, hg_norm, hg_w_o,
           moe_w_router, moe_b_router, moe_w1, moe_b1, moe_w2, moe_b2, final_norm):
    batch, seq, d = x_prompt.shape
    dec_batch, dec_seq, _ = x_sample.shape
    depth = w_mod.shape[0]
    n_prompt = batch * seq
    t = n_prompt + dec_batch * dec_seq
    geom = (n_prompt, dec_seq, seq)
    assert seq % TOKEN_TILE == 0 and dec_seq % TOKEN_TILE == 0 and seq % HG_TILE == 0
    assert n_prompt % (dec_seq * dec_batch) == 0 and dec_seq % GRID_W == 0 and 1 + dec_batch <= SUBLANES
    assert d == 2 * PACKED_CHUNKS * CHUNK_W and t % SC_WINDOW == 0 and MOE_ROWS % SC_WINDOW == 0

    x = jnp.concatenate([x_prompt.reshape(n_prompt, d), x_sample.reshape(dec_batch * dec_seq, d)], axis=0)
    cvecs = jnp.zeros((SUBLANES, d), F32).at[0].set(c_ctx).at[1:1 + dec_batch].set(c)
    mods = _adaln_all(cvecs, w_mod, b_mod)[:, :1 + dec_batch].reshape(depth, 1 + dec_batch, 1, 6 * d)

    lb_soft = jax.nn.softmax(hg_lb.astype(F32), axis=0)
    lower_bounds = jnp.cumsum(lb_soft, axis=0) - lb_soft[0]

    new_k, new_v, new_s = [], [], []
    pending = None
    for l in range(depth):
        kind, j = l % N_MIXERS, l // N_MIXERS
        g1 = norm_mix[l].reshape(1, d)
        g2 = norm_ffn[l].reshape(1, d)

        def project(x, w, conv=False, rows=None, x2_buf=None):
            if pending is None:
                return x, _mixer_in(x, mods[l], g1, w, geom, conv=conv, rows=rows)
            return _combine_mixer_in(*pending, mods[l], g1, w, geom, conv=conv, rows=rows, x2_buf=x2_buf)

        if kind == 0:
            x, (b, cu) = project(x, conv_w_in[j].astype(BF16), conv=True)
        elif kind == 1:
            w_qkv = na_w_qkv[j].astype(BF16)
            x, (qp, kp, vp) = project(x, w_qkv, rows=(0, n_prompt))
            x, (qs, ks, vs) = project(x, w_qkv, rows=(n_prompt, t - n_prompt), x2_buf=x)
        else:
            w_in = jnp.concatenate([hg_w_qig[j], hg_w_f[j, 0], hg_w_f[j, 1]], axis=1).astype(BF16)
            x, (q, v, gout, z0, z1) = project(x, w_in)
        tail_args = (x, mods[l], g2, moe_w_router[l], moe_b_router[l], geom)
        if kind == 0:
            x1, h2c, idx, gw, rank, counts = _mixer_out("conv", (b, cu), conv_w_out[j].astype(BF16), *tail_args,
                                         extra=(conv_k[j],))
        elif kind == 1:
            o = _na_context(qp, kp, vp, batch, seq, t, dec_seq)
            o = _na_latent(qs, ks, vs, cache_na_k[:, j], cache_na_v[:, j], na_rpb[j], o, dec_batch, dec_seq)
            new_k.append(kp.reshape(batch, seq, NA_HEADS, d // NA_HEADS))
            new_v.append(vp.reshape(batch, seq, NA_HEADS, d // NA_HEADS))
            x1, h2c, idx, gw, rank, counts = _mixer_out("plain", (o.reshape(t, d),),
                                         na_w_o[j].astype(BF16), *tail_args)
        else:
            lb = lower_bounds[l]
            o, s_p = _hgrn_scan(q, v, z0, z1, lb, 0, batch, seq, HG_HEADS_SHORT, want_state=True)
            s0_t = jnp.swapaxes(state_hgrn[:, j].astype(F32), -1, -2)
            o, _ = _hgrn_scan(q, v, z0, z1, lb, n_prompt, dec_batch, dec_seq, HG_HEADS_LONG, s0_t=s0_t, out_buf=o)
            new_s.append(s_p)
            x1, h2c, idx, gw, rank, counts = _mixer_out("hgrn", (o, gout),
                                         hg_w_o[j].astype(BF16), *tail_args, extra=(hg_norm[j].reshape(1, d),))
        sc_index, block_expert, n_used, n_rows = _route(idx, rank, counts, t)
        x_rows = _sc_dispatch(h2c, sc_index, t, n_rows)
        y_rows = _moe_experts(x_rows, block_expert, n_used, l, moe_w1, moe_b1, moe_w2, moe_b2)
        y_slots = _sc_collect(y_rows, sc_index, t, n_rows)
        pending = (y_slots, gw, x1, mods[l])

    final_g = final_norm.reshape(1, d)
    y_prompt = _combine_final(*pending, geom, final_g, (0, n_prompt)).reshape(batch, seq, d)
    y_sample = _combine_final(*pending, geom, final_g, (n_prompt, t - n_prompt)).reshape(dec_batch, dec_seq, d)
    new_na_k = jnp.stack(new_k, axis=1).astype(x_prompt.dtype)
    new_na_v = jnp.stack(new_v, axis=1).astype(x_prompt.dtype)
    new_hgrn_state = jnp.stack(new_s, axis=1).astype(x_prompt.dtype)
    return (y_prompt, y_sample, new_na_k, new_na_v, new_hgrn_state)
```

```python
import functools

import numpy as np
import jax
import jax.numpy as jnp
from jax import lax
from jax.experimental import pallas as pl
from jax.experimental.pallas import tpu as pltpu
from jax.experimental.pallas import tpu_sc as plsc

F32 = jnp.float32
BF16 = jnp.bfloat16
HIGHEST = lax.Precision.HIGHEST

N_MIXERS = 3
GRID_W = 64
CONV_WIDTH = 3
NA_HEADS = 16
NA_ROWS_MAX = 8
NA_COLS = 16
HG_HEADS = 8
HG_KEY_DIM = 128
HG_CHUNK = 32
N_EXPERTS = 32
TOP_K = 4
SWIGLU_LIMIT = 7.0
SWIGLU_ALPHA = 1.702
EPS = 1e-6

LANES = 128
SUBLANES = 8
VMEM_LIMIT_CAP = 60 * 1024 * 1024

NEG_BIG = -1e30
TOKEN_TILE = 256
MOE_ROWS = 512
HG_TILE = 128
HG_HEADS_SHORT = 4
HG_HEADS_LONG = 2
PACKED_CHUNKS = 2
CHUNK_W = 256
SC_WINDOW = 128


def _params(sem, vmem_mb, flags=None):
    return pltpu.CompilerParams(dimension_semantics=sem, flags=flags,
                                vmem_limit_bytes=min(vmem_mb * 1024 * 1024, VMEM_LIMIT_CAP))


def _rms(x):
    return x * lax.rsqrt(jnp.mean(x * x, axis=-1, keepdims=True) + EPS)


def _modulate(x, g, shift, scale):
    return (_rms(x) * g) * (1.0 + scale) + shift


def _bf16_part(x):
    bits = lax.bitcast_convert_type(x, jnp.uint32) & jnp.uint32(0xFFFF0000)
    return lax.bitcast_convert_type(bits, F32)


def _pack_bf16_pairs(x):
    n = x.shape[1]
    bits = lax.bitcast_convert_type(x, jnp.uint32)
    top = (bits + (jnp.uint32(0x7FFF) + ((bits >> 16) & jnp.uint32(1)))) & jnp.uint32(0xFFFF0000)
    return lax.bitcast_convert_type(top[:, :n // 2] | (top[:, n // 2:] >> 16), jnp.int32)


def _unpack_bf16_pairs(words):
    bits = lax.bitcast_convert_type(words, jnp.uint32)
    return (lax.bitcast_convert_type(bits & jnp.uint32(0xFFFF0000), F32),
            lax.bitcast_convert_type(bits << 16, F32))


def _group_of_tile(i, tile, n_prompt, dec_seq):
    start = i * tile
    return jnp.where(start < n_prompt, 0, 1 + (start - n_prompt) // dec_seq)


def _adaln_kernel(c_ref, w_ref, b_ref, o_ref):
    cv = c_ref[...]
    s = cv * jax.nn.sigmoid(cv)
    o_ref[...] = jnp.dot(s, w_ref[...], precision=HIGHEST, preferred_element_type=F32) + b_ref[...]


def _adaln_all(cvecs, w_mod, b_mod):
    depth, d, n = w_mod.shape
    tn = 2048
    return pl.pallas_call(
        _adaln_kernel,
        grid=(depth, n // tn),
        in_specs=[pl.BlockSpec((SUBLANES, d), lambda l, j: (0, 0)),
                  pl.BlockSpec((None, d, tn), lambda l, j: (l, 0, j)),
                  pl.BlockSpec((None, 1, tn), lambda l, j: (l, 0, j))],
        out_specs=pl.BlockSpec((None, SUBLANES, tn), lambda l, j: (l, 0, j)),
        out_shape=jax.ShapeDtypeStruct((depth, SUBLANES, n), F32),
        compiler_params=_params(("arbitrary", "arbitrary"), 40),
        name="adaln",
    )(cvecs, w_mod, b_mod.reshape(depth, 1, n))


def _project(x, mod, g_ref, w_ref, out_refs, d, conv):
    h = _modulate(x, g_ref[...], mod[:, 0:d], mod[:, d:2 * d]).astype(BF16)
    if conv:
        b_ref, cu_ref = out_refs
        b_ref[...] = jnp.dot(h, w_ref[:, 0:d], preferred_element_type=F32)
        c = jnp.dot(h, w_ref[:, d:2 * d], preferred_element_type=F32)
        u = jnp.dot(h, w_ref[:, 2 * d:3 * d], preferred_element_type=F32)
        cu_ref[...] = c * u
    else:
        for j, o_ref in enumerate(out_refs):
            o_ref[...] = jnp.dot(h, w_ref[:, j * d:(j + 1) * d], preferred_element_type=F32)


def _in_kernel(x_ref, mod_ref, g_ref, w_ref, *out_refs, d, conv):
    _project(x_ref[...], mod_ref[...], g_ref, w_ref, out_refs, d, conv)


def _mixer_in(x, mods_l, g, w, geom, conv=False, rows=None):
    d = x.shape[1]
    row0, t = rows if rows is not None else (0, x.shape[0])
    n = w.shape[1]
    n_out = 2 if conv else n // d
    tm = TOKEN_TILE
    t0 = row0 // tm
    grp = functools.partial(_group_of_tile, tile=tm, n_prompt=geom[0], dec_seq=geom[1])
    return pl.pallas_call(
        functools.partial(_in_kernel, d=d, conv=conv),
        grid=(t // tm,),
        in_specs=[pl.BlockSpec((tm, d), lambda i: (t0 + i, 0)),
                  pl.BlockSpec((None, 1, 6 * d), lambda i: (grp(t0 + i), 0, 0)),
                  pl.BlockSpec((1, d), lambda i: (0, 0)),
                  pl.BlockSpec((d, n), lambda i: (0, 0))],
        out_specs=[pl.BlockSpec((tm, d), lambda i: (i, 0))] * n_out,
        out_shape=[jax.ShapeDtypeStruct((t, d), F32)] * n_out,
        compiler_params=_params(("arbitrary",), 48),
        name="mixer_in",
    )(x, mods_l, g, w)


def _tail(m, x_ref, mod, g2_ref, wr_ref, br_ref, outs, run_ref, d):
    x1_ref, h2_ref, idx_ref, gw_ref, rank_ref, cnt_ref = outs
    x1 = x_ref[...] + mod[:, 2 * d:3 * d] * m
    x1_ref[...] = x1
    h2 = _modulate(x1, g2_ref[...], mod[:, 3 * d:4 * d], mod[:, 4 * d:5 * d])
    words = _pack_bf16_pairs(h2)
    for j in range(PACKED_CHUNKS):
        h2_ref[j] = words[:, j * CHUNK_W:(j + 1) * CHUNK_W]
    h2_top = _bf16_part(h2)
    h2_hi = h2_top.astype(BF16)
    h2_lo = (h2 - h2_top).astype(BF16)
    logits = (jnp.dot(h2_hi, wr_ref[0], preferred_element_type=F32)
              + jnp.dot(h2_lo, wr_ref[0], preferred_element_type=F32)
              + jnp.dot(h2_hi, wr_ref[1], preferred_element_type=F32)) + br_ref[...]
    lane = lax.broadcasted_iota(jnp.int32, logits.shape, 1)
    vals, idxs = [], []
    for _ in range(TOP_K):
        mx = jnp.max(logits, axis=-1, keepdims=True)
        ix = jnp.argmax(logits, axis=-1, keepdims=True).astype(jnp.int32)
        vals.append(mx)
        idxs.append(ix)
        logits = jnp.where(lane == ix, NEG_BIG, logits)
    es = [jnp.exp(v - vals[0]) for v in vals]
    den = es[0] + es[1] + es[2] + es[3]

    @pl.when(pl.program_id(0) == 0)
    def _():
        run_ref[...] = jnp.zeros(run_ref.shape, F32)

    tm = lane.shape[0]
    earlier = (lax.broadcasted_iota(jnp.int32, (tm, tm), 0)
               > lax.broadcasted_iota(jnp.int32, (tm, tm), 1)).astype(BF16)
    seen = run_ref[...]
    idx_out = jnp.zeros(lane.shape, jnp.int32)
    gw_out = jnp.zeros(lane.shape, F32)
    rank_out = jnp.zeros(lane.shape, jnp.int32)
    for k in range(TOP_K):
        hit = lane == idxs[k]
        before = jnp.dot(earlier, hit.astype(BF16), preferred_element_type=F32) + seen
        rank_k = jnp.sum(jnp.where(hit, before, 0.0), axis=-1, keepdims=True)
        seen = seen + jnp.sum(hit.astype(F32), axis=0, keepdims=True)
        idx_out = jnp.where(lane == k, idxs[k], idx_out)
        gw_out = jnp.where(lane == k, es[k] / den, gw_out)
        rank_out = jnp.where(lane == k, rank_k.astype(jnp.int32), rank_out)
    run_ref[...] = seen
    cnt_ref[...] = seen
    idx_ref[...] = idx_out
    gw_ref[...] = gw_out
    rank_ref[...] = rank_out


def _plain_out_kernel(a_ref, w_ref, x_ref, mod_ref, g2_ref, wr_ref, br_ref, *rest, d):
    m = jnp.dot(a_ref[...].astype(BF16), w_ref[...], preferred_element_type=F32)
    _tail(m, x_ref, mod_ref[...], g2_ref, wr_ref, br_ref, rest[:-1], rest[-1], d)


def _conv_out_kernel(b_ref, cu_ref, prev_ref, next_ref, ck_ref, w_ref, x_ref, mod_ref, g2_ref, wr_ref,
                     br_ref, *rest, d, tm, n_prompt, seq, dec_seq):
    i = pl.program_id(0)
    cu = cu_ref[...]
    row = lax.broadcasted_iota(jnp.int32, (tm, 1), 0)
    start = i * tm
    in_prompt = start < n_prompt
    seq_len = jnp.where(in_prompt, seq, dec_seq)
    pos = lax.rem(jnp.where(in_prompt, start, start - n_prompt), seq_len) + row
    prev = jnp.where(row == 0, prev_ref[SUBLANES - 1:SUBLANES, :], pltpu.roll(cu, 1, 0))
    prev = jnp.where(pos == 0, 0.0, prev)
    nxt = jnp.where(row == tm - 1, next_ref[0:1, :], pltpu.roll(cu, tm - 1, 0))
    nxt = jnp.where(pos == seq_len - 1, 0.0, nxt)
    conv = prev * ck_ref[0:1, :] + cu * ck_ref[1:2, :] + nxt * ck_ref[2:3, :]
    m = jnp.dot((b_ref[...] * conv).astype(BF16), w_ref[...], preferred_element_type=F32)
    _tail(m, x_ref, mod_ref[...], g2_ref, wr_ref, br_ref, rest[:-1], rest[-1], d)


def _hgrn_out_kernel(o_ref, gout_ref, ng_ref, w_ref, x_ref, mod_ref, g2_ref, wr_ref, br_ref, *rest, d):
    parts = []
    for h in range(HG_HEADS):
        oh = o_ref[:, h * LANES:(h + 1) * LANES]
        parts.append(oh * lax.rsqrt(jnp.mean(oh * oh, axis=-1, keepdims=True) + EPS))
    gout = gout_ref[...]
    y = (jnp.concatenate(parts, axis=1) * ng_ref[...]) * (gout * jax.nn.sigmoid(gout))
    m = jnp.dot(y.astype(BF16), w_ref[...], preferred_element_type=F32)
    _tail(m, x_ref, mod_ref[...], g2_ref, wr_ref, br_ref, rest[:-1], rest[-1], d)


def _mixer_out(kind, acts, w_out, x, mods_l, g2, w_router, b_router, geom, extra=()):
    t, d = x.shape
    tm = TOKEN_TILE
    n_prompt, dec_seq, seq = geom
    grp = functools.partial(_group_of_tile, tile=tm, n_prompt=n_prompt, dec_seq=dec_seq)
    row_spec = pl.BlockSpec((tm, d), lambda i: (i, 0))
    wr = jnp.zeros((d, LANES), F32).at[:, :N_EXPERTS].set(w_router)
    wr_top = _bf16_part(wr)
    wr = jnp.stack([wr_top.astype(BF16), (wr - wr_top).astype(BF16)])
    br = jnp.full((1, LANES), NEG_BIG, F32).at[0, :N_EXPERTS].set(b_router)
    common_specs = [pl.BlockSpec((d, d), lambda i: (0, 0)),
                    row_spec,
                    pl.BlockSpec((None, 1, 6 * d), lambda i: (grp(i), 0, 0)),
                    pl.BlockSpec((1, d), lambda i: (0, 0)),
                    pl.BlockSpec((2, d, LANES), lambda i: (0, 0, 0)),
                    pl.BlockSpec((1, LANES), lambda i: (0, 0))]
    common_args = [w_out, x, mods_l, g2, wr, br]
    if kind == "plain":
        body = functools.partial(_plain_out_kernel, d=d)
        specs = [row_spec] + common_specs
        args = list(acts) + common_args
    elif kind == "conv":
        body = functools.partial(_conv_out_kernel, d=d, tm=tm, n_prompt=n_prompt, seq=seq, dec_seq=dec_seq)
        per = tm // SUBLANES
        last = t // SUBLANES - 1
        specs = [row_spec, row_spec,
                 pl.BlockSpec((SUBLANES, d), lambda i: (jnp.maximum(i * per - 1, 0), 0)),
                 pl.BlockSpec((SUBLANES, d), lambda i: (jnp.minimum((i + 1) * per, last), 0)),
                 pl.BlockSpec((CONV_WIDTH, d), lambda i: (0, 0))] + common_specs
        b, cu = acts
        args = [b, cu, cu, cu, extra[0]] + common_args
    else:
        body = functools.partial(_hgrn_out_kernel, d=d)
        specs = [row_spec, row_spec, pl.BlockSpec((1, d), lambda i: (0, 0))] + common_specs
        args = list(acts) + [extra[0]] + common_args
    lane_spec = pl.BlockSpec((tm, LANES), lambda i: (i, 0))
    return pl.pallas_call(
        body,
        grid=(t // tm,),
        in_specs=specs,
        out_specs=[row_spec, pl.BlockSpec((PACKED_CHUNKS, tm, CHUNK_W), lambda i: (0, i, 0)),
                   lane_spec, lane_spec, lane_spec, pl.BlockSpec((1, LANES), lambda i: (0, 0))],
        out_shape=[jax.ShapeDtypeStruct((t, d), F32), jax.ShapeDtypeStruct((PACKED_CHUNKS, t, CHUNK_W), jnp.int32),
                   jax.ShapeDtypeStruct((t, LANES), jnp.int32), jax.ShapeDtypeStruct((t, LANES), F32),
                   jax.ShapeDtypeStruct((t, LANES), jnp.int32), jax.ShapeDtypeStruct((1, LANES), F32)],
        scratch_shapes=[pltpu.VMEM((1, LANES), F32)],
        compiler_params=_params(("arbitrary",), 40),
        name="mixer_out_" + kind,
    )(*args)


def _softmax_rows(parts):
    mx = functools.reduce(jnp.maximum, [jnp.max(s, axis=-1, keepdims=True) for s in parts])
    es = [jnp.exp(s - mx) for s in parts]
    den = functools.reduce(lambda a, b: a + b, [jnp.sum(e, axis=-1, keepdims=True) for e in es])
    return [e / den for e in es]


def _head_pair_queries(q, dh):
    first = lax.broadcasted_iota(jnp.int32, (1, q.shape[1]), 1) < dh
    return first, (jnp.where(first, q, 0.0).astype(BF16), jnp.where(first, 0.0, q).astype(BF16))


def _na_ctx_kernel(q_ref, k_ref, v_ref, o_ref, *, dh, scale):
    nt = (((1,), (1,)), ((), ()))
    n_q = q_ref.shape[0]
    first = lax.broadcasted_iota(jnp.int32, (1, LANES), 1) < dh
    groups = [slice(pair * LANES, (pair + 1) * LANES) for pair in range(NA_HEADS * dh // LANES)]
    scores = []
    for sl in groups:
        _, queries = _head_pair_queries(q_ref[:, sl], dh)
        q2 = jnp.concatenate(queries, axis=0)
        scores.append(lax.dot_general(q2, k_ref[:, sl].astype(BF16), nt, preferred_element_type=F32))
    for sl, s in zip(groups, scores):
        (p,) = _softmax_rows([s * scale])
        o2 = jnp.dot(p.astype(BF16), v_ref[:, sl].astype(BF16), preferred_element_type=F32)
        o_ref[:, sl] = jnp.where(first, o2[:n_q], o2[n_q:])


def _na_context(q, k, v, batch, seq, t_total, dec_seq):
    d = q.shape[1]
    dh = d // NA_HEADS
    per = dec_seq // seq
    spec = pl.BlockSpec((seq, d), lambda b: (b, 0))
    return pl.pallas_call(
        functools.partial(_na_ctx_kernel, dh=dh, scale=dh ** -0.5),
        grid=(batch,),
        in_specs=[spec, spec, spec],
        out_specs=pl.BlockSpec((None, seq, d), lambda b: (b // per, b % per, 0)),
        out_shape=jax.ShapeDtypeStruct((t_total // dec_seq, dec_seq, d), F32),
        compiler_params=_params(("arbitrary",), 32),
        name="na_context",
    )(q, k, v)


def _na_lat_kernel(*refs, dh, scale, kr, n_batch):
    per = 1 + 2 * kr
    ck_ref, cv_ref, bias_ref, _, o_ref = refs[n_batch * per:]
    nt = (((1,), (1,)), ((), ()))
    heads_per_group = LANES // dh
    n_q = refs[0].shape[0]
    jobs = [(pair, b) for pair in range(NA_HEADS // heads_per_group) for b in range(n_batch)]
    scores = []
    for pair, b in jobs:
        sl = slice(pair * LANES, (pair + 1) * LANES)
        k_refs = refs[b * per + 1:b * per + 1 + kr]
        _, queries = _head_pair_queries(refs[b * per][:, sl], dh)
        q2 = jnp.concatenate(queries, axis=0)
        kw = jnp.concatenate([r[:, sl] for r in k_refs], axis=0).astype(BF16)
        scores.append((lax.dot_general(q2, kw, nt, preferred_element_type=F32),
                       lax.dot_general(q2, ck_ref[b, :, sl].astype(BF16), nt, preferred_element_type=F32)))
    for (pair, b), (s_nb, s_cx) in zip(jobs, scores):
        sl = slice(pair * LANES, (pair + 1) * LANES)
        v_refs = refs[b * per + 1 + kr:(b + 1) * per]
        first = lax.broadcasted_iota(jnp.int32, (1, LANES), 1) < dh
        bias2 = jnp.concatenate([bias_ref[pair * heads_per_group + j] for j in range(heads_per_group)], axis=0)
        vw = jnp.concatenate([r[:, sl] for r in v_refs], axis=0).astype(BF16)
        p_nb, p_cx = _softmax_rows([s_nb * scale + bias2, s_cx * scale])
        o2 = (jnp.dot(p_nb.astype(BF16), vw, preferred_element_type=F32)
              + jnp.dot(p_cx.astype(BF16), cv_ref[b, :, sl].astype(BF16), preferred_element_type=F32))
        o_ref[b, :, sl] = jnp.where(first, o2[:n_q], o2[n_q:])


def _na_bias_table(rpb, kr):
    col = np.arange(GRID_W)
    col_start = np.clip(col - NA_COLS // 2, 0, GRID_W - NA_COLS)
    delta = col[None, :] - col[:, None] + (NA_COLS - 1)
    inside = (col[None, :] >= col_start[:, None]) & (col[None, :] < col_start[:, None] + NA_COLS)
    pick = (np.arange(2 * NA_COLS - 1)[:, None, None] == delta[None]).astype(np.float32)
    tab = jnp.einsum("hrk,kwc->hrwc", rpb, jnp.asarray(pick), precision=HIGHEST)
    tab = jnp.where(jnp.asarray(inside)[None, None], tab, NEG_BIG)
    out = []
    for d0 in range(NA_ROWS_MAX):
        rows = tab[:, d0:d0 + kr]
        out.append(rows.transpose(0, 2, 1, 3).reshape(NA_HEADS, GRID_W, kr * GRID_W))
    return jnp.stack(out, axis=0)


def _na_latent(q, k, v, ctx_k, ctx_v, rpb, out_buf, dec_batch, dec_seq):
    d = q.shape[1]
    dh = d // NA_HEADS
    rows = dec_seq // GRID_W
    kr = min(NA_ROWS_MAX, rows)
    base = 0
    past = ctx_k.shape[1]
    out_slab = (out_buf.shape[0] - dec_batch) // dec_batch
    bias = _na_bias_table(rpb, kr)

    def row_start(r):
        return jnp.clip(r - kr // 2, 0, rows - kr)

    def win_spec(b, j):
        return pl.BlockSpec((GRID_W, d), lambda r: (base + b * rows + row_start(r) + j, 0))

    in_specs, args = [], []
    for b in range(dec_batch):
        in_specs += ([pl.BlockSpec((GRID_W, d), lambda r, b=b: (base + b * rows + r, 0))]
                     + [win_spec(b, j) for j in range(kr)] * 2)
        args += [q] + [k] * kr + [v] * kr
    in_specs += ([pl.BlockSpec((dec_batch, past, d), lambda r: (0, 0, 0))] * 2
                 + [pl.BlockSpec((None, NA_HEADS, GRID_W, kr * GRID_W),
                                 lambda r: (row_start(r) - r + NA_ROWS_MAX - 1, 0, 0, 0))])
    in_specs.append(pl.BlockSpec(memory_space=pl.ANY))
    args += [ctx_k.reshape(dec_batch, past, d), ctx_v.reshape(dec_batch, past, d), bias, out_buf]
    return pl.pallas_call(
        functools.partial(_na_lat_kernel, dh=dh, scale=dh ** -0.5, kr=kr, n_batch=dec_batch),
        grid=(rows,),
        in_specs=in_specs,
        out_specs=pl.BlockSpec((dec_batch, GRID_W, d), lambda r: (out_slab, r, 0)),
        out_shape=jax.ShapeDtypeStruct(out_buf.shape, F32),
        input_output_aliases={len(args) - 1: 0},
        compiler_params=_params(("arbitrary",), 40),
        name="na_latent",
    )(*args)


def _hgrn_scan_kernel(*refs, seq_len, heads, has_s0, has_buf, want_state):
    q_ref, v_ref, z0_ref, z1_ref, lb_ref, tri_ref = refs[:6]
    rest = list(refs[6:])
    s0_ref = rest.pop(0) if has_s0 else None
    if has_buf:
        rest.pop(0)
    o_ref = rest.pop(0)
    sf_ref = rest.pop(0) if want_state else None
    ob_ref = rest.pop(0)
    c, r = HG_CHUNK, HG_TILE
    nc, nt = r // c, seq_len // r
    dk = HG_KEY_DIM
    tpos8 = lax.broadcasted_iota(jnp.int32, (1, SUBLANES, 1), 1)
    row = lax.broadcasted_iota(jnp.int32, (r, r), 0)
    col = lax.broadcasted_iota(jnp.int32, (r, r), 1)
    same16 = (row // (2 * SUBLANES)) == (col // (2 * SUBLANES))
    same32 = (row // c) == (col // c)
    nt_dims = (((1,), (1,)), ((), ()))
    tn_dims = (((0,), (0,)), ((), ()))

    def tile(dr, hh, i, st):
        z_ref = (z0_ref, z1_ref)[dr]
        cols = slice(hh * dk, (hh + 1) * dk)
        lb = lb_ref[dr:dr + 1, cols]
        tri = tri_ref[dr]
        ti = i if dr == 0 else nt - 1 - i
        r0 = pl.multiple_of(ti * r, r)
        q = q_ref[pl.ds(r0, r), cols]
        v = v_ref[pl.ds(r0, r), cols]
        z = z_ref[pl.ds(r0, r), cols]
        g = jnp.log(lb + (1.0 - lb) * jax.nn.sigmoid(z))
        k = (1.0 - lb) * jax.nn.sigmoid(-z)
        g_hi = _bf16_part(g)
        g_mid = _bf16_part(g - g_hi)
        g_lo = (g - g_hi) - g_mid
        b = (jnp.dot(tri, g_hi.astype(BF16), preferred_element_type=F32)
             + jnp.dot(tri, g_mid.astype(BF16), preferred_element_type=F32)
             + jnp.dot(tri, g_lo.astype(BF16), preferred_element_type=F32))
        b3, q3, k3, v3 = (a.reshape(nc, c, dk) for a in (b, q, k, v))
        nb = r // SUBLANES
        b8, q8, k8, v8 = (a.reshape(nb, SUBLANES, dk) for a in (b, q, k, v))
        oi = jnp.zeros((nb, SUBLANES, dk), F32)
        for s in range(SUBLANES):
            keep = (tpos8 >= s) if dr == 0 else (tpos8 <= s)
            e = jnp.exp(jnp.where(keep, b8 - b8[:, s:s + 1, :], NEG_BIG))
            a = jnp.sum(q8 * e * k8[:, s:s + 1, :], axis=-1, keepdims=True)
            oi = oi + a * v8[:, s:s + 1, :]
        att = None
        for span, same_group in ((2 * SUBLANES, same16), (c, same32)):
            half = span // 2
            bg = b.reshape(r // span, span, dk)
            pos = lax.broadcasted_iota(jnp.int32, (1, span, 1), 1)
            edge = bg[:, half - 1:half, :] if dr == 0 else bg[:, half:half + 1, :]
            is_query = (pos >= half) if dr == 0 else (pos < half)
            qf = q.reshape(bg.shape) * jnp.exp(jnp.where(is_query, bg - edge, NEG_BIG))
            kf = k.reshape(bg.shape) * jnp.exp(jnp.where(is_query, NEG_BIG, edge - bg))
            part = lax.dot_general(qf.reshape(r, dk).astype(BF16), kf.reshape(r, dk).astype(BF16), nt_dims,
                                   preferred_element_type=F32)
            part = jnp.where(same_group, part, 0.0)
            att = part if att is None else att + part
        oi = oi.reshape(r, dk) + jnp.dot(att.astype(BF16), v.astype(BF16), preferred_element_type=F32)
        oi = oi.reshape(nc, c, dk)
        bl = b3[:, c - 1:c, :] if dr == 0 else b3[:, 0:1, :]
        qt = (q3 * jnp.exp(b3)).astype(BF16)
        kt = (k3 * jnp.exp(bl - b3)).astype(BF16)
        vb = v3.astype(BF16)
        outs = [None] * nc
        for ci in (range(nc) if dr == 0 else range(nc - 1, -1, -1)):
            oc = lax.dot_general(qt[ci], st.astype(BF16), nt_dims, preferred_element_type=F32)
            outs[ci] = oi[ci] + oc
            kv = lax.dot_general(vb[ci], kt[ci], tn_dims, preferred_element_type=F32)
            st = st * jnp.exp(bl[ci]) + kv
        return r0, jnp.concatenate(outs, axis=0), st

    chains = [(dr, hh) for hh in range(heads) for dr in range(2)]

    def step(i, carry):
        new = []
        for (dr, hh), st in zip(chains, carry):
            r0, o, st = tile(dr, hh, i, st)
            (o_ref if dr == 0 else ob_ref)[pl.ds(r0, r), hh * dk:(hh + 1) * dk] = o
            new.append(st)
        return tuple(new)

    init = tuple(s0_ref[dr, hh] if has_s0 else jnp.zeros((dk, dk), F32) for dr, hh in chains)
    final = lax.fori_loop(0, nt, step, init)
    o_ref[...] += ob_ref[...]
    if want_state:
        for (dr, hh), st in zip(chains, final):
            sf_ref[dr, hh] = st.T


def _hgrn_tri():
    t = np.arange(HG_TILE)
    same = (t[:, None] // HG_CHUNK) == (t[None, :] // HG_CHUNK)
    fwd = same & (t[None, :] <= t[:, None])
    bwd = same & (t[None, :] >= t[:, None])
    return jnp.asarray(np.stack([fwd, bwd]).astype(np.float32)).astype(BF16)


def _hgrn_scan(q, v, z0, z1, lb, row0, n_seq, seq_len, hp, s0_t=None, want_state=False, out_buf=None):
    t_total, d = q.shape
    dk = HG_KEY_DIM
    blk0 = row0 // seq_len
    seq_spec = pl.BlockSpec((seq_len, hp * dk), lambda s, h: (blk0 + s, h))
    st_spec = pl.BlockSpec((None, 2, hp, dk, dk), lambda s, h: (s, 0, h, 0, 0))
    in_specs = [seq_spec] * 4 + [pl.BlockSpec((2, hp * dk), lambda s, h: (0, h)),
                                 pl.BlockSpec((2, HG_TILE, HG_TILE), lambda s, h: (0, 0, 0))]
    args = [q, v, z0, z1, lb, _hgrn_tri()]
    if s0_t is not None:
        in_specs.append(st_spec)
        args.append(s0_t)
    aliases = {}
    if out_buf is not None:
        in_specs.append(pl.BlockSpec(memory_space=pl.ANY))
        args.append(out_buf)
        aliases = {len(args) - 1: 0}
    out_specs = [seq_spec]
    out_shape = [jax.ShapeDtypeStruct((t_total, d), F32)]
    if want_state:
        out_specs.append(st_spec)
        out_shape.append(jax.ShapeDtypeStruct((n_seq, 2, HG_HEADS, dk, dk), F32))
    res = pl.pallas_call(
        functools.partial(_hgrn_scan_kernel, seq_len=seq_len, heads=hp, has_s0=s0_t is not None,
                          has_buf=out_buf is not None, want_state=want_state),
        grid=(n_seq, HG_HEADS // hp),
        in_specs=in_specs,
        out_specs=out_specs,
        out_shape=out_shape,
        input_output_aliases=aliases,
        scratch_shapes=[pltpu.VMEM((seq_len, hp * dk), F32)],
        compiler_params=_params(("arbitrary", "arbitrary"), 32),
        name="hgrn_scan",
    )(*args)
    return res if want_state else (res[0], None)


def _route(idx, rank, counts, t):
    n = t * TOP_K
    rb = MOE_ROWS
    n_blocks = -(-n // rb) + N_EXPERTS
    n_rows = (n_blocks + N_EXPERTS) * rb
    experts = jnp.arange(N_EXPERTS, dtype=jnp.int32)
    counts = counts[0, :N_EXPERTS].astype(jnp.int32)
    padded = (counts + rb - 1) // rb * rb
    padded_end = jnp.cumsum(padded)
    start = padded_end - padded
    e = idx[:, :TOP_K]
    dest = jnp.sum(jnp.where(e[:, :, None] == experts, start, 0), axis=-1) + rank[:, :TOP_K]
    dest_km = dest.T.astype(jnp.int32)
    q = jnp.arange(rb, dtype=jnp.int32)
    spare = n_blocks * rb + experts[:, None] * rb + q[None, :]
    pad_rows = jnp.where(q[None, :] < (padded - counts)[:, None], (start + counts)[:, None] + q[None, :], spare)
    per_chunk = jnp.concatenate([dest_km.reshape(-1), pad_rows.reshape(-1)])
    sc_index = (jnp.arange(PACKED_CHUNKS, dtype=jnp.int32)[:, None] * n_rows + per_chunk[None, :]).reshape(1, -1)
    block_start = jnp.arange(n_blocks, dtype=jnp.int32) * rb
    block_expert = jnp.minimum(jnp.sum((padded_end[None, :] <= block_start[:, None]).astype(jnp.int32), axis=1),
                               N_EXPERTS - 1)
    n_used = (padded_end[-1] // rb).reshape(1)
    return sc_index, block_expert, n_used, n_rows


def _sc_mesh():
    return plsc.VectorSubcoreMesh(core_axis_name="c", subcore_axis_name="s")


def _sc_dispatch(h2c, sc_index, t, n_rows):
    win = SC_WINDOW
    tw = t // win
    nw = (t * TOP_K + N_EXPERTS * MOE_ROWS) // win
    n_chunks = h2c.shape[0]

    def scatter(x_hbm, i_hbm, o_hbm):
        def body(x_vmem, i_vmem):
            pltpu.sync_copy(x_vmem, o_hbm.at[i_vmem.at[0]])

        pltpu.emit_pipeline(
            body,
            grid=(n_chunks * nw,),
            in_specs=[pl.BlockSpec((win, CHUNK_W), index_map=lambda i: ((i // nw) * tw + (i % nw) % tw, 0)),
                      pl.BlockSpec((1, win), index_map=lambda i: (0, i))],
            out_specs=[],
            core_axis_name=("c", "s"),
            dimension_semantics=(pltpu.PARALLEL,),
        )(x_hbm, i_hbm)

    out = pl.kernel(scatter, out_type=jax.ShapeDtypeStruct((n_chunks * n_rows, CHUNK_W), h2c.dtype),
                    mesh=_sc_mesh(), name="moe_dispatch")(h2c.reshape(n_chunks * t, CHUNK_W), sc_index)
    return out.reshape(n_chunks, n_rows, CHUNK_W)


def _sc_collect(y_rows, sc_index, t, n_rows):
    win = SC_WINDOW
    n_chunks = y_rows.shape[0]
    nw = (t * TOP_K + N_EXPERTS * MOE_ROWS) // win
    aw = t * TOP_K // win

    def gather(y_hbm, i_hbm, o_hbm):
        def body(i_vmem, o_vmem):
            pltpu.sync_copy(y_hbm.at[i_vmem.at[0]], o_vmem)

        pltpu.emit_pipeline(
            body,
            grid=(n_chunks * aw,),
            in_specs=[pl.BlockSpec((1, win), index_map=lambda i: (0, (i // aw) * nw + i % aw))],
            out_specs=[pl.BlockSpec((win, CHUNK_W), index_map=lambda i: (i, 0))],
            core_axis_name=("c", "s"),
            dimension_semantics=(pltpu.PARALLEL,),
        )(i_hbm, o_hbm)

    out = pl.kernel(gather, out_type=jax.ShapeDtypeStruct((n_chunks * TOP_K * t, CHUNK_W), y_rows.dtype),
                    mesh=_sc_mesh(), name="moe_collect")(y_rows.reshape(n_chunks * n_rows, CHUNK_W), sc_index)
    return out.reshape(n_chunks, TOP_K, t, CHUNK_W)


def _moe_kernel(be_ref, nu_ref, x_ref, w1_ref, b1_ref, w2_ref, b2_ref, o_ref, w1b, w2b, *, dff):
    i = pl.program_id(0)
    e = be_ref[i]
    prev = be_ref[jnp.maximum(i - 1, 0)]

    @pl.when(jnp.logical_or(i == 0, e != prev))
    def _():
        w1b[...] = w1_ref[...].astype(BF16)
        w2b[...] = w2_ref[...].astype(BF16)

    @pl.when(i < nu_ref[0])
    def _():
        halves = [_unpack_bf16_pairs(x_ref[j]) for j in range(PACKED_CHUNKS)]
        xb = jnp.concatenate([h[0] for h in halves] + [h[1] for h in halves], axis=1).astype(BF16)
        gu = jnp.dot(xb, w1b[...], preferred_element_type=F32) + b1_ref[...]
        g = jnp.minimum(gu[:, :dff], SWIGLU_LIMIT)
        u = jnp.clip(gu[:, dff:], -SWIGLU_LIMIT, SWIGLU_LIMIT)
        a = (u + 1.0) * (g * jax.nn.sigmoid(SWIGLU_ALPHA * g))
        y = jnp.dot(a.astype(BF16), w2b[...], preferred_element_type=F32) + b2_ref[...]
        words = _pack_bf16_pairs(y)
        for j in range(PACKED_CHUNKS):
            o_ref[j] = words[:, j * CHUNK_W:(j + 1) * CHUNK_W]

    @pl.when(i >= nu_ref[0])
    def _():
        o_ref[...] = jnp.zeros(o_ref.shape, o_ref.dtype)


def _moe_experts(x_rows, block_expert, n_used, layer, w1, b1, w2, b2):
    _, n_rows, _ = x_rows.shape
    rb = MOE_ROWS
    n_blocks = n_rows // rb - N_EXPERTS
    depth, n_e, d, dff2 = w1.shape
    dff = dff2 // 2
    row_spec = pl.BlockSpec((PACKED_CHUNKS, rb, CHUNK_W), lambda i, be, nu: (0, jnp.minimum(i, nu[0] - 1), 0))
    grid_spec = pltpu.PrefetchScalarGridSpec(
        num_scalar_prefetch=2,
        grid=(n_blocks,),
        in_specs=[row_spec,
                  pl.BlockSpec((None, None, d, dff2), lambda i, be, nu: (layer, be[i], 0, 0)),
                  pl.BlockSpec((None, None, 1, dff2), lambda i, be, nu: (layer, be[i], 0, 0)),
                  pl.BlockSpec((None, None, dff, d), lambda i, be, nu: (layer, be[i], 0, 0)),
                  pl.BlockSpec((None, None, 1, d), lambda i, be, nu: (layer, be[i], 0, 0))],
        out_specs=pl.BlockSpec((PACKED_CHUNKS, rb, CHUNK_W), lambda i, be, nu: (0, i, 0)),
        scratch_shapes=[pltpu.VMEM((d, dff2), BF16), pltpu.VMEM((dff, d), BF16)],
    )
    return pl.pallas_call(
        functools.partial(_moe_kernel, dff=dff),
        grid_spec=grid_spec,
        out_shape=jax.ShapeDtypeStruct((PACKED_CHUNKS, n_rows, CHUNK_W), jnp.int32),
        compiler_params=_params(("arbitrary",), 56),
        name="moe_experts",
    )(block_expert, n_used, x_rows, w1, b1.reshape(depth, n_e, 1, dff2), w2, b2.reshape(depth, n_e, 1, d))


def _combined_rows(y_ref, gw_ref, x_ref, mod_ref, d):
    gw = gw_ref[...]
    pieces = {}
    for j in range(PACKED_CHUNKS):
        accs = [None, None]
        for k in range(TOP_K):
            for side, part in enumerate(_unpack_bf16_pairs(y_ref[j, k])):
                term = gw[:, k:k + 1] * part
                accs[side] = term if accs[side] is None else accs[side] + term
        for side, acc in enumerate(accs):
            c0 = side * (d // 2) + j * CHUNK_W
            pieces[c0] = x_ref[:, c0:c0 + CHUNK_W] + mod_ref[:, 5 * d + c0:5 * d + c0 + CHUNK_W] * acc
    return jnp.concatenate([pieces[c0] for c0 in sorted(pieces)], axis=1)


def _combine_final_kernel(y_ref, gw_ref, x_ref, mod_ref, g_ref, o_ref, *, d):
    o_ref[...] = _rms(_combined_rows(y_ref, gw_ref, x_ref, mod_ref, d)) * g_ref[...]


def _combine_in_kernel(y_ref, gw_ref, x_ref, mod_ref, modn_ref, g_ref, w_ref, *rest, d, conv, has_buf):
    x2_ref, *out_refs = rest[1:] if has_buf else rest
    x2 = _combined_rows(y_ref, gw_ref, x_ref, mod_ref, d)
    x2_ref[...] = x2
    _project(x2, modn_ref[...], g_ref, w_ref, out_refs, d, conv)


def _combine_specs(y_slots, gw, x1, mods_l, geom, rows):
    d = x1.shape[1]
    row0, t = rows if rows is not None else (0, x1.shape[0])
    tm = TOKEN_TILE
    t0 = row0 // tm
    grp = functools.partial(_group_of_tile, tile=tm, n_prompt=geom[0], dec_seq=geom[1])
    in_specs = [pl.BlockSpec((PACKED_CHUNKS, TOP_K, tm, CHUNK_W), lambda i: (0, 0, t0 + i, 0)),
                pl.BlockSpec((tm, LANES), lambda i: (t0 + i, 0)),
                pl.BlockSpec((tm, d), lambda i: (t0 + i, 0)),
                pl.BlockSpec((None, 1, 6 * d), lambda i: (grp(t0 + i), 0, 0))]
    return in_specs, [y_slots, gw, x1, mods_l], t, t0, grp


def _combine_final(y_slots, gw, x1, mods_l, geom, final_g, rows):
    d = x1.shape[1]
    tm = TOKEN_TILE
    in_specs, args, t, _, _ = _combine_specs(y_slots, gw, x1, mods_l, geom, rows)
    return pl.pallas_call(
        functools.partial(_combine_final_kernel, d=d),
        grid=(t // tm,),
        in_specs=in_specs + [pl.BlockSpec((1, d), lambda i: (0, 0))],
        out_specs=pl.BlockSpec((tm, d), lambda i: (i, 0)),
        out_shape=jax.ShapeDtypeStruct((t, d), F32),
        compiler_params=_params(("arbitrary",), 24),
        name="moe_combine_final",
    )(*args, final_g)


def _combine_mixer_in(y_slots, gw, x1, mods_l, mods_next, g, w, geom, conv=False, rows=None, x2_buf=None):
    t_total, d = x1.shape
    tm = TOKEN_TILE
    n = w.shape[1]
    n_out = 2 if conv else n // d
    in_specs, args, t, t0, grp = _combine_specs(y_slots, gw, x1, mods_l, geom, rows)
    in_specs += [pl.BlockSpec((None, 1, 6 * d), lambda i: (grp(t0 + i), 0, 0)),
                 pl.BlockSpec((1, d), lambda i: (0, 0)),
                 pl.BlockSpec((d, n), lambda i: (0, 0))]
    args += [mods_next, g, w]
    aliases = {}
    if x2_buf is not None:
        in_specs.append(pl.BlockSpec(memory_space=pl.ANY))
        args.append(x2_buf)
        aliases = {len(args) - 1: 0}
    res = pl.pallas_call(
        functools.partial(_combine_in_kernel, d=d, conv=conv, has_buf=x2_buf is not None),
        grid=(t // tm,),
        in_specs=in_specs,
        out_specs=[pl.BlockSpec((tm, d), lambda i: (t0 + i, 0))] + [pl.BlockSpec((tm, d), lambda i: (i, 0))] * n_out,
        out_shape=[jax.ShapeDtypeStruct((t_total, d), F32)] + [jax.ShapeDtypeStruct((t, d), F32)] * n_out,
        input_output_aliases=aliases,
        compiler_params=_params(("arbitrary",), 56),
        name="combine_mixer_in",
    )(*args)
    return res[0], res[1:]


def kernel(x_prompt, x_sample, c, c_ctx, cache_na_k, cache_na_v, state_hgrn, norm_mix, norm_ffn, w_mod, b_mod,
           conv_w_in, conv_k, conv_w_out, na_w_qkv, na_rpb, na_w_o, hg_w_qig, hg_w_f, hg_lb, hg_norm, hg_w_o,
           moe_w_router, moe_b_router, moe_w1, moe_b1, moe_w2, moe_b2, final_norm):
    batch, seq, d = x_prompt.shape
    dec_batch, dec_seq, _ = x_sample.shape
    depth = w_mod.shape[0]
    n_prompt = batch * seq
    t = n_prompt + dec_batch * dec_seq
    geom = (n_prompt, dec_seq, seq)
    assert seq % TOKEN_TILE == 0 and dec_seq % TOKEN_TILE == 0 and seq % HG_TILE == 0
    assert n_prompt % (dec_seq * dec_batch) == 0 and dec_seq % GRID_W == 0 and 1 + dec_batch <= SUBLANES
    assert d == 2 * PACKED_CHUNKS * CHUNK_W and t % SC_WINDOW == 0 and MOE_ROWS % SC_WINDOW == 0

    x = jnp.concatenate([x_prompt.reshape(n_prompt, d), x_sample.reshape(dec_batch * dec_seq, d)], axis=0)
    cvecs = jnp.zeros((SUBLANES, d), F32).at[0].set(c_ctx).at[1:1 + dec_batch].set(c)
    mods = _adaln_all(cvecs, w_mod, b_mod)[:, :1 + dec_batch].reshape(depth, 1 + dec_batch, 1, 6 * d)

    lb_soft = jax.nn.softmax(hg_lb.astype(F32), axis=0)
    lower_bounds = jnp.cumsum(lb_soft, axis=0) - lb_soft[0]

    new_k, new_v, new_s = [], [], []
    pending = None
    for l in range(depth):
        kind, j = l % N_MIXERS, l // N_MIXERS
        g1 = norm_mix[l].reshape(1, d)
        g2 = norm_ffn[l].reshape(1, d)

        def project(x, w, conv=False, rows=None, x2_buf=None):
            if pending is None:
                return x, _mixer_in(x, mods[l], g1, w, geom, conv=conv, rows=rows)
            return _combine_mixer_in(*pending, mods[l], g1, w, geom, conv=conv, rows=rows, x2_buf=x2_buf)

        if kind == 0:
            x, (b, cu) = project(x, conv_w_in[j].astype(BF16), conv=True)
        elif kind == 1:
            w_qkv = na_w_qkv[j].astype(BF16)
            x, (qp, kp, vp) = project(x, w_qkv, rows=(0, n_prompt))
            x, (qs, ks, vs) = project(x, w_qkv, rows=(n_prompt, t - n_prompt), x2_buf=x)
        else:
            w_in = jnp.concatenate([hg_w_qig[j], hg_w_f[j, 0], hg_w_f[j, 1]], axis=1).astype(BF16)
            x, (q, v, gout, z0, z1) = project(x, w_in)
        tail_args = (x, mods[l], g2, moe_w_router[l], moe_b_router[l], geom)
        if kind == 0:
            x1, h2c, idx, gw, rank, counts = _mixer_out("conv", (b, cu), conv_w_out[j].astype(BF16), *tail_args,
                                         extra=(conv_k[j],))
        elif kind == 1:
            o = _na_context(qp, kp, vp, batch, seq, t, dec_seq)
            o = _na_latent(qs, ks, vs, cache_na_k[:, j], cache_na_v[:, j], na_rpb[j], o, dec_batch, dec_seq)
            new_k.append(kp.reshape(batch, seq, NA_HEADS, d // NA_HEADS))
            new_v.append(vp.reshape(batch, seq, NA_HEADS, d // NA_HEADS))
            x1, h2c, idx, gw, rank, counts = _mixer_out("plain", (o.reshape(t, d),),
                                         na_w_o[j].astype(BF16), *tail_args)
        else:
            lb = lower_bounds[l]
            o, s_p = _hgrn_scan(q, v, z0, z1, lb, 0, batch, seq, HG_HEADS_SHORT, want_state=True)
            s0_t = jnp.swapaxes(state_hgrn[:, j].astype(F32), -1, -2)
            o, _ = _hgrn_scan(q, v, z0, z1, lb, n_prompt, dec_batch, dec_seq, HG_HEADS_LONG, s0_t=s0_t, out_buf=o)
            new_s.append(s_p)
            x1, h2c, idx, gw, rank, counts = _mixer_out("hgrn", (o, gout),
                                         hg_w_o[j].astype(BF16), *tail_args, extra=(hg_norm[j].reshape(1, d),))
        sc_index, block_expert, n_used, n_rows = _route(idx, rank, counts, t)
        x_rows = _sc_dispatch(h2c, sc_index, t, n_rows)
        y_rows = _moe_experts(x_rows, block_expert, n_used, l, moe_w1, moe_b1, moe_w2, moe_b2)
        y_slots = _sc_collect(y_rows, sc_index, t, n_rows)
        pending = (y_slots, gw, x1, mods[l])

    final_g = final_norm.reshape(1, d)
    y_prompt = _combine_final(*pending, geom, final_g, (0, n_prompt)).reshape(batch, seq, d)
    y_sample = _combine_final(*pending, geom, final_g, (n_prompt, t - n_prompt)).reshape(dec_batch, dec_seq, d)
    new_na_k = jnp.stack(new_k, axis=1).astype(x_prompt.dtype)
    new_na_v = jnp.stack(new_v, axis=1).astype(x_prompt.dtype)
    new_hgrn_state = jnp.stack(new_s, axis=1).astype(x_prompt.dtype)
    return (y_prompt, y_sample, new_na_k, new_na_v, new_hgrn_state)
```

```python
import functools

import numpy as np
import jax
import jax.numpy as jnp
from jax import lax
from jax.experimental import pallas as pl
from jax.experimental.pallas import tpu as pltpu
from jax.experimental.pallas import tpu_sc as plsc

F32 = jnp.float32
BF16 = jnp.bfloat16
HIGHEST = lax.Precision.HIGHEST

N_MIXERS = 3
GRID_W = 64
CONV_WIDTH = 3
NA_HEADS = 16
NA_ROWS_MAX = 8
NA_COLS = 16
HG_HEADS = 8
HG_KEY_DIM = 128
HG_CHUNK = 32
N_EXPERTS = 32
TOP_K = 4
SWIGLU_LIMIT = 7.0
SWIGLU_ALPHA = 1.702
EPS = 1e-6

LANES = 128
SUBLANES = 8
VMEM_LIMIT_CAP = 60 * 1024 * 1024

NEG_BIG = -1e30
TOKEN_TILE = 256
MOE_ROWS = 512
HG_TILE = 128
HG_HEADS_SHORT = 8
HG_HEADS_LONG = 2
PACKED_CHUNKS = 2
CHUNK_W = 256
SC_WINDOW = 128


def _params(sem, vmem_mb, flags=None):
    return pltpu.CompilerParams(dimension_semantics=sem, flags=flags,
                                vmem_limit_bytes=min(vmem_mb * 1024 * 1024, VMEM_LIMIT_CAP))


def _rms(x):
    return x * lax.rsqrt(jnp.mean(x * x, axis=-1, keepdims=True) + EPS)


def _modulate(x, g, shift, scale):
    return (_rms(x) * g) * (1.0 + scale) + shift


def _bf16_part(x):
    bits = lax.bitcast_convert_type(x, jnp.uint32) & jnp.uint32(0xFFFF0000)
    return lax.bitcast_convert_type(bits, F32)


def _pack_bf16_pairs(x):
    n = x.shape[1]
    bits = lax.bitcast_convert_type(x, jnp.uint32)
    top = (bits + (jnp.uint32(0x7FFF) + ((bits >> 16) & jnp.uint32(1)))) & jnp.uint32(0xFFFF0000)
    return lax.bitcast_convert_type(top[:, :n // 2] | (top[:, n // 2:] >> 16), jnp.int32)


def _unpack_bf16_pairs(words):
    bits = lax.bitcast_convert_type(words, jnp.uint32)
    return (lax.bitcast_convert_type(bits & jnp.uint32(0xFFFF0000), F32),
            lax.bitcast_convert_type(bits << 16, F32))


def _group_of_tile(i, tile, n_prompt, dec_seq):
    start = i * tile
    return jnp.where(start < n_prompt, 0, 1 + (start - n_prompt) // dec_seq)


def _adaln_kernel(c_ref, w_ref, b_ref, o_ref):
    cv = c_ref[...]
    s = cv * jax.nn.sigmoid(cv)
    o_ref[...] = jnp.dot(s, w_ref[...], precision=HIGHEST, preferred_element_type=F32) + b_ref[...]


def _adaln_all(cvecs, w_mod, b_mod):
    depth, d, n = w_mod.shape
    tn = 2048
    return pl.pallas_call(
        _adaln_kernel,
        grid=(depth, n // tn),
        in_specs=[pl.BlockSpec((SUBLANES, d), lambda l, j: (0, 0)),
                  pl.BlockSpec((None, d, tn), lambda l, j: (l, 0, j)),
                  pl.BlockSpec((None, 1, tn), lambda l, j: (l, 0, j))],
        out_specs=pl.BlockSpec((None, SUBLANES, tn), lambda l, j: (l, 0, j)),
        out_shape=jax.ShapeDtypeStruct((depth, SUBLANES, n), F32),
        compiler_params=_params(("arbitrary", "arbitrary"), 40),
        name="adaln",
    )(cvecs, w_mod, b_mod.reshape(depth, 1, n))


def _project(x, mod, g_ref, w_ref, out_refs, d, conv):
    h = _modulate(x, g_ref[...], mod[:, 0:d], mod[:, d:2 * d]).astype(BF16)
    if conv:
        b_ref, cu_ref = out_refs
        b_ref[...] = jnp.dot(h, w_ref[:, 0:d], preferred_element_type=F32)
        c = jnp.dot(h, w_ref[:, d:2 * d], preferred_element_type=F32)
        u = jnp.dot(h, w_ref[:, 2 * d:3 * d], preferred_element_type=F32)
        cu_ref[...] = c * u
    else:
        for j, o_ref in enumerate(out_refs):
            o_ref[...] = jnp.dot(h, w_ref[:, j * d:(j + 1) * d], preferred_element_type=F32)


def _in_kernel(x_ref, mod_ref, g_ref, w_ref, *out_refs, d, conv):
    _project(x_ref[...], mod_ref[...], g_ref, w_ref, out_refs, d, conv)


def _mixer_in(x, mods_l, g, w, geom, conv=False, rows=None):
    d = x.shape[1]
    row0, t = rows if rows is not None else (0, x.shape[0])
    n = w.shape[1]
    n_out = 2 if conv else n // d
    tm = TOKEN_TILE
    t0 = row0 // tm
    grp = functools.partial(_group_of_tile, tile=tm, n_prompt=geom[0], dec_seq=geom[1])
    return pl.pallas_call(
        functools.partial(_in_kernel, d=d, conv=conv),
        grid=(t // tm,),
        in_specs=[pl.BlockSpec((tm, d), lambda i: (t0 + i, 0)),
                  pl.BlockSpec((None, 1, 6 * d), lambda i: (grp(t0 + i), 0, 0)),
                  pl.BlockSpec((1, d), lambda i: (0, 0)),
                  pl.BlockSpec((d, n), lambda i: (0, 0))],
        out_specs=[pl.BlockSpec((tm, d), lambda i: (i, 0))] * n_out,
        out_shape=[jax.ShapeDtypeStruct((t, d), F32)] * n_out,
        compiler_params=_params(("arbitrary",), 48),
        name="mixer_in",
    )(x, mods_l, g, w)


def _tail(m, x_ref, mod, g2_ref, wr_ref, br_ref, outs, run_ref, d):
    x1_ref, h2_ref, idx_ref, gw_ref, rank_ref, cnt_ref = outs
    x1 = x_ref[...] + mod[:, 2 * d:3 * d] * m
    x1_ref[...] = x1
    h2 = _modulate(x1, g2_ref[...], mod[:, 3 * d:4 * d], mod[:, 4 * d:5 * d])
    words = _pack_bf16_pairs(h2)
    for j in range(PACKED_CHUNKS):
        h2_ref[j] = words[:, j * CHUNK_W:(j + 1) * CHUNK_W]
    h2_top = _bf16_part(h2)
    h2_hi = h2_top.astype(BF16)
    h2_lo = (h2 - h2_top).astype(BF16)
    logits = (jnp.dot(h2_hi, wr_ref[0], preferred_element_type=F32)
              + jnp.dot(h2_lo, wr_ref[0], preferred_element_type=F32)
              + jnp.dot(h2_hi, wr_ref[1], preferred_element_type=F32)) + br_ref[...]
    lane = lax.broadcasted_iota(jnp.int32, logits.shape, 1)
    vals, idxs = [], []
    for _ in range(TOP_K):
        mx = jnp.max(logits, axis=-1, keepdims=True)
        ix = jnp.argmax(logits, axis=-1, keepdims=True).astype(jnp.int32)
        vals.append(mx)
        idxs.append(ix)
        logits = jnp.where(lane == ix, NEG_BIG, logits)
    es = [jnp.exp(v - vals[0]) for v in vals]
    den = es[0] + es[1] + es[2] + es[3]

    @pl.when(pl.program_id(0) == 0)
    def _():
        run_ref[...] = jnp.zeros(run_ref.shape, F32)

    tm = lane.shape[0]
    earlier = (lax.broadcasted_iota(jnp.int32, (tm, tm), 0)
               > lax.broadcasted_iota(jnp.int32, (tm, tm), 1)).astype(BF16)
    seen = run_ref[...]
    idx_out = jnp.zeros(lane.shape, jnp.int32)
    gw_out = jnp.zeros(lane.shape, F32)
    rank_out = jnp.zeros(lane.shape, jnp.int32)
    for k in range(TOP_K):
        hit = lane == idxs[k]
        before = jnp.dot(earlier, hit.astype(BF16), preferred_element_type=F32) + seen
        rank_k = jnp.sum(jnp.where(hit, before, 0.0), axis=-1, keepdims=True)
        seen = seen + jnp.sum(hit.astype(F32), axis=0, keepdims=True)
        idx_out = jnp.where(lane == k, idxs[k], idx_out)
        gw_out = jnp.where(lane == k, es[k] / den, gw_out)
        rank_out = jnp.where(lane == k, rank_k.astype(jnp.int32), rank_out)
    run_ref[...] = seen
    cnt_ref[...] = seen
    idx_ref[...] = idx_out
    gw_ref[...] = gw_out
    rank_ref[...] = rank_out


def _plain_out_kernel(a_ref, w_ref, x_ref, mod_ref, g2_ref, wr_ref, br_ref, *rest, d):
    m = jnp.dot(a_ref[...].astype(BF16), w_ref[...], preferred_element_type=F32)
    _tail(m, x_ref, mod_ref[...], g2_ref, wr_ref, br_ref, rest[:-1], rest[-1], d)


def _conv_out_kernel(b_ref, cu_ref, prev_ref, next_ref, ck_ref, w_ref, x_ref, mod_ref, g2_ref, wr_ref,
                     br_ref, *rest, d, tm, n_prompt, seq, dec_seq):
    i = pl.program_id(0)
    cu = cu_ref[...]
    row = lax.broadcasted_iota(jnp.int32, (tm, 1), 0)
    start = i * tm
    in_prompt = start < n_prompt
    seq_len = jnp.where(in_prompt, seq, dec_seq)
    pos = lax.rem(jnp.where(in_prompt, start, start - n_prompt), seq_len) + row
    prev = jnp.where(row == 0, prev_ref[SUBLANES - 1:SUBLANES, :], pltpu.roll(cu, 1, 0))
    prev = jnp.where(pos == 0, 0.0, prev)
    nxt = jnp.where(row == tm - 1, next_ref[0:1, :], pltpu.roll(cu, tm - 1, 0))
    nxt = jnp.where(pos == seq_len - 1, 0.0, nxt)
    conv = prev * ck_ref[0:1, :] + cu * ck_ref[1:2, :] + nxt * ck_ref[2:3, :]
    m = jnp.dot((b_ref[...] * conv).astype(BF16), w_ref[...], preferred_element_type=F32)
    _tail(m, x_ref, mod_ref[...], g2_ref, wr_ref, br_ref, rest[:-1], rest[-1], d)


def _hgrn_out_kernel(o_ref, gout_ref, ng_ref, w_ref, x_ref, mod_ref, g2_ref, wr_ref, br_ref, *rest, d):
    parts = []
    for h in range(HG_HEADS):
        oh = o_ref[:, h * LANES:(h + 1) * LANES]
        parts.append(oh * lax.rsqrt(jnp.mean(oh * oh, axis=-1, keepdims=True) + EPS))
    gout = gout_ref[...]
    y = (jnp.concatenate(parts, axis=1) * ng_ref[...]) * (gout * jax.nn.sigmoid(gout))
    m = jnp.dot(y.astype(BF16), w_ref[...], preferred_element_type=F32)
    _tail(m, x_ref, mod_ref[...], g2_ref, wr_ref, br_ref, rest[:-1], rest[-1], d)


def _mixer_out(kind, acts, w_out, x, mods_l, g2, w_router, b_router, geom, extra=()):
    t, d = x.shape
    tm = TOKEN_TILE
    n_prompt, dec_seq, seq = geom
    grp = functools.partial(_group_of_tile, tile=tm, n_prompt=n_prompt, dec_seq=dec_seq)
    row_spec = pl.BlockSpec((tm, d), lambda i: (i, 0))
    wr = jnp.zeros((d, LANES), F32).at[:, :N_EXPERTS].set(w_router)
    wr_top = _bf16_part(wr)
    wr = jnp.stack([wr_top.astype(BF16), (wr - wr_top).astype(BF16)])
    br = jnp.full((1, LANES), NEG_BIG, F32).at[0, :N_EXPERTS].set(b_router)
    common_specs = [pl.BlockSpec((d, d), lambda i: (0, 0)),
                    row_spec,
                    pl.BlockSpec((None, 1, 6 * d), lambda i: (grp(i), 0, 0)),
                    pl.BlockSpec((1, d), lambda i: (0, 0)),
                    pl.BlockSpec((2, d, LANES), lambda i: (0, 0, 0)),
                    pl.BlockSpec((1, LANES), lambda i: (0, 0))]
    common_args = [w_out, x, mods_l, g2, wr, br]
    if kind == "plain":
        body = functools.partial(_plain_out_kernel, d=d)
        specs = [row_spec] + common_specs
        args = list(acts) + common_args
    elif kind == "conv":
        body = functools.partial(_conv_out_kernel, d=d, tm=tm, n_prompt=n_prompt, seq=seq, dec_seq=dec_seq)
        per = tm // SUBLANES
        last = t // SUBLANES - 1
        specs = [row_spec, row_spec,
                 pl.BlockSpec((SUBLANES, d), lambda i: (jnp.maximum(i * per - 1, 0), 0)),
                 pl.BlockSpec((SUBLANES, d), lambda i: (jnp.minimum((i + 1) * per, last), 0)),
                 pl.BlockSpec((CONV_WIDTH, d), lambda i: (0, 0))] + common_specs
        b, cu = acts
        args = [b, cu, cu, cu, extra[0]] + common_args
    else:
        body = functools.partial(_hgrn_out_kernel, d=d)
        specs = [row_spec, row_spec, pl.BlockSpec((1, d), lambda i: (0, 0))] + common_specs
        args = list(acts) + [extra[0]] + common_args
    lane_spec = pl.BlockSpec((tm, LANES), lambda i: (i, 0))
    return pl.pallas_call(
        body,
        grid=(t // tm,),
        in_specs=specs,
        out_specs=[row_spec, pl.BlockSpec((PACKED_CHUNKS, tm, CHUNK_W), lambda i: (0, i, 0)),
                   lane_spec, lane_spec, lane_spec, pl.BlockSpec((1, LANES), lambda i: (0, 0))],
        out_shape=[jax.ShapeDtypeStruct((t, d), F32), jax.ShapeDtypeStruct((PACKED_CHUNKS, t, CHUNK_W), jnp.int32),
                   jax.ShapeDtypeStruct((t, LANES), jnp.int32), jax.ShapeDtypeStruct((t, LANES), F32),
                   jax.ShapeDtypeStruct((t, LANES), jnp.int32), jax.ShapeDtypeStruct((1, LANES), F32)],
        scratch_shapes=[pltpu.VMEM((1, LANES), F32)],
        compiler_params=_params(("arbitrary",), 40),
        name="mixer_out_" + kind,
    )(*args)


def _softmax_rows(parts):
    mx = functools.reduce(jnp.maximum, [jnp.max(s, axis=-1, keepdims=True) for s in parts])
    es = [jnp.exp(s - mx) for s in parts]
    den = functools.reduce(lambda a, b: a + b, [jnp.sum(e, axis=-1, keepdims=True) for e in es])
    return [e / den for e in es]


def _head_pair_queries(q, dh):
    first = lax.broadcasted_iota(jnp.int32, (1, q.shape[1]), 1) < dh
    return first, (jnp.where(first, q, 0.0).astype(BF16), jnp.where(first, 0.0, q).astype(BF16))


def _na_ctx_kernel(q_ref, k_ref, v_ref, o_ref, *, dh, scale):
    nt = (((1,), (1,)), ((), ()))
    n_q = q_ref.shape[0]
    first = lax.broadcasted_iota(jnp.int32, (1, LANES), 1) < dh
    groups = [slice(pair * LANES, (pair + 1) * LANES) for pair in range(NA_HEADS * dh // LANES)]
    scores = []
    for sl in groups:
        _, queries = _head_pair_queries(q_ref[:, sl], dh)
        q2 = jnp.concatenate(queries, axis=0)
        scores.append(lax.dot_general(q2, k_ref[:, sl].astype(BF16), nt, preferred_element_type=F32))
    for sl, s in zip(groups, scores):
        (p,) = _softmax_rows([s * scale])
        o2 = jnp.dot(p.astype(BF16), v_ref[:, sl].astype(BF16), preferred_element_type=F32)
        o_ref[:, sl] = jnp.where(first, o2[:n_q], o2[n_q:])


def _na_context(q, k, v, batch, seq, t_total, dec_seq):
    d = q.shape[1]
    dh = d // NA_HEADS
    per = dec_seq // seq
    spec = pl.BlockSpec((seq, d), lambda b: (b, 0))
    return pl.pallas_call(
        functools.partial(_na_ctx_kernel, dh=dh, scale=dh ** -0.5),
        grid=(batch,),
        in_specs=[spec, spec, spec],
        out_specs=pl.BlockSpec((None, seq, d), lambda b: (b // per, b % per, 0)),
        out_shape=jax.ShapeDtypeStruct((t_total // dec_seq, dec_seq, d), F32),
        compiler_params=_params(("arbitrary",), 32),
        name="na_context",
    )(q, k, v)


def _na_lat_kernel(*refs, dh, scale, kr, n_batch):
    per = 1 + 2 * kr
    ck_ref, cv_ref, bias_ref, _, o_ref = refs[n_batch * per:]
    nt = (((1,), (1,)), ((), ()))
    heads_per_group = LANES // dh
    n_q = refs[0].shape[0]
    jobs = [(pair, b) for pair in range(NA_HEADS // heads_per_group) for b in range(n_batch)]
    scores = []
    for pair, b in jobs:
        sl = slice(pair * LANES, (pair + 1) * LANES)
        k_refs = refs[b * per + 1:b * per + 1 + kr]
        _, queries = _head_pair_queries(refs[b * per][:, sl], dh)
        q2 = jnp.concatenate(queries, axis=0)
        kw = jnp.concatenate([r[:, sl] for r in k_refs], axis=0).astype(BF16)
        scores.append((lax.dot_general(q2, kw, nt, preferred_element_type=F32),
                       lax.dot_general(q2, ck_ref[b, :, sl].astype(BF16), nt, preferred_element_type=F32)))
    for (pair, b), (s_nb, s_cx) in zip(jobs, scores):
        sl = slice(pair * LANES, (pair + 1) * LANES)
        v_refs = refs[b * per + 1 + kr:(b + 1) * per]
        first = lax.broadcasted_iota(jnp.int32, (1, LANES), 1) < dh
        bias2 = jnp.concatenate([bias_ref[pair * heads_per_group + j] for j in range(heads_per_group)], axis=0)
        vw = jnp.concatenate([r[:, sl] for r in v_refs], axis=0).astype(BF16)
        p_nb, p_cx = _softmax_rows([s_nb * scale + bias2, s_cx * scale])
        o2 = (jnp.dot(p_nb.astype(BF16), vw, preferred_element_type=F32)
              + jnp.dot(p_cx.astype(BF16), cv_ref[b, :, sl].astype(BF16), preferred_element_type=F32))
        o_ref[b, :, sl] = jnp.where(first, o2[:n_q], o2[n_q:])


def _na_bias_table(rpb, kr):
    col = np.arange(GRID_W)
    col_start = np.clip(col - NA_COLS // 2, 0, GRID_W - NA_COLS)
    delta = col[None, :] - col[:, None] + (NA_COLS - 1)
    inside = (col[None, :] >= col_start[:, None]) & (col[None, :] < col_start[:, None] + NA_COLS)
    pick = (np.arange(2 * NA_COLS - 1)[:, None, None] == delta[None]).astype(np.float32)
    tab = jnp.einsum("hrk,kwc->hrwc", rpb, jnp.asarray(pick), precision=HIGHEST)
    tab = jnp.where(jnp.asarray(inside)[None, None], tab, NEG_BIG)
    out = []
    for d0 in range(NA_ROWS_MAX):
        rows = tab[:, d0:d0 + kr]
        out.append(rows.transpose(0, 2, 1, 3).reshape(NA_HEADS, GRID_W, kr * GRID_W))
    return jnp.stack(out, axis=0)


def _na_latent(q, k, v, ctx_k, ctx_v, rpb, out_buf, dec_batch, dec_seq):
    d = q.shape[1]
    dh = d // NA_HEADS
    rows = dec_seq // GRID_W
    kr = min(NA_ROWS_MAX, rows)
    base = 0
    past = ctx_k.shape[1]
    out_slab = (out_buf.shape[0] - dec_batch) // dec_batch
    bias = _na_bias_table(rpb, kr)

    def row_start(r):
        return jnp.clip(r - kr // 2, 0, rows - kr)

    def win_spec(b, j):
        return pl.BlockSpec((GRID_W, d), lambda r: (base + b * rows + row_start(r) + j, 0))

    in_specs, args = [], []
    for b in range(dec_batch):
        in_specs += ([pl.BlockSpec((GRID_W, d), lambda r, b=b: (base + b * rows + r, 0))]
                     + [win_spec(b, j) for j in range(kr)] * 2)
        args += [q] + [k] * kr + [v] * kr
    in_specs += ([pl.BlockSpec((dec_batch, past, d), lambda r: (0, 0, 0))] * 2
                 + [pl.BlockSpec((None, NA_HEADS, GRID_W, kr * GRID_W),
                                 lambda r: (row_start(r) - r + NA_ROWS_MAX - 1, 0, 0, 0))])
    in_specs.append(pl.BlockSpec(memory_space=pl.ANY))
    args += [ctx_k.reshape(dec_batch, past, d), ctx_v.reshape(dec_batch, past, d), bias, out_buf]
    return pl.pallas_call(
        functools.partial(_na_lat_kernel, dh=dh, scale=dh ** -0.5, kr=kr, n_batch=dec_batch),
        grid=(rows,),
        in_specs=in_specs,
        out_specs=pl.BlockSpec((dec_batch, GRID_W, d), lambda r: (out_slab, r, 0)),
        out_shape=jax.ShapeDtypeStruct(out_buf.shape, F32),
        input_output_aliases={len(args) - 1: 0},
        compiler_params=_params(("arbitrary",), 40),
        name="na_latent",
    )(*args)


def _hgrn_scan_kernel(*refs, seq_len, heads, has_s0, has_buf, want_state):
    q_ref, v_ref, z0_ref, z1_ref, lb_ref, tri_ref = refs[:6]
    rest = list(refs[6:])
    s0_ref = rest.pop(0) if has_s0 else None
    if has_buf:
        rest.pop(0)
    o_ref = rest.pop(0)
    sf_ref = rest.pop(0) if want_state else None
    ob_ref = rest.pop(0)
    c, r = HG_CHUNK, HG_TILE
    nc, nt = r // c, seq_len // r
    dk = HG_KEY_DIM
    tpos8 = lax.broadcasted_iota(jnp.int32, (1, SUBLANES, 1), 1)
    row = lax.broadcasted_iota(jnp.int32, (r, r), 0)
    col = lax.broadcasted_iota(jnp.int32, (r, r), 1)
    same16 = (row // (2 * SUBLANES)) == (col // (2 * SUBLANES))
    same32 = (row // c) == (col // c)
    nt_dims = (((1,), (1,)), ((), ()))
    tn_dims = (((0,), (0,)), ((), ()))

    def tile(dr, hh, i, st):
        z_ref = (z0_ref, z1_ref)[dr]
        cols = slice(hh * dk, (hh + 1) * dk)
        lb = lb_ref[dr:dr + 1, cols]
        tri = tri_ref[dr]
        ti = i if dr == 0 else nt - 1 - i
        r0 = pl.multiple_of(ti * r, r)
        q = q_ref[pl.ds(r0, r), cols]
        v = v_ref[pl.ds(r0, r), cols]
        z = z_ref[pl.ds(r0, r), cols]
        g = jnp.log(lb + (1.0 - lb) * jax.nn.sigmoid(z))
        k = (1.0 - lb) * jax.nn.sigmoid(-z)
        g_hi = _bf16_part(g)
        g_mid = _bf16_part(g - g_hi)
        g_lo = (g - g_hi) - g_mid
        b = (jnp.dot(tri, g_hi.astype(BF16), preferred_element_type=F32)
             + jnp.dot(tri, g_mid.astype(BF16), preferred_element_type=F32)
             + jnp.dot(tri, g_lo.astype(BF16), preferred_element_type=F32))
        b3, q3, k3, v3 = (a.reshape(nc, c, dk) for a in (b, q, k, v))
        nb = r // SUBLANES
        b8, q8, k8, v8 = (a.reshape(nb, SUBLANES, dk) for a in (b, q, k, v))
        oi = jnp.zeros((nb, SUBLANES, dk), F32)
        for s in range(SUBLANES):
            keep = (tpos8 >= s) if dr == 0 else (tpos8 <= s)
            e = jnp.exp(jnp.where(keep, b8 - b8[:, s:s + 1, :], NEG_BIG))
            a = jnp.sum(q8 * e * k8[:, s:s + 1, :], axis=-1, keepdims=True)
            oi = oi + a * v8[:, s:s + 1, :]
        att = None
        for span, same_group in ((2 * SUBLANES, same16), (c, same32)):
            half = span // 2
            bg = b.reshape(r // span, span, dk)
            pos = lax.broadcasted_iota(jnp.int32, (1, span, 1), 1)
            edge = bg[:, half - 1:half, :] if dr == 0 else bg[:, half:half + 1, :]
            is_query = (pos >= half) if dr == 0 else (pos < half)
            qf = q.reshape(bg.shape) * jnp.exp(jnp.where(is_query, bg - edge, NEG_BIG))
            kf = k.reshape(bg.shape) * jnp.exp(jnp.where(is_query, NEG_BIG, edge - bg))
            part = lax.dot_general(qf.reshape(r, dk).astype(BF16), kf.reshape(r, dk).astype(BF16), nt_dims,
                                   preferred_element_type=F32)
            part = jnp.where(same_group, part, 0.0)
            att = part if att is None else att + part
        oi = oi.reshape(r, dk) + jnp.dot(att.astype(BF16), v.astype(BF16), preferred_element_type=F32)
        oi = oi.reshape(nc, c, dk)
        bl = b3[:, c - 1:c, :] if dr == 0 else b3[:, 0:1, :]
        qt = (q3 * jnp.exp(b3)).astype(BF16)
        kt = (k3 * jnp.exp(bl - b3)).astype(BF16)
        vb = v3.astype(BF16)
        outs = [None] * nc
        for ci in (range(nc) if dr == 0 else range(nc - 1, -1, -1)):
            oc = lax.dot_general(qt[ci], st.astype(BF16), nt_dims, preferred_element_type=F32)
            outs[ci] = oi[ci] + oc
            kv = lax.dot_general(vb[ci], kt[ci], tn_dims, preferred_element_type=F32)
            st = st * jnp.exp(bl[ci]) + kv
        return r0, jnp.concatenate(outs, axis=0), st

    chains = [(dr, hh) for hh in range(heads) for dr in range(2)]

    def step(i, carry):
        new = []
        for (dr, hh), st in zip(chains, carry):
            r0, o, st = tile(dr, hh, i, st)
            (o_ref if dr == 0 else ob_ref)[pl.ds(r0, r), hh * dk:(hh + 1) * dk] = o
            new.append(st)
        return tuple(new)

    init = tuple(s0_ref[dr, hh] if has_s0 else jnp.zeros((dk, dk), F32) for dr, hh in chains)
    final = lax.fori_loop(0, nt, step, init)
    o_ref[...] += ob_ref[...]
    if want_state:
        for (dr, hh), st in zip(chains, final):
            sf_ref[dr, hh] = st.T


def _hgrn_tri():
    t = np.arange(HG_TILE)
    same = (t[:, None] // HG_CHUNK) == (t[None, :] // HG_CHUNK)
    fwd = same & (t[None, :] <= t[:, None])
    bwd = same & (t[None, :] >= t[:, None])
    return jnp.asarray(np.stack([fwd, bwd]).astype(np.float32)).astype(BF16)


def _hgrn_scan(q, v, z0, z1, lb, row0, n_seq, seq_len, hp, s0_t=None, want_state=False, out_buf=None):
    t_total, d = q.shape
    dk = HG_KEY_DIM
    blk0 = row0 // seq_len
    seq_spec = pl.BlockSpec((seq_len, hp * dk), lambda s, h: (blk0 + s, h))
    st_spec = pl.BlockSpec((None, 2, hp, dk, dk), lambda s, h: (s, 0, h, 0, 0))
    in_specs = [seq_spec] * 4 + [pl.BlockSpec((2, hp * dk), lambda s, h: (0, h)),
                                 pl.BlockSpec((2, HG_TILE, HG_TILE), lambda s, h: (0, 0, 0))]
    args = [q, v, z0, z1, lb, _hgrn_tri()]
    if s0_t is not None:
        in_specs.append(st_spec)
        args.append(s0_t)
    aliases = {}
    if out_buf is not None:
        in_specs.append(pl.BlockSpec(memory_space=pl.ANY))
        args.append(out_buf)
        aliases = {len(args) - 1: 0}
    out_specs = [seq_spec]
    out_shape = [jax.ShapeDtypeStruct((t_total, d), F32)]
    if want_state:
        out_specs.append(st_spec)
        out_shape.append(jax.ShapeDtypeStruct((n_seq, 2, HG_HEADS, dk, dk), F32))
    res = pl.pallas_call(
        functools.partial(_hgrn_scan_kernel, seq_len=seq_len, heads=hp, has_s0=s0_t is not None,
                          has_buf=out_buf is not None, want_state=want_state),
        grid=(n_seq, HG_HEADS // hp),
        in_specs=in_specs,
        out_specs=out_specs,
        out_shape=out_shape,
        input_output_aliases=aliases,
        scratch_shapes=[pltpu.VMEM((seq_len, hp * dk), F32)],
        compiler_params=_params(("arbitrary", "arbitrary"), 32),
        name="hgrn_scan",
    )(*args)
    return res if want_state else (res[0], None)


def _route(idx, rank, counts, t):
    n = t * TOP_K
    rb = MOE_ROWS
    n_blocks = -(-n // rb) + N_EXPERTS
    n_rows = (n_blocks + N_EXPERTS) * rb
    experts = jnp.arange(N_EXPERTS, dtype=jnp.int32)
    counts = counts[0, :N_EXPERTS].astype(jnp.int32)
    padded = (counts + rb - 1) // rb * rb
    padded_end = jnp.cumsum(padded)
    start = padded_end - padded
    e = idx[:, :TOP_K]
    dest = jnp.sum(jnp.where(e[:, :, None] == experts, start, 0), axis=-1) + rank[:, :TOP_K]
    dest_km = dest.T.astype(jnp.int32)
    q = jnp.arange(rb, dtype=jnp.int32)
    spare = n_blocks * rb + experts[:, None] * rb + q[None, :]
    pad_rows = jnp.where(q[None, :] < (padded - counts)[:, None], (start + counts)[:, None] + q[None, :], spare)
    per_chunk = jnp.concatenate([dest_km.reshape(-1), pad_rows.reshape(-1)])
    sc_index = (jnp.arange(PACKED_CHUNKS, dtype=jnp.int32)[:, None] * n_rows + per_chunk[None, :]).reshape(1, -1)
    block_start = jnp.arange(n_blocks, dtype=jnp.int32) * rb
    block_expert = jnp.minimum(jnp.sum((padded_end[None, :] <= block_start[:, None]).astype(jnp.int32), axis=1),
                               N_EXPERTS - 1)
    n_used = (padded_end[-1] // rb).reshape(1)
    return sc_index, block_expert, n_used, n_rows


def _sc_mesh():
    return plsc.VectorSubcoreMesh(core_axis_name="c", subcore_axis_name="s")


def _sc_dispatch(h2c, sc_index, t, n_rows):
    win = SC_WINDOW
    tw = t // win
    nw = (t * TOP_K + N_EXPERTS * MOE_ROWS) // win
    n_chunks = h2c.shape[0]

    def scatter(x_hbm, i_hbm, o_hbm):
        def body(x_vmem, i_vmem):
            pltpu.sync_copy(x_vmem, o_hbm.at[i_vmem.at[0]])

        pltpu.emit_pipeline(
            body,
            grid=(n_chunks * nw,),
            in_specs=[pl.BlockSpec((win, CHUNK_W), index_map=lambda i: ((i // nw) * tw + (i % nw) % tw, 0)),
                      pl.BlockSpec((1, win), index_map=lambda i: (0, i))],
            out_specs=[],
            core_axis_name=("c", "s"),
            dimension_semantics=(pltpu.PARALLEL,),
        )(x_hbm, i_hbm)

    out = pl.kernel(scatter, out_type=jax.ShapeDtypeStruct((n_chunks * n_rows, CHUNK_W), h2c.dtype),
                    mesh=_sc_mesh(), name="moe_dispatch")(h2c.reshape(n_chunks * t, CHUNK_W), sc_index)
    return out.reshape(n_chunks, n_rows, CHUNK_W)


def _sc_collect(y_rows, sc_index, t, n_rows):
    win = SC_WINDOW
    n_chunks = y_rows.shape[0]
    nw = (t * TOP_K + N_EXPERTS * MOE_ROWS) // win
    aw = t * TOP_K // win

    def gather(y_hbm, i_hbm, o_hbm):
        def body(i_vmem, o_vmem):
            pltpu.sync_copy(y_hbm.at[i_vmem.at[0]], o_vmem)

        pltpu.emit_pipeline(
            body,
            grid=(n_chunks * aw,),
            in_specs=[pl.BlockSpec((1, win), index_map=lambda i: (0, (i // aw) * nw + i % aw))],
            out_specs=[pl.BlockSpec((win, CHUNK_W), index_map=lambda i: (i, 0))],
            core_axis_name=("c", "s"),
            dimension_semantics=(pltpu.PARALLEL,),
        )(i_hbm, o_hbm)

    out = pl.kernel(gather, out_type=jax.ShapeDtypeStruct((n_chunks * TOP_K * t, CHUNK_W), y_rows.dtype),
                    mesh=_sc_mesh(), name="moe_collect")(y_rows.reshape(n_chunks * n_rows, CHUNK_W), sc_index)
    return out.reshape(n_chunks, TOP_K, t, CHUNK_W)


def _moe_kernel(be_ref, nu_ref, x_ref, w1_ref, b1_ref, w2_ref, b2_ref, o_ref, w1b, w2b, *, dff):
    i = pl.program_id(0)
    e = be_ref[i]
    prev = be_ref[jnp.maximum(i - 1, 0)]

    @pl.when(jnp.logical_or(i == 0, e != prev))
    def _():
        w1b[...] = w1_ref[...].astype(BF16)
        w2b[...] = w2_ref[...].astype(BF16)

    @pl.when(i < nu_ref[0])
    def _():
        halves = [_unpack_bf16_pairs(x_ref[j]) for j in range(PACKED_CHUNKS)]
        xb = jnp.concatenate([h[0] for h in halves] + [h[1] for h in halves], axis=1).astype(BF16)
        gu = jnp.dot(xb, w1b[...], preferred_element_type=F32) + b1_ref[...]
        g = jnp.minimum(gu[:, :dff], SWIGLU_LIMIT)
        u = jnp.clip(gu[:, dff:], -SWIGLU_LIMIT, SWIGLU_LIMIT)
        a = (u + 1.0) * (g * jax.nn.sigmoid(SWIGLU_ALPHA * g))
        y = jnp.dot(a.astype(BF16), w2b[...], preferred_element_type=F32) + b2_ref[...]
        words = _pack_bf16_pairs(y)
        for j in range(PACKED_CHUNKS):
            o_ref[j] = words[:, j * CHUNK_W:(j + 1) * CHUNK_W]

    @pl.when(i >= nu_ref[0])
    def _():
        o_ref[...] = jnp.zeros(o_ref.shape, o_ref.dtype)


def _moe_experts(x_rows, block_expert, n_used, layer, w1, b1, w2, b2):
    _, n_rows, _ = x_rows.shape
    rb = MOE_ROWS
    n_blocks = n_rows // rb - N_EXPERTS
    depth, n_e, d, dff2 = w1.shape
    dff = dff2 // 2
    row_spec = pl.BlockSpec((PACKED_CHUNKS, rb, CHUNK_W), lambda i, be, nu: (0, jnp.minimum(i, nu[0] - 1), 0))
    grid_spec = pltpu.PrefetchScalarGridSpec(
        num_scalar_prefetch=2,
        grid=(n_blocks,),
        in_specs=[row_spec,
                  pl.BlockSpec((None, None, d, dff2), lambda i, be, nu: (layer, be[i], 0, 0)),
                  pl.BlockSpec((None, None, 1, dff2), lambda i, be, nu: (layer, be[i], 0, 0)),
                  pl.BlockSpec((None, None, dff, d), lambda i, be, nu: (layer, be[i], 0, 0)),
                  pl.BlockSpec((None, None, 1, d), lambda i, be, nu: (layer, be[i], 0, 0))],
        out_specs=pl.BlockSpec((PACKED_CHUNKS, rb, CHUNK_W), lambda i, be, nu: (0, i, 0)),
        scratch_shapes=[pltpu.VMEM((d, dff2), BF16), pltpu.VMEM((dff, d), BF16)],
    )
    return pl.pallas_call(
        functools.partial(_moe_kernel, dff=dff),
        grid_spec=grid_spec,
        out_shape=jax.ShapeDtypeStruct((PACKED_CHUNKS, n_rows, CHUNK_W), jnp.int32),
        compiler_params=_params(("arbitrary",), 56),
        name="moe_experts",
    )(block_expert, n_used, x_rows, w1, b1.reshape(depth, n_e, 1, dff2), w2, b2.reshape(depth, n_e, 1, d))


def _combined_rows(y_ref, gw_ref, x_ref, mod_ref, d):
    gw = gw_ref[...]
    pieces = {}
    for j in range(PACKED_CHUNKS):
        accs = [None, None]
        for k in range(TOP_K):
            for side, part in enumerate(_unpack_bf16_pairs(y_ref[j, k])):
                term = gw[:, k:k + 1] * part
                accs[side] = term if accs[side] is None else accs[side] + term
        for side, acc in enumerate(accs):
            c0 = side * (d // 2) + j * CHUNK_W
            pieces[c0] = x_ref[:, c0:c0 + CHUNK_W] + mod_ref[:, 5 * d + c0:5 * d + c0 + CHUNK_W] * acc
    return jnp.concatenate([pieces[c0] for c0 in sorted(pieces)], axis=1)


def _combine_final_kernel(y_ref, gw_ref, x_ref, mod_ref, g_ref, o_ref, *, d):
    o_ref[...] = _rms(_combined_rows(y_ref, gw_ref, x_ref, mod_ref, d)) * g_ref[...]


def _combine_in_kernel(y_ref, gw_ref, x_ref, mod_ref, modn_ref, g_ref, w_ref, *rest, d, conv, has_buf):
    x2_ref, *out_refs = rest[1:] if has_buf else rest
    x2 = _combined_rows(y_ref, gw_ref, x_ref, mod_ref, d)
    x2_ref[...] = x2
    _project(x2, modn_ref[...], g_ref, w_ref, out_refs, d, conv)


def _combine_specs(y_slots, gw, x1, mods_l, geom, rows):
    d = x1.shape[1]
    row0, t = rows if rows is not None else (0, x1.shape[0])
    tm = TOKEN_TILE
    t0 = row0 // tm
    grp = functools.partial(_group_of_tile, tile=tm, n_prompt=geom[0], dec_seq=geom[1])
    in_specs = [pl.BlockSpec((PACKED_CHUNKS, TOP_K, tm, CHUNK_W), lambda i: (0, 0, t0 + i, 0)),
                pl.BlockSpec((tm, LANES), lambda i: (t0 + i, 0)),
                pl.BlockSpec((tm, d), lambda i: (t0 + i, 0)),
                pl.BlockSpec((None, 1, 6 * d), lambda i: (grp(t0 + i), 0, 0))]
    return in_specs, [y_slots, gw, x1, mods_l], t, t0, grp


def _combine_final(y_slots, gw, x1, mods_l, geom, final_g, rows):
    d = x1.shape[1]
    tm = TOKEN_TILE
    in_specs, args, t, _, _ = _combine_specs(y_slots, gw, x1, mods_l, geom, rows)
    return pl.pallas_call(
        functools.partial(_combine_final_kernel, d=d),
        grid=(t // tm,),
        in_specs=in_specs + [pl.BlockSpec((1, d), lambda i: (0, 0))],
        out_specs=pl.BlockSpec((tm, d), lambda i: (i, 0)),
        out_shape=jax.ShapeDtypeStruct((t, d), F32),
        compiler_params=_params(("arbitrary",), 24),
        name="moe_combine_final",
    )(*args, final_g)


def _combine_mixer_in(y_slots, gw, x1, mods_l, mods_next, g, w, geom, conv=False, rows=None, x2_buf=None):
    t_total, d = x1.shape
    tm = TOKEN_TILE
    n = w.shape[1]
    n_out = 2 if conv else n // d
    in_specs, args, t, t0, grp = _combine_specs(y_slots, gw, x1, mods_l, geom, rows)
    in_specs += [pl.BlockSpec((None, 1, 6 * d), lambda i: (grp(t0 + i), 0, 0)),
                 pl.BlockSpec((1, d), lambda i: (0, 0)),
                 pl.BlockSpec((d, n), lambda i: (0, 0))]
    args += [mods_next, g, w]
    aliases = {}
    if x2_buf is not None:
        in_specs.append(pl.BlockSpec(memory_space=pl.ANY))
        args.append(x2_buf)
        aliases = {len(args) - 1: 0}
    res = pl.pallas_call(
        functools.partial(_combine_in_kernel, d=d, conv=conv, has_buf=x2_buf is not None),
        grid=(t // tm,),
        in_specs=in_specs,
        out_specs=[pl.BlockSpec((tm, d), lambda i: (t0 + i, 0))] + [pl.BlockSpec((tm, d), lambda i: (i, 0))] * n_out,
        out_shape=[jax.ShapeDtypeStruct((t_total, d), F32)] + [jax.ShapeDtypeStruct((t, d), F32)] * n_out,
        input_output_aliases=aliases,
        compiler_params=_params(("arbitrary",), 56),
        name="combine_mixer_in",
    )(*args)
    return res[0], res[1:]


def kernel(x_prompt, x_sample, c, c_ctx, cache_na_k, cache_na_v, state_hgrn, norm_mix, norm_ffn, w_mod, b_mod,
           conv_w_in, conv_k, conv_w_out, na_w_qkv, na_rpb, na_w_o, hg_w_qig, hg_w_f, hg_lb, hg_norm, hg_w_o,
           moe_w_router, moe_b_router, moe_w1, moe_b1, moe_w2, moe_b2, final_norm):
    batch, seq, d = x_prompt.shape
    dec_batch, dec_seq, _ = x_sample.shape
    depth = w_mod.shape[0]
    n_prompt = batch * seq
    t = n_prompt + dec_batch * dec_seq
    geom = (n_prompt, dec_seq, seq)
    assert seq % TOKEN_TILE == 0 and dec_seq % TOKEN_TILE == 0 and seq % HG_TILE == 0
    assert n_prompt % (dec_seq * dec_batch) == 0 and dec_seq % GRID_W == 0 and 1 + dec_batch <= SUBLANES
    assert d == 2 * PACKED_CHUNKS * CHUNK_W and t % SC_WINDOW == 0 and MOE_ROWS % SC_WINDOW == 0

    x = jnp.concatenate([x_prompt.reshape(n_prompt, d), x_sample.reshape(dec_batch * dec_seq, d)], axis=0)
    cvecs = jnp.zeros((SUBLANES, d), F32).at[0].set(c_ctx).at[1:1 + dec_batch].set(c)
    mods = _adaln_all(cvecs, w_mod, b_mod)[:, :1 + dec_batch].reshape(depth, 1 + dec_batch, 1, 6 * d)

    lb_soft = jax.nn.softmax(hg_lb.astype(F32), axis=0)
    lower_bounds = jnp.cumsum(lb_soft, axis=0) - lb_soft[0]

    new_k, new_v, new_s = [], [], []
    pending = None
    for l in range(depth):
        kind, j = l % N_MIXERS, l // N_MIXERS
        g1 = norm_mix[l].reshape(1, d)
        g2 = norm_ffn[l].reshape(1, d)

        def project(x, w, conv=False, rows=None, x2_buf=None):
            if pending is None:
                return x, _mixer_in(x, mods[l], g1, w, geom, conv=conv, rows=rows)
            return _combine_mixer_in(*pending, mods[l], g1, w, geom, conv=conv, rows=rows, x2_buf=x2_buf)

        if kind == 0:
            x, (b, cu) = project(x, conv_w_in[j].astype(BF16), conv=True)
        elif kind == 1:
            w_qkv = na_w_qkv[j].astype(BF16)
            x, (qp, kp, vp) = project(x, w_qkv, rows=(0, n_prompt))
            x, (qs, ks, vs) = project(x, w_qkv, rows=(n_prompt, t - n_prompt), x2_buf=x)
        else:
            w_in = jnp.concatenate([hg_w_qig[j], hg_w_f[j, 0], hg_w_f[j, 1]], axis=1).astype(BF16)
            x, (q, v, gout, z0, z1) = project(x, w_in)
        tail_args = (x, mods[l], g2, moe_w_router[l], moe_b_router[l], geom)
        if kind == 0:
            x1, h2c, idx, gw, rank, counts = _mixer_out("conv", (b, cu), conv_w_out[j].astype(BF16), *tail_args,
                                         extra=(conv_k[j],))
        elif kind == 1:
            o = _na_context(qp, kp, vp, batch, seq, t, dec_seq)
            o = _na_latent(qs, ks, vs, cache_na_k[:, j], cache_na_v[:, j], na_rpb[j], o, dec_batch, dec_seq)
            new_k.append(kp.reshape(batch, seq, NA_HEADS, d // NA_HEADS))
            new_v.append(vp.reshape(batch, seq, NA_HEADS, d // NA_HEADS))
            x1, h2c, idx, gw, rank, counts = _mixer_out("plain", (o.reshape(t, d),),
                                         na_w_o[j].astype(BF16), *tail_args)
        else:
            lb = lower_bounds[l]
            o, s_p = _hgrn_scan(q, v, z0, z1, lb, 0, batch, seq, HG_HEADS_SHORT, want_state=True)
            s0_t = jnp.swapaxes(state_hgrn[:, j].astype(F32), -1, -2)
            o, _ = _hgrn_scan(q, v, z0, z1, lb, n_prompt, dec_batch, dec_seq, HG_HEADS_LONG, s0_t=s0_t, out_buf=o)
            new_s.append(s_p)
            x1, h2c, idx, gw, rank, counts = _mixer_out("hgrn", (o, gout),
                                         hg_w_o[j].astype(BF16), *tail_args, extra=(hg_norm[j].reshape(1, d),))
        sc_index, block_expert, n_used, n_rows = _route(idx, rank, counts, t)
        x_rows = _sc_dispatch(h2c, sc_index, t, n_rows)
        y_rows = _moe_experts(x_rows, block_expert, n_used, l, moe_w1, moe_b1, moe_w2, moe_b2)
        y_slots = _sc_collect(y_rows, sc_index, t, n_rows)
        pending = (y_slots, gw, x1, mods[l])

    final_g = final_norm.reshape(1, d)
    y_prompt = _combine_final(*pending, geom, final_g, (0, n_prompt)).reshape(batch, seq, d)
    y_sample = _combine_final(*pending, geom, final_g, (n_prompt, t - n_prompt)).reshape(dec_batch, dec_seq, d)
    new_na_k = jnp.stack(new_k, axis=1).astype(x_prompt.dtype)
    new_na_v = jnp.stack(new_v, axis=1).astype(x_prompt.dtype)
    new_hgrn_state = jnp.stack(new_s, axis=1).astype(x_prompt.dtype)
    return (y_prompt, y_sample, new_na_k, new_na_v, new_hgrn_state)
```
